```python
import math
import jax
import jax.numpy as jnp
from jax import lax
import numpy as np

D_MODEL = 2048
BATCH = 2
SEQ = 4096
DEPTH = 2
DEC_BATCH = 8
DEC_SEQ = 1
PAST_LEN = 16384
PAGE_SIZE = 128

HEAD_DIM = 64
N_BRANCH = 3
BRANCH_WIDTH = 3 * D_MODEL // 4
NORM_EPS = 1e-5
ATT_WINDOWS = (128, 512, 2048)
ATT_DILATIONS = (1, 4, 16)
ATT_N_GROUPS = 3
ATT_SPAN = 128
ATT_BLOCK = 128
ATT_HEADS = BRANCH_WIDTH // (ATT_N_GROUPS * HEAD_DIM)
ATT_OUT = ATT_HEADS * HEAD_DIM
ATT_SCALE = HEAD_DIM ** -0.5
SSM_HEADS = BRANCH_WIDTH // HEAD_DIM
SSM_GROUPS = 4
SSM_STATE = 128
SSM_CONV = 4
SSM_CHUNK = 128
SSM_CONV_DIM = BRANCH_WIDTH + 2 * SSM_GROUPS * SSM_STATE
RWKV_HEADS = BRANCH_WIDTH // HEAD_DIM
DECAY_LORA = max(32, int(round(1.8 * D_MODEL ** 0.5 / 32)) * 32)
AAA_LORA = max(32, int(round(1.8 * D_MODEL ** 0.5 / 32)) * 32)
GATE_LORA = max(32, int(round(0.6 * D_MODEL ** 0.8 / 32)) * 32)
RWKV_WIDTHS = (BRANCH_WIDTH, BRANCH_WIDTH, BRANCH_WIDTH, DECAY_LORA, AAA_LORA, GATE_LORA)
RWKV_SPLITS = tuple(int(s) for s in np.cumsum(RWKV_WIDTHS)[:-1])
RWKV_SHIFT_DIM = sum(RWKV_WIDTHS)
RWKV_LN_EPS = 64e-5
D_FF = 4 * D_MODEL
IN_WIDTHS = (3 * BRANCH_WIDTH, BRANCH_WIDTH, SSM_CONV_DIM, SSM_HEADS, RWKV_SHIFT_DIM, N_BRANCH * D_MODEL)
IN_SPLITS = tuple(int(s) for s in np.cumsum(IN_WIDTHS)[:-1])
IN_DIM = sum(IN_WIDTHS)

kernel_name = 'hybrid_dilated_ssd_rwkv7_decoder_step'


def rmsnorm(x, g):
    xf = x.astype(jnp.float32)
    y = xf * lax.rsqrt(jnp.mean(xf * xf, axis=-1, keepdims=True) + NORM_EPS)
    return (y * g.astype(jnp.float32)).astype(x.dtype)


def alibi_slopes():
    n = ATT_N_GROUPS * ATT_HEADS
    idx = jnp.arange(1, n + 1, dtype=jnp.float32)
    return jnp.exp2(-8.0 * idx / n).reshape(ATT_N_GROUPS, ATT_HEADS)


def last_rows(a, n):
    t = a.shape[1]
    if t >= n:
        return a[:, t - n:]
    return jnp.pad(a, ((0, 0), (n - t, 0)) + ((0, 0),) * (a.ndim - 2))


def dilated_attention_prompt(q, k, v, dil, slopes):
    bn, t, h, e = q.shape
    u_len = t // dil
    nb = -(-u_len // ATT_BLOCK)
    u_pad = nb * ATT_BLOCK
    n = bn * dil

    def residue_major(a):
        a = a.astype(jnp.float32).reshape(bn, u_len, dil, h, e).transpose(0, 2, 1, 3, 4).reshape(n, u_len, h, e)
        return jnp.pad(a, ((0, 0), (0, u_pad - u_len), (0, 0), (0, 0)))

    def key_blocks(a):
        a = jnp.pad(a, ((0, 0), (ATT_SPAN, 0), (0, 0), (0, 0))).reshape(n, nb + 1, ATT_BLOCK, h, e)
        return jnp.concatenate([a[:, :-1], a[:, 1:]], axis=2)

    qb = residue_major(q).reshape(n, nb, ATT_BLOCK, h, e)
    kb = key_blocks(residue_major(k))
    vb = key_blocks(residue_major(v))
    s = jnp.einsum('nbqhe,nbkhe->nbhqk', qb, kb) * ATT_SCALE
    qi = jnp.arange(ATT_BLOCK)[:, None]
    kj = jnp.arange(2 * ATT_BLOCK)[None, :]
    delta = qi - kj + ATT_SPAN
    u_k = jnp.arange(nb)[:, None, None] * ATT_BLOCK + kj[None] - ATT_SPAN
    valid = (delta >= 0) & (delta <= ATT_SPAN) & (u_k >= 0)
    bias = -slopes[:, None, None] * (delta * dil).astype(jnp.float32)
    s = jnp.where(valid[None, :, None], s + bias[None, None], -jnp.inf)
    m = jnp.max(s, axis=-1, keepdims=True)
    p = jnp.exp(s - m)
    l = jnp.sum(p, axis=-1)
    o = jnp.einsum('nbhqk,nbkhe->nbqhe', p, vb) / jnp.swapaxes(l, 2, 3)[..., None]
    lse = jnp.swapaxes(m[..., 0] + jnp.log(l), 2, 3)

    def back(a):
        a = a.reshape((n, u_pad) + a.shape[3:])[:, :u_len]
        a = jnp.swapaxes(a.reshape((bn, dil, u_len) + a.shape[2:]), 1, 2)
        return a.reshape((bn, t) + a.shape[3:])

    return back(o), back(lse)


def dilated_attention_sample(q, k_new, v_new, k_buf, v_buf, dil, slopes):
    s_len = q.shape[1]
    win = k_buf.shape[1]
    kc = jnp.concatenate([k_buf.astype(k_new.dtype), k_new], axis=1)
    vc = jnp.concatenate([v_buf.astype(v_new.dtype), v_new], axis=1)
    i = jnp.arange(s_len)[:, None]
    dist = jnp.arange(ATT_SPAN + 1)[None, :] * dil
    idx = win + i - dist
    valid = (PAST_LEN + i - dist) >= 0
    kg = kc[:, idx].astype(jnp.float32)
    vg = vc[:, idx].astype(jnp.float32)
    s = jnp.einsum('bqhe,bqjhe->bhqj', q.astype(jnp.float32), kg) * ATT_SCALE
    s = s - slopes[:, None, None] * dist.astype(jnp.float32)
    s = jnp.where(valid, s, -jnp.inf)
    m = jnp.max(s, axis=-1, keepdims=True)
    p = jnp.exp(s - m)
    l = jnp.sum(p, axis=-1)
    o = jnp.einsum('bhqj,bqjhe->bqhe', p, vg) / jnp.swapaxes(l, 1, 2)[..., None]
    lse = jnp.swapaxes(m[..., 0] + jnp.log(l), 1, 2)
    return o, lse, kc[:, -win:], vc[:, -win:]


def attention_branch(qkv, bufs):
    bn, t, _ = qkv.shape
    q, k, v = (a.reshape(bn, t, ATT_N_GROUPS, ATT_HEADS, HEAD_DIM) for a in jnp.split(qkv, 3, axis=-1))
    slopes = alibi_slopes()
    outs, lses, new_bufs = [], [], []
    for g in range(ATT_N_GROUPS):
        win, dil = ATT_WINDOWS[g], ATT_DILATIONS[g]
        qg, kg, vg = q[:, :, g], k[:, :, g], v[:, :, g]
        if bufs is None:
            o, lse = dilated_attention_prompt(qg, kg, vg, dil, slopes[g])
            nk, nv = last_rows(kg, win), last_rows(vg, win)
        else:
            o, lse, nk, nv = dilated_attention_sample(qg, kg, vg, bufs[2 * g], bufs[2 * g + 1], dil, slopes[g])
        outs.append(o)
        lses.append(lse)
        new_bufs += [nk, nv]
    o = jnp.stack(outs, axis=2)
    wts = jax.nn.softmax(jnp.stack(lses, axis=2), axis=2)
    o = jnp.sum(wts[..., None] * o, axis=2).reshape(bn, t, ATT_OUT)
    return o.astype(qkv.dtype), tuple(new_bufs)


def ssd_chunked(x, dt, a_neg, bm, cm, h0):
    bn, t, h, p = x.shape
    g, e, L = SSM_GROUPS, SSM_HEADS // SSM_GROUPS, SSM_CHUNK
    nc = t // L
    x = x.reshape(bn, nc, L, g, e, p)
    dt = dt.reshape(bn, nc, L, g, e)
    bm = bm.reshape(bn, nc, L, g, SSM_STATE)
    cm = cm.reshape(bn, nc, L, g, SSM_STATE)
    a_cum = jnp.cumsum(dt * a_neg.reshape(g, e), axis=2)
    diff = a_cum[:, :, :, None] - a_cum[:, :, None, :]
    causal = jnp.tril(jnp.ones((L, L), dtype=bool))[:, :, None, None]
    decay = jnp.exp(jnp.where(causal, diff, -jnp.inf))
    cb = jnp.einsum('bclgn,bcsgn->bclsg', cm, bm)
    w_ls = cb[..., None] * decay * dt[:, :, None]
    y_diag = jnp.einsum('bclsge,bcsgep->bclgep', w_ls, x)
    xw = x * (jnp.exp(a_cum[:, :, -1:] - a_cum) * dt)[..., None]
    states = jnp.einsum('bclgn,bclgep->bcgepn', bm, xw)
    chunk_decay = jnp.exp(a_cum[:, :, -1])

    def step(hc, inp):
        st, cd = inp
        return cd[..., None, None] * hc + st, hc

    h_fin, h_prev = lax.scan(step, h0.reshape(bn, g, e, p, SSM_STATE),
                             (jnp.moveaxis(states, 1, 0), jnp.moveaxis(chunk_decay, 1, 0)))
    h_prev = jnp.moveaxis(h_prev, 0, 1)
    y_off = jnp.einsum('bclgn,bcgepn->bclgep', cm, h_prev) * jnp.exp(a_cum)[..., None]
    y = (y_diag + y_off).reshape(bn, t, h, p)
    return y, h_fin.reshape(bn, h, p, SSM_STATE)


def ssd_recurrent(x, dt, a_neg, bm, cm, h0):
    rep = SSM_HEADS // SSM_GROUPS
    bh = jnp.repeat(bm, rep, axis=2)
    ch = jnp.repeat(cm, rep, axis=2)
    da = jnp.exp(dt * a_neg)

    def step(hc, inp):
        x_t, dt_t, da_t, b_t, c_t = inp
        hc = da_t[..., None, None] * hc + (dt_t[..., None] * x_t)[..., None] * b_t[:, :, None, :]
        return hc, jnp.einsum('bhpn,bhn->bhp', hc, c_t)

    xs = tuple(jnp.moveaxis(a, 1, 0) for a in (x, dt, da, bh, ch))
    h_fin, ys = lax.scan(step, h0, xs)
    return jnp.moveaxis(ys, 0, 1), h_fin


def mamba2_branch(z, xbc, dt_raw, conv_st, ssm_st, conv_w, conv_b, dt_bias, a_log, d_skip, norm_g):
    bn, t, _ = xbc.shape
    prompt = conv_st is None
    if prompt:
        conv_st = jnp.zeros((bn, SSM_CONV - 1, SSM_CONV_DIM), xbc.dtype)
    xp = jnp.concatenate([conv_st.astype(xbc.dtype), xbc], axis=1)
    xc = lax.conv_general_dilated(xp, conv_w.astype(xp.dtype)[:, None, :], window_strides=(1,), padding='VALID',
                                  dimension_numbers=('NWC', 'WIO', 'NWC'), feature_group_count=SSM_CONV_DIM)
    xc = jax.nn.silu((xc + conv_b).astype(jnp.float32))
    xs, bm, cm = jnp.split(xc, (BRANCH_WIDTH, BRANCH_WIDTH + SSM_GROUPS * SSM_STATE), axis=-1)
    xs = xs.reshape(bn, t, SSM_HEADS, HEAD_DIM)
    bm = bm.reshape(bn, t, SSM_GROUPS, SSM_STATE)
    cm = cm.reshape(bn, t, SSM_GROUPS, SSM_STATE)
    dt = jax.nn.softplus(dt_raw.astype(jnp.float32) + dt_bias.astype(jnp.float32))
    a_neg = -jnp.exp(a_log.astype(jnp.float32))
    if prompt:
        h0 = jnp.zeros((bn, SSM_HEADS, HEAD_DIM, SSM_STATE), jnp.float32)
        y, h_fin = ssd_chunked(xs, dt, a_neg, bm, cm, h0)
    else:
        y, h_fin = ssd_recurrent(xs, dt, a_neg, bm, cm, ssm_st.astype(jnp.float32))
    y = (y + d_skip.astype(jnp.float32)[:, None] * xs).reshape(bn, t, BRANCH_WIDTH)
    y = y * jax.nn.silu(z.astype(jnp.float32))
    yg = y.reshape(bn, t, SSM_GROUPS, BRANCH_WIDTH // SSM_GROUPS)
    yg = yg * lax.rsqrt(jnp.mean(yg * yg, axis=-1, keepdims=True) + NORM_EPS)
    y = yg.reshape(bn, t, BRANCH_WIDTH) * norm_g.astype(jnp.float32)
    return y.astype(z.dtype), xp[:, -(SSM_CONV - 1):], h_fin


def rwkv7_recurrence(s0, r, w, k, v, kk, a):
    def step(s, inp):
        r_t, w_t, k_t, v_t, kk_t, a_t = inp
        sa = jnp.einsum('bhvk,bhk->bhv', s, -kk_t)
        s = s * w_t[:, :, None, :] + sa[..., None] * (kk_t * a_t)[:, :, None, :] + v_t[..., None] * k_t[:, :, None, :]
        return s, jnp.einsum('bhvk,bhk->bhv', s, r_t)

    xs = tuple(jnp.moveaxis(a_, 1, 0) for a_ in (r, w, k, v, kk, a))
    s_fin, ys = lax.scan(step, s0, xs)
    return jnp.moveaxis(ys, 0, 1), s_fin


def rwkv7_branch(ru, shift_st, s0, mu, w0, w_up, a0, a_up, g_up, k_k, k_a, r_k, ln_g, ln_b):
    bn, t, _ = ru.shape
    if shift_st is None:
        shift_st = jnp.zeros((bn, RWKV_SHIFT_DIM), ru.dtype)
        s0 = jnp.zeros((bn, RWKV_HEADS, HEAD_DIM, HEAD_DIM), jnp.float32)
    prev = jnp.concatenate([shift_st[:, None].astype(ru.dtype), ru[:, :-1]], axis=1)
    u = (ru + mu * (prev - ru)).astype(jnp.float32)
    r, k, v, uw, ua, ug = jnp.split(u, RWKV_SPLITS, axis=-1)
    w_log = -jax.nn.softplus(-(w0 + jnp.tanh(uw) @ w_up)) - 0.5
    decay = jnp.exp(-jnp.exp(w_log))
    a = jax.nn.sigmoid(a0 + ua @ a_up)
    g = jax.nn.sigmoid(ug) @ g_up
    heads = lambda arr: arr.reshape(bn, t, RWKV_HEADS, HEAD_DIM)
    kk = heads(k * k_k)
    kk = kk / jnp.maximum(jnp.sqrt(jnp.sum(kk * kk, axis=-1, keepdims=True)), 1e-12)
    k = k * (1.0 + (a - 1.0) * k_a)
    r, k, v, decay, a = heads(r), heads(k), heads(v), heads(decay), heads(a)
    y, s_fin = rwkv7_recurrence(s0.astype(jnp.float32), r, decay, k, v, kk, a)
    mean = jnp.mean(y, axis=-1, keepdims=True)
    var = jnp.mean(jnp.square(y - mean), axis=-1, keepdims=True)
    y = ((y - mean) * lax.rsqrt(var + RWKV_LN_EPS)).reshape(bn, t, BRANCH_WIDTH) * ln_g + ln_b
    bonus = jnp.sum(r * k * r_k, axis=-1, keepdims=True) * v
    y = (y + bonus.reshape(bn, t, BRANCH_WIDTH)) * g
    return y.astype(ru.dtype), ru[:, -1], s_fin


def trunk_layer(x, st, ln1_g, w_in, ssm_conv_w, ssm_conv_b, ssm_dt_bias, ssm_a_log, ssm_d, ssm_norm_g,
                rwkv_mu, rwkv_w0, rwkv_w_up, rwkv_a0, rwkv_a_up, rwkv_g_up, rwkv_k_k, rwkv_k_a, rwkv_r_k,
                rwkv_ln_g, rwkv_ln_b, w_branch_att, w_branch_ssm, w_branch_rwkv, w_out, ln2_g, w_up, w_down):
    h = rmsnorm(x, ln1_g)
    proj = h @ w_in
    qkv, ssm_z, ssm_xbc, ssm_dt, rwkv_in, gate_in = jnp.split(proj, IN_SPLITS, axis=-1)
    if st is None:
        att_bufs, conv_st, ssm_st, shift_st, rwkv_st = None, None, None, None, None
    else:
        att_bufs = st[:6]
        conv_st, ssm_st, shift_st, rwkv_st = st[6], st[7], st[8], st[9]
    o_att, att_new = attention_branch(qkv, att_bufs)
    o_ssm, conv_new, ssm_new = mamba2_branch(ssm_z, ssm_xbc, ssm_dt, conv_st, ssm_st, ssm_conv_w, ssm_conv_b,
                                             ssm_dt_bias, ssm_a_log, ssm_d, ssm_norm_g)
    o_rwkv, shift_new, rwkv_new = rwkv7_branch(rwkv_in, shift_st, rwkv_st, rwkv_mu, rwkv_w0, rwkv_w_up, rwkv_a0,
                                               rwkv_a_up, rwkv_g_up, rwkv_k_k, rwkv_k_a, rwkv_r_k, rwkv_ln_g, rwkv_ln_b)
    gates = jax.nn.sigmoid(gate_in.reshape(gate_in.shape[:-1] + (N_BRANCH, D_MODEL)))
    merged = (gates[..., 0, :] * (o_att @ w_branch_att)
              + gates[..., 1, :] * (o_ssm @ w_branch_ssm)
              + gates[..., 2, :] * (o_rwkv @ w_branch_rwkv))
    x = x + merged @ w_out
    h2 = rmsnorm(x, ln2_g)
    x = x + jnp.square(jax.nn.relu(h2 @ w_up)) @ w_down
    return x, att_new + (conv_new, ssm_new, shift_new, rwkv_new)


def stack_states(per_layer, like):
    return tuple(jnp.stack([s[i] for s in per_layer]).astype(like[i].dtype) for i in range(len(like)))


def setup_inputs(seed: int = 0) -> dict:
    key = jax.random.key(seed)
    keys = iter(jax.random.split(key, 64))

    def nrm(shape, scale):
        return scale * jax.random.normal(next(keys), shape, jnp.float32)

    def uni(shape, lo, hi):
        return jax.random.uniform(next(keys), shape, jnp.float32, lo, hi)

    L, D, BW = DEPTH, D_MODEL, BRANCH_WIDTH
    dt0 = jnp.exp(uni((L, SSM_HEADS), math.log(1e-3), math.log(1e-1)))
    return {
        'x_prompt': nrm((BATCH, SEQ, D), 1.0),
        'x_sample': nrm((DEC_BATCH, DEC_SEQ, D), 1.0),
        'cache_att_k0': nrm((L, DEC_BATCH, ATT_WINDOWS[0], ATT_HEADS, HEAD_DIM), 1.0),
        'cache_att_v0': nrm((L, DEC_BATCH, ATT_WINDOWS[0], ATT_HEADS, HEAD_DIM), 1.0),
        'cache_att_k1': nrm((L, DEC_BATCH, ATT_WINDOWS[1], ATT_HEADS, HEAD_DIM), 1.0),
        'cache_att_v1': nrm((L, DEC_BATCH, ATT_WINDOWS[1], ATT_HEADS, HEAD_DIM), 1.0),
        'cache_att_k2': nrm((L, DEC_BATCH, ATT_WINDOWS[2], ATT_HEADS, HEAD_DIM), 1.0),
        'cache_att_v2': nrm((L, DEC_BATCH, ATT_WINDOWS[2], ATT_HEADS, HEAD_DIM), 1.0),
        'state_ssm_conv': nrm((L, DEC_BATCH, SSM_CONV - 1, SSM_CONV_DIM), 1.0),
        'state_ssm': nrm((L, DEC_BATCH, SSM_HEADS, HEAD_DIM, SSM_STATE), 0.3),
        'state_rwkv_shift': nrm((L, DEC_BATCH, RWKV_SHIFT_DIM), 1.0),
        'state_rwkv': nrm((L, DEC_BATCH, RWKV_HEADS, HEAD_DIM, HEAD_DIM), 0.3),
        'ln1_g': 1.0 + nrm((L, D), 0.02),
        'w_in': nrm((L, D, IN_DIM), D ** -0.5),
        'ssm_conv_w': nrm((L, SSM_CONV, SSM_CONV_DIM), SSM_CONV ** -0.5),
        'ssm_conv_b': nrm((L, SSM_CONV_DIM), 0.02),
        'ssm_dt_bias': dt0 + jnp.log(-jnp.expm1(-dt0)),
        'ssm_a_log': jnp.log(uni((L, SSM_HEADS), 1.0, 16.0)),
        'ssm_d': 1.0 + nrm((L, SSM_HEADS), 0.1),
        'ssm_norm_g': 1.0 + nrm((L, BW), 0.02),
        'rwkv_mu': uni((L, RWKV_SHIFT_DIM), 0.0, 1.0),
        'rwkv_w0': uni((L, BW), -6.0, -1.0),
        'rwkv_w_up': nrm((L, DECAY_LORA, BW), 0.5 * DECAY_LORA ** -0.5),
        'rwkv_a0': nrm((L, BW), 0.1),
        'rwkv_a_up': nrm((L, AAA_LORA, BW), 0.5 * AAA_LORA ** -0.5),
        'rwkv_g_up': nrm((L, GATE_LORA, BW), GATE_LORA ** -0.5),
        'rwkv_k_k': 0.85 + nrm((L, BW), 0.05),
        'rwkv_k_a': 1.0 + nrm((L, BW), 0.05),
        'rwkv_r_k': nrm((L, RWKV_HEADS, HEAD_DIM), 0.1),
        'rwkv_ln_g': 1.0 + nrm((L, BW), 0.02),
        'rwkv_ln_b': nrm((L, BW), 0.02),
        'w_branch_att': nrm((L, ATT_OUT, D), ATT_OUT ** -0.5),
        'w_branch_ssm': nrm((L, BW, D), BW ** -0.5),
        'w_branch_rwkv': nrm((L, BW, D), BW ** -0.5),
        'w_out': nrm((L, D, D), D ** -0.5),
        'ln2_g': 1.0 + nrm((L, D), 0.02),
        'w_up': nrm((L, D, D_FF), D ** -0.5),
        'w_down': nrm((L, D_FF, D), D_FF ** -0.5),
        'final_g': 1.0 + nrm((D,), 0.02),
    }


def reference(x_prompt, x_sample, cache_att_k0, cache_att_v0, cache_att_k1, cache_att_v1, cache_att_k2, cache_att_v2,
              state_ssm_conv, state_ssm, state_rwkv_shift, state_rwkv, ln1_g, w_in, ssm_conv_w, ssm_conv_b,
              ssm_dt_bias, ssm_a_log, ssm_d, ssm_norm_g, rwkv_mu, rwkv_w0, rwkv_w_up, rwkv_a0, rwkv_a_up, rwkv_g_up,
              rwkv_k_k, rwkv_k_a, rwkv_r_k, rwkv_ln_g, rwkv_ln_b, w_branch_att, w_branch_ssm, w_branch_rwkv, w_out,
              ln2_g, w_up, w_down, final_g):
    layer_w = (ln1_g, w_in, ssm_conv_w, ssm_conv_b, ssm_dt_bias, ssm_a_log, ssm_d, ssm_norm_g, rwkv_mu, rwkv_w0,
               rwkv_w_up, rwkv_a0, rwkv_a_up, rwkv_g_up, rwkv_k_k, rwkv_k_a, rwkv_r_k, rwkv_ln_g, rwkv_ln_b,
               w_branch_att, w_branch_ssm, w_branch_rwkv, w_out, ln2_g, w_up, w_down)
    sample_st = (cache_att_k0, cache_att_v0, cache_att_k1, cache_att_v1, cache_att_k2, cache_att_v2,
                 state_ssm_conv, state_ssm, state_rwkv_shift, state_rwkv)
    xp, xs = x_prompt, x_sample
    p_new, s_new = [], []
    for l in range(DEPTH):
        lw = [w[l] for w in layer_w]
        xp, st_p = trunk_layer(xp, None, *lw)
        xs, st_s = trunk_layer(xs, tuple(c[l] for c in sample_st), *lw)
        p_new.append(st_p)
        s_new.append(st_s)
    y_prompt = rmsnorm(xp, final_g)
    y_sample = rmsnorm(xs, final_g)
    (p_k0, p_v0, p_k1, p_v1, p_k2, p_v2, p_conv, p_ssm, p_shift, p_rwkv) = stack_states(p_new, sample_st)
    (s_k0, s_v0, s_k1, s_v1, s_k2, s_v2, s_conv, s_ssm, s_shift, s_rwkv) = stack_states(s_new, sample_st)
    return (y_prompt, y_sample,
            p_k0, p_v0, p_k1, p_v1, p_k2, p_v2, p_conv, p_ssm, p_shift, p_rwkv,
            s_k0, s_v0, s_k1, s_v1, s_k2, s_v2, s_conv, s_ssm, s_shift, s_rwkv)
```

```python
import functools

import jax
import jax.numpy as jnp
from jax import lax
from jax.experimental import pallas as pl
from jax.experimental.pallas import tpu as pltpu

F32 = jnp.float32
BF16 = jnp.bfloat16

D_MODEL = 2048
HEAD_DIM = 64
BRANCH_W = 1536
NORM_EPS = 1e-5
ATT_WINDOWS = (128, 512, 2048)
ATT_DILS = (1, 4, 16)
ATT_SPAN = 128
ATT_HEADS = 8
ATT_OUT = ATT_HEADS * HEAD_DIM
ATT_SCALE = HEAD_DIM ** -0.5
SSM_HEADS = 24
SSM_GROUPS = 4
SSM_STATE = 128
SSM_CONV = 4
SSM_CHUNK = 128
RWKV_HEADS = 24
LORA_W = 96
LORA_A = 96
LORA_G = 256
RWKV_LN_EPS = 64e-5
D_FF = 4 * D_MODEL
RWKV_CHUNK = 64
N_PAIRS = BRANCH_W // 128

LANES = 128
SUBLANES = 8
VMEM_LIMIT_BYTES = 56 * 1024 * 1024

C_Q, C_K, C_V, C_Z, C_X, C_R, C_RK, C_RV = (i * BRANCH_W for i in range(8))
C_G = 8 * BRANCH_W
C_B = C_G + 3 * D_MODEL
C_C = C_B + 512
C_L = C_C + 512
N_PROJ = C_L + 512
DT_LANE0 = LORA_W

_O_QKV, _O_Z, _O_XBC, _O_DT, _O_RW, _O_GATE = 0, 4608, 6144, 8704, 8728, 13784


def _cparams(sem):
    return pltpu.CompilerParams(dimension_semantics=sem, vmem_limit_bytes=VMEM_LIMIT_BYTES)


def _softplus(x):
    return jnp.maximum(x, 0.0) + jnp.log1p(jnp.exp(-jnp.abs(x)))


def _sigmoid(x):
    return 1.0 / (1.0 + jnp.exp(-x))


def _silu(x):
    return x * _sigmoid(x)


def _split3(x):
    hi = x.astype(BF16)
    r1 = x - hi.astype(F32)
    mid = r1.astype(BF16)
    lo = (r1 - mid.astype(F32)).astype(BF16)
    return hi, mid, lo


def _dot_exact_rhs(x, mat):
    hi, mid, lo = _split3(x)
    f = lambda t: jnp.dot(t, mat, preferred_element_type=F32)
    return f(hi) + f(mid) + f(lo)


def _dot_exact_lhs(mat, x):
    hi, mid, lo = _split3(x)
    f = lambda t: jnp.dot(mat, t, preferred_element_type=F32)
    return f(hi) + f(mid) + f(lo)


def _dot_nt(a, b):
    return lax.dot_general(a, b, (((1,), (1,)), ((), ())), preferred_element_type=F32)


def _dot_tn(a, b):
    return lax.dot_general(a, b, (((0,), (0,)), ((), ())), preferred_element_type=F32)


def _rmsnorm_kernel(x_ref, g_ref, o_ref):
    x = x_ref[...]
    y = x * lax.rsqrt(jnp.mean(x * x, axis=-1, keepdims=True) + NORM_EPS)
    o_ref[...] = (y * g_ref[...]).astype(o_ref.dtype)


def rmsnorm(x, g, out_dtype):
    m, d = x.shape
    tm = min(m, 512)
    return pl.pallas_call(
        _rmsnorm_kernel,
        grid=(m // tm,),
        in_specs=[pl.BlockSpec((tm, d), lambda i: (i, 0)), pl.BlockSpec((1, d), lambda i: (0, 0))],
        out_specs=pl.BlockSpec((tm, d), lambda i: (i, 0)),
        out_shape=jax.ShapeDtypeStruct((m, d), out_dtype),
        compiler_params=_cparams(("parallel",)),
        name="rmsnorm",
    )(x, g.reshape(1, d))


def _mm_kernel(*refs, nk, epi):
    if epi == "res":
        x_ref, w_ref, r_ref, o_ref = refs[:4]
    else:
        x_ref, w_ref, o_ref = refs[:3]
        r_ref = None

    def finish(a):
        if epi == "res":
            a = r_ref[...] + a
        elif epi == "relu2":
            a = jnp.square(jnp.maximum(a, 0.0))
        o_ref[...] = a.astype(o_ref.dtype)

    part = jnp.dot(x_ref[...], w_ref[...], preferred_element_type=F32)
    if nk == 1:
        finish(part)
        return
    acc_ref = refs[-1]
    k = pl.program_id(2)

    @pl.when(k == 0)
    def _():
        acc_ref[...] = part

    @pl.when(k > 0)
    def _():
        acc_ref[...] += part

    @pl.when(k == nk - 1)
    def _():
        finish(acc_ref[...])


def _pick(n, cands):
    for c in cands:
        if n % c == 0:
            return c
    return n


def matmul(x, w, *, epi="none", res=None, out_dtype=F32):
    m, kdim = x.shape
    n = w.shape[1]
    tm = min(m, 1024)
    tn = _pick(n, (1536, 1024, 512, 256, 128)) if m > 64 else _pick(n, (2048, 1536, 1024, 512, 256, 128))
    tk = min(kdim, 2048)
    nk = kdim // tk
    in_specs = [pl.BlockSpec((tm, tk), lambda i, j, k: (i, k)), pl.BlockSpec((tk, tn), lambda i, j, k: (k, j))]
    args = [x, w]
    if epi == "res":
        in_specs.append(pl.BlockSpec((tm, tn), lambda i, j, k: (i, j)))
        args.append(res)
    scratch = [pltpu.VMEM((tm, tn), F32)] if nk > 1 else []
    return pl.pallas_call(
        functools.partial(_mm_kernel, nk=nk, epi=epi),
        grid=(m // tm, n // tn, nk),
        in_specs=in_specs,
        out_specs=pl.BlockSpec((tm, tn), lambda i, j, k: (i, j)),
        out_shape=jax.ShapeDtypeStruct((m, n), out_dtype),
        scratch_shapes=scratch,
        compiler_params=_cparams(("parallel", "parallel", "arbitrary")),
        name="matmul_" + epi,
    )(*args)


def _merge_kernel(oa_ref, os_ref, or_ref, wa_ref, ws_ref, wr_ref, ga_ref, gs_ref, gr_ref, o_ref):
    f = lambda a, b: jnp.dot(a[...], b[...], preferred_element_type=F32)
    acc = _sigmoid(ga_ref[...]) * f(oa_ref, wa_ref)
    acc = acc + _sigmoid(gs_ref[...]) * f(os_ref, ws_ref)
    acc = acc + _sigmoid(gr_ref[...]) * f(or_ref, wr_ref)
    o_ref[...] = acc.astype(o_ref.dtype)


def merge_branches(o_att, o_ssm, o_rwkv, wa, ws, wr, proj):
    m = o_att.shape[0]
    tm = min(m, 512)
    tn = 1024
    gb = C_G // tn
    nj = D_MODEL // tn
    row = lambda width: pl.BlockSpec((tm, width), lambda i, j: (i, 0))
    col = lambda k: pl.BlockSpec((k, tn), lambda i, j: (0, j))
    gate = lambda g: pl.BlockSpec((tm, tn), lambda i, j, g=g: (i, gb + g * nj + j))
    return pl.pallas_call(
        _merge_kernel,
        grid=(m // tm, nj),
        in_specs=[row(ATT_OUT), row(BRANCH_W), row(BRANCH_W), col(ATT_OUT), col(BRANCH_W), col(BRANCH_W),
                  gate(0), gate(1), gate(2)],
        out_specs=pl.BlockSpec((tm, tn), lambda i, j: (i, j)),
        out_shape=jax.ShapeDtypeStruct((m, D_MODEL), BF16),
        compiler_params=_cparams(("parallel", "parallel")),
        name="merge_branches",
    )(o_att, o_ssm, o_rwkv, wa, ws, wr, proj, proj, proj)


def _att_prompt_kernel(sl_ref, q_ref, kc_ref, kp_ref, vc_ref, vp_ref, o_ref, lse_ref, *, d, nblk, g):
    j = pl.program_id(1)
    hp = pl.program_id(2)
    lane = lax.broadcasted_iota(jnp.int32, (1, LANES), 1)
    head0 = lane < HEAD_DIM
    qi = lax.broadcasted_iota(jnp.int32, (ATT_SPAN, 2 * ATT_SPAN), 0)
    kj = lax.broadcasted_iota(jnp.int32, (ATT_SPAN, 2 * ATT_SPAN), 1)
    delta = qi - kj + ATT_SPAN
    band = (delta >= 0) & (delta <= ATT_SPAN)
    first_ok = band & ((kj >= ATT_SPAN) | (j > 0))
    dist = (delta * d).astype(F32)
    slopes = (sl_ref[g, 2 * hp], sl_ref[g, 2 * hp + 1])

    def rows(base):
        return pl.ds(base, ATT_SPAN) if d == 1 else pl.ds(base, ATT_SPAN, stride=d)

    for ub in range(nblk):
        valid = first_ok if ub == 0 else band
        for r in range(d):
            base = ub * ATT_SPAN * d + r
            q = q_ref[rows(base), :]
            if ub == 0:
                kp, vp = kp_ref[rows(r), :], vp_ref[rows(r), :]
            else:
                kp, vp = kc_ref[rows(base - ATT_SPAN * d), :], vc_ref[rows(base - ATT_SPAN * d), :]
            k = jnp.concatenate([kp, kc_ref[rows(base), :]], axis=0).astype(BF16)
            v = jnp.concatenate([vp, vc_ref[rows(base), :]], axis=0).astype(BF16)
            outs, lses = [], []
            for e in range(2):
                sel = head0 if e == 0 else jnp.logical_not(head0)
                qe = jnp.where(sel, q, 0.0).astype(BF16)
                s = _dot_nt(qe, k) * ATT_SCALE
                s = jnp.where(valid, s - slopes[e] * dist, -jnp.inf)
                mx = jnp.max(s, axis=-1, keepdims=True)
                p = jnp.exp(s - mx)
                den = jnp.sum(p, axis=-1, keepdims=True)
                outs.append(jnp.dot(p.astype(BF16), v, preferred_element_type=F32) / den)
                lses.append(mx + jnp.log(den))
            o_ref[rows(base), :] = jnp.where(head0, outs[0], outs[1])
            lse_ref[rows(base), :] = jnp.where(head0, lses[0], lses[1])


def att_prompt_group(proj, slopes, g, bsz, t):
    d = ATT_DILS[g]
    sb = ATT_SPAN * d
    tb = max(sb, 512)
    nblk = tb // sb
    nt = t // tb
    cq, ck, cv = (C_Q + g * ATT_OUT) // LANES, (C_K + g * ATT_OUT) // LANES, (C_V + g * ATT_OUT) // LANES
    cur = lambda c0: pl.BlockSpec((tb, LANES), lambda b, j, h: (b * nt + j, c0 + h))
    prev = lambda c0: pl.BlockSpec((sb, LANES), lambda b, j, h: (jnp.maximum((b * nt + j) * nblk - 1, 0), c0 + h))
    out = pl.BlockSpec((tb, LANES), lambda b, j, h: (b * nt + j, h))
    shp = jax.ShapeDtypeStruct((bsz * t, ATT_OUT), F32)
    return pl.pallas_call(
        functools.partial(_att_prompt_kernel, d=d, nblk=nblk, g=g),
        grid=(bsz, nt, ATT_HEADS // 2),
        in_specs=[pl.BlockSpec(memory_space=pltpu.SMEM), cur(cq), cur(ck), prev(ck), cur(cv), prev(cv)],
        out_specs=[out, out],
        out_shape=[shp, shp],
        compiler_params=_cparams(("parallel", "parallel", "parallel")),
        name=f"att_prompt_g{g}",
    )(slopes, proj, proj, proj, proj, proj)


def _att_combine_kernel(o0, l0, o1, l1, o2, l2, out_ref):
    a, b, c = l0[...], l1[...], l2[...]
    mx = jnp.maximum(jnp.maximum(a, b), c)
    ea, eb, ec = jnp.exp(a - mx), jnp.exp(b - mx), jnp.exp(c - mx)
    num = ea * o0[...] + eb * o1[...] + ec * o2[...]
    out_ref[...] = (num / (ea + eb + ec)).astype(out_ref.dtype)


def att_combine(parts):
    m = parts[0].shape[0]
    tm = min(m, 1024)
    spec = pl.BlockSpec((tm, ATT_OUT), lambda i: (i, 0))
    return pl.pallas_call(
        _att_combine_kernel,
        grid=(m // tm,),
        in_specs=[spec] * 6,
        out_specs=spec,
        out_shape=jax.ShapeDtypeStruct((m, ATT_OUT), BF16),
        compiler_params=_cparams(("parallel",)),
        name="att_combine",
    )(*parts)


def _att_sample_kernel(sl_ref, qkv_ref, k0, v0, k1, v1, k2, v2, o_ref):
    caches = ((k0, v0), (k1, v1), (k2, v2))
    idx = lax.broadcasted_iota(jnp.int32, (ATT_SPAN, ATT_HEADS, 1), 0)
    outs, lses = [], []
    for g in range(3):
        d = ATT_DILS[g]
        q = qkv_ref[0, g]
        kn = qkv_ref[1, g]
        vn = qkv_ref[2, g]
        kc = caches[g][0][...]
        vc = caches[g][1][...]
        sl = sl_ref[g]
        dist = ((ATT_SPAN - idx) * d).astype(F32)
        s = jnp.sum(kc * q[None], axis=-1, keepdims=True) * ATT_SCALE - sl[None] * dist
        s_new = jnp.sum(kn * q, axis=-1, keepdims=True) * ATT_SCALE
        mx = jnp.maximum(jnp.max(s, axis=0), s_new)
        p = jnp.exp(s - mx[None])
        p_new = jnp.exp(s_new - mx)
        den = jnp.sum(p, axis=0) + p_new
        outs.append((jnp.sum(p * vc, axis=0) + p_new * vn) / den)
        lses.append(mx + jnp.log(den))
    mx = jnp.maximum(jnp.maximum(lses[0], lses[1]), lses[2])
    es = [jnp.exp(l - mx) for l in lses]
    o_ref[...] = (es[0] * outs[0] + es[1] * outs[1] + es[2] * outs[2]) / (es[0] + es[1] + es[2])


def att_sample(qkv, caches, slopes, layer):
    bd = qkv.shape[0]
    in_specs = [pl.BlockSpec((3, ATT_HEADS, 1), lambda b: (0, 0, 0)),
                pl.BlockSpec((None, 3, 3, ATT_HEADS, HEAD_DIM), lambda b: (b, 0, 0, 0, 0))]
    args = [slopes.reshape(3, ATT_HEADS, 1), qkv]
    for g in range(3):
        d = ATT_DILS[g]
        for c in caches[2 * g:2 * g + 2]:
            args.append(c.reshape(c.shape[0], bd, ATT_SPAN, d, ATT_HEADS, HEAD_DIM))
            in_specs.append(pl.BlockSpec((None, None, ATT_SPAN, None, ATT_HEADS, HEAD_DIM),
                                         lambda b: (layer, b, 0, 0, 0, 0)))
    return pl.pallas_call(
        _att_sample_kernel,
        grid=(bd,),
        in_specs=in_specs,
        out_specs=pl.BlockSpec((None, ATT_HEADS, HEAD_DIM), lambda b: (b, 0, 0)),
        out_shape=jax.ShapeDtypeStruct((bd, ATT_HEADS, HEAD_DIM), F32),
        compiler_params=_cparams(("parallel",)),
        name="att_sample",
    )(*args)


def _ssd_prompt_kernel(x_ref, b_ref, c_ref, z_ref, l_ref, cwx_ref, cwb_ref, cwc_ref, cbx_ref, cbb_ref, cbc_ref,
                       dtb_ref, aneg_ref, dskip_ref, ng_ref, o_ref, hout_ref, xpad, bpad, cpad, h_scr, y_scr, *, nc):
    c = pl.program_id(1)
    L = SSM_CHUNK

    @pl.when(c == 0)
    def _():
        xpad[0:SUBLANES, :] = jnp.zeros((SUBLANES, xpad.shape[1]), F32)
        bpad[0:SUBLANES, :] = jnp.zeros((SUBLANES, bpad.shape[1]), F32)
        cpad[0:SUBLANES, :] = jnp.zeros((SUBLANES, cpad.shape[1]), F32)
        h_scr[...] = jnp.zeros_like(h_scr)

    def conv_silu(pad, src_ref, w_ref, bias_ref):
        pad[SUBLANES:SUBLANES + L, :] = src_ref[...]
        acc = bias_ref[...] + w_ref[0:1, :] * pad[pl.ds(SUBLANES - 3, L), :]
        for jj in range(1, SSM_CONV):
            acc = acc + w_ref[jj:jj + 1, :] * pad[pl.ds(SUBLANES - 3 + jj, L), :]
        pad[0:SUBLANES, :] = pad[L:L + SUBLANES, :]
        return _silu(acc)

    xs = conv_silu(xpad, x_ref, cwx_ref, cbx_ref)
    bm = conv_silu(bpad, b_ref, cwb_ref, cbb_ref).astype(BF16)
    cm = conv_silu(cpad, c_ref, cwc_ref, cbc_ref).astype(BF16)

    lane = lax.broadcasted_iota(jnp.int32, (1, LANES), 1)
    head0 = lane < HEAD_DIM
    sub_head0 = lax.broadcasted_iota(jnp.int32, (LANES, 1), 0) < HEAD_DIM
    dt_lanes = (lane >= DT_LANE0) & (lane < DT_LANE0 + SSM_HEADS)
    dtv = jnp.where(dt_lanes, _softplus(l_ref[:, 0:LANES] + dtb_ref[...]), 0.0)
    da = dtv * aneg_ref[...]
    ri = lax.broadcasted_iota(jnp.int32, (L, L), 0)
    ci = lax.broadcasted_iota(jnp.int32, (L, L), 1)
    causal = ri >= ci
    a_cum = _dot_exact_lhs(causal.astype(BF16), da)
    a_cum_t = a_cum.T
    dt_t = dtv.T

    for g in range(SSM_GROUPS):
        bg = bm[:, g * SSM_STATE:(g + 1) * SSM_STATE]
        cg = cm[:, g * SSM_STATE:(g + 1) * SSM_STATE]
        cb = _dot_nt(cg, bg)
        for pp in range(3):
            pair = g * 3 + pp
            xp = xs[:, pair * LANES:(pair + 1) * LANES]
            xpb = xp.astype(BF16)
            yd, sc, ea, cd = [], [], [], []
            for e in range(2):
                hl = DT_LANE0 + 2 * pair + e
                ac_col = a_cum[:, hl:hl + 1]
                ac_row = a_cum_t[hl:hl + 1, :]
                a_last = ac_col[L - 1:L, :]
                dec = jnp.exp(jnp.where(causal, ac_col - ac_row, -jnp.inf))
                wm = (cb * dec * dt_t[hl:hl + 1, :]).astype(BF16)
                yd.append(jnp.dot(wm, xpb, preferred_element_type=F32))
                sc.append(jnp.exp(a_last - ac_col) * dtv[:, hl:hl + 1])
                ea.append(jnp.exp(ac_col))
                cd.append(jnp.exp(a_last))
            hp = h_scr[pair]
            y_off = _dot_nt(cg, hp.astype(BF16)) * jnp.where(head0, ea[0], ea[1])
            xw = (xp * jnp.where(head0, sc[0], sc[1])).astype(BF16)
            h_scr[pair] = jnp.where(sub_head0, cd[0], cd[1]) * hp + _dot_tn(xw, bg)
            y_scr[:, pair * LANES:(pair + 1) * LANES] = jnp.where(head0, yd[0], yd[1]) + y_off

    y = (y_scr[...] + dskip_ref[...] * xs) * _silu(z_ref[...])
    gw = BRANCH_W // SSM_GROUPS
    for g in range(SSM_GROUPS):
        yg = y[:, g * gw:(g + 1) * gw]
        yg = yg * lax.rsqrt(jnp.mean(yg * yg, axis=-1, keepdims=True) + NORM_EPS)
        o_ref[:, g * gw:(g + 1) * gw] = (yg * ng_ref[:, g * gw:(g + 1) * gw]).astype(o_ref.dtype)

    @pl.when(c == nc - 1)
    def _():
        hout_ref[...] = h_scr[...]


def ssd_prompt(proj, lw, bsz, t):
    nc = t // SSM_CHUNK
    L = SSM_CHUNK
    rowblk = lambda width, c0: pl.BlockSpec((L, width), lambda b, c: (b * nc + c, c0 // width))
    par = lambda a: pl.BlockSpec(a.shape, lambda b, c: (0,) * a.ndim)
    params = [lw["cw_x"], lw["cw_b"], lw["cw_c"], lw["cb_x"], lw["cb_b"], lw["cb_c"], lw["dt_bias"], lw["a_neg"],
              lw["d_skip"], lw["ssm_norm_g"]]
    return pl.pallas_call(
        functools.partial(_ssd_prompt_kernel, nc=nc),
        grid=(bsz, nc),
        in_specs=[rowblk(BRANCH_W, C_X), rowblk(512, C_B), rowblk(512, C_C), rowblk(BRANCH_W, C_Z), rowblk(512, C_L)]
        + [par(a) for a in params],
        out_specs=[pl.BlockSpec((L, BRANCH_W), lambda b, c: (b * nc + c, 0)),
                   pl.BlockSpec((None, N_PAIRS, LANES, SSM_STATE), lambda b, c: (b, 0, 0, 0))],
        out_shape=[jax.ShapeDtypeStruct((bsz * t, BRANCH_W), BF16),
                   jax.ShapeDtypeStruct((bsz, N_PAIRS, LANES, SSM_STATE), F32)],
        scratch_shapes=[pltpu.VMEM((L + SUBLANES, BRANCH_W), F32), pltpu.VMEM((L + SUBLANES, 512), F32),
                        pltpu.VMEM((L + SUBLANES, 512), F32), pltpu.VMEM((N_PAIRS, LANES, SSM_STATE), F32),
                        pltpu.VMEM((L, BRANCH_W), F32)],
        compiler_params=_cparams(("parallel", "arbitrary")),
        name="ssd_prompt",
    )(proj, proj, proj, proj, proj, *params)


def _ssd_sample_conv_kernel(new_ref, st_ref, w_ref, b_ref, o_ref):
    acc = b_ref[...] + w_ref[SSM_CONV - 1:SSM_CONV, :] * new_ref[...]
    for jj in range(SSM_CONV - 1):
        acc = acc + w_ref[jj:jj + 1, :] * st_ref[jj]
    o_ref[...] = _silu(acc)


def ssd_sample_conv(xbc_new, conv_st_t, w, b):
    bd, cdim = xbc_new.shape
    return pl.pallas_call(
        _ssd_sample_conv_kernel,
        out_shape=jax.ShapeDtypeStruct((bd, cdim), F32),
        name="ssd_sample_conv",
    )(xbc_new, conv_st_t, w, b.reshape(1, cdim))


def _ssd_sample_step_kernel(h_ref, x_ref, z_ref, b_ref, c_ref, dt_ref, dtb_ref, alog_ref, d_ref, ng_ref, o_ref, hout_ref):
    h = h_ref[...]
    dt = _softplus(dt_ref[...] + dtb_ref[...])
    da = jnp.exp(dt * (-jnp.exp(alog_ref[...])))
    rep = SSM_HEADS // SSM_GROUPS
    bh = jnp.broadcast_to(b_ref[...][:, None], (SSM_GROUPS, rep, 1, SSM_STATE)).reshape(SSM_HEADS, 1, SSM_STATE)
    ch = jnp.broadcast_to(c_ref[...][:, None], (SSM_GROUPS, rep, 1, SSM_STATE)).reshape(SSM_HEADS, 1, SSM_STATE)
    x = x_ref[...]
    hn = da * h + (dt * x) * bh
    hout_ref[...] = hn
    y = jnp.sum(hn * ch, axis=-1, keepdims=True)
    y = (y + d_ref[...] * x) * _silu(z_ref[...])
    y4 = y.reshape(SSM_GROUPS, rep, HEAD_DIM, 1)
    ms = jnp.sum(jnp.sum(y4 * y4, axis=2, keepdims=True), axis=1, keepdims=True) / (rep * HEAD_DIM)
    y4 = y4 * lax.rsqrt(ms + NORM_EPS)
    o_ref[...] = y4.reshape(SSM_HEADS, HEAD_DIM, 1) * ng_ref[...]


def ssd_sample_step(h, x_col, z_col, bmat, cmat, dt_raw, lw, layer):
    bd = x_col.shape[0]
    col = pl.BlockSpec((None, SSM_HEADS, HEAD_DIM, 1), lambda b: (b, 0, 0, 0))
    grp = pl.BlockSpec((None, SSM_GROUPS, 1, SSM_STATE), lambda b: (b, 0, 0, 0))
    hd1 = pl.BlockSpec((SSM_HEADS, 1, 1), lambda b: (0, 0, 0))
    return pl.pallas_call(
        _ssd_sample_step_kernel,
        grid=(bd,),
        in_specs=[pl.BlockSpec((None, None, SSM_HEADS, HEAD_DIM, SSM_STATE), lambda b: (layer, b, 0, 0, 0)),
                  col, col, grp, grp, pl.BlockSpec((None, SSM_HEADS, 1, 1), lambda b: (b, 0, 0, 0)), hd1, hd1, hd1,
                  pl.BlockSpec((SSM_HEADS, HEAD_DIM, 1), lambda b: (0, 0, 0))],
        out_specs=[col, pl.BlockSpec((None, SSM_HEADS, HEAD_DIM, SSM_STATE), lambda b: (b, 0, 0, 0))],
        out_shape=[jax.ShapeDtypeStruct((bd, SSM_HEADS, HEAD_DIM, 1), F32),
                   jax.ShapeDtypeStruct((bd, SSM_HEADS, HEAD_DIM, SSM_STATE), F32)],
        compiler_params=_cparams(("parallel",)),
        name="ssd_sample_step",
    )(h, x_col, z_col, bmat, cmat, dt_raw, lw["dt_bias_h"], lw["a_log_h"], lw["d_h"], lw["ssm_norm_g_col"])


def _rwkv_prep_kernel(r_ref, k_ref, v_ref, l_ref, hr_ref, hk_ref, hv_ref, hl_ref, ir_ref, ik_ref, iv_ref, il_ref,
                      mur_ref, muk_ref, muv_ref, mul_ref, w0_ref, wup_ref, a0_ref, aup_ref, gup_ref, kk_ref, ka_ref,
                      or_ref, old_ref, ok_ref, ov_ref, okk_ref, oa_ref, og_ref, *, tm):
    i = pl.program_id(1)

    def shift(x_ref, h_ref, i_ref, mu_ref):
        x = x_ref[...]
        hrows = h_ref.shape[0]
        prev0 = jnp.where(i == 0, i_ref[...], h_ref[hrows - 1:hrows, :])
        if tm == 1:
            xp = prev0
        else:
            row = lax.broadcasted_iota(jnp.int32, (tm, 1), 0)
            xp = jnp.where(row == 0, prev0, pltpu.roll(x, 1, 0))
        return x + mu_ref[...] * (xp - x)

    ur = shift(r_ref, hr_ref, ir_ref, mur_ref)
    uk = shift(k_ref, hk_ref, ik_ref, muk_ref)
    uv = shift(v_ref, hv_ref, iv_ref, muv_ref)
    ul = shift(l_ref, hl_ref, il_ref, mul_ref)
    f = lambda a, w_ref: jnp.dot(a.astype(BF16), w_ref[...], preferred_element_type=F32)
    w_log = -_softplus(-(w0_ref[...] + f(jnp.tanh(ul[:, 0:LANES]), wup_ref))) - 0.5
    a = _sigmoid(a0_ref[...] + f(ul[:, LANES:2 * LANES], aup_ref))
    or_ref[...] = ur
    old_ref[...] = -jnp.exp(w_log)
    ok_ref[...] = uk * (1.0 + (a - 1.0) * ka_ref[...])
    ov_ref[...] = uv
    okk_ref[...] = uk * kk_ref[...]
    oa_ref[...] = a
    og_ref[...] = f(_sigmoid(ul[:, 2 * LANES:4 * LANES]), gup_ref)


def rwkv_prep(proj, init, lw, bsz, t):
    tm = min(t, 128)
    nt = t // tm
    hrows = SUBLANES if t >= SUBLANES else t
    hper = tm // hrows
    proj = proj.reshape(bsz, t, N_PROJ)
    blk = lambda width, c0: pl.BlockSpec((None, tm, width), lambda b, i: (b, i, c0 // width))
    halo = lambda width, c0: pl.BlockSpec(
        (None, hrows, width), lambda b, i: (b, jnp.maximum(i * hper - 1, 0), c0 // width))
    ini = lambda width: pl.BlockSpec((None, 1, width), lambda b, i: (b, 0, 0))
    par = lambda a: pl.BlockSpec(a.shape, lambda b, i: (0,) * a.ndim)
    secs = ((BRANCH_W, C_R), (BRANCH_W, C_RK), (BRANCH_W, C_RV), (512, C_L))
    params = [lw["mu_r"], lw["mu_k"], lw["mu_v"], lw["mu_l"], lw["w0"], lw["w_up"], lw["a0"], lw["a_up"], lw["g_up"],
              lw["k_k"], lw["k_a"]]
    out = pl.BlockSpec((None, tm, BRANCH_W), lambda b, i: (b, i, 0))
    shp = jax.ShapeDtypeStruct((bsz, t, BRANCH_W), F32)
    outs = pl.pallas_call(
        functools.partial(_rwkv_prep_kernel, tm=tm),
        grid=(bsz, nt),
        in_specs=[blk(*s) for s in secs] + [halo(*s) for s in secs] + [ini(s[0]) for s in secs]
        + [par(a) for a in params],
        out_specs=[out] * 7,
        out_shape=[shp] * 7,
        compiler_params=_cparams(("parallel", "arbitrary")),
        name="rwkv_prep",
    )(proj, proj, proj, proj, proj, proj, proj, proj, *init, *params)
    return [o.reshape(bsz * t, BRANCH_W) for o in outs]


def _rwkv_chunk_kernel(r_ref, ld_ref, k_ref, v_ref, kk_ref, a_ref, g_ref, rk_ref, lng_ref, lnb_ref, o_ref, sout_ref,
                       s_scr, *, nchunk, nt):
    tstep = pl.program_id(2)
    L = RWKV_CHUNK
    L2 = 2 * L

    @pl.when(tstep == 0)
    def _():
        s_scr[...] = jnp.zeros_like(s_scr)

    lane = lax.broadcasted_iota(jnp.int32, (1, LANES), 1)
    head0 = lane < HEAD_DIM
    ri = lax.broadcasted_iota(jnp.int32, (L2, L2), 0)
    ci = lax.broadcasted_iota(jnp.int32, (L2, L2), 1)
    same = (ri < L) == (ci < L)
    strict = same & (ri > ci)
    incl = same & (ri >= ci)
    eye = (ri == ci).astype(F32)
    bd_ones = same.astype(BF16)
    tril = (lax.broadcasted_iota(jnp.int32, (L, L), 0) >= lax.broadcasted_iota(jnp.int32, (L, L), 1)).astype(BF16)
    bf = lambda x: x.astype(BF16)
    dot = lambda a, b: jnp.dot(bf(a), bf(b), preferred_element_type=F32)

    def stack(x):
        return jnp.concatenate([jnp.where(head0, x, 0.0), jnp.where(head0, 0.0, x)], axis=0)

    for c in range(nchunk):
        sl = pl.ds(c * L, L)
        r, ld, kmod, v, kkraw, a = r_ref[sl, :], ld_ref[sl, :], k_ref[sl, :], v_ref[sl, :], kk_ref[sl, :], a_ref[sl, :]
        ss = _dot_exact_rhs(kkraw * kkraw, bd_ones)
        kkn = kkraw / jnp.maximum(jnp.sqrt(ss), 1e-12)
        cum = _dot_exact_lhs(tril, ld)
        e_in = jnp.exp(cum)
        e_inv = jnp.exp(-cum)
        pm = stack(-kkn * jnp.exp(cum - ld))
        rm = stack(r * e_in)
        qm = stack(kkn * a * e_inv)
        km = stack(kmod * e_inv)
        vm = stack(v)
        x = _dot_nt(bf(jnp.concatenate([pm, rm], axis=0)), bf(jnp.concatenate([qm, km], axis=0)))
        n_mat = jnp.where(strict, x[0:L2, 0:L2], 0.0)
        a_kp = jnp.where(strict, x[0:L2, L2:2 * L2], 0.0)
        r_q = jnp.where(incl, x[L2:2 * L2, 0:L2], 0.0)
        r_k = jnp.where(incl, x[L2:2 * L2, L2:2 * L2], 0.0)
        t_inv = eye + n_mat
        n_pow = n_mat
        for _ in range(5):
            n_pow = dot(n_pow, n_pow)
            t_inv = t_inv + dot(n_pow, t_inv)
        s0 = s_scr[...]
        u = dot(t_inv, dot(pm, s0) + dot(a_kp, vm))
        ys = dot(rm, s0) + dot(r_q, u) + dot(r_k, vm)
        y = ys[0:L, :] + ys[L:L2, :]
        w_col = jnp.sum(eye * e_in[L - 1:L, :], axis=1, keepdims=True)
        s_scr[...] = w_col * (s0 + _dot_tn(bf(jnp.concatenate([qm, km], axis=0)), bf(jnp.concatenate([u, vm], axis=0))))
        mean = _dot_exact_rhs(y, bd_ones) * (1.0 / HEAD_DIM)
        yc = y - mean
        var = _dot_exact_rhs(yc * yc, bd_ones) * (1.0 / HEAD_DIM)
        yn = yc * lax.rsqrt(var + RWKV_LN_EPS) * lng_ref[...] + lnb_ref[...]
        bonus = _dot_exact_rhs(r * kmod * rk_ref[...], bd_ones) * v
        o_ref[sl, :] = ((yn + bonus) * g_ref[sl, :]).astype(o_ref.dtype)

    @pl.when(tstep == nt - 1)
    def _():
        sout_ref[...] = s_scr[...]


def rwkv_prompt(parts, lw, bsz, t):
    tb = 256
    nt = t // tb
    blk = pl.BlockSpec((tb, LANES), lambda b, p, i: (b * nt + i, p))
    par = pl.BlockSpec((1, LANES), lambda b, p, i: (0, p))
    return pl.pallas_call(
        functools.partial(_rwkv_chunk_kernel, nchunk=tb // RWKV_CHUNK, nt=nt),
        grid=(bsz, N_PAIRS, nt),
        in_specs=[blk] * 7 + [par] * 3,
        out_specs=[blk, pl.BlockSpec((None, None, LANES, LANES), lambda b, p, i: (b, p, 0, 0))],
        out_shape=[jax.ShapeDtypeStruct((bsz * t, BRANCH_W), BF16),
                   jax.ShapeDtypeStruct((bsz, N_PAIRS, LANES, LANES), F32)],
        scratch_shapes=[pltpu.VMEM((LANES, LANES), F32)],
        compiler_params=_cparams(("parallel", "parallel", "arbitrary")),
        name="rwkv_prompt",
    )(*parts, lw["r_k_row"], lw["ln_g_row"], lw["ln_b_row"])


def _rwkv_sample_step_kernel(s_ref, r_ref, ld_ref, k_ref, kk_ref, a_ref, v_ref, g_ref, rk_ref, lng_ref, lnb_ref,
                             o_ref, sout_ref):
    s = s_ref[...]
    r, kmod, kkraw, a = r_ref[...], k_ref[...], kk_ref[...], a_ref[...]
    v = v_ref[...]
    kkn = kkraw / jnp.maximum(jnp.sqrt(jnp.sum(kkraw * kkraw, axis=-1, keepdims=True)), 1e-12)
    sa = jnp.sum(s * (-kkn), axis=-1, keepdims=True)
    sn = s * jnp.exp(ld_ref[...]) + sa * (kkn * a) + v * kmod
    sout_ref[...] = sn
    y = jnp.sum(sn * r, axis=-1, keepdims=True)
    mean = jnp.mean(y, axis=1, keepdims=True)
    yc = y - mean
    var = jnp.mean(yc * yc, axis=1, keepdims=True)
    yn = yc * lax.rsqrt(var + RWKV_LN_EPS) * lng_ref[...] + lnb_ref[...]
    bonus = jnp.sum(r * kmod * rk_ref[...], axis=-1, keepdims=True) * v
    o_ref[...] = (yn + bonus) * g_ref[...]


def rwkv_sample_step(state, rows, cols, lw, layer):
    bd = rows[0].shape[0]
    row = pl.BlockSpec((None, RWKV_HEADS, 1, HEAD_DIM), lambda b: (b, 0, 0, 0))
    col = pl.BlockSpec((None, RWKV_HEADS, HEAD_DIM, 1), lambda b: (b, 0, 0, 0))
    prow = pl.BlockSpec((RWKV_HEADS, 1, HEAD_DIM), lambda b: (0, 0, 0))
    pcol = pl.BlockSpec((RWKV_HEADS, HEAD_DIM, 1), lambda b: (0, 0, 0))
    return pl.pallas_call(
        _rwkv_sample_step_kernel,
        grid=(bd,),
        in_specs=[pl.BlockSpec((None, None, RWKV_HEADS, HEAD_DIM, HEAD_DIM), lambda b: (layer, b, 0, 0, 0))]
        + [row] * 5 + [col] * 2 + [prow, pcol, pcol],
        out_specs=[col, pl.BlockSpec((None, RWKV_HEADS, HEAD_DIM, HEAD_DIM), lambda b: (b, 0, 0, 0))],
        out_shape=[jax.ShapeDtypeStruct((bd, RWKV_HEADS, HEAD_DIM, 1), F32),
                   jax.ShapeDtypeStruct((bd, RWKV_HEADS, HEAD_DIM, HEAD_DIM), F32)],
        compiler_params=_cparams(("parallel",)),
        name="rwkv_sample_step",
    )(state, *rows, *cols, lw["r_k_h"], lw["ln_g_col"], lw["ln_b_col"])


def _pack_layer(l, p):
    w_in = p["w_in"][l]
    rw = _O_RW
    lo = rw + 3 * BRANCH_W
    zeros = lambda n: jnp.zeros((D_MODEL, n), F32)
    w_proj = jnp.concatenate([
        w_in[:, _O_QKV:_O_Z], w_in[:, _O_Z:_O_XBC], w_in[:, _O_XBC:_O_XBC + BRANCH_W], w_in[:, rw:lo],
        w_in[:, _O_GATE:], w_in[:, _O_XBC + BRANCH_W:_O_DT],
        w_in[:, lo:lo + LORA_W], w_in[:, _O_DT:_O_RW], zeros(8), w_in[:, lo + LORA_W:lo + LORA_W + LORA_A], zeros(32),
        w_in[:, lo + LORA_W + LORA_A:_O_GATE]], axis=1).astype(BF16)
    row = lambda a: a.reshape(1, -1)
    pad_rows = lambda a, n: jnp.concatenate([a, jnp.zeros((n - a.shape[0], a.shape[1]), a.dtype)], axis=0).astype(BF16)
    lane_pad = lambda a: jnp.zeros((1, LANES), F32).at[0, DT_LANE0:DT_LANE0 + SSM_HEADS].set(a)
    mu = p["rwkv_mu"][l]
    z1 = lambda n: jnp.zeros((n,), F32)
    cw = p["ssm_conv_w"][l]
    cb = p["ssm_conv_b"][l]
    rep64 = lambda a: jnp.repeat(a, HEAD_DIM)
    return dict(
        ln1_g=p["ln1_g"][l], ln2_g=p["ln2_g"][l], w_proj=w_proj,
        cw_x=cw[:, :BRANCH_W], cw_b=cw[:, BRANCH_W:BRANCH_W + 512], cw_c=cw[:, BRANCH_W + 512:],
        cb_x=row(cb[:BRANCH_W]), cb_b=row(cb[BRANCH_W:BRANCH_W + 512]), cb_c=row(cb[BRANCH_W + 512:]),
        conv_w=cw, conv_b=cb,
        dt_bias=lane_pad(p["ssm_dt_bias"][l]), a_neg=lane_pad(-jnp.exp(p["ssm_a_log"][l])),
        d_skip=row(rep64(p["ssm_d"][l])), ssm_norm_g=row(p["ssm_norm_g"][l]),
        dt_bias_h=p["ssm_dt_bias"][l].reshape(SSM_HEADS, 1, 1), a_log_h=p["ssm_a_log"][l].reshape(SSM_HEADS, 1, 1),
        d_h=p["ssm_d"][l].reshape(SSM_HEADS, 1, 1), ssm_norm_g_col=p["ssm_norm_g"][l].reshape(SSM_HEADS, HEAD_DIM, 1),
        mu_r=row(mu[:BRANCH_W]), mu_k=row(mu[BRANCH_W:2 * BRANCH_W]), mu_v=row(mu[2 * BRANCH_W:3 * BRANCH_W]),
        mu_l=row(jnp.concatenate([mu[3 * BRANCH_W:3 * BRANCH_W + LORA_W], z1(32),
                                  mu[3 * BRANCH_W + LORA_W:3 * BRANCH_W + LORA_W + LORA_A], z1(32),
                                  mu[3 * BRANCH_W + LORA_W + LORA_A:]])),
        w0=row(p["rwkv_w0"][l]), w_up=pad_rows(p["rwkv_w_up"][l], LANES), a0=row(p["rwkv_a0"][l]),
        a_up=pad_rows(p["rwkv_a_up"][l], LANES), g_up=p["rwkv_g_up"][l].astype(BF16),
        k_k=row(p["rwkv_k_k"][l]), k_a=row(p["rwkv_k_a"][l]),
        r_k_row=row(p["rwkv_r_k"][l]), ln_g_row=row(p["rwkv_ln_g"][l]), ln_b_row=row(p["rwkv_ln_b"][l]),
        r_k_h=p["rwkv_r_k"][l].reshape(RWKV_HEADS, 1, HEAD_DIM),
        ln_g_col=p["rwkv_ln_g"][l].reshape(RWKV_HEADS, HEAD_DIM, 1),
        ln_b_col=p["rwkv_ln_b"][l].reshape(RWKV_HEADS, HEAD_DIM, 1),
        w_att=p["w_branch_att"][l].astype(BF16), w_ssm=p["w_branch_ssm"][l].astype(BF16),
        w_rwkv=p["w_branch_rwkv"][l].astype(BF16), w_out=p["w_out"][l].astype(BF16),
        w_up_ff=p["w_up"][l].astype(BF16), w_down_ff=p["w_down"][l].astype(BF16),
    )


def _unpack_lora(rows):
    return jnp.concatenate([rows[..., 0:LORA_W], rows[..., LANES:LANES + LORA_A], rows[..., 2 * LANES:]], axis=-1)


def _pack_lora(rows):
    z = lambda n: jnp.zeros(rows.shape[:-1] + (n,), rows.dtype)
    return jnp.concatenate([rows[..., 0:LORA_W], z(32), rows[..., LORA_W:LORA_W + LORA_A], z(32),
                            rows[..., LORA_W + LORA_A:]], axis=-1)


def _alibi_slopes():
    n = 3 * ATT_HEADS
    idx = jnp.arange(1, n + 1, dtype=F32)
    return jnp.exp2(-8.0 * idx / n).reshape(3, ATT_HEADS)


def _finish_layer(x, proj, o_att, o_ssm, o_rwkv, lw):
    merged = merge_branches(o_att, o_ssm, o_rwkv, lw["w_att"], lw["w_ssm"], lw["w_rwkv"], proj)
    x = matmul(merged, lw["w_out"], epi="res", res=x)
    h2 = rmsnorm(x, lw["ln2_g"], BF16)
    up = matmul(h2, lw["w_up_ff"], epi="relu2", out_dtype=BF16)
    return matmul(up, lw["w_down_ff"], epi="res", res=x)


def _prompt_layer(x, lw, slopes, bsz, t):
    h = rmsnorm(x, lw["ln1_g"], BF16)
    proj = matmul(h, lw["w_proj"])
    parts = []
    for g in range(3):
        parts += att_prompt_group(proj, slopes, g, bsz, t)
    o_att = att_combine(parts)
    o_ssm, h_fin = ssd_prompt(proj, lw, bsz, t)
    zero_init = [jnp.zeros((bsz, 1, w), F32) for w in (BRANCH_W, BRANCH_W, BRANCH_W, 512)]
    rw = rwkv_prep(proj, zero_init, lw, bsz, t)
    o_rwkv, s_fin = rwkv_prompt(rw, lw, bsz, t)
    x = _finish_layer(x, proj, o_att, o_ssm, o_rwkv, lw)

    p3 = proj.reshape(bsz, t, N_PROJ)
    kv = []
    for g in range(3):
        win = ATT_WINDOWS[g]
        for c0 in (C_K, C_V):
            kv.append(p3[:, t - win:, c0 + g * ATT_OUT:c0 + (g + 1) * ATT_OUT].reshape(bsz, win, ATT_HEADS, HEAD_DIM))
    tail = p3[:, t - (SSM_CONV - 1):]
    conv_new = jnp.concatenate([tail[..., C_X:C_X + BRANCH_W], tail[..., C_B:C_B + 512], tail[..., C_C:C_C + 512]], -1)
    last = p3[:, t - 1]
    shift_new = jnp.concatenate([last[:, C_R:C_R + 3 * BRANCH_W], _unpack_lora(last[:, C_L:C_L + 512])], axis=-1)
    ssm_new = h_fin.reshape(bsz, SSM_HEADS, HEAD_DIM, SSM_STATE)
    s6 = s_fin.reshape(bsz, N_PAIRS, 2, HEAD_DIM, 2, HEAD_DIM)
    s_heads = jnp.stack([s6[:, :, 0, :, 0, :], s6[:, :, 1, :, 1, :]], axis=2)
    rwkv_new = jnp.swapaxes(s_heads, -1, -2).reshape(bsz, RWKV_HEADS, HEAD_DIM, HEAD_DIM)
    return x, tuple(kv) + (conv_new, ssm_new, shift_new, rwkv_new)


def _sample_layer(x, lw, slopes, st, layer, bd):
    (caches, conv_st, ssm_st, shift_st, rwkv_st) = st
    h = rmsnorm(x, lw["ln1_g"], BF16)
    proj = matmul(h, lw["w_proj"])
    qkv = proj[:, :3 * BRANCH_W].reshape(bd, 3, 3, ATT_HEADS, HEAD_DIM)
    o_att = att_sample(qkv, caches, slopes, layer).reshape(bd, ATT_OUT).astype(BF16)
    new_caches = []
    for g in range(3):
        for ci, part in ((2 * g, 1), (2 * g + 1, 2)):
            new_caches.append(jnp.concatenate([caches[ci][layer][:, 1:], qkv[:, part, g][:, None]], axis=1))

    xbc_new = jnp.concatenate([proj[:, C_X:C_X + BRANCH_W], proj[:, C_B:C_B + 512], proj[:, C_C:C_C + 512]], axis=-1)
    cst = conv_st[layer]
    xc = ssd_sample_conv(xbc_new, jnp.swapaxes(cst, 0, 1), lw["conv_w"], lw["conv_b"])
    conv_new = jnp.concatenate([cst[:, 1:], xbc_new[:, None]], axis=1)
    x_col = xc[:, :BRANCH_W].reshape(bd, SSM_HEADS, HEAD_DIM, 1)
    z_col = proj[:, C_Z:C_Z + BRANCH_W].reshape(bd, SSM_HEADS, HEAD_DIM, 1)
    bmat = xc[:, BRANCH_W:BRANCH_W + 512].reshape(bd, SSM_GROUPS, 1, SSM_STATE)
    cmat = xc[:, BRANCH_W + 512:].reshape(bd, SSM_GROUPS, 1, SSM_STATE)
    dt_raw = proj[:, C_L + DT_LANE0:C_L + DT_LANE0 + SSM_HEADS].reshape(bd, SSM_HEADS, 1, 1)
    y_col, ssm_new = ssd_sample_step(ssm_st, x_col, z_col, bmat, cmat, dt_raw, lw, layer)
    o_ssm = y_col.reshape(bd, BRANCH_W).astype(BF16)

    sh = shift_st[layer]
    init = [sh[:, None, i * BRANCH_W:(i + 1) * BRANCH_W] for i in range(3)] + [_pack_lora(sh[:, None, 3 * BRANCH_W:])]
    r, ld, kmod, v, kkraw, a, gg = rwkv_prep(proj, init, lw, bd, 1)
    rowf = lambda z: z.reshape(bd, RWKV_HEADS, 1, HEAD_DIM)
    colf = lambda z: z.reshape(bd, RWKV_HEADS, HEAD_DIM, 1)
    o_col, rwkv_new = rwkv_sample_step(rwkv_st, [rowf(r), rowf(ld), rowf(kmod), rowf(kkraw), rowf(a)],
                                       [colf(v), colf(gg)], lw, layer)
    o_rwkv = o_col.reshape(bd, BRANCH_W).astype(BF16)
    shift_new = jnp.concatenate([proj[:, C_R:C_R + 3 * BRANCH_W], _unpack_lora(proj[:, C_L:C_L + 512])], axis=-1)

    x = _finish_layer(x, proj, o_att, o_ssm, o_rwkv, lw)
    return x, tuple(new_caches) + (conv_new, ssm_new, shift_new, rwkv_new)


def kernel(x_prompt, x_sample, cache_att_k0, cache_att_v0, cache_att_k1, cache_att_v1, cache_att_k2, cache_att_v2, state_ssm_conv, state_ssm, state_rwkv_shift, state_rwkv, ln1_g, w_in, ssm_conv_w, ssm_conv_b, ssm_dt_bias, ssm_a_log, ssm_d, ssm_norm_g, rwkv_mu, rwkv_w0, rwkv_w_up, rwkv_a0, rwkv_a_up, rwkv_g_up, rwkv_k_k, rwkv_k_a, rwkv_r_k, rwkv_ln_g, rwkv_ln_b, w_branch_att, w_branch_ssm, w_branch_rwkv, w_out, ln2_g, w_up, w_down, final_g):
    p = dict(ln1_g=ln1_g, w_in=w_in, ssm_conv_w=ssm_conv_w, ssm_conv_b=ssm_conv_b, ssm_dt_bias=ssm_dt_bias,
             ssm_a_log=ssm_a_log, ssm_d=ssm_d, ssm_norm_g=ssm_norm_g, rwkv_mu=rwkv_mu, rwkv_w0=rwkv_w0,
             rwkv_w_up=rwkv_w_up, rwkv_a0=rwkv_a0, rwkv_a_up=rwkv_a_up, rwkv_g_up=rwkv_g_up, rwkv_k_k=rwkv_k_k,
             rwkv_k_a=rwkv_k_a, rwkv_r_k=rwkv_r_k, rwkv_ln_g=rwkv_ln_g, rwkv_ln_b=rwkv_ln_b,
             w_branch_att=w_branch_att, w_branch_ssm=w_branch_ssm, w_branch_rwkv=w_branch_rwkv, w_out=w_out,
             ln2_g=ln2_g, w_up=w_up, w_down=w_down)
    depth = w_in.shape[0]
    bsz, t, _ = x_prompt.shape
    bd = x_sample.shape[0]
    slopes = _alibi_slopes()
    caches = (cache_att_k0, cache_att_v0, cache_att_k1, cache_att_v1, cache_att_k2, cache_att_v2)
    st = (caches, state_ssm_conv, state_ssm, state_rwkv_shift, state_rwkv)
    xp = x_prompt.reshape(bsz * t, D_MODEL)
    xs = x_sample.reshape(bd, D_MODEL)
    p_new, s_new = [], []
    for l in range(depth):
        lw = _pack_layer(l, p)
        xp, sp = _prompt_layer(xp, lw, slopes, bsz, t)
        xs, ss = _sample_layer(xs, lw, slopes, st, l, bd)
        p_new.append(sp)
        s_new.append(ss)
    y_prompt = rmsnorm(xp, final_g, F32).reshape(bsz, t, D_MODEL)
    y_sample = rmsnorm(xs, final_g, F32).reshape(bd, 1, D_MODEL)
    stack = lambda per_layer: tuple(jnp.stack([s[i] for s in per_layer]) for i in range(10))
    return (y_prompt, y_sample) + stack(p_new) + stack(s_new)
```

```python
import functools

import jax
import jax.numpy as jnp
from jax import lax
from jax.experimental import pallas as pl
from jax.experimental.pallas import tpu as pltpu

F32 = jnp.float32
BF16 = jnp.bfloat16

D_MODEL = 2048
HEAD_DIM = 64
BRANCH_W = 1536
NORM_EPS = 1e-5
ATT_WINDOWS = (128, 512, 2048)
ATT_DILS = (1, 4, 16)
ATT_SPAN = 128
ATT_HEADS = 8
ATT_OUT = ATT_HEADS * HEAD_DIM
ATT_SCALE = HEAD_DIM ** -0.5
SSM_HEADS = 24
SSM_GROUPS = 4
SSM_STATE = 128
SSM_CONV = 4
SSM_CHUNK = 128
RWKV_HEADS = 24
LORA_W = 96
LORA_A = 96
LORA_G = 256
RWKV_LN_EPS = 64e-5
D_FF = 4 * D_MODEL
RWKV_CHUNK = 64
N_PAIRS = BRANCH_W // 128

LANES = 128
SUBLANES = 8
VMEM_LIMIT_BYTES = 56 * 1024 * 1024

C_Q, C_K, C_V, C_Z, C_X, C_R, C_RK, C_RV = (i * BRANCH_W for i in range(8))
C_G = 8 * BRANCH_W
C_B = C_G + 3 * D_MODEL
C_C = C_B + 512
C_L = C_C + 512
N_PROJ = C_L + 512
DT_LANE0 = LORA_W

_O_QKV, _O_Z, _O_XBC, _O_DT, _O_RW, _O_GATE = 0, 4608, 6144, 8704, 8728, 13784


def _cparams(sem):
    return pltpu.CompilerParams(dimension_semantics=sem, vmem_limit_bytes=VMEM_LIMIT_BYTES)


def _softplus(x):
    return jnp.maximum(x, 0.0) + jnp.log1p(jnp.exp(-jnp.abs(x)))


def _sigmoid(x):
    return 1.0 / (1.0 + jnp.exp(-x))


def _silu(x):
    return x * _sigmoid(x)


def _split3(x):
    hi = x.astype(BF16)
    r1 = x - hi.astype(F32)
    mid = r1.astype(BF16)
    lo = (r1 - mid.astype(F32)).astype(BF16)
    return hi, mid, lo


def _dot_exact_rhs(x, mat):
    hi, mid, lo = _split3(x)
    f = lambda t: jnp.dot(t, mat, preferred_element_type=F32)
    return f(hi) + f(mid) + f(lo)


def _dot_exact_lhs(mat, x):
    hi, mid, lo = _split3(x)
    f = lambda t: jnp.dot(mat, t, preferred_element_type=F32)
    return f(hi) + f(mid) + f(lo)


def _dot_nt(a, b):
    return lax.dot_general(a, b, (((1,), (1,)), ((), ())), preferred_element_type=F32)


def _dot_tn(a, b):
    return lax.dot_general(a, b, (((0,), (0,)), ((), ())), preferred_element_type=F32)


def _rmsnorm_kernel(x_ref, g_ref, o_ref):
    x = x_ref[...]
    y = x * lax.rsqrt(jnp.mean(x * x, axis=-1, keepdims=True) + NORM_EPS)
    o_ref[...] = (y * g_ref[...]).astype(o_ref.dtype)


def rmsnorm(x, g, out_dtype):
    m, d = x.shape
    tm = min(m, 512)
    return pl.pallas_call(
        _rmsnorm_kernel,
        grid=(m // tm,),
        in_specs=[pl.BlockSpec((tm, d), lambda i: (i, 0)), pl.BlockSpec((1, d), lambda i: (0, 0))],
        out_specs=pl.BlockSpec((tm, d), lambda i: (i, 0)),
        out_shape=jax.ShapeDtypeStruct((m, d), out_dtype),
        compiler_params=_cparams(("parallel",)),
        name="rmsnorm",
    )(x, g.reshape(1, d))


def _mm_kernel(*refs, nk, epi):
    if epi == "res":
        x_ref, w_ref, r_ref, o_ref = refs[:4]
    else:
        x_ref, w_ref, o_ref = refs[:3]
        r_ref = None

    def finish(a):
        if epi == "res":
            a = r_ref[...] + a
        elif epi == "relu2":
            a = jnp.square(jnp.maximum(a, 0.0))
        o_ref[...] = a.astype(o_ref.dtype)

    part = jnp.dot(x_ref[...], w_ref[...], preferred_element_type=F32)
    if nk == 1:
        finish(part)
        return
    acc_ref = refs[-1]
    k = pl.program_id(2)

    @pl.when(k == 0)
    def _():
        acc_ref[...] = part

    @pl.when(k > 0)
    def _():
        acc_ref[...] += part

    @pl.when(k == nk - 1)
    def _():
        finish(acc_ref[...])


def _pick(n, cands):
    for c in cands:
        if n % c == 0:
            return c
    return n


def matmul(x, w, *, epi="none", res=None, out_dtype=F32):
    m, kdim = x.shape
    n = w.shape[1]
    tm = min(m, 1024)
    tn = _pick(n, (1536, 1024, 512, 256, 128)) if m > 64 else _pick(n, (2048, 1536, 1024, 512, 256, 128))
    tk = min(kdim, 2048)
    nk = kdim // tk
    in_specs = [pl.BlockSpec((tm, tk), lambda i, j, k: (i, k)), pl.BlockSpec((tk, tn), lambda i, j, k: (k, j))]
    args = [x, w]
    if epi == "res":
        in_specs.append(pl.BlockSpec((tm, tn), lambda i, j, k: (i, j)))
        args.append(res)
    scratch = [pltpu.VMEM((tm, tn), F32)] if nk > 1 else []
    return pl.pallas_call(
        functools.partial(_mm_kernel, nk=nk, epi=epi),
        grid=(m // tm, n // tn, nk),
        in_specs=in_specs,
        out_specs=pl.BlockSpec((tm, tn), lambda i, j, k: (i, j)),
        out_shape=jax.ShapeDtypeStruct((m, n), out_dtype),
        scratch_shapes=scratch,
        compiler_params=_cparams(("parallel", "parallel", "arbitrary")),
        name="matmul_" + epi,
    )(*args)


def _merge_kernel(oa_ref, os_ref, or_ref, wa_ref, ws_ref, wr_ref, ga_ref, gs_ref, gr_ref, o_ref):
    f = lambda a, b: jnp.dot(a[...], b[...], preferred_element_type=F32)
    acc = _sigmoid(ga_ref[...]) * f(oa_ref, wa_ref)
    acc = acc + _sigmoid(gs_ref[...]) * f(os_ref, ws_ref)
    acc = acc + _sigmoid(gr_ref[...]) * f(or_ref, wr_ref)
    o_ref[...] = acc.astype(o_ref.dtype)


def merge_branches(o_att, o_ssm, o_rwkv, wa, ws, wr, proj):
    m = o_att.shape[0]
    tm = min(m, 512)
    tn = 1024
    gb = C_G // tn
    nj = D_MODEL // tn
    row = lambda width: pl.BlockSpec((tm, width), lambda i, j: (i, 0))
    col = lambda k: pl.BlockSpec((k, tn), lambda i, j: (0, j))
    gate = lambda g: pl.BlockSpec((tm, tn), lambda i, j, g=g: (i, gb + g * nj + j))
    return pl.pallas_call(
        _merge_kernel,
        grid=(m // tm, nj),
        in_specs=[row(ATT_OUT), row(BRANCH_W), row(BRANCH_W), col(ATT_OUT), col(BRANCH_W), col(BRANCH_W),
                  gate(0), gate(1), gate(2)],
        out_specs=pl.BlockSpec((tm, tn), lambda i, j: (i, j)),
        out_shape=jax.ShapeDtypeStruct((m, D_MODEL), BF16),
        compiler_params=_cparams(("parallel", "parallel")),
        name="merge_branches",
    )(o_att, o_ssm, o_rwkv, wa, ws, wr, proj, proj, proj)


def _att_prompt_kernel(sl_ref, q_ref, kc_ref, kp_ref, vc_ref, vp_ref, o_ref, lse_ref, *, d, nblk, g):
    j = pl.program_id(1)
    hp = pl.program_id(2)
    lane = lax.broadcasted_iota(jnp.int32, (1, LANES), 1)
    head0 = lane < HEAD_DIM
    qi = lax.broadcasted_iota(jnp.int32, (ATT_SPAN, 2 * ATT_SPAN), 0)
    kj = lax.broadcasted_iota(jnp.int32, (ATT_SPAN, 2 * ATT_SPAN), 1)
    delta = qi - kj + ATT_SPAN
    band = (delta >= 0) & (delta <= ATT_SPAN)
    first_ok = band & ((kj >= ATT_SPAN) | (j > 0))
    dist = (delta * d).astype(F32)
    slopes = (sl_ref[g, 2 * hp], sl_ref[g, 2 * hp + 1])

    def rows(base):
        return pl.ds(base, ATT_SPAN) if d == 1 else pl.ds(base, ATT_SPAN, stride=d)

    for ub in range(nblk):
        valid = first_ok if ub == 0 else band
        for r in range(d):
            base = ub * ATT_SPAN * d + r
            q = q_ref[rows(base), :]
            if ub == 0:
                kp, vp = kp_ref[rows(r), :], vp_ref[rows(r), :]
            else:
                kp, vp = kc_ref[rows(base - ATT_SPAN * d), :], vc_ref[rows(base - ATT_SPAN * d), :]
            k = jnp.concatenate([kp, kc_ref[rows(base), :]], axis=0).astype(BF16)
            v = jnp.concatenate([vp, vc_ref[rows(base), :]], axis=0).astype(BF16)
            outs, lses = [], []
            for e in range(2):
                sel = head0 if e == 0 else jnp.logical_not(head0)
                qe = jnp.where(sel, q, 0.0).astype(BF16)
                s = _dot_nt(qe, k) * ATT_SCALE
                s = jnp.where(valid, s - slopes[e] * dist, -jnp.inf)
                mx = jnp.max(s, axis=-1, keepdims=True)
                p = jnp.exp(s - mx)
                den = jnp.sum(p, axis=-1, keepdims=True)
                outs.append(jnp.dot(p.astype(BF16), v, preferred_element_type=F32) / den)
                lses.append(mx + jnp.log(den))
            o_ref[rows(base), :] = jnp.where(head0, outs[0], outs[1])
            lse_ref[rows(base), :] = jnp.where(head0, lses[0], lses[1])


def att_prompt_group(proj, slopes, g, bsz, t):
    d = ATT_DILS[g]
    sb = ATT_SPAN * d
    tb = max(sb, 512)
    nblk = tb // sb
    nt = t // tb
    cq, ck, cv = (C_Q + g * ATT_OUT) // LANES, (C_K + g * ATT_OUT) // LANES, (C_V + g * ATT_OUT) // LANES
    cur = lambda c0: pl.BlockSpec((tb, LANES), lambda b, j, h: (b * nt + j, c0 + h))
    prev = lambda c0: pl.BlockSpec((sb, LANES), lambda b, j, h: (jnp.maximum((b * nt + j) * nblk - 1, 0), c0 + h))
    out = pl.BlockSpec((tb, LANES), lambda b, j, h: (b * nt + j, h))
    shp = jax.ShapeDtypeStruct((bsz * t, ATT_OUT), F32)
    return pl.pallas_call(
        functools.partial(_att_prompt_kernel, d=d, nblk=nblk, g=g),
        grid=(bsz, nt, ATT_HEADS // 2),
        in_specs=[pl.BlockSpec(memory_space=pltpu.SMEM), cur(cq), cur(ck), prev(ck), cur(cv), prev(cv)],
        out_specs=[out, out],
        out_shape=[shp, shp],
        compiler_params=_cparams(("parallel", "parallel", "parallel")),
        name=f"att_prompt_g{g}",
    )(slopes, proj, proj, proj, proj, proj)


def _att_combine_kernel(o0, l0, o1, l1, o2, l2, out_ref):
    a, b, c = l0[...], l1[...], l2[...]
    mx = jnp.maximum(jnp.maximum(a, b), c)
    ea, eb, ec = jnp.exp(a - mx), jnp.exp(b - mx), jnp.exp(c - mx)
    num = ea * o0[...] + eb * o1[...] + ec * o2[...]
    out_ref[...] = (num / (ea + eb + ec)).astype(out_ref.dtype)


def att_combine(parts):
    m = parts[0].shape[0]
    tm = min(m, 1024)
    spec = pl.BlockSpec((tm, ATT_OUT), lambda i: (i, 0))
    return pl.pallas_call(
        _att_combine_kernel,
        grid=(m // tm,),
        in_specs=[spec] * 6,
        out_specs=spec,
        out_shape=jax.ShapeDtypeStruct((m, ATT_OUT), BF16),
        compiler_params=_cparams(("parallel",)),
        name="att_combine",
    )(*parts)


def _att_sample_kernel(sl_ref, qkv_ref, k0, v0, k1, v1, k2, v2, o_ref):
    caches = ((k0, v0), (k1, v1), (k2, v2))
    idx = lax.broadcasted_iota(jnp.int32, (ATT_SPAN, ATT_HEADS, 1), 0)
    outs, lses = [], []
    for g in range(3):
        d = ATT_DILS[g]
        q = qkv_ref[0, g]
        kn = qkv_ref[1, g]
        vn = qkv_ref[2, g]
        kc = caches[g][0][...]
        vc = caches[g][1][...]
        sl = sl_ref[g]
        dist = ((ATT_SPAN - idx) * d).astype(F32)
        s = jnp.sum(kc * q[None], axis=-1, keepdims=True) * ATT_SCALE - sl[None] * dist
        s_new = jnp.sum(kn * q, axis=-1, keepdims=True) * ATT_SCALE
        mx = jnp.maximum(jnp.max(s, axis=0), s_new)
        p = jnp.exp(s - mx[None])
        p_new = jnp.exp(s_new - mx)
        den = jnp.sum(p, axis=0) + p_new
        outs.append((jnp.sum(p * vc, axis=0) + p_new * vn) / den)
        lses.append(mx + jnp.log(den))
    mx = jnp.maximum(jnp.maximum(lses[0], lses[1]), lses[2])
    es = [jnp.exp(l - mx) for l in lses]
    o_ref[...] = (es[0] * outs[0] + es[1] * outs[1] + es[2] * outs[2]) / (es[0] + es[1] + es[2])


def att_sample(qkv, caches, slopes, layer):
    bd = qkv.shape[0]
    in_specs = [pl.BlockSpec((3, ATT_HEADS, 1), lambda b: (0, 0, 0)),
                pl.BlockSpec((None, 3, 3, ATT_HEADS, HEAD_DIM), lambda b: (b, 0, 0, 0, 0))]
    args = [slopes.reshape(3, ATT_HEADS, 1), qkv]
    for g in range(3):
        d = ATT_DILS[g]
        for c in caches[2 * g:2 * g + 2]:
            args.append(c.reshape(c.shape[0], bd, ATT_SPAN, d, ATT_HEADS, HEAD_DIM))
            in_specs.append(pl.BlockSpec((None, None, ATT_SPAN, None, ATT_HEADS, HEAD_DIM),
                                         lambda b: (layer, b, 0, 0, 0, 0)))
    return pl.pallas_call(
        _att_sample_kernel,
        grid=(bd,),
        in_specs=in_specs,
        out_specs=pl.BlockSpec((None, ATT_HEADS, HEAD_DIM), lambda b: (b, 0, 0)),
        out_shape=jax.ShapeDtypeStruct((bd, ATT_HEADS, HEAD_DIM), F32),
        compiler_params=_cparams(("parallel",)),
        name="att_sample",
    )(*args)


def _ssd_prompt_kernel(x_ref, b_ref, c_ref, z_ref, l_ref, cwx_ref, cwb_ref, cwc_ref, cbx_ref, cbb_ref, cbc_ref,
                       dtb_ref, aneg_ref, dskip_ref, ng_ref, o_ref, hout_ref, xpad, bpad, cpad, h_scr, y_scr, *, nc):
    c = pl.program_id(1)
    L = SSM_CHUNK

    @pl.when(c == 0)
    def _():
        xpad[0:SUBLANES, :] = jnp.zeros((SUBLANES, xpad.shape[1]), F32)
        bpad[0:SUBLANES, :] = jnp.zeros((SUBLANES, bpad.shape[1]), F32)
        cpad[0:SUBLANES, :] = jnp.zeros((SUBLANES, cpad.shape[1]), F32)
        h_scr[...] = jnp.zeros_like(h_scr)

    def conv_silu(pad, src_ref, w_ref, bias_ref):
        pad[SUBLANES:SUBLANES + L, :] = src_ref[...]
        acc = bias_ref[...] + w_ref[0:1, :] * pad[pl.ds(SUBLANES - 3, L), :]
        for jj in range(1, SSM_CONV):
            acc = acc + w_ref[jj:jj + 1, :] * pad[pl.ds(SUBLANES - 3 + jj, L), :]
        pad[0:SUBLANES, :] = pad[L:L + SUBLANES, :]
        return _silu(acc)

    xs = conv_silu(xpad, x_ref, cwx_ref, cbx_ref)
    bm = conv_silu(bpad, b_ref, cwb_ref, cbb_ref).astype(BF16)
    cm = conv_silu(cpad, c_ref, cwc_ref, cbc_ref).astype(BF16)

    lane = lax.broadcasted_iota(jnp.int32, (1, LANES), 1)
    head0 = lane < HEAD_DIM
    sub_head0 = lax.broadcasted_iota(jnp.int32, (LANES, 1), 0) < HEAD_DIM
    dt_lanes = (lane >= DT_LANE0) & (lane < DT_LANE0 + SSM_HEADS)
    dtv = jnp.where(dt_lanes, _softplus(l_ref[:, 0:LANES] + dtb_ref[...]), 0.0)
    da = dtv * aneg_ref[...]
    ri = lax.broadcasted_iota(jnp.int32, (L, L), 0)
    ci = lax.broadcasted_iota(jnp.int32, (L, L), 1)
    causal = ri >= ci
    a_cum = _dot_exact_lhs(causal.astype(BF16), da)
    a_cum_t = a_cum.T
    dt_t = dtv.T

    for g in range(SSM_GROUPS):
        bg = bm[:, g * SSM_STATE:(g + 1) * SSM_STATE]
        cg = cm[:, g * SSM_STATE:(g + 1) * SSM_STATE]
        cb = _dot_nt(cg, bg)
        for pp in range(3):
            pair = g * 3 + pp
            xp = xs[:, pair * LANES:(pair + 1) * LANES]
            xpb = xp.astype(BF16)
            yd, sc, ea, cd = [], [], [], []
            for e in range(2):
                hl = DT_LANE0 + 2 * pair + e
                ac_col = a_cum[:, hl:hl + 1]
                ac_row = a_cum_t[hl:hl + 1, :]
                a_last = ac_col[L - 1:L, :]
                dec = jnp.exp(jnp.where(causal, ac_col - ac_row, -jnp.inf))
                wm = (cb * dec * dt_t[hl:hl + 1, :]).astype(BF16)
                yd.append(jnp.dot(wm, xpb, preferred_element_type=F32))
                sc.append(jnp.exp(a_last - ac_col) * dtv[:, hl:hl + 1])
                ea.append(jnp.exp(ac_col))
                cd.append(jnp.exp(a_last))
            hp = h_scr[pair]
            y_off = _dot_nt(cg, hp.astype(BF16)) * jnp.where(head0, ea[0], ea[1])
            xw = (xp * jnp.where(head0, sc[0], sc[1])).astype(BF16)
            h_scr[pair] = jnp.where(sub_head0, cd[0], cd[1]) * hp + _dot_tn(xw, bg)
            y_scr[:, pair * LANES:(pair + 1) * LANES] = jnp.where(head0, yd[0], yd[1]) + y_off

    y = (y_scr[...] + dskip_ref[...] * xs) * _silu(z_ref[...])
    gw = BRANCH_W // SSM_GROUPS
    for g in range(SSM_GROUPS):
        yg = y[:, g * gw:(g + 1) * gw]
        yg = yg * lax.rsqrt(jnp.mean(yg * yg, axis=-1, keepdims=True) + NORM_EPS)
        o_ref[:, g * gw:(g + 1) * gw] = (yg * ng_ref[:, g * gw:(g + 1) * gw]).astype(o_ref.dtype)

    @pl.when(c == nc - 1)
    def _():
        hout_ref[...] = h_scr[...]


def ssd_prompt(proj, lw, bsz, t):
    nc = t // SSM_CHUNK
    L = SSM_CHUNK
    rowblk = lambda width, c0: pl.BlockSpec((L, width), lambda b, c: (b * nc + c, c0 // width))
    par = lambda a: pl.BlockSpec(a.shape, lambda b, c: (0,) * a.ndim)
    params = [lw["cw_x"], lw["cw_b"], lw["cw_c"], lw["cb_x"], lw["cb_b"], lw["cb_c"], lw["dt_bias"], lw["a_neg"],
              lw["d_skip"], lw["ssm_norm_g"]]
    return pl.pallas_call(
        functools.partial(_ssd_prompt_kernel, nc=nc),
        grid=(bsz, nc),
        in_specs=[rowblk(BRANCH_W, C_X), rowblk(512, C_B), rowblk(512, C_C), rowblk(BRANCH_W, C_Z), rowblk(512, C_L)]
        + [par(a) for a in params],
        out_specs=[pl.BlockSpec((L, BRANCH_W), lambda b, c: (b * nc + c, 0)),
                   pl.BlockSpec((None, N_PAIRS, LANES, SSM_STATE), lambda b, c: (b, 0, 0, 0))],
        out_shape=[jax.ShapeDtypeStruct((bsz * t, BRANCH_W), BF16),
                   jax.ShapeDtypeStruct((bsz, N_PAIRS, LANES, SSM_STATE), F32)],
        scratch_shapes=[pltpu.VMEM((L + SUBLANES, BRANCH_W), F32), pltpu.VMEM((L + SUBLANES, 512), F32),
                        pltpu.VMEM((L + SUBLANES, 512), F32), pltpu.VMEM((N_PAIRS, LANES, SSM_STATE), F32),
                        pltpu.VMEM((L, BRANCH_W), F32)],
        compiler_params=_cparams(("parallel", "arbitrary")),
        name="ssd_prompt",
    )(proj, proj, proj, proj, proj, *params)


def _ssd_sample_conv_kernel(new_ref, st_ref, w_ref, b_ref, o_ref):
    acc = b_ref[...] + w_ref[SSM_CONV - 1:SSM_CONV, :] * new_ref[...]
    for jj in range(SSM_CONV - 1):
        acc = acc + w_ref[jj:jj + 1, :] * st_ref[jj]
    o_ref[...] = _silu(acc)


def ssd_sample_conv(xbc_new, conv_st_t, w, b):
    bd, cdim = xbc_new.shape
    return pl.pallas_call(
        _ssd_sample_conv_kernel,
        out_shape=jax.ShapeDtypeStruct((bd, cdim), F32),
        name="ssd_sample_conv",
    )(xbc_new, conv_st_t, w, b.reshape(1, cdim))


def _ssd_sample_step_kernel(h_ref, x_ref, z_ref, b_ref, c_ref, dt_ref, dtb_ref, alog_ref, d_ref, ng_ref, o_ref, hout_ref):
    h = h_ref[...]
    dt = _softplus(dt_ref[...] + dtb_ref[...])
    da = jnp.exp(dt * (-jnp.exp(alog_ref[...])))
    rep = SSM_HEADS // SSM_GROUPS
    bh = jnp.broadcast_to(b_ref[...][:, None], (SSM_GROUPS, rep, 1, SSM_STATE)).reshape(SSM_HEADS, 1, SSM_STATE)
    ch = jnp.broadcast_to(c_ref[...][:, None], (SSM_GROUPS, rep, 1, SSM_STATE)).reshape(SSM_HEADS, 1, SSM_STATE)
    x = x_ref[...]
    hn = da * h + (dt * x) * bh
    hout_ref[...] = hn
    y = jnp.sum(hn * ch, axis=-1, keepdims=True)
    y = (y + d_ref[...] * x) * _silu(z_ref[...])
    y4 = y.reshape(SSM_GROUPS, rep, HEAD_DIM, 1)
    ms = jnp.sum(jnp.sum(y4 * y4, axis=2, keepdims=True), axis=1, keepdims=True) / (rep * HEAD_DIM)
    y4 = y4 * lax.rsqrt(ms + NORM_EPS)
    o_ref[...] = y4.reshape(SSM_HEADS, HEAD_DIM, 1) * ng_ref[...]


def ssd_sample_step(h, x_col, z_col, bmat, cmat, dt_raw, lw, layer):
    bd = x_col.shape[0]
    col = pl.BlockSpec((None, SSM_HEADS, HEAD_DIM, 1), lambda b: (b, 0, 0, 0))
    grp = pl.BlockSpec((None, SSM_GROUPS, 1, SSM_STATE), lambda b: (b, 0, 0, 0))
    hd1 = pl.BlockSpec((SSM_HEADS, 1, 1), lambda b: (0, 0, 0))
    return pl.pallas_call(
        _ssd_sample_step_kernel,
        grid=(bd,),
        in_specs=[pl.BlockSpec((None, None, SSM_HEADS, HEAD_DIM, SSM_STATE), lambda b: (layer, b, 0, 0, 0)),
                  col, col, grp, grp, pl.BlockSpec((None, SSM_HEADS, 1, 1), lambda b: (b, 0, 0, 0)), hd1, hd1, hd1,
                  pl.BlockSpec((SSM_HEADS, HEAD_DIM, 1), lambda b: (0, 0, 0))],
        out_specs=[col, pl.BlockSpec((None, SSM_HEADS, HEAD_DIM, SSM_STATE), lambda b: (b, 0, 0, 0))],
        out_shape=[jax.ShapeDtypeStruct((bd, SSM_HEADS, HEAD_DIM, 1), F32),
                   jax.ShapeDtypeStruct((bd, SSM_HEADS, HEAD_DIM, SSM_STATE), F32)],
        compiler_params=_cparams(("parallel",)),
        name="ssd_sample_step",
    )(h, x_col, z_col, bmat, cmat, dt_raw, lw["dt_bias_h"], lw["a_log_h"], lw["d_h"], lw["ssm_norm_g_col"])


def _rwkv_mix(i, tm, x_refs, h_refs, i_refs, mu_refs, w0_ref, wup_ref, a0_ref, aup_ref, gup_ref):
    def shift(x_ref, h_ref, i_ref, mu_ref):
        x = x_ref[...]
        hrows = h_ref.shape[0]
        prev0 = jnp.where(i == 0, i_ref[...], h_ref[hrows - 1:hrows, :])
        if tm == 1:
            xp = prev0
        else:
            row = lax.broadcasted_iota(jnp.int32, (tm, 1), 0)
            xp = jnp.where(row == 0, prev0, pltpu.roll(x, 1, 0))
        return x + mu_ref[...] * (xp - x)

    ur, uk, uv, ul = (shift(*refs) for refs in zip(x_refs, h_refs, i_refs, mu_refs))
    f = lambda a, w_ref: jnp.dot(a.astype(BF16), w_ref[...], preferred_element_type=F32)
    w_log = -_softplus(-(w0_ref[...] + f(jnp.tanh(ul[:, 0:LANES]), wup_ref))) - 0.5
    a = _sigmoid(a0_ref[...] + f(ul[:, LANES:2 * LANES], aup_ref))
    g = f(_sigmoid(ul[:, 2 * LANES:4 * LANES]), gup_ref)
    return ur, uk, uv, -jnp.exp(w_log), a, g


def _rwkv_prep_kernel(*refs, tm):
    x_refs, h_refs, i_refs, mu_refs = refs[0:4], refs[4:8], refs[8:12], refs[12:16]
    w0_ref, wup_ref, a0_ref, aup_ref, gup_ref, kk_ref, ka_ref = refs[16:23]
    or_ref, old_ref, ok_ref, ov_ref, okk_ref, oa_ref, og_ref = refs[23:30]
    ur, uk, uv, ld, a, g = _rwkv_mix(pl.program_id(1), tm, x_refs, h_refs, i_refs, mu_refs,
                                     w0_ref, wup_ref, a0_ref, aup_ref, gup_ref)
    or_ref[...] = ur
    old_ref[...] = ld
    ok_ref[...] = uk * (1.0 + (a - 1.0) * ka_ref[...])
    ov_ref[...] = uv
    okk_ref[...] = uk * kk_ref[...]
    oa_ref[...] = a
    og_ref[...] = g


def _rwkv_prep_prompt_kernel(*refs, tm):
    x_refs, h_refs, i_refs, mu_refs = refs[0:4], refs[4:8], refs[8:12], refs[12:16]
    w0_ref, wup_ref, a0_ref, aup_ref, gup_ref, kk_ref, ka_ref, rk_ref = refs[16:24]
    opt_ref, ort_ref, oqh_ref, okh_ref, ov_ref, obv_ref, og_ref, oel_ref = refs[24:32]
    L = RWKV_CHUNK
    ur, uk, uv, ld, a, g = _rwkv_mix(pl.program_id(1), tm, x_refs, h_refs, i_refs, mu_refs,
                                     w0_ref, wup_ref, a0_ref, aup_ref, gup_ref)
    kmod = uk * (1.0 + (a - 1.0) * ka_ref[...])
    kkraw = uk * kk_ref[...]
    assert tm == 2 * L
    ri = lax.broadcasted_iota(jnp.int32, (tm, tm), 0)
    ci = lax.broadcasted_iota(jnp.int32, (tm, tm), 1)
    tril = ((ri >= ci) & ((ri < L) == (ci < L))).astype(BF16)
    cum = _dot_exact_lhs(tril, ld)
    e_in = jnp.exp(cum)
    e_inv = jnp.exp(-cum)
    e_ex = jnp.exp(cum - ld)
    ort_ref[...] = (ur * e_in).astype(ort_ref.dtype)
    okh_ref[...] = (kmod * e_inv).astype(okh_ref.dtype)
    ov_ref[...] = uv.astype(ov_ref.dtype)
    og_ref[...] = g.astype(og_ref.dtype)
    for c in range(tm // L):
        oel_ref[c] = e_in[c * L + L - 1:c * L + L, :]
    li = lax.broadcasted_iota(jnp.int32, (LANES, LANES), 0)
    lj = lax.broadcasted_iota(jnp.int32, (LANES, LANES), 1)
    bd_ones = ((li < HEAD_DIM) == (lj < HEAD_DIM)).astype(BF16)
    rkr = ur * kmod * rk_ref[...]
    for p in range(N_PAIRS):
        cs = slice(p * LANES, (p + 1) * LANES)
        kkp = kkraw[:, cs]
        kkn = kkp / jnp.maximum(jnp.sqrt(_dot_exact_rhs(kkp * kkp, bd_ones)), 1e-12)
        opt_ref[:, cs] = (-kkn * e_ex[:, cs]).astype(opt_ref.dtype)
        oqh_ref[:, cs] = (kkn * a[:, cs] * e_inv[:, cs]).astype(oqh_ref.dtype)
        obv_ref[:, cs] = (_dot_exact_rhs(rkr[:, cs], bd_ones) * uv[:, cs]).astype(obv_ref.dtype)


def _rwkv_prep_call(kernel_fn, proj, init, params, bsz, t, tm, out_specs, out_shape, name):
    nt = t // tm
    hrows = SUBLANES if t >= SUBLANES else t
    hper = tm // hrows
    proj = proj.reshape(bsz, t, N_PROJ)
    blk = lambda width, c0: pl.BlockSpec((None, tm, width), lambda b, i: (b, i, c0 // width))
    halo = lambda width, c0: pl.BlockSpec(
        (None, hrows, width), lambda b, i: (b, jnp.maximum(i * hper - 1, 0), c0 // width))
    ini = lambda width: pl.BlockSpec((None, 1, width), lambda b, i: (b, 0, 0))
    par = lambda a: pl.BlockSpec(a.shape, lambda b, i: (0,) * a.ndim)
    secs = ((BRANCH_W, C_R), (BRANCH_W, C_RK), (BRANCH_W, C_RV), (512, C_L))
    return pl.pallas_call(
        functools.partial(kernel_fn, tm=tm),
        grid=(bsz, nt),
        in_specs=[blk(*s) for s in secs] + [halo(*s) for s in secs] + [ini(s[0]) for s in secs]
        + [par(a) for a in params],
        out_specs=out_specs,
        out_shape=out_shape,
        compiler_params=_cparams(("parallel", "arbitrary")),
        name=name,
    )(proj, proj, proj, proj, proj, proj, proj, proj, *init, *params)


def _rwkv_prep_params(lw):
    return [lw["mu_r"], lw["mu_k"], lw["mu_v"], lw["mu_l"], lw["w0"], lw["w_up"], lw["a0"], lw["a_up"], lw["g_up"],
            lw["k_k"], lw["k_a"]]


def rwkv_prep(proj, init, lw, bsz, t):
    tm = min(t, 128)
    out = pl.BlockSpec((None, tm, BRANCH_W), lambda b, i: (b, i, 0))
    shp = jax.ShapeDtypeStruct((bsz, t, BRANCH_W), F32)
    outs = _rwkv_prep_call(_rwkv_prep_kernel, proj, init, _rwkv_prep_params(lw), bsz, t, tm, [out] * 7, [shp] * 7,
                           "rwkv_prep")
    return [o.reshape(bsz * t, BRANCH_W) for o in outs]


def rwkv_prep_prompt(proj, init, lw, bsz, t):
    tm = 128
    nch = tm // RWKV_CHUNK
    out = pl.BlockSpec((None, tm, BRANCH_W), lambda b, i: (b, i, 0))
    shp = jax.ShapeDtypeStruct((bsz, t, BRANCH_W), BF16)
    el_spec = pl.BlockSpec((None, nch, 1, BRANCH_W), lambda b, i: (b, i, 0, 0))
    el_shape = jax.ShapeDtypeStruct((bsz, t // RWKV_CHUNK, 1, BRANCH_W), F32)
    outs = _rwkv_prep_call(_rwkv_prep_prompt_kernel, proj, init, _rwkv_prep_params(lw) + [lw["r_k_row"]], bsz, t, tm,
                           [out] * 7 + [el_spec], [shp] * 7 + [el_shape], "rwkv_prep_prompt")
    return [o.reshape(bsz * t, BRANCH_W) for o in outs[:7]] + [outs[7].reshape(bsz * t // RWKV_CHUNK, 1, BRANCH_W)]


def _rwkv_chain_kernel(pt_ref, rt_ref, qh_ref, kh_ref, v_ref, bv_ref, g_ref, el_ref, lng_ref, lnb_ref, o_ref, sout_ref,
                       s_scr, *, nchunk, nt, npair):
    tstep = pl.program_id(2)
    L = RWKV_CHUNK
    L2 = 2 * L

    @pl.when(tstep == 0)
    def _():
        s_scr[...] = jnp.zeros_like(s_scr)

    lane = lax.broadcasted_iota(jnp.int32, (1, LANES), 1)
    head0 = lane < HEAD_DIM
    ri = lax.broadcasted_iota(jnp.int32, (L2, L2), 0)
    ci = lax.broadcasted_iota(jnp.int32, (L2, L2), 1)
    same = (ri < L) == (ci < L)
    strict = same & (ri > ci)
    incl = same & (ri >= ci)
    eye = (ri == ci).astype(F32)
    bd_ones = same.astype(BF16)
    bf = lambda x: x.astype(BF16)
    dot = lambda a, b: jnp.dot(bf(a), bf(b), preferred_element_type=F32)
    zero = jnp.zeros((), BF16)

    def stack(x):
        return jnp.concatenate([jnp.where(head0, x, zero), jnp.where(head0, zero, x)], axis=0)

    def exact_sum(x):
        hi = bf(x)
        lo = bf(x - hi.astype(F32))
        return (jnp.dot(hi, bd_ones, preferred_element_type=F32) + jnp.dot(lo, bd_ones, preferred_element_type=F32))

    probs = [(c, p) for c in range(nchunk) for p in range(npair)]
    rows = lambda c: pl.ds(c * L, L)
    cols = lambda p: slice(p * LANES, (p + 1) * LANES)
    pm = {k: stack(pt_ref[rows(k[0]), cols(k[1])]) for k in probs}
    rm = {k: stack(rt_ref[rows(k[0]), cols(k[1])]) for k in probs}
    qk = {k: jnp.concatenate([stack(qh_ref[rows(k[0]), cols(k[1])]), stack(kh_ref[rows(k[0]), cols(k[1])])], axis=0)
          for k in probs}
    vm = {k: stack(v_ref[rows(k[0]), cols(k[1])]) for k in probs}
    x = {k: _dot_nt(jnp.concatenate([pm[k], rm[k]], axis=0), qk[k]) for k in probs}
    n_pow = {k: jnp.where(strict, x[k][0:L2, 0:L2], 0.0) for k in probs}
    a_kp = {k: bf(jnp.where(strict, x[k][0:L2, L2:2 * L2], 0.0)) for k in probs}
    r_q = {k: bf(jnp.where(incl, x[k][L2:2 * L2, 0:L2], 0.0)) for k in probs}
    r_k = {k: bf(jnp.where(incl, x[k][L2:2 * L2, L2:2 * L2], 0.0)) for k in probs}
    t_inv = {k: eye + n_pow[k] for k in probs}
    for _ in range(5):
        n_pow = {k: dot(n_pow[k], n_pow[k]) for k in probs}
        t_inv = {k: t_inv[k] + dot(n_pow[k], t_inv[k]) for k in probs}
    t_inv = {k: bf(t_inv[k]) for k in probs}
    av = {k: dot(a_kp[k], vm[k]) for k in probs}
    rkv = {k: dot(r_k[k], vm[k]) for k in probs}

    state = [s_scr[p] for p in range(npair)]
    ys = {}
    for c in range(nchunk):
        ks = [(c, p) for p in range(npair)]
        s0b = {k: bf(state[k[1]]) for k in ks}
        rhs = {k: dot(pm[k], s0b[k]) + av[k] for k in ks}
        rs = {k: dot(rm[k], s0b[k]) + rkv[k] for k in ks}
        u = {k: bf(dot(t_inv[k], rhs[k])) for k in ks}
        upd = {k: _dot_tn(qk[k], jnp.concatenate([u[k], vm[k]], axis=0)) for k in ks}
        for k in ks:
            ys[k] = rs[k] + dot(r_q[k], u[k])
            w_col = jnp.sum(eye * el_ref[c, :, cols(k[1])], axis=1, keepdims=True)
            state[k[1]] = w_col * (state[k[1]] + upd[k])
    for p in range(npair):
        s_scr[p] = state[p]

    y = {k: ys[k][0:L, :] + ys[k][L:L2, :] for k in probs}
    yc = {k: y[k] - exact_sum(y[k]) * (1.0 / HEAD_DIM) for k in probs}
    var = {k: exact_sum(yc[k] * yc[k]) * (1.0 / HEAD_DIM) for k in probs}
    for k in probs:
        cs = cols(k[1])
        yn = yc[k] * lax.rsqrt(var[k] + RWKV_LN_EPS) * lng_ref[:, cs] + lnb_ref[:, cs]
        o_ref[rows(k[0]), cs] = ((yn + bv_ref[rows(k[0]), cs].astype(F32)) * g_ref[rows(k[0]), cs].astype(F32)
                                 ).astype(o_ref.dtype)

    @pl.when(tstep == nt - 1)
    def _():
        sout_ref[...] = s_scr[...]


def rwkv_prompt(parts, lw, bsz, t, npair=6, nchunk=2):
    tb = nchunk * RWKV_CHUNK
    nt = t // tb
    wl = npair * LANES
    blk = pl.BlockSpec((tb, wl), lambda b, p, i: (b * nt + i, p))
    par = pl.BlockSpec((1, wl), lambda b, p, i: (0, p))
    el = pl.BlockSpec((nchunk, 1, wl), lambda b, p, i: (b * nt + i, 0, p))
    return pl.pallas_call(
        functools.partial(_rwkv_chain_kernel, nchunk=nchunk, nt=nt, npair=npair),
        grid=(bsz, N_PAIRS // npair, nt),
        in_specs=[blk] * 7 + [el, par, par],
        out_specs=[blk, pl.BlockSpec((None, npair, LANES, LANES), lambda b, p, i: (b, p, 0, 0))],
        out_shape=[jax.ShapeDtypeStruct((bsz * t, BRANCH_W), BF16),
                   jax.ShapeDtypeStruct((bsz, N_PAIRS, LANES, LANES), F32)],
        scratch_shapes=[pltpu.VMEM((npair, LANES, LANES), F32)],
        compiler_params=_cparams(("parallel", "parallel", "arbitrary")),
        name="rwkv_prompt",
    )(*parts, lw["ln_g_row"], lw["ln_b_row"])


def _rwkv_sample_step_kernel(s_ref, r_ref, ld_ref, k_ref, kk_ref, a_ref, v_ref, g_ref, rk_ref, lng_ref, lnb_ref,
                             o_ref, sout_ref):
    s = s_ref[...]
    r, kmod, kkraw, a = r_ref[...], k_ref[...], kk_ref[...], a_ref[...]
    v = v_ref[...]
    kkn = kkraw / jnp.maximum(jnp.sqrt(jnp.sum(kkraw * kkraw, axis=-1, keepdims=True)), 1e-12)
    sa = jnp.sum(s * (-kkn), axis=-1, keepdims=True)
    sn = s * jnp.exp(ld_ref[...]) + sa * (kkn * a) + v * kmod
    sout_ref[...] = sn
    y = jnp.sum(sn * r, axis=-1, keepdims=True)
    mean = jnp.mean(y, axis=1, keepdims=True)
    yc = y - mean
    var = jnp.mean(yc * yc, axis=1, keepdims=True)
    yn = yc * lax.rsqrt(var + RWKV_LN_EPS) * lng_ref[...] + lnb_ref[...]
    bonus = jnp.sum(r * kmod * rk_ref[...], axis=-1, keepdims=True) * v
    o_ref[...] = (yn + bonus) * g_ref[...]


def rwkv_sample_step(state, rows, cols, lw, layer):
    bd = rows[0].shape[0]
    row = pl.BlockSpec((None, RWKV_HEADS, 1, HEAD_DIM), lambda b: (b, 0, 0, 0))
    col = pl.BlockSpec((None, RWKV_HEADS, HEAD_DIM, 1), lambda b: (b, 0, 0, 0))
    prow = pl.BlockSpec((RWKV_HEADS, 1, HEAD_DIM), lambda b: (0, 0, 0))
    pcol = pl.BlockSpec((RWKV_HEADS, HEAD_DIM, 1), lambda b: (0, 0, 0))
    return pl.pallas_call(
        _rwkv_sample_step_kernel,
        grid=(bd,),
        in_specs=[pl.BlockSpec((None, None, RWKV_HEADS, HEAD_DIM, HEAD_DIM), lambda b: (layer, b, 0, 0, 0))]
        + [row] * 5 + [col] * 2 + [prow, pcol, pcol],
        out_specs=[col, pl.BlockSpec((None, RWKV_HEADS, HEAD_DIM, HEAD_DIM), lambda b: (b, 0, 0, 0))],
        out_shape=[jax.ShapeDtypeStruct((bd, RWKV_HEADS, HEAD_DIM, 1), F32),
                   jax.ShapeDtypeStruct((bd, RWKV_HEADS, HEAD_DIM, HEAD_DIM), F32)],
        compiler_params=_cparams(("parallel",)),
        name="rwkv_sample_step",
    )(state, *rows, *cols, lw["r_k_h"], lw["ln_g_col"], lw["ln_b_col"])


def _pack_layer(l, p):
    w_in = p["w_in"][l]
    rw = _O_RW
    lo = rw + 3 * BRANCH_W
    zeros = lambda n: jnp.zeros((D_MODEL, n), F32)
    w_proj = jnp.concatenate([
        w_in[:, _O_QKV:_O_Z], w_in[:, _O_Z:_O_XBC], w_in[:, _O_XBC:_O_XBC + BRANCH_W], w_in[:, rw:lo],
        w_in[:, _O_GATE:], w_in[:, _O_XBC + BRANCH_W:_O_DT],
        w_in[:, lo:lo + LORA_W], w_in[:, _O_DT:_O_RW], zeros(8), w_in[:, lo + LORA_W:lo + LORA_W + LORA_A], zeros(32),
        w_in[:, lo + LORA_W + LORA_A:_O_GATE]], axis=1).astype(BF16)
    row = lambda a: a.reshape(1, -1)
    pad_rows = lambda a, n: jnp.concatenate([a, jnp.zeros((n - a.shape[0], a.shape[1]), a.dtype)], axis=0).astype(BF16)
    lane_pad = lambda a: jnp.zeros((1, LANES), F32).at[0, DT_LANE0:DT_LANE0 + SSM_HEADS].set(a)
    mu = p["rwkv_mu"][l]
    z1 = lambda n: jnp.zeros((n,), F32)
    cw = p["ssm_conv_w"][l]
    cb = p["ssm_conv_b"][l]
    rep64 = lambda a: jnp.repeat(a, HEAD_DIM)
    return dict(
        ln1_g=p["ln1_g"][l], ln2_g=p["ln2_g"][l], w_proj=w_proj,
        cw_x=cw[:, :BRANCH_W], cw_b=cw[:, BRANCH_W:BRANCH_W + 512], cw_c=cw[:, BRANCH_W + 512:],
        cb_x=row(cb[:BRANCH_W]), cb_b=row(cb[BRANCH_W:BRANCH_W + 512]), cb_c=row(cb[BRANCH_W + 512:]),
        conv_w=cw, conv_b=cb,
        dt_bias=lane_pad(p["ssm_dt_bias"][l]), a_neg=lane_pad(-jnp.exp(p["ssm_a_log"][l])),
        d_skip=row(rep64(p["ssm_d"][l])), ssm_norm_g=row(p["ssm_norm_g"][l]),
        dt_bias_h=p["ssm_dt_bias"][l].reshape(SSM_HEADS, 1, 1), a_log_h=p["ssm_a_log"][l].reshape(SSM_HEADS, 1, 1),
        d_h=p["ssm_d"][l].reshape(SSM_HEADS, 1, 1), ssm_norm_g_col=p["ssm_norm_g"][l].reshape(SSM_HEADS, HEAD_DIM, 1),
        mu_r=row(mu[:BRANCH_W]), mu_k=row(mu[BRANCH_W:2 * BRANCH_W]), mu_v=row(mu[2 * BRANCH_W:3 * BRANCH_W]),
        mu_l=row(jnp.concatenate([mu[3 * BRANCH_W:3 * BRANCH_W + LORA_W], z1(32),
                                  mu[3 * BRANCH_W + LORA_W:3 * BRANCH_W + LORA_W + LORA_A], z1(32),
                                  mu[3 * BRANCH_W + LORA_W + LORA_A:]])),
        w0=row(p["rwkv_w0"][l]), w_up=pad_rows(p["rwkv_w_up"][l], LANES), a0=row(p["rwkv_a0"][l]),
        a_up=pad_rows(p["rwkv_a_up"][l], LANES), g_up=p["rwkv_g_up"][l].astype(BF16),
        k_k=row(p["rwkv_k_k"][l]), k_a=row(p["rwkv_k_a"][l]),
        r_k_row=row(p["rwkv_r_k"][l]), ln_g_row=row(p["rwkv_ln_g"][l]), ln_b_row=row(p["rwkv_ln_b"][l]),
        r_k_h=p["rwkv_r_k"][l].reshape(RWKV_HEADS, 1, HEAD_DIM),
        ln_g_col=p["rwkv_ln_g"][l].reshape(RWKV_HEADS, HEAD_DIM, 1),
        ln_b_col=p["rwkv_ln_b"][l].reshape(RWKV_HEADS, HEAD_DIM, 1),
        w_att=p["w_branch_att"][l].astype(BF16), w_ssm=p["w_branch_ssm"][l].astype(BF16),
        w_rwkv=p["w_branch_rwkv"][l].astype(BF16), w_out=p["w_out"][l].astype(BF16),
        w_up_ff=p["w_up"][l].astype(BF16), w_down_ff=p["w_down"][l].astype(BF16),
    )


def _unpack_lora(rows):
    return jnp.concatenate([rows[..., 0:LORA_W], rows[..., LANES:LANES + LORA_A], rows[..., 2 * LANES:]], axis=-1)


def _pack_lora(rows):
    z = lambda n: jnp.zeros(rows.shape[:-1] + (n,), rows.dtype)
    return jnp.concatenate([rows[..., 0:LORA_W], z(32), rows[..., LORA_W:LORA_W + LORA_A], z(32),
                            rows[..., LORA_W + LORA_A:]], axis=-1)


def _alibi_slopes():
    n = 3 * ATT_HEADS
    idx = jnp.arange(1, n + 1, dtype=F32)
    return jnp.exp2(-8.0 * idx / n).reshape(3, ATT_HEADS)


def _finish_layer(x, proj, o_att, o_ssm, o_rwkv, lw):
    merged = merge_branches(o_att, o_ssm, o_rwkv, lw["w_att"], lw["w_ssm"], lw["w_rwkv"], proj)
    x = matmul(merged, lw["w_out"], epi="res", res=x)
    h2 = rmsnorm(x, lw["ln2_g"], BF16)
    up = matmul(h2, lw["w_up_ff"], epi="relu2", out_dtype=BF16)
    return matmul(up, lw["w_down_ff"], epi="res", res=x)


def _prompt_layer(x, lw, slopes, bsz, t):
    h = rmsnorm(x, lw["ln1_g"], BF16)
    proj = matmul(h, lw["w_proj"])
    parts = []
    for g in range(3):
        parts += att_prompt_group(proj, slopes, g, bsz, t)
    o_att = att_combine(parts)
    o_ssm, h_fin = ssd_prompt(proj, lw, bsz, t)
    zero_init = [jnp.zeros((bsz, 1, w), F32) for w in (BRANCH_W, BRANCH_W, BRANCH_W, 512)]
    rw = rwkv_prep_prompt(proj, zero_init, lw, bsz, t)
    o_rwkv, s_fin = rwkv_prompt(rw, lw, bsz, t)
    x = _finish_layer(x, proj, o_att, o_ssm, o_rwkv, lw)

    p3 = proj.reshape(bsz, t, N_PROJ)
    kv = []
    for g in range(3):
        win = ATT_WINDOWS[g]
        for c0 in (C_K, C_V):
            kv.append(p3[:, t - win:, c0 + g * ATT_OUT:c0 + (g + 1) * ATT_OUT].reshape(bsz, win, ATT_HEADS, HEAD_DIM))
    tail = p3[:, t - (SSM_CONV - 1):]
    conv_new = jnp.concatenate([tail[..., C_X:C_X + BRANCH_W], tail[..., C_B:C_B + 512], tail[..., C_C:C_C + 512]], -1)
    last = p3[:, t - 1]
    shift_new = jnp.concatenate([last[:, C_R:C_R + 3 * BRANCH_W], _unpack_lora(last[:, C_L:C_L + 512])], axis=-1)
    ssm_new = h_fin.reshape(bsz, SSM_HEADS, HEAD_DIM, SSM_STATE)
    s6 = s_fin.reshape(bsz, N_PAIRS, 2, HEAD_DIM, 2, HEAD_DIM)
    s_heads = jnp.stack([s6[:, :, 0, :, 0, :], s6[:, :, 1, :, 1, :]], axis=2)
    rwkv_new = jnp.swapaxes(s_heads, -1, -2).reshape(bsz, RWKV_HEADS, HEAD_DIM, HEAD_DIM)
    return x, tuple(kv) + (conv_new, ssm_new, shift_new, rwkv_new)


def _sample_layer(x, lw, slopes, st, layer, bd):
    (caches, conv_st, ssm_st, shift_st, rwkv_st) = st
    h = rmsnorm(x, lw["ln1_g"], BF16)
    proj = matmul(h, lw["w_proj"])
    qkv = proj[:, :3 * BRANCH_W].reshape(bd, 3, 3, ATT_HEADS, HEAD_DIM)
    o_att = att_sample(qkv, caches, slopes, layer).reshape(bd, ATT_OUT).astype(BF16)
    new_caches = []
    for g in range(3):
        for ci, part in ((2 * g, 1), (2 * g + 1, 2)):
            new_caches.append(jnp.concatenate([caches[ci][layer][:, 1:], qkv[:, part, g][:, None]], axis=1))

    xbc_new = jnp.concatenate([proj[:, C_X:C_X + BRANCH_W], proj[:, C_B:C_B + 512], proj[:, C_C:C_C + 512]], axis=-1)
    cst = conv_st[layer]
    xc = ssd_sample_conv(xbc_new, jnp.swapaxes(cst, 0, 1), lw["conv_w"], lw["conv_b"])
    conv_new = jnp.concatenate([cst[:, 1:], xbc_new[:, None]], axis=1)
    x_col = xc[:, :BRANCH_W].reshape(bd, SSM_HEADS, HEAD_DIM, 1)
    z_col = proj[:, C_Z:C_Z + BRANCH_W].reshape(bd, SSM_HEADS, HEAD_DIM, 1)
    bmat = xc[:, BRANCH_W:BRANCH_W + 512].reshape(bd, SSM_GROUPS, 1, SSM_STATE)
    cmat = xc[:, BRANCH_W + 512:].reshape(bd, SSM_GROUPS, 1, SSM_STATE)
    dt_raw = proj[:, C_L + DT_LANE0:C_L + DT_LANE0 + SSM_HEADS].reshape(bd, SSM_HEADS, 1, 1)
    y_col, ssm_new = ssd_sample_step(ssm_st, x_col, z_col, bmat, cmat, dt_raw, lw, layer)
    o_ssm = y_col.reshape(bd, BRANCH_W).astype(BF16)

    sh = shift_st[layer]
    init = [sh[:, None, i * BRANCH_W:(i + 1) * BRANCH_W] for i in range(3)] + [_pack_lora(sh[:, None, 3 * BRANCH_W:])]
    r, ld, kmod, v, kkraw, a, gg = rwkv_prep(proj, init, lw, bd, 1)
    rowf = lambda z: z.reshape(bd, RWKV_HEADS, 1, HEAD_DIM)
    colf = lambda z: z.reshape(bd, RWKV_HEADS, HEAD_DIM, 1)
    o_col, rwkv_new = rwkv_sample_step(rwkv_st, [rowf(r), rowf(ld), rowf(kmod), rowf(kkraw), rowf(a)],
                                       [colf(v), colf(gg)], lw, layer)
    o_rwkv = o_col.reshape(bd, BRANCH_W).astype(BF16)
    shift_new = jnp.concatenate([proj[:, C_R:C_R + 3 * BRANCH_W], _unpack_lora(proj[:, C_L:C_L + 512])], axis=-1)

    x = _finish_layer(x, proj, o_att, o_ssm, o_rwkv, lw)
    return x, tuple(new_caches) + (conv_new, ssm_new, shift_new, rwkv_new)


def kernel(x_prompt, x_sample, cache_att_k0, cache_att_v0, cache_att_k1, cache_att_v1, cache_att_k2, cache_att_v2, state_ssm_conv, state_ssm, state_rwkv_shift, state_rwkv, ln1_g, w_in, ssm_conv_w, ssm_conv_b, ssm_dt_bias, ssm_a_log, ssm_d, ssm_norm_g, rwkv_mu, rwkv_w0, rwkv_w_up, rwkv_a0, rwkv_a_up, rwkv_g_up, rwkv_k_k, rwkv_k_a, rwkv_r_k, rwkv_ln_g, rwkv_ln_b, w_branch_att, w_branch_ssm, w_branch_rwkv, w_out, ln2_g, w_up, w_down, final_g):
    p = dict(ln1_g=ln1_g, w_in=w_in, ssm_conv_w=ssm_conv_w, ssm_conv_b=ssm_conv_b, ssm_dt_bias=ssm_dt_bias,
             ssm_a_log=ssm_a_log, ssm_d=ssm_d, ssm_norm_g=ssm_norm_g, rwkv_mu=rwkv_mu, rwkv_w0=rwkv_w0,
             rwkv_w_up=rwkv_w_up, rwkv_a0=rwkv_a0, rwkv_a_up=rwkv_a_up, rwkv_g_up=rwkv_g_up, rwkv_k_k=rwkv_k_k,
             rwkv_k_a=rwkv_k_a, rwkv_r_k=rwkv_r_k, rwkv_ln_g=rwkv_ln_g, rwkv_ln_b=rwkv_ln_b,
             w_branch_att=w_branch_att, w_branch_ssm=w_branch_ssm, w_branch_rwkv=w_branch_rwkv, w_out=w_out,
             ln2_g=ln2_g, w_up=w_up, w_down=w_down)
    depth = w_in.shape[0]
    bsz, t, _ = x_prompt.shape
    bd = x_sample.shape[0]
    slopes = _alibi_slopes()
    caches = (cache_att_k0, cache_att_v0, cache_att_k1, cache_att_v1, cache_att_k2, cache_att_v2)
    st = (caches, state_ssm_conv, state_ssm, state_rwkv_shift, state_rwkv)
    xp = x_prompt.reshape(bsz * t, D_MODEL)
    xs = x_sample.reshape(bd, D_MODEL)
    p_new, s_new = [], []
    for l in range(depth):
        lw = _pack_layer(l, p)
        xp, sp = _prompt_layer(xp, lw, slopes, bsz, t)
        xs, ss = _sample_layer(xs, lw, slopes, st, l, bd)
        p_new.append(sp)
        s_new.append(ss)
    y_prompt = rmsnorm(xp, final_g, F32).reshape(bsz, t, D_MODEL)
    y_sample = rmsnorm(xs, final_g, F32).reshape(bd, 1, D_MODEL)
    stack = lambda per_layer: tuple(jnp.stack([s[i] for s in per_layer]) for i in range(10))
    return (y_prompt, y_sample) + stack(p_new) + stack(s_new)
```

```python
import functools

import jax
import jax.numpy as jnp
from jax import lax
from jax.experimental import pallas as pl
from jax.experimental.pallas import tpu as pltpu

F32 = jnp.float32
BF16 = jnp.bfloat16

D_MODEL = 2048
HEAD_DIM = 64
BRANCH_W = 1536
NORM_EPS = 1e-5
ATT_WINDOWS = (128, 512, 2048)
ATT_DILS = (1, 4, 16)
ATT_SPAN = 128
ATT_HEADS = 8
ATT_OUT = ATT_HEADS * HEAD_DIM
ATT_SCALE = HEAD_DIM ** -0.5
SSM_HEADS = 24
SSM_GROUPS = 4
SSM_STATE = 128
SSM_CONV = 4
SSM_CHUNK = 128
RWKV_HEADS = 24
LORA_W = 96
LORA_A = 96
LORA_G = 256
RWKV_LN_EPS = 64e-5
D_FF = 4 * D_MODEL
RWKV_CHUNK = 64
N_PAIRS = BRANCH_W // 128

LANES = 128
SUBLANES = 8
VMEM_LIMIT_BYTES = 56 * 1024 * 1024

C_Q, C_K, C_V, C_Z, C_X, C_R, C_RK, C_RV = (i * BRANCH_W for i in range(8))
C_G = 8 * BRANCH_W
C_B = C_G + 3 * D_MODEL
C_C = C_B + 512
C_L = C_C + 512
N_PROJ = C_L + 512
DT_LANE0 = LORA_W

_O_QKV, _O_Z, _O_XBC, _O_DT, _O_RW, _O_GATE = 0, 4608, 6144, 8704, 8728, 13784


def _cparams(sem):
    return pltpu.CompilerParams(dimension_semantics=sem, vmem_limit_bytes=VMEM_LIMIT_BYTES)


def _softplus(x):
    return jnp.maximum(x, 0.0) + jnp.log1p(jnp.exp(-jnp.abs(x)))


def _sigmoid(x):
    return 1.0 / (1.0 + jnp.exp(-x))


def _silu(x):
    return x * _sigmoid(x)


def _split(x, terms):
    out = []
    for _ in range(terms - 1):
        hi = x.astype(BF16)
        out.append(hi)
        x = x - hi.astype(F32)
    out.append(x.astype(BF16))
    return out


def _dot_exact_rhs(x, mat, terms=3):
    parts = [jnp.dot(t, mat, preferred_element_type=F32) for t in _split(x, terms)]
    return functools.reduce(lambda a, b: a + b, parts)


def _dot_exact_lhs(mat, x, terms=3):
    parts = [jnp.dot(mat, t, preferred_element_type=F32) for t in _split(x, terms)]
    return functools.reduce(lambda a, b: a + b, parts)


def _dot_nt(a, b):
    return lax.dot_general(a, b, (((1,), (1,)), ((), ())), preferred_element_type=F32)


def _dot_tn(a, b):
    return lax.dot_general(a, b, (((0,), (0,)), ((), ())), preferred_element_type=F32)


def _rmsnorm_kernel(x_ref, g_ref, o_ref):
    x = x_ref[...]
    y = x * lax.rsqrt(jnp.mean(x * x, axis=-1, keepdims=True) + NORM_EPS)
    o_ref[...] = (y * g_ref[...]).astype(o_ref.dtype)


def rmsnorm(x, g, out_dtype):
    m, d = x.shape
    tm = min(m, 512)
    return pl.pallas_call(
        _rmsnorm_kernel,
        grid=(m // tm,),
        in_specs=[pl.BlockSpec((tm, d), lambda i: (i, 0)), pl.BlockSpec((1, d), lambda i: (0, 0))],
        out_specs=pl.BlockSpec((tm, d), lambda i: (i, 0)),
        out_shape=jax.ShapeDtypeStruct((m, d), out_dtype),
        compiler_params=_cparams(("parallel",)),
        name="rmsnorm",
    )(x, g.reshape(1, d))


def _mm_kernel(*refs, nk, epi):
    if epi == "res":
        x_ref, w_ref, r_ref, o_ref = refs[:4]
    else:
        x_ref, w_ref, o_ref = refs[:3]
        r_ref = None

    def finish(a):
        if epi == "res":
            a = r_ref[...] + a
        elif epi == "relu2":
            a = jnp.square(jnp.maximum(a, 0.0))
        o_ref[...] = a.astype(o_ref.dtype)

    part = jnp.dot(x_ref[...], w_ref[...], preferred_element_type=F32)
    if nk == 1:
        finish(part)
        return
    acc_ref = refs[-1]
    k = pl.program_id(2)

    @pl.when(k == 0)
    def _():
        acc_ref[...] = part

    @pl.when(k > 0)
    def _():
        acc_ref[...] += part

    @pl.when(k == nk - 1)
    def _():
        finish(acc_ref[...])


def _pick(n, cands):
    for c in cands:
        if n % c == 0:
            return c
    return n


def matmul(x, w, layer, *, epi="none", res=None, out_dtype=F32):
    m, kdim = x.shape
    n = w.shape[2]
    tm = min(m, 1024)
    tn = _pick(n, (1536, 1024, 512, 256, 128)) if m > 64 else _pick(n, (2048, 1536, 1024, 512, 256, 128))
    tk = min(kdim, 2048)
    nk = kdim // tk
    in_specs = [pl.BlockSpec((tm, tk), lambda i, j, k: (i, k)),
                pl.BlockSpec((None, tk, tn), lambda i, j, k: (layer, k, j))]
    args = [x, w]
    if epi == "res":
        in_specs.append(pl.BlockSpec((tm, tn), lambda i, j, k: (i, j)))
        args.append(res)
    scratch = [pltpu.VMEM((tm, tn), F32)] if nk > 1 else []
    return pl.pallas_call(
        functools.partial(_mm_kernel, nk=nk, epi=epi),
        grid=(m // tm, n // tn, nk),
        in_specs=in_specs,
        out_specs=pl.BlockSpec((tm, tn), lambda i, j, k: (i, j)),
        out_shape=jax.ShapeDtypeStruct((m, n), out_dtype),
        scratch_shapes=scratch,
        compiler_params=_cparams(("parallel", "parallel", "arbitrary")),
        name="matmul_" + epi,
    )(*args)


def _merge_kernel(oa_ref, os_ref, or_ref, wa_ref, ws_ref, wr_ref, ga_ref, gs_ref, gr_ref, o_ref):
    f = lambda a, b: jnp.dot(a[...], b[...], preferred_element_type=F32)
    acc = _sigmoid(ga_ref[...]) * f(oa_ref, wa_ref)
    acc = acc + _sigmoid(gs_ref[...]) * f(os_ref, ws_ref)
    acc = acc + _sigmoid(gr_ref[...]) * f(or_ref, wr_ref)
    o_ref[...] = acc.astype(o_ref.dtype)


def merge_branches(o_att, o_ssm, o_rwkv, wa, ws, wr, proj, layer):
    m = o_att.shape[0]
    tm = min(m, 512)
    tn = 1024
    gb = C_G // tn
    nj = D_MODEL // tn
    row = lambda width: pl.BlockSpec((tm, width), lambda j, i: (i, 0))
    col = lambda k: pl.BlockSpec((None, k, tn), lambda j, i: (layer, 0, j))
    gate = lambda g: pl.BlockSpec((tm, tn), lambda j, i, g=g: (i, gb + g * nj + j))
    return pl.pallas_call(
        _merge_kernel,
        grid=(nj, m // tm),
        in_specs=[row(ATT_OUT), row(BRANCH_W), row(BRANCH_W), col(ATT_OUT), col(BRANCH_W), col(BRANCH_W),
                  gate(0), gate(1), gate(2)],
        out_specs=pl.BlockSpec((tm, tn), lambda j, i: (i, j)),
        out_shape=jax.ShapeDtypeStruct((m, D_MODEL), BF16),
        compiler_params=_cparams(("parallel", "parallel")),
        name="merge_branches",
    )(o_att, o_ssm, o_rwkv, wa, ws, wr, proj, proj, proj)


def _att_prompt_kernel(sl_ref, q_ref, kc_ref, kp_ref, vc_ref, vp_ref, o_ref, lse_ref, *, d, nblk, g):
    j = pl.program_id(1)
    hp = pl.program_id(2)
    lane = lax.broadcasted_iota(jnp.int32, (1, LANES), 1)
    head0 = lane < HEAD_DIM
    qi = lax.broadcasted_iota(jnp.int32, (ATT_SPAN, 2 * ATT_SPAN), 0)
    kj = lax.broadcasted_iota(jnp.int32, (ATT_SPAN, 2 * ATT_SPAN), 1)
    delta = qi - kj + ATT_SPAN
    band = (delta >= 0) & (delta <= ATT_SPAN)
    first_ok = band & ((kj >= ATT_SPAN) | (j > 0))
    dist = (delta * d).astype(F32)
    slopes = (sl_ref[g, 2 * hp], sl_ref[g, 2 * hp + 1])

    def rows(base):
        return pl.ds(base, ATT_SPAN) if d == 1 else pl.ds(base, ATT_SPAN, stride=d)

    for ub in range(nblk):
        valid = first_ok if ub == 0 else band
        for r in range(d):
            base = ub * ATT_SPAN * d + r
            q = q_ref[rows(base), :]
            if ub == 0:
                kp, vp = kp_ref[rows(r), :], vp_ref[rows(r), :]
            else:
                kp, vp = kc_ref[rows(base - ATT_SPAN * d), :], vc_ref[rows(base - ATT_SPAN * d), :]
            k = jnp.concatenate([kp, kc_ref[rows(base), :]], axis=0).astype(BF16)
            v = jnp.concatenate([vp, vc_ref[rows(base), :]], axis=0).astype(BF16)
            outs, lses = [], []
            for e in range(2):
                sel = head0 if e == 0 else jnp.logical_not(head0)
                qe = jnp.where(sel, q, 0.0).astype(BF16)
                s = _dot_nt(qe, k) * ATT_SCALE
                s = jnp.where(valid, s - slopes[e] * dist, -jnp.inf)
                mx = jnp.max(s, axis=-1, keepdims=True)
                p = jnp.exp(s - mx)
                den = jnp.sum(p, axis=-1, keepdims=True)
                outs.append(jnp.dot(p.astype(BF16), v, preferred_element_type=F32) / den)
                lses.append(mx + jnp.log(den))
            o_ref[rows(base), :] = jnp.where(head0, outs[0], outs[1])
            lse_ref[rows(base), :] = jnp.where(head0, lses[0], lses[1])


def att_prompt_group(proj, slopes, g, bsz, t):
    d = ATT_DILS[g]
    sb = ATT_SPAN * d
    tb = max(sb, 512)
    nblk = tb // sb
    nt = t // tb
    cq, ck, cv = (C_Q + g * ATT_OUT) // LANES, (C_K + g * ATT_OUT) // LANES, (C_V + g * ATT_OUT) // LANES
    cur = lambda c0: pl.BlockSpec((tb, LANES), lambda b, j, h: (b * nt + j, c0 + h))
    prev = lambda c0: pl.BlockSpec((sb, LANES), lambda b, j, h: (jnp.maximum((b * nt + j) * nblk - 1, 0), c0 + h))
    out = pl.BlockSpec((tb, LANES), lambda b, j, h: (b * nt + j, h))
    shp = jax.ShapeDtypeStruct((bsz * t, ATT_OUT), F32)
    return pl.pallas_call(
        functools.partial(_att_prompt_kernel, d=d, nblk=nblk, g=g),
        grid=(bsz, nt, ATT_HEADS // 2),
        in_specs=[pl.BlockSpec(memory_space=pltpu.SMEM), cur(cq), cur(ck), prev(ck), cur(cv), prev(cv)],
        out_specs=[out, out],
        out_shape=[shp, shp],
        compiler_params=_cparams(("parallel", "parallel", "parallel")),
        name=f"att_prompt_g{g}",
    )(slopes, proj, proj, proj, proj, proj)


def _att_combine_kernel(o0, l0, o1, l1, o2, l2, out_ref):
    a, b, c = l0[...], l1[...], l2[...]
    mx = jnp.maximum(jnp.maximum(a, b), c)
    ea, eb, ec = jnp.exp(a - mx), jnp.exp(b - mx), jnp.exp(c - mx)
    num = ea * o0[...] + eb * o1[...] + ec * o2[...]
    out_ref[...] = (num / (ea + eb + ec)).astype(out_ref.dtype)


def att_combine(parts):
    m = parts[0].shape[0]
    tm = min(m, 1024)
    spec = pl.BlockSpec((tm, ATT_OUT), lambda i: (i, 0))
    return pl.pallas_call(
        _att_combine_kernel,
        grid=(m // tm,),
        in_specs=[spec] * 6,
        out_specs=spec,
        out_shape=jax.ShapeDtypeStruct((m, ATT_OUT), BF16),
        compiler_params=_cparams(("parallel",)),
        name="att_combine",
    )(*parts)


def _att_sample_kernel(sl_ref, qkv_ref, k0, v0, k1, v1, k2, v2, o_ref):
    caches = ((k0, v0), (k1, v1), (k2, v2))
    idx = lax.broadcasted_iota(jnp.int32, (ATT_SPAN, ATT_HEADS, 1), 0)
    outs, lses = [], []
    for g in range(3):
        d = ATT_DILS[g]
        q = qkv_ref[0, g]
        kn = qkv_ref[1, g]
        vn = qkv_ref[2, g]
        kc = caches[g][0][...]
        vc = caches[g][1][...]
        sl = sl_ref[g]
        dist = ((ATT_SPAN - idx) * d).astype(F32)
        s = jnp.sum(kc * q[None], axis=-1, keepdims=True) * ATT_SCALE - sl[None] * dist
        s_new = jnp.sum(kn * q, axis=-1, keepdims=True) * ATT_SCALE
        mx = jnp.maximum(jnp.max(s, axis=0), s_new)
        p = jnp.exp(s - mx[None])
        p_new = jnp.exp(s_new - mx)
        den = jnp.sum(p, axis=0) + p_new
        outs.append((jnp.sum(p * vc, axis=0) + p_new * vn) / den)
        lses.append(mx + jnp.log(den))
    mx = jnp.maximum(jnp.maximum(lses[0], lses[1]), lses[2])
    es = [jnp.exp(l - mx) for l in lses]
    o_ref[...] = (es[0] * outs[0] + es[1] * outs[1] + es[2] * outs[2]) / (es[0] + es[1] + es[2])


def att_sample(qkv, caches, slopes, layer):
    bd = qkv.shape[0]
    in_specs = [pl.BlockSpec((3, ATT_HEADS, 1), lambda b: (0, 0, 0)),
                pl.BlockSpec((None, 3, 3, ATT_HEADS, HEAD_DIM), lambda b: (b, 0, 0, 0, 0))]
    args = [slopes.reshape(3, ATT_HEADS, 1), qkv]
    for g in range(3):
        d = ATT_DILS[g]
        for c in caches[2 * g:2 * g + 2]:
            args.append(c.reshape(c.shape[0], bd, ATT_SPAN, d, ATT_HEADS, HEAD_DIM))
            in_specs.append(pl.BlockSpec((None, None, ATT_SPAN, None, ATT_HEADS, HEAD_DIM),
                                         lambda b: (layer, b, 0, 0, 0, 0)))
    return pl.pallas_call(
        _att_sample_kernel,
        grid=(bd,),
        in_specs=in_specs,
        out_specs=pl.BlockSpec((None, ATT_HEADS, HEAD_DIM), lambda b: (b, 0, 0)),
        out_shape=jax.ShapeDtypeStruct((bd, ATT_HEADS, HEAD_DIM), F32),
        compiler_params=_cparams(("parallel",)),
        name="att_sample",
    )(*args)


def _ssd_prompt_kernel(x_ref, b_ref, c_ref, z_ref, l_ref, cwx_ref, cwb_ref, cwc_ref, cbx_ref, cbb_ref, cbc_ref,
                       dtb_ref, aneg_ref, dskip_ref, ng_ref, o_ref, hout_ref, xpad, bpad, cpad, h_scr, y_scr, *, nc):
    c = pl.program_id(1)
    L = SSM_CHUNK

    @pl.when(c == 0)
    def _():
        xpad[0:SUBLANES, :] = jnp.zeros((SUBLANES, xpad.shape[1]), F32)
        bpad[0:SUBLANES, :] = jnp.zeros((SUBLANES, bpad.shape[1]), F32)
        cpad[0:SUBLANES, :] = jnp.zeros((SUBLANES, cpad.shape[1]), F32)
        h_scr[...] = jnp.zeros_like(h_scr)

    def conv_silu(pad, src_ref, w_ref, bias_ref):
        pad[SUBLANES:SUBLANES + L, :] = src_ref[...]
        acc = bias_ref[...] + w_ref[0:1, :] * pad[pl.ds(SUBLANES - 3, L), :]
        for jj in range(1, SSM_CONV):
            acc = acc + w_ref[jj:jj + 1, :] * pad[pl.ds(SUBLANES - 3 + jj, L), :]
        pad[0:SUBLANES, :] = pad[L:L + SUBLANES, :]
        return _silu(acc)

    xs = conv_silu(xpad, x_ref, cwx_ref, cbx_ref)
    bm = conv_silu(bpad, b_ref, cwb_ref, cbb_ref).astype(BF16)
    cm = conv_silu(cpad, c_ref, cwc_ref, cbc_ref).astype(BF16)

    lane = lax.broadcasted_iota(jnp.int32, (1, LANES), 1)
    head0 = lane < HEAD_DIM
    sub_head0 = lax.broadcasted_iota(jnp.int32, (LANES, 1), 0) < HEAD_DIM
    dt_lanes = (lane >= DT_LANE0) & (lane < DT_LANE0 + SSM_HEADS)
    dtv = jnp.where(dt_lanes, _softplus(l_ref[:, 0:LANES] + dtb_ref[...]), 0.0)
    da = dtv * aneg_ref[...]
    ri = lax.broadcasted_iota(jnp.int32, (L, L), 0)
    ci = lax.broadcasted_iota(jnp.int32, (L, L), 1)
    causal = ri >= ci
    a_cum = _dot_exact_lhs(causal.astype(BF16), da)
    a_cum_t = a_cum.T
    dt_t = dtv.T

    for g in range(SSM_GROUPS):
        bg = bm[:, g * SSM_STATE:(g + 1) * SSM_STATE]
        cg = cm[:, g * SSM_STATE:(g + 1) * SSM_STATE]
        cb = _dot_nt(cg, bg)
        for pp in range(3):
            pair = g * 3 + pp
            xp = xs[:, pair * LANES:(pair + 1) * LANES]
            xpb = xp.astype(BF16)
            yd, sc, ea, cd = [], [], [], []
            for e in range(2):
                hl = DT_LANE0 + 2 * pair + e
                ac_col = a_cum[:, hl:hl + 1]
                ac_row = a_cum_t[hl:hl + 1, :]
                a_last = ac_col[L - 1:L, :]
                dec = jnp.exp(jnp.where(causal, ac_col - ac_row, -jnp.inf))
                wm = (cb * dec * dt_t[hl:hl + 1, :]).astype(BF16)
                yd.append(jnp.dot(wm, xpb, preferred_element_type=F32))
                sc.append(jnp.exp(a_last - ac_col) * dtv[:, hl:hl + 1])
                ea.append(jnp.exp(ac_col))
                cd.append(jnp.exp(a_last))
            hp = h_scr[pair]
            y_off = _dot_nt(cg, hp.astype(BF16)) * jnp.where(head0, ea[0], ea[1])
            xw = (xp * jnp.where(head0, sc[0], sc[1])).astype(BF16)
            h_scr[pair] = jnp.where(sub_head0, cd[0], cd[1]) * hp + _dot_tn(xw, bg)
            y_scr[:, pair * LANES:(pair + 1) * LANES] = jnp.where(head0, yd[0], yd[1]) + y_off

    y = (y_scr[...] + dskip_ref[...] * xs) * _silu(z_ref[...])
    gw = BRANCH_W // SSM_GROUPS
    for g in range(SSM_GROUPS):
        yg = y[:, g * gw:(g + 1) * gw]
        yg = yg * lax.rsqrt(jnp.mean(yg * yg, axis=-1, keepdims=True) + NORM_EPS)
        o_ref[:, g * gw:(g + 1) * gw] = (yg * ng_ref[:, g * gw:(g + 1) * gw]).astype(o_ref.dtype)

    @pl.when(c == nc - 1)
    def _():
        hout_ref[...] = h_scr[...]


def ssd_prompt(proj, lw, bsz, t):
    nc = t // SSM_CHUNK
    L = SSM_CHUNK
    rowblk = lambda width, c0: pl.BlockSpec((L, width), lambda b, c: (b * nc + c, c0 // width))
    par = lambda a: pl.BlockSpec(a.shape, lambda b, c: (0,) * a.ndim)
    params = [lw["cw_x"], lw["cw_b"], lw["cw_c"], lw["cb_x"], lw["cb_b"], lw["cb_c"], lw["dt_bias"], lw["a_neg"],
              lw["d_skip"], lw["ssm_norm_g"]]
    return pl.pallas_call(
        functools.partial(_ssd_prompt_kernel, nc=nc),
        grid=(bsz, nc),
        in_specs=[rowblk(BRANCH_W, C_X), rowblk(512, C_B), rowblk(512, C_C), rowblk(BRANCH_W, C_Z), rowblk(512, C_L)]
        + [par(a) for a in params],
        out_specs=[pl.BlockSpec((L, BRANCH_W), lambda b, c: (b * nc + c, 0)),
                   pl.BlockSpec((None, N_PAIRS, LANES, SSM_STATE), lambda b, c: (b, 0, 0, 0))],
        out_shape=[jax.ShapeDtypeStruct((bsz * t, BRANCH_W), BF16),
                   jax.ShapeDtypeStruct((bsz, N_PAIRS, LANES, SSM_STATE), F32)],
        scratch_shapes=[pltpu.VMEM((L + SUBLANES, BRANCH_W), F32), pltpu.VMEM((L + SUBLANES, 512), F32),
                        pltpu.VMEM((L + SUBLANES, 512), F32), pltpu.VMEM((N_PAIRS, LANES, SSM_STATE), F32),
                        pltpu.VMEM((L, BRANCH_W), F32)],
        compiler_params=_cparams(("parallel", "arbitrary")),
        name="ssd_prompt",
    )(proj, proj, proj, proj, proj, *params)


def _ssd_sample_conv_kernel(new_ref, st_ref, w_ref, b_ref, o_ref):
    acc = b_ref[...] + w_ref[SSM_CONV - 1:SSM_CONV, :] * new_ref[...]
    for jj in range(SSM_CONV - 1):
        acc = acc + w_ref[jj:jj + 1, :] * st_ref[jj]
    o_ref[...] = _silu(acc)


def ssd_sample_conv(xbc_new, conv_st_t, w, b):
    bd, cdim = xbc_new.shape
    return pl.pallas_call(
        _ssd_sample_conv_kernel,
        out_shape=jax.ShapeDtypeStruct((bd, cdim), F32),
        name="ssd_sample_conv",
    )(xbc_new, conv_st_t, w, b.reshape(1, cdim))


def _ssd_sample_step_kernel(h_ref, x_ref, z_ref, b_ref, c_ref, dt_ref, dtb_ref, alog_ref, d_ref, ng_ref, o_ref, hout_ref):
    h = h_ref[...]
    dt = _softplus(dt_ref[...] + dtb_ref[...])
    da = jnp.exp(dt * (-jnp.exp(alog_ref[...])))
    rep = SSM_HEADS // SSM_GROUPS
    bh = jnp.broadcast_to(b_ref[...][:, None], (SSM_GROUPS, rep, 1, SSM_STATE)).reshape(SSM_HEADS, 1, SSM_STATE)
    ch = jnp.broadcast_to(c_ref[...][:, None], (SSM_GROUPS, rep, 1, SSM_STATE)).reshape(SSM_HEADS, 1, SSM_STATE)
    x = x_ref[...]
    hn = da * h + (dt * x) * bh
    hout_ref[...] = hn
    y = jnp.sum(hn * ch, axis=-1, keepdims=True)
    y = (y + d_ref[...] * x) * _silu(z_ref[...])
    y4 = y.reshape(SSM_GROUPS, rep, HEAD_DIM, 1)
    ms = jnp.sum(jnp.sum(y4 * y4, axis=2, keepdims=True), axis=1, keepdims=True) / (rep * HEAD_DIM)
    y4 = y4 * lax.rsqrt(ms + NORM_EPS)
    o_ref[...] = y4.reshape(SSM_HEADS, HEAD_DIM, 1) * ng_ref[...]


def ssd_sample_step(h, x_col, z_col, bmat, cmat, dt_raw, lw, layer):
    bd = x_col.shape[0]
    col = pl.BlockSpec((None, SSM_HEADS, HEAD_DIM, 1), lambda b: (b, 0, 0, 0))
    grp = pl.BlockSpec((None, SSM_GROUPS, 1, SSM_STATE), lambda b: (b, 0, 0, 0))
    hd1 = pl.BlockSpec((SSM_HEADS, 1, 1), lambda b: (0, 0, 0))
    return pl.pallas_call(
        _ssd_sample_step_kernel,
        grid=(bd,),
        in_specs=[pl.BlockSpec((None, None, SSM_HEADS, HEAD_DIM, SSM_STATE), lambda b: (layer, b, 0, 0, 0)),
                  col, col, grp, grp, pl.BlockSpec((None, SSM_HEADS, 1, 1), lambda b: (b, 0, 0, 0)), hd1, hd1, hd1,
                  pl.BlockSpec((SSM_HEADS, HEAD_DIM, 1), lambda b: (0, 0, 0))],
        out_specs=[col, pl.BlockSpec((None, SSM_HEADS, HEAD_DIM, SSM_STATE), lambda b: (b, 0, 0, 0))],
        out_shape=[jax.ShapeDtypeStruct((bd, SSM_HEADS, HEAD_DIM, 1), F32),
                   jax.ShapeDtypeStruct((bd, SSM_HEADS, HEAD_DIM, SSM_STATE), F32)],
        compiler_params=_cparams(("parallel",)),
        name="ssd_sample_step",
    )(h, x_col, z_col, bmat, cmat, dt_raw, lw["dt_bias_h"], lw["a_log_h"], lw["d_h"], lw["ssm_norm_g_col"])


def _rwkv_mix(i, tm, x_refs, h_refs, i_refs, mu_refs, w0_ref, wup_ref, a0_ref, aup_ref, gup_ref):
    def shift(x_ref, h_ref, i_ref, mu_ref):
        x = x_ref[...]
        hrows = h_ref.shape[0]
        prev0 = jnp.where(i == 0, i_ref[...], h_ref[hrows - 1:hrows, :])
        if tm == 1:
            xp = prev0
        else:
            row = lax.broadcasted_iota(jnp.int32, (tm, 1), 0)
            xp = jnp.where(row == 0, prev0, pltpu.roll(x, 1, 0))
        return x + mu_ref[...] * (xp - x)

    ur, uk, uv, ul = (shift(*refs) for refs in zip(x_refs, h_refs, i_refs, mu_refs))
    f = lambda a, w_ref: jnp.dot(a.astype(BF16), w_ref[...], preferred_element_type=F32)
    w_log = -_softplus(-(w0_ref[...] + f(jnp.tanh(ul[:, 0:LANES]), wup_ref))) - 0.5
    a = _sigmoid(a0_ref[...] + f(ul[:, LANES:2 * LANES], aup_ref))
    g = f(_sigmoid(ul[:, 2 * LANES:4 * LANES]), gup_ref)
    return ur, uk, uv, -jnp.exp(w_log), a, g


def _rwkv_prep_kernel(*refs, tm):
    x_refs, h_refs, i_refs, mu_refs = refs[0:4], refs[4:8], refs[8:12], refs[12:16]
    w0_ref, wup_ref, a0_ref, aup_ref, gup_ref, kk_ref, ka_ref = refs[16:23]
    or_ref, old_ref, ok_ref, ov_ref, okk_ref, oa_ref, og_ref = refs[23:30]
    ur, uk, uv, ld, a, g = _rwkv_mix(pl.program_id(1), tm, x_refs, h_refs, i_refs, mu_refs,
                                     w0_ref, wup_ref, a0_ref, aup_ref, gup_ref)
    or_ref[...] = ur
    old_ref[...] = ld
    ok_ref[...] = uk * (1.0 + (a - 1.0) * ka_ref[...])
    ov_ref[...] = uv
    okk_ref[...] = uk * kk_ref[...]
    oa_ref[...] = a
    og_ref[...] = g


def _rwkv_prep_prompt_kernel(*refs, tm):
    x_refs, h_refs, i_refs, mu_refs = refs[0:4], refs[4:8], refs[8:12], refs[12:16]
    w0_ref, wup_ref, a0_ref, aup_ref, gup_ref, kk_ref, ka_ref, rk_ref = refs[16:24]
    opt_ref, ort_ref, oqh_ref, okh_ref, ov_ref, obv_ref, og_ref, oel_ref = refs[24:32]
    L = RWKV_CHUNK
    ur, uk, uv, ld, a, g = _rwkv_mix(pl.program_id(1), tm, x_refs, h_refs, i_refs, mu_refs,
                                     w0_ref, wup_ref, a0_ref, aup_ref, gup_ref)
    kmod = uk * (1.0 + (a - 1.0) * ka_ref[...])
    kkraw = uk * kk_ref[...]
    assert tm == 2 * L
    ri = lax.broadcasted_iota(jnp.int32, (tm, tm), 0)
    ci = lax.broadcasted_iota(jnp.int32, (tm, tm), 1)
    tril = ((ri >= ci) & ((ri < L) == (ci < L))).astype(BF16)
    cum = _dot_exact_lhs(tril, ld, 2)
    e_in = jnp.exp(cum)
    e_inv = jnp.exp(-cum)
    e_ex = jnp.exp(cum - ld)
    ort_ref[...] = (ur * e_in).astype(ort_ref.dtype)
    okh_ref[...] = (kmod * e_inv).astype(okh_ref.dtype)
    ov_ref[...] = uv.astype(ov_ref.dtype)
    og_ref[...] = g.astype(og_ref.dtype)
    for c in range(tm // L):
        oel_ref[c] = e_in[c * L + L - 1:c * L + L, :]
    li = lax.broadcasted_iota(jnp.int32, (LANES, LANES), 0)
    lj = lax.broadcasted_iota(jnp.int32, (LANES, LANES), 1)
    bd_ones = ((li < HEAD_DIM) == (lj < HEAD_DIM)).astype(BF16)
    rkr = ur * kmod * rk_ref[...]
    for p in range(N_PAIRS):
        cs = slice(p * LANES, (p + 1) * LANES)
        kkp = kkraw[:, cs]
        kkn = kkp / jnp.maximum(jnp.sqrt(_dot_exact_rhs(kkp * kkp, bd_ones, 2)), 1e-12)
        opt_ref[:, cs] = (-kkn * e_ex[:, cs]).astype(opt_ref.dtype)
        oqh_ref[:, cs] = (kkn * a[:, cs] * e_inv[:, cs]).astype(oqh_ref.dtype)
        obv_ref[:, cs] = (_dot_exact_rhs(rkr[:, cs], bd_ones, 2) * uv[:, cs]).astype(obv_ref.dtype)


def _rwkv_prep_call(kernel_fn, proj, init, params, bsz, t, tm, out_specs, out_shape, name):
    nt = t // tm
    hrows = SUBLANES if t >= SUBLANES else t
    hper = tm // hrows
    proj = proj.reshape(bsz, t, N_PROJ)
    blk = lambda width, c0: pl.BlockSpec((None, tm, width), lambda b, i: (b, i, c0 // width))
    halo = lambda width, c0: pl.BlockSpec(
        (None, hrows, width), lambda b, i: (b, jnp.maximum(i * hper - 1, 0), c0 // width))
    ini = lambda width: pl.BlockSpec((None, 1, width), lambda b, i: (b, 0, 0))
    par = lambda a: pl.BlockSpec(a.shape, lambda b, i: (0,) * a.ndim)
    secs = ((BRANCH_W, C_R), (BRANCH_W, C_RK), (BRANCH_W, C_RV), (512, C_L))
    return pl.pallas_call(
        functools.partial(kernel_fn, tm=tm),
        grid=(bsz, nt),
        in_specs=[blk(*s) for s in secs] + [halo(*s) for s in secs] + [ini(s[0]) for s in secs]
        + [par(a) for a in params],
        out_specs=out_specs,
        out_shape=out_shape,
        compiler_params=_cparams(("parallel", "arbitrary")),
        name=name,
    )(proj, proj, proj, proj, proj, proj, proj, proj, *init, *params)


def _rwkv_prep_params(lw):
    return [lw["mu_r"], lw["mu_k"], lw["mu_v"], lw["mu_l"], lw["w0"], lw["w_up"], lw["a0"], lw["a_up"], lw["g_up"],
            lw["k_k"], lw["k_a"]]


def rwkv_prep(proj, init, lw, bsz, t):
    tm = min(t, 128)
    out = pl.BlockSpec((None, tm, BRANCH_W), lambda b, i: (b, i, 0))
    shp = jax.ShapeDtypeStruct((bsz, t, BRANCH_W), F32)
    outs = _rwkv_prep_call(_rwkv_prep_kernel, proj, init, _rwkv_prep_params(lw), bsz, t, tm, [out] * 7, [shp] * 7,
                           "rwkv_prep")
    return [o.reshape(bsz * t, BRANCH_W) for o in outs]


def rwkv_prep_prompt(proj, init, lw, bsz, t):
    tm = 128
    nch = tm // RWKV_CHUNK
    out = pl.BlockSpec((None, tm, BRANCH_W), lambda b, i: (b, i, 0))
    shp = jax.ShapeDtypeStruct((bsz, t, BRANCH_W), BF16)
    el_spec = pl.BlockSpec((None, nch, 1, BRANCH_W), lambda b, i: (b, i, 0, 0))
    el_shape = jax.ShapeDtypeStruct((bsz, t // RWKV_CHUNK, 1, BRANCH_W), F32)
    outs = _rwkv_prep_call(_rwkv_prep_prompt_kernel, proj, init, _rwkv_prep_params(lw) + [lw["r_k_row"]], bsz, t, tm,
                           [out] * 7 + [el_spec], [shp] * 7 + [el_shape], "rwkv_prep_prompt")
    return [o.reshape(bsz * t, BRANCH_W) for o in outs[:7]] + [outs[7].reshape(bsz * t // RWKV_CHUNK, 1, BRANCH_W)]


def _rwkv_chain_kernel(pt_ref, rt_ref, qh_ref, kh_ref, v_ref, bv_ref, g_ref, el_ref, lng_ref, lnb_ref, o_ref, sout_ref,
                       s_scr, *, nchunk, nt, npair):
    tstep = pl.program_id(2)
    L = RWKV_CHUNK
    L2 = 2 * L

    @pl.when(tstep == 0)
    def _():
        s_scr[...] = jnp.zeros_like(s_scr)

    lane = lax.broadcasted_iota(jnp.int32, (1, LANES), 1)
    head0 = lane < HEAD_DIM
    ri = lax.broadcasted_iota(jnp.int32, (L2, L2), 0)
    ci = lax.broadcasted_iota(jnp.int32, (L2, L2), 1)
    same = (ri < L) == (ci < L)
    strict = same & (ri > ci)
    incl = same & (ri >= ci)
    eye = (ri == ci).astype(F32)
    bd_ones = same.astype(BF16)
    bf = lambda x: x.astype(BF16)
    dot = lambda a, b: jnp.dot(bf(a), bf(b), preferred_element_type=F32)
    zero = jnp.zeros((), BF16)

    def stack(x):
        return jnp.concatenate([jnp.where(head0, x, zero), jnp.where(head0, zero, x)], axis=0)

    def exact_sum(x):
        hi = bf(x)
        lo = bf(x - hi.astype(F32))
        return (jnp.dot(hi, bd_ones, preferred_element_type=F32) + jnp.dot(lo, bd_ones, preferred_element_type=F32))

    probs = [(c, p) for c in range(nchunk) for p in range(npair)]
    rows = lambda c: pl.ds(c * L, L)
    cols = lambda p: slice(p * LANES, (p + 1) * LANES)
    pm = {k: stack(pt_ref[rows(k[0]), cols(k[1])]) for k in probs}
    rm = {k: stack(rt_ref[rows(k[0]), cols(k[1])]) for k in probs}
    qk = {k: jnp.concatenate([stack(qh_ref[rows(k[0]), cols(k[1])]), stack(kh_ref[rows(k[0]), cols(k[1])])], axis=0)
          for k in probs}
    vm = {k: stack(v_ref[rows(k[0]), cols(k[1])]) for k in probs}
    x = {k: _dot_nt(jnp.concatenate([pm[k], rm[k]], axis=0), qk[k]) for k in probs}
    n_pow = {k: jnp.where(strict, x[k][0:L2, 0:L2], 0.0) for k in probs}
    a_kp = {k: bf(jnp.where(strict, x[k][0:L2, L2:2 * L2], 0.0)) for k in probs}
    r_q = {k: bf(jnp.where(incl, x[k][L2:2 * L2, 0:L2], 0.0)) for k in probs}
    r_k = {k: bf(jnp.where(incl, x[k][L2:2 * L2, L2:2 * L2], 0.0)) for k in probs}
    t_inv = {k: eye + n_pow[k] for k in probs}
    for _ in range(5):
        n_pow = {k: dot(n_pow[k], n_pow[k]) for k in probs}
        t_inv = {k: t_inv[k] + dot(n_pow[k], t_inv[k]) for k in probs}
    t_inv = {k: bf(t_inv[k]) for k in probs}
    av = {k: dot(a_kp[k], vm[k]) for k in probs}
    rkv = {k: dot(r_k[k], vm[k]) for k in probs}

    state = [s_scr[p] for p in range(npair)]
    ys = {}
    for c in range(nchunk):
        ks = [(c, p) for p in range(npair)]
        s0b = {k: bf(state[k[1]]) for k in ks}
        rhs = {k: dot(pm[k], s0b[k]) + av[k] for k in ks}
        rs = {k: dot(rm[k], s0b[k]) + rkv[k] for k in ks}
        u = {k: bf(dot(t_inv[k], rhs[k])) for k in ks}
        upd = {k: _dot_tn(qk[k], jnp.concatenate([u[k], vm[k]], axis=0)) for k in ks}
        for k in ks:
            ys[k] = rs[k] + dot(r_q[k], u[k])
            w_col = jnp.sum(eye * el_ref[c, :, cols(k[1])], axis=1, keepdims=True)
            state[k[1]] = w_col * (state[k[1]] + upd[k])
    for p in range(npair):
        s_scr[p] = state[p]

    y = {k: ys[k][0:L, :] + ys[k][L:L2, :] for k in probs}
    yc = {k: y[k] - exact_sum(y[k]) * (1.0 / HEAD_DIM) for k in probs}
    var = {k: exact_sum(yc[k] * yc[k]) * (1.0 / HEAD_DIM) for k in probs}
    for k in probs:
        cs = cols(k[1])
        yn = yc[k] * lax.rsqrt(var[k] + RWKV_LN_EPS) * lng_ref[:, cs] + lnb_ref[:, cs]
        o_ref[rows(k[0]), cs] = ((yn + bv_ref[rows(k[0]), cs].astype(F32)) * g_ref[rows(k[0]), cs].astype(F32)
                                 ).astype(o_ref.dtype)

    @pl.when(tstep == nt - 1)
    def _():
        sout_ref[...] = s_scr[...]


def rwkv_prompt(parts, lw, bsz, t, npair=6, nchunk=2):
    tb = nchunk * RWKV_CHUNK
    nt = t // tb
    wl = npair * LANES
    blk = pl.BlockSpec((tb, wl), lambda b, p, i: (b * nt + i, p))
    par = pl.BlockSpec((1, wl), lambda b, p, i: (0, p))
    el = pl.BlockSpec((nchunk, 1, wl), lambda b, p, i: (b * nt + i, 0, p))
    return pl.pallas_call(
        functools.partial(_rwkv_chain_kernel, nchunk=nchunk, nt=nt, npair=npair),
        grid=(bsz, N_PAIRS // npair, nt),
        in_specs=[blk] * 7 + [el, par, par],
        out_specs=[blk, pl.BlockSpec((None, npair, LANES, LANES), lambda b, p, i: (b, p, 0, 0))],
        out_shape=[jax.ShapeDtypeStruct((bsz * t, BRANCH_W), BF16),
                   jax.ShapeDtypeStruct((bsz, N_PAIRS, LANES, LANES), F32)],
        scratch_shapes=[pltpu.VMEM((npair, LANES, LANES), F32)],
        compiler_params=_cparams(("parallel", "parallel", "arbitrary")),
        name="rwkv_prompt",
    )(*parts, lw["ln_g_row"], lw["ln_b_row"])


def _rwkv_sample_step_kernel(s_ref, r_ref, ld_ref, k_ref, kk_ref, a_ref, v_ref, g_ref, rk_ref, lng_ref, lnb_ref,
                             o_ref, sout_ref):
    s = s_ref[...]
    r, kmod, kkraw, a = r_ref[...], k_ref[...], kk_ref[...], a_ref[...]
    v = v_ref[...]
    kkn = kkraw / jnp.maximum(jnp.sqrt(jnp.sum(kkraw * kkraw, axis=-1, keepdims=True)), 1e-12)
    sa = jnp.sum(s * (-kkn), axis=-1, keepdims=True)
    sn = s * jnp.exp(ld_ref[...]) + sa * (kkn * a) + v * kmod
    sout_ref[...] = sn
    y = jnp.sum(sn * r, axis=-1, keepdims=True)
    mean = jnp.mean(y, axis=1, keepdims=True)
    yc = y - mean
    var = jnp.mean(yc * yc, axis=1, keepdims=True)
    yn = yc * lax.rsqrt(var + RWKV_LN_EPS) * lng_ref[...] + lnb_ref[...]
    bonus = jnp.sum(r * kmod * rk_ref[...], axis=-1, keepdims=True) * v
    o_ref[...] = (yn + bonus) * g_ref[...]


def rwkv_sample_step(state, rows, cols, lw, layer):
    bd = rows[0].shape[0]
    row = pl.BlockSpec((None, RWKV_HEADS, 1, HEAD_DIM), lambda b: (b, 0, 0, 0))
    col = pl.BlockSpec((None, RWKV_HEADS, HEAD_DIM, 1), lambda b: (b, 0, 0, 0))
    prow = pl.BlockSpec((RWKV_HEADS, 1, HEAD_DIM), lambda b: (0, 0, 0))
    pcol = pl.BlockSpec((RWKV_HEADS, HEAD_DIM, 1), lambda b: (0, 0, 0))
    return pl.pallas_call(
        _rwkv_sample_step_kernel,
        grid=(bd,),
        in_specs=[pl.BlockSpec((None, None, RWKV_HEADS, HEAD_DIM, HEAD_DIM), lambda b: (layer, b, 0, 0, 0))]
        + [row] * 5 + [col] * 2 + [prow, pcol, pcol],
        out_specs=[col, pl.BlockSpec((None, RWKV_HEADS, HEAD_DIM, HEAD_DIM), lambda b: (b, 0, 0, 0))],
        out_shape=[jax.ShapeDtypeStruct((bd, RWKV_HEADS, HEAD_DIM, 1), F32),
                   jax.ShapeDtypeStruct((bd, RWKV_HEADS, HEAD_DIM, HEAD_DIM), F32)],
        compiler_params=_cparams(("parallel",)),
        name="rwkv_sample_step",
    )(state, *rows, *cols, lw["r_k_h"], lw["ln_g_col"], lw["ln_b_col"])


def _cache_update_kernel(k_ref, kn_ref, v_ref, vn_ref, newk_ref, newv_ref, ok_ref, ov_ref, *, nblk):
    j = pl.program_id(2)
    wb = k_ref.shape[0]
    last = j == nblk - 1
    for c_ref, n_ref, new_ref, o_ref in ((k_ref, kn_ref, newk_ref, ok_ref), (v_ref, vn_ref, newv_ref, ov_ref)):
        o_ref[0:wb - 1] = c_ref[1:wb]
        o_ref[wb - 1] = jnp.where(last, new_ref[...], n_ref[0])


def cache_update(cache_k, cache_v, new_k, new_v):
    nl, bd, win = cache_k.shape[:3]
    wb = min(win, 256)
    nblk = win // wb
    cur = pl.BlockSpec((None, None, wb, ATT_HEADS, HEAD_DIM), lambda l, b, j: (l, b, j, 0, 0))
    nxt = pl.BlockSpec((None, None, 1, ATT_HEADS, HEAD_DIM),
                       lambda l, b, j: (l, b, jnp.minimum((j + 1) * wb, win - 1), 0, 0))
    new = pl.BlockSpec((None, None, ATT_HEADS, HEAD_DIM), lambda l, b, j: (l, b, 0, 0))
    shp = jax.ShapeDtypeStruct(cache_k.shape, cache_k.dtype)
    return pl.pallas_call(
        functools.partial(_cache_update_kernel, nblk=nblk),
        grid=(nl, bd, nblk),
        in_specs=[cur, nxt, cur, nxt, new, new],
        out_specs=[cur, cur],
        out_shape=[shp, shp],
        compiler_params=_cparams(("parallel", "parallel", "parallel")),
        name="cache_update",
    )(cache_k, cache_k, cache_v, cache_v, new_k, new_v)


_O_LORA = _O_RW + 3 * BRANCH_W
_PACK_SECTIONS = (
    (C_Q, _O_QKV, 3 * BRANCH_W), (C_Z, _O_Z, BRANCH_W), (C_X, _O_XBC, BRANCH_W), (C_R, _O_RW, 3 * BRANCH_W),
    (C_G, _O_GATE, 3 * D_MODEL), (C_B, _O_XBC + BRANCH_W, 1024),
    (C_L, _O_LORA, LORA_W), (C_L + DT_LANE0, _O_DT, SSM_HEADS), (C_L + LANES, _O_LORA + LORA_W, LORA_A),
    (C_L + 2 * LANES, _O_LORA + LORA_W + LORA_A, LORA_G))
_PACK_ZEROS = ((C_L + DT_LANE0 + SSM_HEADS, 8), (C_L + LANES + LORA_A, 32))
_PACK_CHUNK = 1024


def _pack_w_in_kernel(w_ref, o_ref):
    rows = w_ref.shape[0]
    for dst, src, width in _PACK_SECTIONS:
        for c0 in range(0, width, _PACK_CHUNK):
            n = min(_PACK_CHUNK, width - c0)
            o_ref[:, dst + c0:dst + c0 + n] = w_ref[:, src + c0:src + c0 + n].astype(o_ref.dtype)
    for dst, width in _PACK_ZEROS:
        o_ref[:, dst:dst + width] = jnp.zeros((rows, width), o_ref.dtype)


def pack_w_in(w_in):
    nl, kdim, ncol = w_in.shape
    tk = 128
    return pl.pallas_call(
        _pack_w_in_kernel,
        grid=(nl, kdim // tk),
        in_specs=[pl.BlockSpec((None, tk, ncol), lambda l, i: (l, i, 0))],
        out_specs=pl.BlockSpec((None, tk, N_PROJ), lambda l, i: (l, i, 0)),
        out_shape=jax.ShapeDtypeStruct((nl, kdim, N_PROJ), BF16),
        compiler_params=_cparams(("parallel", "parallel")),
        name="pack_w_in",
    )(w_in)


def _pack_layer(l, p):
    row = lambda a: a.reshape(1, -1)
    pad_rows = lambda a, n: jnp.concatenate([a, jnp.zeros((n - a.shape[0], a.shape[1]), a.dtype)], axis=0).astype(BF16)
    lane_pad = lambda a: jnp.zeros((1, LANES), F32).at[0, DT_LANE0:DT_LANE0 + SSM_HEADS].set(a)
    mu = p["rwkv_mu"][l]
    z1 = lambda n: jnp.zeros((n,), F32)
    cw = p["ssm_conv_w"][l]
    cb = p["ssm_conv_b"][l]
    rep64 = lambda a: jnp.repeat(a, HEAD_DIM)
    return dict(
        ln1_g=p["ln1_g"][l], ln2_g=p["ln2_g"][l],
        cw_x=cw[:, :BRANCH_W], cw_b=cw[:, BRANCH_W:BRANCH_W + 512], cw_c=cw[:, BRANCH_W + 512:],
        cb_x=row(cb[:BRANCH_W]), cb_b=row(cb[BRANCH_W:BRANCH_W + 512]), cb_c=row(cb[BRANCH_W + 512:]),
        conv_w=cw, conv_b=cb,
        dt_bias=lane_pad(p["ssm_dt_bias"][l]), a_neg=lane_pad(-jnp.exp(p["ssm_a_log"][l])),
        d_skip=row(rep64(p["ssm_d"][l])), ssm_norm_g=row(p["ssm_norm_g"][l]),
        dt_bias_h=p["ssm_dt_bias"][l].reshape(SSM_HEADS, 1, 1), a_log_h=p["ssm_a_log"][l].reshape(SSM_HEADS, 1, 1),
        d_h=p["ssm_d"][l].reshape(SSM_HEADS, 1, 1), ssm_norm_g_col=p["ssm_norm_g"][l].reshape(SSM_HEADS, HEAD_DIM, 1),
        mu_r=row(mu[:BRANCH_W]), mu_k=row(mu[BRANCH_W:2 * BRANCH_W]), mu_v=row(mu[2 * BRANCH_W:3 * BRANCH_W]),
        mu_l=row(jnp.concatenate([mu[3 * BRANCH_W:3 * BRANCH_W + LORA_W], z1(32),
                                  mu[3 * BRANCH_W + LORA_W:3 * BRANCH_W + LORA_W + LORA_A], z1(32),
                                  mu[3 * BRANCH_W + LORA_W + LORA_A:]])),
        w0=row(p["rwkv_w0"][l]), w_up=pad_rows(p["rwkv_w_up"][l], LANES), a0=row(p["rwkv_a0"][l]),
        a_up=pad_rows(p["rwkv_a_up"][l], LANES), g_up=p["rwkv_g_up"][l].astype(BF16),
        k_k=row(p["rwkv_k_k"][l]), k_a=row(p["rwkv_k_a"][l]),
        r_k_row=row(p["rwkv_r_k"][l]), ln_g_row=row(p["rwkv_ln_g"][l]), ln_b_row=row(p["rwkv_ln_b"][l]),
        r_k_h=p["rwkv_r_k"][l].reshape(RWKV_HEADS, 1, HEAD_DIM),
        ln_g_col=p["rwkv_ln_g"][l].reshape(RWKV_HEADS, HEAD_DIM, 1),
        ln_b_col=p["rwkv_ln_b"][l].reshape(RWKV_HEADS, HEAD_DIM, 1),
    )


def _pack_big(p):
    return dict(w_proj=pack_w_in(p["w_in"]), w_att=p["w_branch_att"].astype(BF16),
                w_ssm=p["w_branch_ssm"].astype(BF16), w_rwkv=p["w_branch_rwkv"].astype(BF16),
                w_out=p["w_out"].astype(BF16), w_up_ff=p["w_up"].astype(BF16), w_down_ff=p["w_down"].astype(BF16))


def _unpack_lora(rows):
    return jnp.concatenate([rows[..., 0:LORA_W], rows[..., LANES:LANES + LORA_A], rows[..., 2 * LANES:]], axis=-1)


def _pack_lora(rows):
    z = lambda n: jnp.zeros(rows.shape[:-1] + (n,), rows.dtype)
    return jnp.concatenate([rows[..., 0:LORA_W], z(32), rows[..., LORA_W:LORA_W + LORA_A], z(32),
                            rows[..., LORA_W + LORA_A:]], axis=-1)


def _alibi_slopes():
    n = 3 * ATT_HEADS
    idx = jnp.arange(1, n + 1, dtype=F32)
    return jnp.exp2(-8.0 * idx / n).reshape(3, ATT_HEADS)


def _finish_layer(x, proj, o_att, o_ssm, o_rwkv, lw, big, layer):
    merged = merge_branches(o_att, o_ssm, o_rwkv, big["w_att"], big["w_ssm"], big["w_rwkv"], proj, layer)
    x = matmul(merged, big["w_out"], layer, epi="res", res=x)
    h2 = rmsnorm(x, lw["ln2_g"], BF16)
    up = matmul(h2, big["w_up_ff"], layer, epi="relu2", out_dtype=BF16)
    return matmul(up, big["w_down_ff"], layer, epi="res", res=x)


def _prompt_layer(x, lw, big, layer, slopes, bsz, t):
    h = rmsnorm(x, lw["ln1_g"], BF16)
    proj = matmul(h, big["w_proj"], layer)
    parts = []
    for g in range(3):
        parts += att_prompt_group(proj, slopes, g, bsz, t)
    o_att = att_combine(parts)
    o_ssm, h_fin = ssd_prompt(proj, lw, bsz, t)
    zero_init = [jnp.zeros((bsz, 1, w), F32) for w in (BRANCH_W, BRANCH_W, BRANCH_W, 512)]
    rw = rwkv_prep_prompt(proj, zero_init, lw, bsz, t)
    o_rwkv, s_fin = rwkv_prompt(rw, lw, bsz, t)
    x = _finish_layer(x, proj, o_att, o_ssm, o_rwkv, lw, big, layer)

    p3 = proj.reshape(bsz, t, N_PROJ)
    kv = []
    for g in range(3):
        win = ATT_WINDOWS[g]
        for c0 in (C_K, C_V):
            kv.append(p3[:, t - win:, c0 + g * ATT_OUT:c0 + (g + 1) * ATT_OUT].reshape(bsz, win, ATT_HEADS, HEAD_DIM))
    tail = p3[:, t - (SSM_CONV - 1):]
    conv_new = jnp.concatenate([tail[..., C_X:C_X + BRANCH_W], tail[..., C_B:C_B + 512], tail[..., C_C:C_C + 512]], -1)
    last = p3[:, t - 1]
    shift_new = jnp.concatenate([last[:, C_R:C_R + 3 * BRANCH_W], _unpack_lora(last[:, C_L:C_L + 512])], axis=-1)
    ssm_new = h_fin.reshape(bsz, SSM_HEADS, HEAD_DIM, SSM_STATE)
    s6 = s_fin.reshape(bsz, N_PAIRS, 2, HEAD_DIM, 2, HEAD_DIM)
    s_heads = jnp.stack([s6[:, :, 0, :, 0, :], s6[:, :, 1, :, 1, :]], axis=2)
    rwkv_new = jnp.swapaxes(s_heads, -1, -2).reshape(bsz, RWKV_HEADS, HEAD_DIM, HEAD_DIM)
    return x, tuple(kv) + (conv_new, ssm_new, shift_new, rwkv_new)


def _sample_layer(x, lw, big, slopes, st, layer, bd):
    (caches, conv_st, ssm_st, shift_st, rwkv_st) = st
    h = rmsnorm(x, lw["ln1_g"], BF16)
    proj = matmul(h, big["w_proj"], layer)
    qkv = proj[:, :3 * BRANCH_W].reshape(bd, 3, 3, ATT_HEADS, HEAD_DIM)
    o_att = att_sample(qkv, caches, slopes, layer).reshape(bd, ATT_OUT).astype(BF16)
    new_rows = tuple(qkv[:, part, g] for g in range(3) for part in (1, 2))

    xbc_new = jnp.concatenate([proj[:, C_X:C_X + BRANCH_W], proj[:, C_B:C_B + 512], proj[:, C_C:C_C + 512]], axis=-1)
    cst = conv_st[layer]
    xc = ssd_sample_conv(xbc_new, jnp.swapaxes(cst, 0, 1), lw["conv_w"], lw["conv_b"])
    conv_new = jnp.concatenate([cst[:, 1:], xbc_new[:, None]], axis=1)
    x_col = xc[:, :BRANCH_W].reshape(bd, SSM_HEADS, HEAD_DIM, 1)
    z_col = proj[:, C_Z:C_Z + BRANCH_W].reshape(bd, SSM_HEADS, HEAD_DIM, 1)
    bmat = xc[:, BRANCH_W:BRANCH_W + 512].reshape(bd, SSM_GROUPS, 1, SSM_STATE)
    cmat = xc[:, BRANCH_W + 512:].reshape(bd, SSM_GROUPS, 1, SSM_STATE)
    dt_raw = proj[:, C_L + DT_LANE0:C_L + DT_LANE0 + SSM_HEADS].reshape(bd, SSM_HEADS, 1, 1)
    y_col, ssm_new = ssd_sample_step(ssm_st, x_col, z_col, bmat, cmat, dt_raw, lw, layer)
    o_ssm = y_col.reshape(bd, BRANCH_W).astype(BF16)

    sh = shift_st[layer]
    init = [sh[:, None, i * BRANCH_W:(i + 1) * BRANCH_W] for i in range(3)] + [_pack_lora(sh[:, None, 3 * BRANCH_W:])]
    r, ld, kmod, v, kkraw, a, gg = rwkv_prep(proj, init, lw, bd, 1)
    rowf = lambda z: z.reshape(bd, RWKV_HEADS, 1, HEAD_DIM)
    colf = lambda z: z.reshape(bd, RWKV_HEADS, HEAD_DIM, 1)
    o_col, rwkv_new = rwkv_sample_step(rwkv_st, [rowf(r), rowf(ld), rowf(kmod), rowf(kkraw), rowf(a)],
                                       [colf(v), colf(gg)], lw, layer)
    o_rwkv = o_col.reshape(bd, BRANCH_W).astype(BF16)
    shift_new = jnp.concatenate([proj[:, C_R:C_R + 3 * BRANCH_W], _unpack_lora(proj[:, C_L:C_L + 512])], axis=-1)

    x = _finish_layer(x, proj, o_att, o_ssm, o_rwkv, lw, big, layer)
    return x, new_rows, (conv_new, ssm_new, shift_new, rwkv_new)


def kernel(x_prompt, x_sample, cache_att_k0, cache_att_v0, cache_att_k1, cache_att_v1, cache_att_k2, cache_att_v2, state_ssm_conv, state_ssm, state_rwkv_shift, state_rwkv, ln1_g, w_in, ssm_conv_w, ssm_conv_b, ssm_dt_bias, ssm_a_log, ssm_d, ssm_norm_g, rwkv_mu, rwkv_w0, rwkv_w_up, rwkv_a0, rwkv_a_up, rwkv_g_up, rwkv_k_k, rwkv_k_a, rwkv_r_k, rwkv_ln_g, rwkv_ln_b, w_branch_att, w_branch_ssm, w_branch_rwkv, w_out, ln2_g, w_up, w_down, final_g):
    p = dict(ln1_g=ln1_g, w_in=w_in, ssm_conv_w=ssm_conv_w, ssm_conv_b=ssm_conv_b, ssm_dt_bias=ssm_dt_bias,
             ssm_a_log=ssm_a_log, ssm_d=ssm_d, ssm_norm_g=ssm_norm_g, rwkv_mu=rwkv_mu, rwkv_w0=rwkv_w0,
             rwkv_w_up=rwkv_w_up, rwkv_a0=rwkv_a0, rwkv_a_up=rwkv_a_up, rwkv_g_up=rwkv_g_up, rwkv_k_k=rwkv_k_k,
             rwkv_k_a=rwkv_k_a, rwkv_r_k=rwkv_r_k, rwkv_ln_g=rwkv_ln_g, rwkv_ln_b=rwkv_ln_b,
             w_branch_att=w_branch_att, w_branch_ssm=w_branch_ssm, w_branch_rwkv=w_branch_rwkv, w_out=w_out,
             ln2_g=ln2_g, w_up=w_up, w_down=w_down)
    depth = w_in.shape[0]
    bsz, t, _ = x_prompt.shape
    bd = x_sample.shape[0]
    slopes = _alibi_slopes()
    caches = (cache_att_k0, cache_att_v0, cache_att_k1, cache_att_v1, cache_att_k2, cache_att_v2)
    st = (caches, state_ssm_conv, state_ssm, state_rwkv_shift, state_rwkv)
    xp = x_prompt.reshape(bsz * t, D_MODEL)
    xs = x_sample.reshape(bd, D_MODEL)
    big = _pack_big(p)
    p_new, s_new, s_rows = [], [], []
    for l in range(depth):
        lw = _pack_layer(l, p)
        xp, sp = _prompt_layer(xp, lw, big, l, slopes, bsz, t)
        xs, rows, ss = _sample_layer(xs, lw, big, slopes, st, l, bd)
        p_new.append(sp)
        s_new.append(ss)
        s_rows.append(rows)
    y_prompt = rmsnorm(xp, final_g, F32).reshape(bsz, t, D_MODEL)
    y_sample = rmsnorm(xs, final_g, F32).reshape(bd, 1, D_MODEL)
    stack = lambda per_layer, n: tuple(jnp.stack([s[i] for s in per_layer]) for i in range(n))
    new_rows = stack(s_rows, 6)
    s_caches = ()
    for g in range(3):
        s_caches += tuple(cache_update(caches[2 * g], caches[2 * g + 1], new_rows[2 * g], new_rows[2 * g + 1]))
    return (y_prompt, y_sample) + stack(p_new, 10) + s_caches + stack(s_new, 4)
```

```python
import functools

import jax
import jax.numpy as jnp
from jax import lax
from jax.experimental import pallas as pl
from jax.experimental.pallas import tpu as pltpu

F32 = jnp.float32
BF16 = jnp.bfloat16

D_MODEL = 2048
HEAD_DIM = 64
BRANCH_W = 1536
NORM_EPS = 1e-5
ATT_WINDOWS = (128, 512, 2048)
ATT_DILS = (1, 4, 16)
ATT_SPAN = 128
ATT_HEADS = 8
ATT_OUT = ATT_HEADS * HEAD_DIM
ATT_SCALE = HEAD_DIM ** -0.5
SSM_HEADS = 24
SSM_GROUPS = 4
SSM_STATE = 128
SSM_CONV = 4
SSM_CHUNK = 128
RWKV_HEADS = 24
LORA_W = 96
LORA_A = 96
LORA_G = 256
RWKV_LN_EPS = 64e-5
D_FF = 4 * D_MODEL
RWKV_CHUNK = 64
N_PAIRS = BRANCH_W // 128

LANES = 128
SUBLANES = 8
VMEM_LIMIT_BYTES = 56 * 1024 * 1024

C_Q, C_K, C_V, C_Z, C_X, C_R, C_RK, C_RV = (i * BRANCH_W for i in range(8))
C_G = 8 * BRANCH_W
N_MAIN = C_G + 3 * D_MODEL
T_B = 0
T_C = 512
T_L = 1024
N_TAIL = T_L + 512
DT_LANE0 = LORA_W

_O_QKV, _O_Z, _O_XBC, _O_DT, _O_RW, _O_GATE = 0, 4608, 6144, 8704, 8728, 13784
_O_LORA = _O_RW + 3 * BRANCH_W
_W_IN_COLS = _O_GATE + 3 * D_MODEL
PROJ_TN = 768


def _cparams(sem):
    return pltpu.CompilerParams(dimension_semantics=sem, vmem_limit_bytes=VMEM_LIMIT_BYTES)


def _softplus(x):
    return jnp.maximum(x, 0.0) + jnp.log1p(jnp.exp(-jnp.abs(x)))


def _sigmoid(x):
    return 1.0 / (1.0 + jnp.exp(-x))


def _silu(x):
    return x * _sigmoid(x)


def _split(x, terms):
    out = []
    for _ in range(terms - 1):
        hi = x.astype(BF16)
        out.append(hi)
        x = x - hi.astype(F32)
    out.append(x.astype(BF16))
    return out


def _dot_exact_rhs(x, mat, terms=3):
    parts = [jnp.dot(t, mat, preferred_element_type=F32) for t in _split(x, terms)]
    return functools.reduce(lambda a, b: a + b, parts)


def _dot_exact_lhs(mat, x, terms=3):
    parts = [jnp.dot(mat, t, preferred_element_type=F32) for t in _split(x, terms)]
    return functools.reduce(lambda a, b: a + b, parts)


def _dot_nt(a, b):
    return lax.dot_general(a, b, (((1,), (1,)), ((), ())), preferred_element_type=F32)


def _dot_tn(a, b):
    return lax.dot_general(a, b, (((0,), (0,)), ((), ())), preferred_element_type=F32)


def _rmsnorm_kernel(x_ref, g_ref, o_ref):
    x = x_ref[...]
    y = x * lax.rsqrt(jnp.mean(x * x, axis=-1, keepdims=True) + NORM_EPS)
    o_ref[...] = (y * g_ref[...]).astype(o_ref.dtype)


def rmsnorm(x, g, out_dtype):
    m, d = x.shape
    tm = min(m, 512)
    return pl.pallas_call(
        _rmsnorm_kernel,
        grid=(m // tm,),
        in_specs=[pl.BlockSpec((tm, d), lambda i: (i, 0)), pl.BlockSpec((1, d), lambda i: (0, 0))],
        out_specs=pl.BlockSpec((tm, d), lambda i: (i, 0)),
        out_shape=jax.ShapeDtypeStruct((m, d), out_dtype),
        compiler_params=_cparams(("parallel",)),
        name="rmsnorm",
    )(x, g.reshape(1, d))


def _mm_kernel(*refs, nk, epi):
    if epi == "res":
        x_ref, w_ref, r_ref, o_ref = refs[:4]
    else:
        x_ref, w_ref, o_ref = refs[:3]
        r_ref = None

    def finish(a):
        if epi == "res":
            a = r_ref[...] + a
        elif epi == "relu2":
            a = jnp.square(jnp.maximum(a, 0.0))
        o_ref[...] = a.astype(o_ref.dtype)

    part = jnp.dot(x_ref[...], w_ref[...], preferred_element_type=F32)
    if nk == 1:
        finish(part)
        return
    acc_ref = refs[-1]
    k = pl.program_id(2)

    @pl.when(k == 0)
    def _():
        acc_ref[...] = part

    @pl.when(k > 0)
    def _():
        acc_ref[...] += part

    @pl.when(k == nk - 1)
    def _():
        finish(acc_ref[...])


def _pick(n, cands):
    for c in cands:
        if n % c == 0:
            return c
    return n


def matmul(x, w, layer, *, epi="none", res=None, out_dtype=F32):
    m, kdim = x.shape
    n = w.shape[2]
    tm = min(m, 1024)
    tn = _pick(n, (1536, 1024, 512, 256, 128)) if m > 64 else _pick(n, (2048, 1536, 1024, 512, 256, 128))
    tk = min(kdim, 2048)
    nk = kdim // tk
    in_specs = [pl.BlockSpec((tm, tk), lambda i, j, k: (i, k)),
                pl.BlockSpec((None, tk, tn), lambda i, j, k: (layer, k, j))]
    args = [x, w]
    if epi == "res":
        in_specs.append(pl.BlockSpec((tm, tn), lambda i, j, k: (i, j)))
        args.append(res)
    scratch = [pltpu.VMEM((tm, tn), F32)] if nk > 1 else []
    return pl.pallas_call(
        functools.partial(_mm_kernel, nk=nk, epi=epi),
        grid=(m // tm, n // tn, nk),
        in_specs=in_specs,
        out_specs=pl.BlockSpec((tm, tn), lambda i, j, k: (i, j)),
        out_shape=jax.ShapeDtypeStruct((m, n), out_dtype),
        scratch_shapes=scratch,
        compiler_params=_cparams(("parallel", "parallel", "arbitrary")),
        name="matmul_" + epi,
    )(*args)


def _proj_kernel(x_ref, w_ref, o_ref, wb_scr):
    @pl.when(pl.program_id(1) == 0)
    def _():
        wb_scr[...] = w_ref[0].astype(BF16)

    o_ref[...] = _dot_nt(x_ref[...], wb_scr[...])


def _main_row_start(j):
    t_rw, t_gate = C_R // PROJ_TN, C_G // PROJ_TN
    start = j * PROJ_TN + jnp.where(j >= t_rw, _O_RW - C_R, 0) + jnp.where(j >= t_gate, (_O_GATE - C_G) - (_O_RW - C_R), 0)
    return pl.multiple_of(start, SUBLANES)


def project(h, w_t, layer, n_out, row_start):
    m, kdim = h.shape
    tm = min(m, 1024)
    ell = pl.Element
    return pl.pallas_call(
        _proj_kernel,
        grid=(n_out // PROJ_TN, m // tm),
        in_specs=[pl.BlockSpec((tm, kdim), lambda j, i: (i, 0)),
                  pl.BlockSpec((ell(1), ell(PROJ_TN), ell(kdim)), lambda j, i: (layer, row_start(j), 0))],
        out_specs=pl.BlockSpec((tm, PROJ_TN), lambda j, i: (i, j)),
        out_shape=jax.ShapeDtypeStruct((m, n_out), F32),
        scratch_shapes=[pltpu.VMEM((PROJ_TN, kdim), BF16)],
        compiler_params=_cparams(("parallel", "arbitrary")),
        name="project",
    )(h, w_t)


def project_main_and_tail(h, big, layer):
    main = project(h, big["w_in_t"], layer, N_MAIN, _main_row_start)
    tail = project(h, big["w_tail_t"], layer, N_TAIL, lambda j: pl.multiple_of(j * PROJ_TN, SUBLANES))
    return main, tail


def _merge_kernel(oa_ref, os_ref, or_ref, wa_ref, ws_ref, wr_ref, ga_ref, gs_ref, gr_ref, o_ref):
    f = lambda a, b: jnp.dot(a[...], b[...], preferred_element_type=F32)
    acc = _sigmoid(ga_ref[...]) * f(oa_ref, wa_ref)
    acc = acc + _sigmoid(gs_ref[...]) * f(os_ref, ws_ref)
    acc = acc + _sigmoid(gr_ref[...]) * f(or_ref, wr_ref)
    o_ref[...] = acc.astype(o_ref.dtype)


def merge_branches(o_att, o_ssm, o_rwkv, wa, ws, wr, proj, layer):
    m = o_att.shape[0]
    tm = min(m, 512)
    tn = 1024
    gb = C_G // tn
    nj = D_MODEL // tn
    row = lambda width: pl.BlockSpec((tm, width), lambda j, i: (i, 0))
    col = lambda k: pl.BlockSpec((None, k, tn), lambda j, i: (layer, 0, j))
    gate = lambda g: pl.BlockSpec((tm, tn), lambda j, i, g=g: (i, gb + g * nj + j))
    return pl.pallas_call(
        _merge_kernel,
        grid=(nj, m // tm),
        in_specs=[row(ATT_OUT), row(BRANCH_W), row(BRANCH_W), col(ATT_OUT), col(BRANCH_W), col(BRANCH_W),
                  gate(0), gate(1), gate(2)],
        out_specs=pl.BlockSpec((tm, tn), lambda j, i: (i, j)),
        out_shape=jax.ShapeDtypeStruct((m, D_MODEL), BF16),
        compiler_params=_cparams(("parallel", "parallel")),
        name="merge_branches",
    )(o_att, o_ssm, o_rwkv, wa, ws, wr, proj, proj, proj)


def _att_prompt_kernel(sl_ref, q_ref, kc_ref, kp_ref, vc_ref, vp_ref, o_ref, lse_ref, *, d, nblk, g):
    j = pl.program_id(1)
    hp = pl.program_id(2)
    lane = lax.broadcasted_iota(jnp.int32, (1, LANES), 1)
    head0 = lane < HEAD_DIM
    qi = lax.broadcasted_iota(jnp.int32, (ATT_SPAN, 2 * ATT_SPAN), 0)
    kj = lax.broadcasted_iota(jnp.int32, (ATT_SPAN, 2 * ATT_SPAN), 1)
    delta = qi - kj + ATT_SPAN
    band = (delta >= 0) & (delta <= ATT_SPAN)
    first_ok = band & ((kj >= ATT_SPAN) | (j > 0))
    dist = (delta * d).astype(F32)
    slopes = (sl_ref[g, 2 * hp], sl_ref[g, 2 * hp + 1])

    def rows(base):
        return pl.ds(base, ATT_SPAN) if d == 1 else pl.ds(base, ATT_SPAN, stride=d)

    for ub in range(nblk):
        valid = first_ok if ub == 0 else band
        for r in range(d):
            base = ub * ATT_SPAN * d + r
            q = q_ref[rows(base), :]
            if ub == 0:
                kp, vp = kp_ref[rows(r), :], vp_ref[rows(r), :]
            else:
                kp, vp = kc_ref[rows(base - ATT_SPAN * d), :], vc_ref[rows(base - ATT_SPAN * d), :]
            k = jnp.concatenate([kp, kc_ref[rows(base), :]], axis=0).astype(BF16)
            v = jnp.concatenate([vp, vc_ref[rows(base), :]], axis=0).astype(BF16)
            outs, lses = [], []
            for e in range(2):
                sel = head0 if e == 0 else jnp.logical_not(head0)
                qe = jnp.where(sel, q, 0.0).astype(BF16)
                s = _dot_nt(qe, k) * ATT_SCALE
                s = jnp.where(valid, s - slopes[e] * dist, -jnp.inf)
                mx = jnp.max(s, axis=-1, keepdims=True)
                p = jnp.exp(s - mx)
                den = jnp.sum(p, axis=-1, keepdims=True)
                outs.append(jnp.dot(p.astype(BF16), v, preferred_element_type=F32) / den)
                lses.append(mx + jnp.log(den))
            o_ref[rows(base), :] = jnp.where(head0, outs[0], outs[1])
            lse_ref[rows(base), :] = jnp.where(head0, lses[0], lses[1])


def att_prompt_group(proj, slopes, g, bsz, t):
    d = ATT_DILS[g]
    sb = ATT_SPAN * d
    tb = max(sb, 512)
    nblk = tb // sb
    nt = t // tb
    cq, ck, cv = (C_Q + g * ATT_OUT) // LANES, (C_K + g * ATT_OUT) // LANES, (C_V + g * ATT_OUT) // LANES
    cur = lambda c0: pl.BlockSpec((tb, LANES), lambda b, j, h: (b * nt + j, c0 + h))
    prev = lambda c0: pl.BlockSpec((sb, LANES), lambda b, j, h: (jnp.maximum((b * nt + j) * nblk - 1, 0), c0 + h))
    out = pl.BlockSpec((tb, LANES), lambda b, j, h: (b * nt + j, h))
    shp = jax.ShapeDtypeStruct((bsz * t, ATT_OUT), F32)
    return pl.pallas_call(
        functools.partial(_att_prompt_kernel, d=d, nblk=nblk, g=g),
        grid=(bsz, nt, ATT_HEADS // 2),
        in_specs=[pl.BlockSpec(memory_space=pltpu.SMEM), cur(cq), cur(ck), prev(ck), cur(cv), prev(cv)],
        out_specs=[out, out],
        out_shape=[shp, shp],
        compiler_params=_cparams(("parallel", "parallel", "parallel")),
        name=f"att_prompt_g{g}",
    )(slopes, proj, proj, proj, proj, proj)


def _att_combine_kernel(o0, l0, o1, l1, o2, l2, out_ref):
    a, b, c = l0[...], l1[...], l2[...]
    mx = jnp.maximum(jnp.maximum(a, b), c)
    ea, eb, ec = jnp.exp(a - mx), jnp.exp(b - mx), jnp.exp(c - mx)
    num = ea * o0[...] + eb * o1[...] + ec * o2[...]
    out_ref[...] = (num / (ea + eb + ec)).astype(out_ref.dtype)


def att_combine(parts):
    m = parts[0].shape[0]
    tm = min(m, 1024)
    spec = pl.BlockSpec((tm, ATT_OUT), lambda i: (i, 0))
    return pl.pallas_call(
        _att_combine_kernel,
        grid=(m // tm,),
        in_specs=[spec] * 6,
        out_specs=spec,
        out_shape=jax.ShapeDtypeStruct((m, ATT_OUT), BF16),
        compiler_params=_cparams(("parallel",)),
        name="att_combine",
    )(*parts)


def _att_sample_kernel(sl_ref, qkv_ref, k0, v0, k1, v1, k2, v2, o_ref):
    caches = ((k0, v0), (k1, v1), (k2, v2))
    outs, lses = [], []
    for g in range(3):
        d = ATT_DILS[g]
        assert d & (d - 1) == 0
        win = caches[g][0].shape[-1]
        q, kn, vn = qkv_ref[0, g], qkv_ref[1, g], qkv_ref[2, g]
        kc = caches[g][0][...]
        vc = caches[g][1][...]
        pos = lax.broadcasted_iota(jnp.int32, (1, 1, win), 2)
        on_grid = (pos & (d - 1)) == 0
        dist = (win - pos).astype(F32)
        s = jnp.sum(kc * q, axis=1, keepdims=True) * ATT_SCALE - sl_ref[g] * dist
        s = jnp.where(on_grid, s, -jnp.inf)
        s_new = jnp.sum(kn * q, axis=1, keepdims=True) * ATT_SCALE
        mx = jnp.maximum(jnp.max(s, axis=2, keepdims=True), s_new)
        p = jnp.exp(s - mx)
        p_new = jnp.exp(s_new - mx)
        den = jnp.sum(p, axis=2, keepdims=True) + p_new
        outs.append((jnp.sum(vc * p, axis=2, keepdims=True) + p_new * vn) / den)
        lses.append(mx + jnp.log(den))
    mx = jnp.maximum(jnp.maximum(lses[0], lses[1]), lses[2])
    es = [jnp.exp(l - mx) for l in lses]
    o_ref[...] = (es[0] * outs[0] + es[1] * outs[1] + es[2] * outs[2]) / (es[0] + es[1] + es[2])


def att_sample(qkv, caches_t, slopes, layer):
    bd = qkv.shape[0]
    in_specs = [pl.BlockSpec((3, ATT_HEADS, 1, 1), lambda b: (0, 0, 0, 0)),
                pl.BlockSpec((None, 3, 3, ATT_HEADS, HEAD_DIM, 1), lambda b: (b, 0, 0, 0, 0, 0))]
    for c in caches_t:
        in_specs.append(pl.BlockSpec((None, None, ATT_HEADS, HEAD_DIM, c.shape[-1]), lambda b: (layer, b, 0, 0, 0)))
    return pl.pallas_call(
        _att_sample_kernel,
        grid=(bd,),
        in_specs=in_specs,
        out_specs=pl.BlockSpec((None, ATT_HEADS, HEAD_DIM, 1), lambda b: (b, 0, 0, 0)),
        out_shape=jax.ShapeDtypeStruct((bd, ATT_HEADS, HEAD_DIM, 1), F32),
        compiler_params=_cparams(("parallel",)),
        name="att_sample",
    )(slopes.reshape(3, ATT_HEADS, 1, 1), qkv, *caches_t)


def _ssd_prompt_kernel(x_ref, b_ref, c_ref, z_ref, l_ref, cwx_ref, cwb_ref, cwc_ref, cbx_ref, cbb_ref, cbc_ref,
                       dtb_ref, aneg_ref, dskip_ref, ng_ref, o_ref, hout_ref, xpad, bpad, cpad, h_scr, y_scr, *, nc):
    c = pl.program_id(1)
    L = SSM_CHUNK

    @pl.when(c == 0)
    def _():
        xpad[0:SUBLANES, :] = jnp.zeros((SUBLANES, xpad.shape[1]), F32)
        bpad[0:SUBLANES, :] = jnp.zeros((SUBLANES, bpad.shape[1]), F32)
        cpad[0:SUBLANES, :] = jnp.zeros((SUBLANES, cpad.shape[1]), F32)
        h_scr[...] = jnp.zeros_like(h_scr)

    def conv_silu(pad, src_ref, w_ref, bias_ref):
        pad[SUBLANES:SUBLANES + L, :] = src_ref[...]
        acc = bias_ref[...] + w_ref[0:1, :] * pad[pl.ds(SUBLANES - 3, L), :]
        for jj in range(1, SSM_CONV):
            acc = acc + w_ref[jj:jj + 1, :] * pad[pl.ds(SUBLANES - 3 + jj, L), :]
        pad[0:SUBLANES, :] = pad[L:L + SUBLANES, :]
        return _silu(acc)

    xs = conv_silu(xpad, x_ref, cwx_ref, cbx_ref)
    bm = conv_silu(bpad, b_ref, cwb_ref, cbb_ref).astype(BF16)
    cm = conv_silu(cpad, c_ref, cwc_ref, cbc_ref).astype(BF16)

    lane = lax.broadcasted_iota(jnp.int32, (1, LANES), 1)
    head0 = lane < HEAD_DIM
    sub_head0 = lax.broadcasted_iota(jnp.int32, (LANES, 1), 0) < HEAD_DIM
    dt_lanes = (lane >= DT_LANE0) & (lane < DT_LANE0 + SSM_HEADS)
    dtv = jnp.where(dt_lanes, _softplus(l_ref[:, 0:LANES] + dtb_ref[...]), 0.0)
    da = dtv * aneg_ref[...]
    ri = lax.broadcasted_iota(jnp.int32, (L, L), 0)
    ci = lax.broadcasted_iota(jnp.int32, (L, L), 1)
    causal = ri >= ci
    a_cum = _dot_exact_lhs(causal.astype(BF16), da)
    a_cum_t = a_cum.T
    dt_t = dtv.T

    for g in range(SSM_GROUPS):
        bg = bm[:, g * SSM_STATE:(g + 1) * SSM_STATE]
        cg = cm[:, g * SSM_STATE:(g + 1) * SSM_STATE]
        cb = _dot_nt(cg, bg)
        for pp in range(3):
            pair = g * 3 + pp
            xp = xs[:, pair * LANES:(pair + 1) * LANES]
            xpb = xp.astype(BF16)
            yd, sc, ea, cd = [], [], [], []
            for e in range(2):
                hl = DT_LANE0 + 2 * pair + e
                ac_col = a_cum[:, hl:hl + 1]
                ac_row = a_cum_t[hl:hl + 1, :]
                a_last = ac_col[L - 1:L, :]
                dec = jnp.exp(jnp.where(causal, ac_col - ac_row, -jnp.inf))
                wm = (cb * dec * dt_t[hl:hl + 1, :]).astype(BF16)
                yd.append(jnp.dot(wm, xpb, preferred_element_type=F32))
                sc.append(jnp.exp(a_last - ac_col) * dtv[:, hl:hl + 1])
                ea.append(jnp.exp(ac_col))
                cd.append(jnp.exp(a_last))
            hp = h_scr[pair]
            y_off = _dot_nt(cg, hp.astype(BF16)) * jnp.where(head0, ea[0], ea[1])
            xw = (xp * jnp.where(head0, sc[0], sc[1])).astype(BF16)
            h_scr[pair] = jnp.where(sub_head0, cd[0], cd[1]) * hp + _dot_tn(xw, bg)
            y_scr[:, pair * LANES:(pair + 1) * LANES] = jnp.where(head0, yd[0], yd[1]) + y_off

    y = (y_scr[...] + dskip_ref[...] * xs) * _silu(z_ref[...])
    gw = BRANCH_W // SSM_GROUPS
    for g in range(SSM_GROUPS):
        yg = y[:, g * gw:(g + 1) * gw]
        yg = yg * lax.rsqrt(jnp.mean(yg * yg, axis=-1, keepdims=True) + NORM_EPS)
        o_ref[:, g * gw:(g + 1) * gw] = (yg * ng_ref[:, g * gw:(g + 1) * gw]).astype(o_ref.dtype)

    @pl.when(c == nc - 1)
    def _():
        hout_ref[...] = h_scr[...]


def ssd_prompt(proj, tail, lw, bsz, t):
    nc = t // SSM_CHUNK
    L = SSM_CHUNK
    rowblk = lambda width, c0: pl.BlockSpec((L, width), lambda b, c: (b * nc + c, c0 // width))
    par = lambda a: pl.BlockSpec(a.shape, lambda b, c: (0,) * a.ndim)
    params = [lw["cw_x"], lw["cw_b"], lw["cw_c"], lw["cb_x"], lw["cb_b"], lw["cb_c"], lw["dt_bias"], lw["a_neg"],
              lw["d_skip"], lw["ssm_norm_g"]]
    return pl.pallas_call(
        functools.partial(_ssd_prompt_kernel, nc=nc),
        grid=(bsz, nc),
        in_specs=[rowblk(BRANCH_W, C_X), rowblk(512, T_B), rowblk(512, T_C), rowblk(BRANCH_W, C_Z), rowblk(512, T_L)]
        + [par(a) for a in params],
        out_specs=[pl.BlockSpec((L, BRANCH_W), lambda b, c: (b * nc + c, 0)),
                   pl.BlockSpec((None, N_PAIRS, LANES, SSM_STATE), lambda b, c: (b, 0, 0, 0))],
        out_shape=[jax.ShapeDtypeStruct((bsz * t, BRANCH_W), BF16),
                   jax.ShapeDtypeStruct((bsz, N_PAIRS, LANES, SSM_STATE), F32)],
        scratch_shapes=[pltpu.VMEM((L + SUBLANES, BRANCH_W), F32), pltpu.VMEM((L + SUBLANES, 512), F32),
                        pltpu.VMEM((L + SUBLANES, 512), F32), pltpu.VMEM((N_PAIRS, LANES, SSM_STATE), F32),
                        pltpu.VMEM((L, BRANCH_W), F32)],
        compiler_params=_cparams(("parallel", "arbitrary")),
        name="ssd_prompt",
    )(proj, tail, tail, proj, tail, *params)


def _ssd_sample_conv_kernel(new_ref, st_ref, w_ref, b_ref, o_ref):
    acc = b_ref[...] + w_ref[SSM_CONV - 1:SSM_CONV, :] * new_ref[...]
    for jj in range(SSM_CONV - 1):
        acc = acc + w_ref[jj:jj + 1, :] * st_ref[jj]
    o_ref[...] = _silu(acc)


def ssd_sample_conv(xbc_new, conv_st_t, w, b):
    bd, cdim = xbc_new.shape
    return pl.pallas_call(
        _ssd_sample_conv_kernel,
        out_shape=jax.ShapeDtypeStruct((bd, cdim), F32),
        name="ssd_sample_conv",
    )(xbc_new, conv_st_t, w, b.reshape(1, cdim))


def _ssd_sample_step_kernel(h_ref, x_ref, z_ref, b_ref, c_ref, dt_ref, dtb_ref, alog_ref, d_ref, ng_ref, o_ref, hout_ref):
    h = h_ref[...]
    dt = _softplus(dt_ref[...] + dtb_ref[...])
    da = jnp.exp(dt * (-jnp.exp(alog_ref[...])))
    rep = SSM_HEADS // SSM_GROUPS
    bh = jnp.broadcast_to(b_ref[...][:, None], (SSM_GROUPS, rep, 1, SSM_STATE)).reshape(SSM_HEADS, 1, SSM_STATE)
    ch = jnp.broadcast_to(c_ref[...][:, None], (SSM_GROUPS, rep, 1, SSM_STATE)).reshape(SSM_HEADS, 1, SSM_STATE)
    x = x_ref[...]
    hn = da * h + (dt * x) * bh
    hout_ref[...] = hn
    y = jnp.sum(hn * ch, axis=-1, keepdims=True)
    y = (y + d_ref[...] * x) * _silu(z_ref[...])
    y4 = y.reshape(SSM_GROUPS, rep, HEAD_DIM, 1)
    ms = jnp.sum(jnp.sum(y4 * y4, axis=2, keepdims=True), axis=1, keepdims=True) / (rep * HEAD_DIM)
    y4 = y4 * lax.rsqrt(ms + NORM_EPS)
    o_ref[...] = y4.reshape(SSM_HEADS, HEAD_DIM, 1) * ng_ref[...]


def ssd_sample_step(h, x_col, z_col, bmat, cmat, dt_raw, lw, layer):
    bd = x_col.shape[0]
    col = pl.BlockSpec((None, SSM_HEADS, HEAD_DIM, 1), lambda b: (b, 0, 0, 0))
    grp = pl.BlockSpec((None, SSM_GROUPS, 1, SSM_STATE), lambda b: (b, 0, 0, 0))
    hd1 = pl.BlockSpec((SSM_HEADS, 1, 1), lambda b: (0, 0, 0))
    return pl.pallas_call(
        _ssd_sample_step_kernel,
        grid=(bd,),
        in_specs=[pl.BlockSpec((None, None, SSM_HEADS, HEAD_DIM, SSM_STATE), lambda b: (layer, b, 0, 0, 0)),
                  col, col, grp, grp, pl.BlockSpec((None, SSM_HEADS, 1, 1), lambda b: (b, 0, 0, 0)), hd1, hd1, hd1,
                  pl.BlockSpec((SSM_HEADS, HEAD_DIM, 1), lambda b: (0, 0, 0))],
        out_specs=[col, pl.BlockSpec((None, SSM_HEADS, HEAD_DIM, SSM_STATE), lambda b: (b, 0, 0, 0))],
        out_shape=[jax.ShapeDtypeStruct((bd, SSM_HEADS, HEAD_DIM, 1), F32),
                   jax.ShapeDtypeStruct((bd, SSM_HEADS, HEAD_DIM, SSM_STATE), F32)],
        compiler_params=_cparams(("parallel",)),
        name="ssd_sample_step",
    )(h, x_col, z_col, bmat, cmat, dt_raw, lw["dt_bias_h"], lw["a_log_h"], lw["d_h"], lw["ssm_norm_g_col"])


def _rwkv_mix(i, tm, x_refs, h_refs, i_refs, mu_refs, w0_ref, wup_ref, a0_ref, aup_ref, gup_ref):
    def shift(x_ref, h_ref, i_ref, mu_ref):
        x = x_ref[...]
        hrows = h_ref.shape[0]
        prev0 = jnp.where(i == 0, i_ref[...], h_ref[hrows - 1:hrows, :])
        if tm == 1:
            xp = prev0
        else:
            row = lax.broadcasted_iota(jnp.int32, (tm, 1), 0)
            xp = jnp.where(row == 0, prev0, pltpu.roll(x, 1, 0))
        return x + mu_ref[...] * (xp - x)

    ur, uk, uv, ul = (shift(*refs) for refs in zip(x_refs, h_refs, i_refs, mu_refs))
    f = lambda a, w_ref: jnp.dot(a.astype(BF16), w_ref[...], preferred_element_type=F32)
    w_log = -_softplus(-(w0_ref[...] + f(jnp.tanh(ul[:, 0:LANES]), wup_ref))) - 0.5
    a = _sigmoid(a0_ref[...] + f(ul[:, LANES:2 * LANES], aup_ref))
    g = f(_sigmoid(ul[:, 2 * LANES:4 * LANES]), gup_ref)
    return ur, uk, uv, -jnp.exp(w_log), a, g


def _rwkv_prep_kernel(*refs, tm):
    x_refs, h_refs, i_refs, mu_refs = refs[0:4], refs[4:8], refs[8:12], refs[12:16]
    w0_ref, wup_ref, a0_ref, aup_ref, gup_ref, kk_ref, ka_ref = refs[16:23]
    or_ref, old_ref, ok_ref, ov_ref, okk_ref, oa_ref, og_ref = refs[23:30]
    ur, uk, uv, ld, a, g = _rwkv_mix(pl.program_id(1), tm, x_refs, h_refs, i_refs, mu_refs,
                                     w0_ref, wup_ref, a0_ref, aup_ref, gup_ref)
    or_ref[...] = ur
    old_ref[...] = ld
    ok_ref[...] = uk * (1.0 + (a - 1.0) * ka_ref[...])
    ov_ref[...] = uv
    okk_ref[...] = uk * kk_ref[...]
    oa_ref[...] = a
    og_ref[...] = g


def _rwkv_prep_prompt_kernel(*refs, tm):
    x_refs, h_refs, i_refs, mu_refs = refs[0:4], refs[4:8], refs[8:12], refs[12:16]
    w0_ref, wup_ref, a0_ref, aup_ref, gup_ref, kk_ref, ka_ref, rk_ref = refs[16:24]
    opt_ref, ort_ref, oqh_ref, okh_ref, ov_ref, obv_ref, og_ref, oel_ref = refs[24:32]
    L = RWKV_CHUNK
    ur, uk, uv, ld, a, g = _rwkv_mix(pl.program_id(1), tm, x_refs, h_refs, i_refs, mu_refs,
                                     w0_ref, wup_ref, a0_ref, aup_ref, gup_ref)
    kmod = uk * (1.0 + (a - 1.0) * ka_ref[...])
    kkraw = uk * kk_ref[...]
    assert tm == 2 * L
    ri = lax.broadcasted_iota(jnp.int32, (tm, tm), 0)
    ci = lax.broadcasted_iota(jnp.int32, (tm, tm), 1)
    tril = ((ri >= ci) & ((ri < L) == (ci < L))).astype(BF16)
    cum = _dot_exact_lhs(tril, ld, 2)
    e_in = jnp.exp(cum)
    e_inv = jnp.exp(-cum)
    e_ex = jnp.exp(cum - ld)
    ort_ref[...] = (ur * e_in).astype(ort_ref.dtype)
    okh_ref[...] = (kmod * e_inv).astype(okh_ref.dtype)
    ov_ref[...] = uv.astype(ov_ref.dtype)
    og_ref[...] = g.astype(og_ref.dtype)
    for c in range(tm // L):
        oel_ref[c] = e_in[c * L + L - 1:c * L + L, :]
    li = lax.broadcasted_iota(jnp.int32, (LANES, LANES), 0)
    lj = lax.broadcasted_iota(jnp.int32, (LANES, LANES), 1)
    bd_ones = ((li < HEAD_DIM) == (lj < HEAD_DIM)).astype(BF16)
    rkr = ur * kmod * rk_ref[...]
    for p in range(N_PAIRS):
        cs = slice(p * LANES, (p + 1) * LANES)
        kkp = kkraw[:, cs]
        kkn = kkp / jnp.maximum(jnp.sqrt(_dot_exact_rhs(kkp * kkp, bd_ones, 2)), 1e-12)
        opt_ref[:, cs] = (-kkn * e_ex[:, cs]).astype(opt_ref.dtype)
        oqh_ref[:, cs] = (kkn * a[:, cs] * e_inv[:, cs]).astype(oqh_ref.dtype)
        obv_ref[:, cs] = (_dot_exact_rhs(rkr[:, cs], bd_ones, 2) * uv[:, cs]).astype(obv_ref.dtype)


def _rwkv_prep_call(kernel_fn, proj, tail, init, params, bsz, t, tm, out_specs, out_shape, name):
    nt = t // tm
    hrows = SUBLANES if t >= SUBLANES else t
    hper = tm // hrows
    proj = proj.reshape(bsz, t, N_MAIN)
    tail = tail.reshape(bsz, t, N_TAIL)
    blk = lambda width, c0: pl.BlockSpec((None, tm, width), lambda b, i: (b, i, c0 // width))
    halo = lambda width, c0: pl.BlockSpec(
        (None, hrows, width), lambda b, i: (b, jnp.maximum(i * hper - 1, 0), c0 // width))
    ini = lambda width: pl.BlockSpec((None, 1, width), lambda b, i: (b, 0, 0))
    par = lambda a: pl.BlockSpec(a.shape, lambda b, i: (0,) * a.ndim)
    secs = ((BRANCH_W, C_R), (BRANCH_W, C_RK), (BRANCH_W, C_RV), (512, T_L))
    srcs = (proj, proj, proj, tail)
    return pl.pallas_call(
        functools.partial(kernel_fn, tm=tm),
        grid=(bsz, nt),
        in_specs=[blk(*s) for s in secs] + [halo(*s) for s in secs] + [ini(s[0]) for s in secs]
        + [par(a) for a in params],
        out_specs=out_specs,
        out_shape=out_shape,
        compiler_params=_cparams(("parallel", "arbitrary")),
        name=name,
    )(*srcs, *srcs, *init, *params)


def _rwkv_prep_params(lw):
    return [lw["mu_r"], lw["mu_k"], lw["mu_v"], lw["mu_l"], lw["w0"], lw["w_up"], lw["a0"], lw["a_up"], lw["g_up"],
            lw["k_k"], lw["k_a"]]


def rwkv_prep(proj, tail, init, lw, bsz, t):
    tm = min(t, 128)
    out = pl.BlockSpec((None, tm, BRANCH_W), lambda b, i: (b, i, 0))
    shp = jax.ShapeDtypeStruct((bsz, t, BRANCH_W), F32)
    outs = _rwkv_prep_call(_rwkv_prep_kernel, proj, tail, init, _rwkv_prep_params(lw), bsz, t, tm, [out] * 7,
                           [shp] * 7, "rwkv_prep")
    return [o.reshape(bsz * t, BRANCH_W) for o in outs]


def rwkv_prep_prompt(proj, tail, init, lw, bsz, t):
    tm = 128
    nch = tm // RWKV_CHUNK
    out = pl.BlockSpec((None, tm, BRANCH_W), lambda b, i: (b, i, 0))
    shp = jax.ShapeDtypeStruct((bsz, t, BRANCH_W), BF16)
    el_spec = pl.BlockSpec((None, nch, 1, BRANCH_W), lambda b, i: (b, i, 0, 0))
    el_shape = jax.ShapeDtypeStruct((bsz, t // RWKV_CHUNK, 1, BRANCH_W), F32)
    outs = _rwkv_prep_call(_rwkv_prep_prompt_kernel, proj, tail, init, _rwkv_prep_params(lw) + [lw["r_k_row"]], bsz, t,
                           tm, [out] * 7 + [el_spec], [shp] * 7 + [el_shape], "rwkv_prep_prompt")
    return [o.reshape(bsz * t, BRANCH_W) for o in outs[:7]] + [outs[7].reshape(bsz * t // RWKV_CHUNK, 1, BRANCH_W)]


def _rwkv_chain_kernel(pt_ref, rt_ref, qh_ref, kh_ref, v_ref, bv_ref, g_ref, el_ref, lng_ref, lnb_ref, o_ref, sout_ref,
                       s_scr, *, nchunk, nt, npair):
    tstep = pl.program_id(2)
    L = RWKV_CHUNK
    L2 = 2 * L

    @pl.when(tstep == 0)
    def _():
        s_scr[...] = jnp.zeros_like(s_scr)

    lane = lax.broadcasted_iota(jnp.int32, (1, LANES), 1)
    head0 = lane < HEAD_DIM
    ri = lax.broadcasted_iota(jnp.int32, (L2, L2), 0)
    ci = lax.broadcasted_iota(jnp.int32, (L2, L2), 1)
    same = (ri < L) == (ci < L)
    strict = same & (ri > ci)
    incl = same & (ri >= ci)
    eye = (ri == ci).astype(F32)
    bd_ones = same.astype(BF16)
    bf = lambda x: x.astype(BF16)
    dot = lambda a, b: jnp.dot(bf(a), bf(b), preferred_element_type=F32)
    zero = jnp.zeros((), BF16)

    def stack(x):
        return jnp.concatenate([jnp.where(head0, x, zero), jnp.where(head0, zero, x)], axis=0)

    def exact_sum(x):
        hi = bf(x)
        lo = bf(x - hi.astype(F32))
        return (jnp.dot(hi, bd_ones, preferred_element_type=F32) + jnp.dot(lo, bd_ones, preferred_element_type=F32))

    probs = [(c, p) for c in range(nchunk) for p in range(npair)]
    rows = lambda c: pl.ds(c * L, L)
    cols = lambda p: slice(p * LANES, (p + 1) * LANES)
    pm = {k: stack(pt_ref[rows(k[0]), cols(k[1])]) for k in probs}
    rm = {k: stack(rt_ref[rows(k[0]), cols(k[1])]) for k in probs}
    qk = {k: jnp.concatenate([stack(qh_ref[rows(k[0]), cols(k[1])]), stack(kh_ref[rows(k[0]), cols(k[1])])], axis=0)
          for k in probs}
    vm = {k: stack(v_ref[rows(k[0]), cols(k[1])]) for k in probs}
    x = {k: _dot_nt(jnp.concatenate([pm[k], rm[k]], axis=0), qk[k]) for k in probs}
    n_pow = {k: jnp.where(strict, x[k][0:L2, 0:L2], 0.0) for k in probs}
    a_kp = {k: bf(jnp.where(strict, x[k][0:L2, L2:2 * L2], 0.0)) for k in probs}
    r_q = {k: bf(jnp.where(incl, x[k][L2:2 * L2, 0:L2], 0.0)) for k in probs}
    r_k = {k: bf(jnp.where(incl, x[k][L2:2 * L2, L2:2 * L2], 0.0)) for k in probs}
    t_inv = {k: eye + n_pow[k] for k in probs}
    for _ in range(5):
        n_pow = {k: dot(n_pow[k], n_pow[k]) for k in probs}
        t_inv = {k: t_inv[k] + dot(n_pow[k], t_inv[k]) for k in probs}
    t_inv = {k: bf(t_inv[k]) for k in probs}
    av = {k: dot(a_kp[k], vm[k]) for k in probs}
    rkv = {k: dot(r_k[k], vm[k]) for k in probs}

    state = [s_scr[p] for p in range(npair)]
    ys = {}
    for c in range(nchunk):
        ks = [(c, p) for p in range(npair)]
        s0b = {k: bf(state[k[1]]) for k in ks}
        rhs = {k: dot(pm[k], s0b[k]) + av[k] for k in ks}
        rs = {k: dot(rm[k], s0b[k]) + rkv[k] for k in ks}
        u = {k: bf(dot(t_inv[k], rhs[k])) for k in ks}
        upd = {k: _dot_tn(qk[k], jnp.concatenate([u[k], vm[k]], axis=0)) for k in ks}
        for k in ks:
            ys[k] = rs[k] + dot(r_q[k], u[k])
            w_col = jnp.sum(eye * el_ref[c, :, cols(k[1])], axis=1, keepdims=True)
            state[k[1]] = w_col * (state[k[1]] + upd[k])
    for p in range(npair):
        s_scr[p] = state[p]

    y = {k: ys[k][0:L, :] + ys[k][L:L2, :] for k in probs}
    yc = {k: y[k] - exact_sum(y[k]) * (1.0 / HEAD_DIM) for k in probs}
    var = {k: exact_sum(yc[k] * yc[k]) * (1.0 / HEAD_DIM) for k in probs}
    for k in probs:
        cs = cols(k[1])
        yn = yc[k] * lax.rsqrt(var[k] + RWKV_LN_EPS) * lng_ref[:, cs] + lnb_ref[:, cs]
        o_ref[rows(k[0]), cs] = ((yn + bv_ref[rows(k[0]), cs].astype(F32)) * g_ref[rows(k[0]), cs].astype(F32)
                                 ).astype(o_ref.dtype)

    @pl.when(tstep == nt - 1)
    def _():
        sout_ref[...] = s_scr[...]


def rwkv_prompt(parts, lw, bsz, t, npair=6, nchunk=2):
    tb = nchunk * RWKV_CHUNK
    nt = t // tb
    wl = npair * LANES
    blk = pl.BlockSpec((tb, wl), lambda b, p, i: (b * nt + i, p))
    par = pl.BlockSpec((1, wl), lambda b, p, i: (0, p))
    el = pl.BlockSpec((nchunk, 1, wl), lambda b, p, i: (b * nt + i, 0, p))
    return pl.pallas_call(
        functools.partial(_rwkv_chain_kernel, nchunk=nchunk, nt=nt, npair=npair),
        grid=(bsz, N_PAIRS // npair, nt),
        in_specs=[blk] * 7 + [el, par, par],
        out_specs=[blk, pl.BlockSpec((None, npair, LANES, LANES), lambda b, p, i: (b, p, 0, 0))],
        out_shape=[jax.ShapeDtypeStruct((bsz * t, BRANCH_W), BF16),
                   jax.ShapeDtypeStruct((bsz, N_PAIRS, LANES, LANES), F32)],
        scratch_shapes=[pltpu.VMEM((npair, LANES, LANES), F32)],
        compiler_params=_cparams(("parallel", "parallel", "arbitrary")),
        name="rwkv_prompt",
    )(*parts, lw["ln_g_row"], lw["ln_b_row"])


def _rwkv_sample_step_kernel(s_ref, r_ref, ld_ref, k_ref, kk_ref, a_ref, v_ref, g_ref, rk_ref, lng_ref, lnb_ref,
                             o_ref, sout_ref):
    s = s_ref[...]
    r, kmod, kkraw, a = r_ref[...], k_ref[...], kk_ref[...], a_ref[...]
    v = v_ref[...]
    kkn = kkraw / jnp.maximum(jnp.sqrt(jnp.sum(kkraw * kkraw, axis=-1, keepdims=True)), 1e-12)
    sa = jnp.sum(s * (-kkn), axis=-1, keepdims=True)
    sn = s * jnp.exp(ld_ref[...]) + sa * (kkn * a) + v * kmod
    sout_ref[...] = sn
    y = jnp.sum(sn * r, axis=-1, keepdims=True)
    mean = jnp.mean(y, axis=1, keepdims=True)
    yc = y - mean
    var = jnp.mean(yc * yc, axis=1, keepdims=True)
    yn = yc * lax.rsqrt(var + RWKV_LN_EPS) * lng_ref[...] + lnb_ref[...]
    bonus = jnp.sum(r * kmod * rk_ref[...], axis=-1, keepdims=True) * v
    o_ref[...] = (yn + bonus) * g_ref[...]


def rwkv_sample_step(state, rows, cols, lw, layer):
    bd = rows[0].shape[0]
    row = pl.BlockSpec((None, RWKV_HEADS, 1, HEAD_DIM), lambda b: (b, 0, 0, 0))
    col = pl.BlockSpec((None, RWKV_HEADS, HEAD_DIM, 1), lambda b: (b, 0, 0, 0))
    prow = pl.BlockSpec((RWKV_HEADS, 1, HEAD_DIM), lambda b: (0, 0, 0))
    pcol = pl.BlockSpec((RWKV_HEADS, HEAD_DIM, 1), lambda b: (0, 0, 0))
    return pl.pallas_call(
        _rwkv_sample_step_kernel,
        grid=(bd,),
        in_specs=[pl.BlockSpec((None, None, RWKV_HEADS, HEAD_DIM, HEAD_DIM), lambda b: (layer, b, 0, 0, 0))]
        + [row] * 5 + [col] * 2 + [prow, pcol, pcol],
        out_specs=[col, pl.BlockSpec((None, RWKV_HEADS, HEAD_DIM, HEAD_DIM), lambda b: (b, 0, 0, 0))],
        out_shape=[jax.ShapeDtypeStruct((bd, RWKV_HEADS, HEAD_DIM, 1), F32),
                   jax.ShapeDtypeStruct((bd, RWKV_HEADS, HEAD_DIM, HEAD_DIM), F32)],
        compiler_params=_cparams(("parallel",)),
        name="rwkv_sample_step",
    )(state, *rows, *cols, lw["r_k_h"], lw["ln_g_col"], lw["ln_b_col"])


def _cache_update_kernel(k_ref, v_ref, newk_ref, newv_ref, ok_ref, ov_ref):
    win = k_ref.shape[-1]
    is_last = lax.broadcasted_iota(jnp.int32, (1, win), 1) == win - 1
    for c_ref, new_ref, o_ref in ((k_ref, newk_ref, ok_ref), (v_ref, newv_ref, ov_ref)):
        for h in range(ATT_HEADS):
            o_ref[h] = jnp.where(is_last, new_ref[h], pltpu.roll(c_ref[h], win - 1, 1))


def cache_update(cache_k, cache_v, new_k, new_v):
    nl, bd = cache_k.shape[:2]
    win = cache_k.shape[-1]
    blk = pl.BlockSpec((None, None, ATT_HEADS, HEAD_DIM, win), lambda l, b: (l, b, 0, 0, 0))
    new = pl.BlockSpec((None, None, ATT_HEADS, HEAD_DIM, 1), lambda l, b: (l, b, 0, 0, 0))
    shp = jax.ShapeDtypeStruct(cache_k.shape, cache_k.dtype)
    return pl.pallas_call(
        _cache_update_kernel,
        grid=(nl, bd),
        in_specs=[blk, blk, new, new],
        out_specs=[blk, blk],
        out_shape=[shp, shp],
        compiler_params=_cparams(("parallel", "parallel")),
        name="cache_update",
    )(cache_k, cache_v, new_k, new_v)


def _pack_layer(l, p):
    row = lambda a: a.reshape(1, -1)
    pad_rows = lambda a, n: jnp.concatenate([a, jnp.zeros((n - a.shape[0], a.shape[1]), a.dtype)], axis=0).astype(BF16)
    lane_pad = lambda a: jnp.zeros((1, LANES), F32).at[0, DT_LANE0:DT_LANE0 + SSM_HEADS].set(a)
    mu = p["rwkv_mu"][l]
    z1 = lambda n: jnp.zeros((n,), F32)
    cw = p["ssm_conv_w"][l]
    cb = p["ssm_conv_b"][l]
    rep64 = lambda a: jnp.repeat(a, HEAD_DIM)
    return dict(
        ln1_g=p["ln1_g"][l], ln2_g=p["ln2_g"][l],
        cw_x=cw[:, :BRANCH_W], cw_b=cw[:, BRANCH_W:BRANCH_W + 512], cw_c=cw[:, BRANCH_W + 512:],
        cb_x=row(cb[:BRANCH_W]), cb_b=row(cb[BRANCH_W:BRANCH_W + 512]), cb_c=row(cb[BRANCH_W + 512:]),
        conv_w=cw, conv_b=cb,
        dt_bias=lane_pad(p["ssm_dt_bias"][l]), a_neg=lane_pad(-jnp.exp(p["ssm_a_log"][l])),
        d_skip=row(rep64(p["ssm_d"][l])), ssm_norm_g=row(p["ssm_norm_g"][l]),
        dt_bias_h=p["ssm_dt_bias"][l].reshape(SSM_HEADS, 1, 1), a_log_h=p["ssm_a_log"][l].reshape(SSM_HEADS, 1, 1),
        d_h=p["ssm_d"][l].reshape(SSM_HEADS, 1, 1), ssm_norm_g_col=p["ssm_norm_g"][l].reshape(SSM_HEADS, HEAD_DIM, 1),
        mu_r=row(mu[:BRANCH_W]), mu_k=row(mu[BRANCH_W:2 * BRANCH_W]), mu_v=row(mu[2 * BRANCH_W:3 * BRANCH_W]),
        mu_l=row(jnp.concatenate([mu[3 * BRANCH_W:3 * BRANCH_W + LORA_W], z1(32),
                                  mu[3 * BRANCH_W + LORA_W:3 * BRANCH_W + LORA_W + LORA_A], z1(32),
                                  mu[3 * BRANCH_W + LORA_W + LORA_A:]])),
        w0=row(p["rwkv_w0"][l]), w_up=pad_rows(p["rwkv_w_up"][l], LANES), a0=row(p["rwkv_a0"][l]),
        a_up=pad_rows(p["rwkv_a_up"][l], LANES), g_up=p["rwkv_g_up"][l].astype(BF16),
        k_k=row(p["rwkv_k_k"][l]), k_a=row(p["rwkv_k_a"][l]),
        r_k_row=row(p["rwkv_r_k"][l]), ln_g_row=row(p["rwkv_ln_g"][l]), ln_b_row=row(p["rwkv_ln_b"][l]),
        r_k_h=p["rwkv_r_k"][l].reshape(RWKV_HEADS, 1, HEAD_DIM),
        ln_g_col=p["rwkv_ln_g"][l].reshape(RWKV_HEADS, HEAD_DIM, 1),
        ln_b_col=p["rwkv_ln_b"][l].reshape(RWKV_HEADS, HEAD_DIM, 1),
    )


def _pack_big(p):
    w_t = jnp.swapaxes(p["w_in"], 1, 2)
    zrows = lambda n: jnp.zeros((w_t.shape[0], n, D_MODEL), F32)
    w_tail_t = jnp.concatenate([
        w_t[:, _O_XBC + BRANCH_W:_O_DT], w_t[:, _O_LORA:_O_LORA + LORA_W], w_t[:, _O_DT:_O_RW], zrows(8),
        w_t[:, _O_LORA + LORA_W:_O_LORA + LORA_W + LORA_A], zrows(32), w_t[:, _O_LORA + LORA_W + LORA_A:_O_GATE]], axis=1)
    return dict(w_in_t=w_t, w_tail_t=w_tail_t, w_att=p["w_branch_att"].astype(BF16),
                w_ssm=p["w_branch_ssm"].astype(BF16), w_rwkv=p["w_branch_rwkv"].astype(BF16),
                w_out=p["w_out"].astype(BF16), w_up_ff=p["w_up"].astype(BF16), w_down_ff=p["w_down"].astype(BF16))


def _unpack_lora(rows):
    return jnp.concatenate([rows[..., 0:LORA_W], rows[..., LANES:LANES + LORA_A], rows[..., 2 * LANES:]], axis=-1)


def _pack_lora(rows):
    z = lambda n: jnp.zeros(rows.shape[:-1] + (n,), rows.dtype)
    return jnp.concatenate([rows[..., 0:LORA_W], z(32), rows[..., LORA_W:LORA_W + LORA_A], z(32),
                            rows[..., LORA_W + LORA_A:]], axis=-1)


def _alibi_slopes():
    n = 3 * ATT_HEADS
    idx = jnp.arange(1, n + 1, dtype=F32)
    return jnp.exp2(-8.0 * idx / n).reshape(3, ATT_HEADS)


def _finish_layer(x, proj, o_att, o_ssm, o_rwkv, lw, big, layer):
    merged = merge_branches(o_att, o_ssm, o_rwkv, big["w_att"], big["w_ssm"], big["w_rwkv"], proj, layer)
    x = matmul(merged, big["w_out"], layer, epi="res", res=x)
    h2 = rmsnorm(x, lw["ln2_g"], BF16)
    up = matmul(h2, big["w_up_ff"], layer, epi="relu2", out_dtype=BF16)
    return matmul(up, big["w_down_ff"], layer, epi="res", res=x)


def _prompt_layer(x, lw, big, layer, slopes, bsz, t):
    h = rmsnorm(x, lw["ln1_g"], BF16)
    proj, tail = project_main_and_tail(h, big, layer)
    parts = []
    for g in range(3):
        parts += att_prompt_group(proj, slopes, g, bsz, t)
    o_att = att_combine(parts)
    o_ssm, h_fin = ssd_prompt(proj, tail, lw, bsz, t)
    zero_init = [jnp.zeros((bsz, 1, w), F32) for w in (BRANCH_W, BRANCH_W, BRANCH_W, 512)]
    rw = rwkv_prep_prompt(proj, tail, zero_init, lw, bsz, t)
    o_rwkv, s_fin = rwkv_prompt(rw, lw, bsz, t)
    x = _finish_layer(x, proj, o_att, o_ssm, o_rwkv, lw, big, layer)

    p3 = proj.reshape(bsz, t, N_MAIN)
    t3 = tail.reshape(bsz, t, N_TAIL)
    kv = []
    for g in range(3):
        win = ATT_WINDOWS[g]
        for c0 in (C_K, C_V):
            kv.append(p3[:, t - win:, c0 + g * ATT_OUT:c0 + (g + 1) * ATT_OUT].reshape(bsz, win, ATT_HEADS, HEAD_DIM))
    nc = SSM_CONV - 1
    conv_new = jnp.concatenate([p3[:, t - nc:, C_X:C_X + BRANCH_W], t3[:, t - nc:, T_B:T_B + 1024]], -1)
    shift_new = jnp.concatenate([p3[:, t - 1, C_R:C_R + 3 * BRANCH_W], _unpack_lora(t3[:, t - 1, T_L:T_L + 512])], axis=-1)
    ssm_new = h_fin.reshape(bsz, SSM_HEADS, HEAD_DIM, SSM_STATE)
    s6 = s_fin.reshape(bsz, N_PAIRS, 2, HEAD_DIM, 2, HEAD_DIM)
    s_heads = jnp.stack([s6[:, :, 0, :, 0, :], s6[:, :, 1, :, 1, :]], axis=2)
    rwkv_new = jnp.swapaxes(s_heads, -1, -2).reshape(bsz, RWKV_HEADS, HEAD_DIM, HEAD_DIM)
    return x, tuple(kv) + (conv_new, ssm_new, shift_new, rwkv_new)


def _sample_layer(x, lw, big, slopes, st, layer, bd):
    (caches_t, conv_st, ssm_st, shift_st, rwkv_st) = st
    h = rmsnorm(x, lw["ln1_g"], BF16)
    proj, tail = project_main_and_tail(h, big, layer)
    qkv = proj[:, :3 * BRANCH_W].reshape(bd, 3, 3, ATT_HEADS, HEAD_DIM, 1)
    o_att = att_sample(qkv, caches_t, slopes, layer).reshape(bd, ATT_OUT).astype(BF16)
    new_rows = tuple(qkv[:, part, g] for g in range(3) for part in (1, 2))

    xbc_new = jnp.concatenate([proj[:, C_X:C_X + BRANCH_W], tail[:, T_B:T_B + 1024]], axis=-1)
    cst = conv_st[layer]
    xc = ssd_sample_conv(xbc_new, jnp.swapaxes(cst, 0, 1), lw["conv_w"], lw["conv_b"])
    conv_new = jnp.concatenate([cst[:, 1:], xbc_new[:, None]], axis=1)
    x_col = xc[:, :BRANCH_W].reshape(bd, SSM_HEADS, HEAD_DIM, 1)
    z_col = proj[:, C_Z:C_Z + BRANCH_W].reshape(bd, SSM_HEADS, HEAD_DIM, 1)
    bmat = xc[:, BRANCH_W:BRANCH_W + 512].reshape(bd, SSM_GROUPS, 1, SSM_STATE)
    cmat = xc[:, BRANCH_W + 512:].reshape(bd, SSM_GROUPS, 1, SSM_STATE)
    dt_raw = tail[:, T_L + DT_LANE0:T_L + DT_LANE0 + SSM_HEADS].reshape(bd, SSM_HEADS, 1, 1)
    y_col, ssm_new = ssd_sample_step(ssm_st, x_col, z_col, bmat, cmat, dt_raw, lw, layer)
    o_ssm = y_col.reshape(bd, BRANCH_W).astype(BF16)

    sh = shift_st[layer]
    init = [sh[:, None, i * BRANCH_W:(i + 1) * BRANCH_W] for i in range(3)] + [_pack_lora(sh[:, None, 3 * BRANCH_W:])]
    r, ld, kmod, v, kkraw, a, gg = rwkv_prep(proj, tail, init, lw, bd, 1)
    rowf = lambda z: z.reshape(bd, RWKV_HEADS, 1, HEAD_DIM)
    colf = lambda z: z.reshape(bd, RWKV_HEADS, HEAD_DIM, 1)
    o_col, rwkv_new = rwkv_sample_step(rwkv_st, [rowf(r), rowf(ld), rowf(kmod), rowf(kkraw), rowf(a)],
                                       [colf(v), colf(gg)], lw, layer)
    o_rwkv = o_col.reshape(bd, BRANCH_W).astype(BF16)
    shift_new = jnp.concatenate([proj[:, C_R:C_R + 3 * BRANCH_W], _unpack_lora(tail[:, T_L:T_L + 512])], axis=-1)

    x = _finish_layer(x, proj, o_att, o_ssm, o_rwkv, lw, big, layer)
    return x, new_rows, (conv_new, ssm_new, shift_new, rwkv_new)


def kernel(x_prompt, x_sample, cache_att_k0, cache_att_v0, cache_att_k1, cache_att_v1, cache_att_k2, cache_att_v2, state_ssm_conv, state_ssm, state_rwkv_shift, state_rwkv, ln1_g, w_in, ssm_conv_w, ssm_conv_b, ssm_dt_bias, ssm_a_log, ssm_d, ssm_norm_g, rwkv_mu, rwkv_w0, rwkv_w_up, rwkv_a0, rwkv_a_up, rwkv_g_up, rwkv_k_k, rwkv_k_a, rwkv_r_k, rwkv_ln_g, rwkv_ln_b, w_branch_att, w_branch_ssm, w_branch_rwkv, w_out, ln2_g, w_up, w_down, final_g):
    p = dict(ln1_g=ln1_g, w_in=w_in, ssm_conv_w=ssm_conv_w, ssm_conv_b=ssm_conv_b, ssm_dt_bias=ssm_dt_bias,
             ssm_a_log=ssm_a_log, ssm_d=ssm_d, ssm_norm_g=ssm_norm_g, rwkv_mu=rwkv_mu, rwkv_w0=rwkv_w0,
             rwkv_w_up=rwkv_w_up, rwkv_a0=rwkv_a0, rwkv_a_up=rwkv_a_up, rwkv_g_up=rwkv_g_up, rwkv_k_k=rwkv_k_k,
             rwkv_k_a=rwkv_k_a, rwkv_r_k=rwkv_r_k, rwkv_ln_g=rwkv_ln_g, rwkv_ln_b=rwkv_ln_b,
             w_branch_att=w_branch_att, w_branch_ssm=w_branch_ssm, w_branch_rwkv=w_branch_rwkv, w_out=w_out,
             ln2_g=ln2_g, w_up=w_up, w_down=w_down)
    depth = w_in.shape[0]
    bsz, t, _ = x_prompt.shape
    bd = x_sample.shape[0]
    slopes = _alibi_slopes()
    caches = (cache_att_k0, cache_att_v0, cache_att_k1, cache_att_v1, cache_att_k2, cache_att_v2)
    caches_t = tuple(jnp.transpose(c, (0, 1, 3, 4, 2)) for c in caches)
    st = (caches_t, state_ssm_conv, state_ssm, state_rwkv_shift, state_rwkv)
    xp = x_prompt.reshape(bsz * t, D_MODEL)
    xs = x_sample.reshape(bd, D_MODEL)
    big = _pack_big(p)
    p_new, s_new, s_rows = [], [], []
    for l in range(depth):
        lw = _pack_layer(l, p)
        xp, sp = _prompt_layer(xp, lw, big, l, slopes, bsz, t)
        xs, rows, ss = _sample_layer(xs, lw, big, slopes, st, l, bd)
        p_new.append(sp)
        s_new.append(ss)
        s_rows.append(rows)
    y_prompt = rmsnorm(xp, final_g, F32).reshape(bsz, t, D_MODEL)
    y_sample = rmsnorm(xs, final_g, F32).reshape(bd, 1, D_MODEL)
    stack = lambda per_layer, n: tuple(jnp.stack([s[i] for s in per_layer]) for i in range(n))
    new_rows = stack(s_rows, 6)
    s_caches = ()
    for g in range(3):
        upd = cache_update(caches_t[2 * g], caches_t[2 * g + 1], new_rows[2 * g], new_rows[2 * g + 1])
        s_caches += tuple(jnp.transpose(c, (0, 1, 4, 2, 3)) for c in upd)
    return (y_prompt, y_sample) + stack(p_new, 10) + s_caches + stack(s_new, 4)
```

```python
import functools

import jax
import jax.numpy as jnp
from jax import lax
from jax.experimental import pallas as pl
from jax.experimental.pallas import tpu as pltpu

F32 = jnp.float32
BF16 = jnp.bfloat16

D_MODEL = 2048
HEAD_DIM = 64
BRANCH_W = 1536
NORM_EPS = 1e-5
ATT_WINDOWS = (128, 512, 2048)
ATT_DILS = (1, 4, 16)
ATT_SPAN = 128
ATT_HEADS = 8
ATT_OUT = ATT_HEADS * HEAD_DIM
ATT_SCALE = HEAD_DIM ** -0.5
SSM_HEADS = 24
SSM_GROUPS = 4
SSM_STATE = 128
SSM_CONV = 4
SSM_CHUNK = 128
RWKV_HEADS = 24
LORA_W = 96
LORA_A = 96
LORA_G = 256
RWKV_LN_EPS = 64e-5
D_FF = 4 * D_MODEL
RWKV_CHUNK = 64
N_PAIRS = BRANCH_W // 128

LANES = 128
SUBLANES = 8
VMEM_LIMIT_BYTES = 56 * 1024 * 1024

C_Q, C_K, C_V, C_Z, C_X, C_R, C_RK, C_RV = (i * BRANCH_W for i in range(8))
C_G = 8 * BRANCH_W
N_MAIN = C_G + 3 * D_MODEL
T_B = 0
T_C = 512
T_L = 1024
N_TAIL = T_L + 512
DT_LANE0 = LORA_W

_O_QKV, _O_Z, _O_XBC, _O_DT, _O_RW, _O_GATE = 0, 4608, 6144, 8704, 8728, 13784
_O_LORA = _O_RW + 3 * BRANCH_W
_W_IN_COLS = _O_GATE + 3 * D_MODEL
PROJ_TN = 768


def _cparams(sem):
    return pltpu.CompilerParams(dimension_semantics=sem, vmem_limit_bytes=VMEM_LIMIT_BYTES)


def _softplus(x):
    return jnp.maximum(x, 0.0) + jnp.log(1.0 + jnp.exp(-jnp.abs(x)))


def _sigmoid(x):
    return 1.0 / (1.0 + jnp.exp(-x))


def _silu(x):
    return x * _sigmoid(x)


def _split(x, terms):
    out = []
    for _ in range(terms - 1):
        hi = x.astype(BF16)
        out.append(hi)
        x = x - hi.astype(F32)
    out.append(x.astype(BF16))
    return out


def _dot_exact_rhs(x, mat, terms=3):
    parts = [jnp.dot(t, mat, preferred_element_type=F32) for t in _split(x, terms)]
    return functools.reduce(lambda a, b: a + b, parts)


def _dot_exact_lhs(mat, x, terms=3):
    parts = [jnp.dot(mat, t, preferred_element_type=F32) for t in _split(x, terms)]
    return functools.reduce(lambda a, b: a + b, parts)


def _dot_nt(a, b):
    return lax.dot_general(a, b, (((1,), (1,)), ((), ())), preferred_element_type=F32)


def _dot_tn(a, b):
    return lax.dot_general(a, b, (((0,), (0,)), ((), ())), preferred_element_type=F32)


def _rmsnorm_kernel(x_ref, g_ref, o_ref):
    x = x_ref[...]
    y = x * lax.rsqrt(jnp.mean(x * x, axis=-1, keepdims=True) + NORM_EPS)
    o_ref[...] = (y * g_ref[...]).astype(o_ref.dtype)


def rmsnorm(x, g, out_dtype):
    m, d = x.shape
    tm = min(m, 512)
    return pl.pallas_call(
        _rmsnorm_kernel,
        grid=(m // tm,),
        in_specs=[pl.BlockSpec((tm, d), lambda i: (i, 0)), pl.BlockSpec((1, d), lambda i: (0, 0))],
        out_specs=pl.BlockSpec((tm, d), lambda i: (i, 0)),
        out_shape=jax.ShapeDtypeStruct((m, d), out_dtype),
        compiler_params=_cparams(("parallel",)),
        name="rmsnorm",
    )(x, g.reshape(1, d))


def _mm_kernel(*refs, nk, epi):
    if epi == "res":
        x_ref, w_ref, r_ref, o_ref = refs[:4]
    else:
        x_ref, w_ref, o_ref = refs[:3]
        r_ref = None

    def finish(a):
        if epi == "res":
            a = r_ref[...] + a
        elif epi == "relu2":
            a = jnp.square(jnp.maximum(a, 0.0))
        o_ref[...] = a.astype(o_ref.dtype)

    part = jnp.dot(x_ref[...], w_ref[...], preferred_element_type=F32)
    if nk == 1:
        finish(part)
        return
    acc_ref = refs[-1]
    k = pl.program_id(2)

    @pl.when(k == 0)
    def _():
        acc_ref[...] = part

    @pl.when(k > 0)
    def _():
        acc_ref[...] += part

    @pl.when(k == nk - 1)
    def _():
        finish(acc_ref[...])


def _pick(n, cands):
    for c in cands:
        if n % c == 0:
            return c
    return n


def matmul(x, w, layer, *, epi="none", res=None, out_dtype=F32):
    m, kdim = x.shape
    n = w.shape[2]
    tm = min(m, 1024 if epi == "res" else 2048)
    tn = _pick(n, (1536, 1024, 512, 256, 128)) if m > 64 else _pick(n, (2048, 1536, 1024, 512, 256, 128))
    tk = min(kdim, 2048)
    nk = kdim // tk
    in_specs = [pl.BlockSpec((tm, tk), lambda i, j, k: (i, k)),
                pl.BlockSpec((None, tk, tn), lambda i, j, k: (layer, k, j))]
    args = [x, w]
    if epi == "res":
        in_specs.append(pl.BlockSpec((tm, tn), lambda i, j, k: (i, j)))
        args.append(res)
    scratch = [pltpu.VMEM((tm, tn), F32)] if nk > 1 else []
    return pl.pallas_call(
        functools.partial(_mm_kernel, nk=nk, epi=epi),
        grid=(m // tm, n // tn, nk),
        in_specs=in_specs,
        out_specs=pl.BlockSpec((tm, tn), lambda i, j, k: (i, j)),
        out_shape=jax.ShapeDtypeStruct((m, n), out_dtype),
        scratch_shapes=scratch,
        compiler_params=_cparams(("parallel", "parallel", "arbitrary")),
        name="matmul_" + epi,
    )(*args)


def _proj_kernel(x_ref, w_ref, o_ref, wb_scr):
    @pl.when(pl.program_id(1) == 0)
    def _():
        wb_scr[...] = w_ref[0].astype(BF16)

    o_ref[...] = _dot_nt(x_ref[...], wb_scr[...])


def _main_row_start(j):
    t_rw, t_gate = C_R // PROJ_TN, C_G // PROJ_TN
    start = j * PROJ_TN + jnp.where(j >= t_rw, _O_RW - C_R, 0) + jnp.where(j >= t_gate, (_O_GATE - C_G) - (_O_RW - C_R), 0)
    return pl.multiple_of(start, SUBLANES)


def project(h, w_t, layer, n_out, row_start):
    m, kdim = h.shape
    tm = min(m, 2048)
    ell = pl.Element
    return pl.pallas_call(
        _proj_kernel,
        grid=(n_out // PROJ_TN, m // tm),
        in_specs=[pl.BlockSpec((tm, kdim), lambda j, i: (i, 0)),
                  pl.BlockSpec((ell(1), ell(PROJ_TN), ell(kdim)), lambda j, i: (layer, row_start(j), 0))],
        out_specs=pl.BlockSpec((tm, PROJ_TN), lambda j, i: (i, j)),
        out_shape=jax.ShapeDtypeStruct((m, n_out), F32),
        scratch_shapes=[pltpu.VMEM((PROJ_TN, kdim), BF16)],
        compiler_params=_cparams(("parallel", "arbitrary")),
        name="project",
    )(h, w_t)


def project_main_and_tail(h, big, layer):
    main = project(h, big["w_in_t"], layer, N_MAIN, _main_row_start)
    tail = project(h, big["w_tail_t"], layer, N_TAIL, lambda j: pl.multiple_of(j * PROJ_TN, SUBLANES))
    return main, tail


def _merge_kernel(oa_ref, os_ref, or_ref, wa_ref, ws_ref, wr_ref, ga_ref, gs_ref, gr_ref, o_ref):
    f = lambda a, b: jnp.dot(a[...], b[...], preferred_element_type=F32)
    acc = _sigmoid(ga_ref[...]) * f(oa_ref, wa_ref)
    acc = acc + _sigmoid(gs_ref[...]) * f(os_ref, ws_ref)
    acc = acc + _sigmoid(gr_ref[...]) * f(or_ref, wr_ref)
    o_ref[...] = acc.astype(o_ref.dtype)


def merge_branches(o_att, o_ssm, o_rwkv, wa, ws, wr, proj, layer):
    m = o_att.shape[0]
    tm = min(m, 512)
    tn = 1024
    gb = C_G // tn
    nj = D_MODEL // tn
    row = lambda width: pl.BlockSpec((tm, width), lambda j, i: (i, 0))
    col = lambda k: pl.BlockSpec((None, k, tn), lambda j, i: (layer, 0, j))
    gate = lambda g: pl.BlockSpec((tm, tn), lambda j, i, g=g: (i, gb + g * nj + j))
    return pl.pallas_call(
        _merge_kernel,
        grid=(nj, m // tm),
        in_specs=[row(ATT_OUT), row(BRANCH_W), row(BRANCH_W), col(ATT_OUT), col(BRANCH_W), col(BRANCH_W),
                  gate(0), gate(1), gate(2)],
        out_specs=pl.BlockSpec((tm, tn), lambda j, i: (i, j)),
        out_shape=jax.ShapeDtypeStruct((m, D_MODEL), BF16),
        compiler_params=_cparams(("parallel", "parallel")),
        name="merge_branches",
    )(o_att, o_ssm, o_rwkv, wa, ws, wr, proj, proj, proj)


def _att_prompt_kernel(sl_ref, q_ref, kc_ref, kp_ref, vc_ref, vp_ref, o_ref, lse_ref, *, d, nblk, g):
    j = pl.program_id(1)
    hp = pl.program_id(2)
    lane = lax.broadcasted_iota(jnp.int32, (1, LANES), 1)
    head0 = lane < HEAD_DIM
    qi = lax.broadcasted_iota(jnp.int32, (ATT_SPAN, 2 * ATT_SPAN), 0)
    kj = lax.broadcasted_iota(jnp.int32, (ATT_SPAN, 2 * ATT_SPAN), 1)
    delta = qi - kj + ATT_SPAN
    band = (delta >= 0) & (delta <= ATT_SPAN)
    first_ok = band & ((kj >= ATT_SPAN) | (j > 0))
    dist = (delta * d).astype(F32)
    slopes = (sl_ref[g, 2 * hp], sl_ref[g, 2 * hp + 1])

    def rows(base):
        return pl.ds(base, ATT_SPAN) if d == 1 else pl.ds(base, ATT_SPAN, stride=d)

    for ub in range(nblk):
        valid = first_ok if ub == 0 else band
        for r in range(d):
            base = ub * ATT_SPAN * d + r
            q = q_ref[rows(base), :]
            if ub == 0:
                kp, vp = kp_ref[rows(r), :], vp_ref[rows(r), :]
            else:
                kp, vp = kc_ref[rows(base - ATT_SPAN * d), :], vc_ref[rows(base - ATT_SPAN * d), :]
            k = jnp.concatenate([kp, kc_ref[rows(base), :]], axis=0).astype(BF16)
            v = jnp.concatenate([vp, vc_ref[rows(base), :]], axis=0).astype(BF16)
            outs, lses = [], []
            for e in range(2):
                sel = head0 if e == 0 else jnp.logical_not(head0)
                qe = jnp.where(sel, q, 0.0).astype(BF16)
                s = _dot_nt(qe, k) * ATT_SCALE
                s = jnp.where(valid, s - slopes[e] * dist, -jnp.inf)
                mx = jnp.max(s, axis=-1, keepdims=True)
                p = jnp.exp(s - mx)
                den = jnp.sum(p, axis=-1, keepdims=True)
                outs.append(jnp.dot(p.astype(BF16), v, preferred_element_type=F32) / den)
                lses.append(mx + jnp.log(den))
            o_ref[rows(base), :] = jnp.where(head0, outs[0], outs[1])
            lse_ref[rows(base), :] = jnp.where(head0, lses[0], lses[1])


def att_prompt_group(proj, slopes, g, bsz, t):
    d = ATT_DILS[g]
    sb = ATT_SPAN * d
    tb = max(sb, min(t, 2048))
    nblk = tb // sb
    nt = t // tb
    cq, ck, cv = (C_Q + g * ATT_OUT) // LANES, (C_K + g * ATT_OUT) // LANES, (C_V + g * ATT_OUT) // LANES
    cur = lambda c0: pl.BlockSpec((tb, LANES), lambda b, j, h: (b * nt + j, c0 + h))
    prev = lambda c0: pl.BlockSpec((sb, LANES), lambda b, j, h: (jnp.maximum((b * nt + j) * nblk - 1, 0), c0 + h))
    out = pl.BlockSpec((tb, LANES), lambda b, j, h: (b * nt + j, h))
    shp = jax.ShapeDtypeStruct((bsz * t, ATT_OUT), F32)
    return pl.pallas_call(
        functools.partial(_att_prompt_kernel, d=d, nblk=nblk, g=g),
        grid=(bsz, nt, ATT_HEADS // 2),
        in_specs=[pl.BlockSpec(memory_space=pltpu.SMEM), cur(cq), cur(ck), prev(ck), cur(cv), prev(cv)],
        out_specs=[out, out],
        out_shape=[shp, shp],
        compiler_params=_cparams(("parallel", "parallel", "parallel")),
        name=f"att_prompt_g{g}",
    )(slopes, proj, proj, proj, proj, proj)


def _att_combine_kernel(o0, l0, o1, l1, o2, l2, out_ref):
    a, b, c = l0[...], l1[...], l2[...]
    mx = jnp.maximum(jnp.maximum(a, b), c)
    ea, eb, ec = jnp.exp(a - mx), jnp.exp(b - mx), jnp.exp(c - mx)
    num = ea * o0[...] + eb * o1[...] + ec * o2[...]
    out_ref[...] = (num / (ea + eb + ec)).astype(out_ref.dtype)


def att_combine(parts):
    m = parts[0].shape[0]
    tm = min(m, 1024)
    spec = pl.BlockSpec((tm, ATT_OUT), lambda i: (i, 0))
    return pl.pallas_call(
        _att_combine_kernel,
        grid=(m // tm,),
        in_specs=[spec] * 6,
        out_specs=spec,
        out_shape=jax.ShapeDtypeStruct((m, ATT_OUT), BF16),
        compiler_params=_cparams(("parallel",)),
        name="att_combine",
    )(*parts)


def _att_sample_kernel(sl_ref, qkv_ref, k0, v0, k1, v1, k2, v2, o_ref):
    caches = ((k0, v0), (k1, v1), (k2, v2))
    outs, lses = [], []
    for g in range(3):
        d = ATT_DILS[g]
        assert d & (d - 1) == 0
        win = caches[g][0].shape[-1]
        q, kn, vn = qkv_ref[0, g], qkv_ref[1, g], qkv_ref[2, g]
        kc = caches[g][0][...]
        vc = caches[g][1][...]
        pos = lax.broadcasted_iota(jnp.int32, (1, 1, win), 2)
        on_grid = (pos & (d - 1)) == 0
        dist = (win - pos).astype(F32)
        s = jnp.sum(kc * q, axis=1, keepdims=True) * ATT_SCALE - sl_ref[g] * dist
        s = jnp.where(on_grid, s, -jnp.inf)
        s_new = jnp.sum(kn * q, axis=1, keepdims=True) * ATT_SCALE
        mx = jnp.maximum(jnp.max(s, axis=2, keepdims=True), s_new)
        p = jnp.exp(s - mx)
        p_new = jnp.exp(s_new - mx)
        den = jnp.sum(p, axis=2, keepdims=True) + p_new
        outs.append((jnp.sum(vc * p, axis=2, keepdims=True) + p_new * vn) / den)
        lses.append(mx + jnp.log(den))
    mx = jnp.maximum(jnp.maximum(lses[0], lses[1]), lses[2])
    es = [jnp.exp(l - mx) for l in lses]
    o_ref[...] = (es[0] * outs[0] + es[1] * outs[1] + es[2] * outs[2]) / (es[0] + es[1] + es[2])


def att_sample(qkv, caches_t, slopes, layer):
    bd = qkv.shape[0]
    in_specs = [pl.BlockSpec((3, ATT_HEADS, 1, 1), lambda b: (0, 0, 0, 0)),
                pl.BlockSpec((None, 3, 3, ATT_HEADS, HEAD_DIM, 1), lambda b: (b, 0, 0, 0, 0, 0))]
    for c in caches_t:
        in_specs.append(pl.BlockSpec((None, None, ATT_HEADS, HEAD_DIM, c.shape[-1]), lambda b: (layer, b, 0, 0, 0)))
    return pl.pallas_call(
        _att_sample_kernel,
        grid=(bd,),
        in_specs=in_specs,
        out_specs=pl.BlockSpec((None, ATT_HEADS, HEAD_DIM, 1), lambda b: (b, 0, 0, 0)),
        out_shape=jax.ShapeDtypeStruct((bd, ATT_HEADS, HEAD_DIM, 1), F32),
        compiler_params=_cparams(("parallel",)),
        name="att_sample",
    )(slopes.reshape(3, ATT_HEADS, 1, 1), qkv, *caches_t)


def _ssd_prompt_kernel(x_ref, b_ref, c_ref, z_ref, l_ref, cwx_ref, cwb_ref, cwc_ref, cbx_ref, cbb_ref, cbc_ref,
                       dtb_ref, aneg_ref, dskip_ref, ng_ref, o_ref, hout_ref, xpad, bpad, cpad, h_scr, y_scr, *, nc):
    c = pl.program_id(1)
    L = SSM_CHUNK

    @pl.when(c == 0)
    def _():
        xpad[0:SUBLANES, :] = jnp.zeros((SUBLANES, xpad.shape[1]), F32)
        bpad[0:SUBLANES, :] = jnp.zeros((SUBLANES, bpad.shape[1]), F32)
        cpad[0:SUBLANES, :] = jnp.zeros((SUBLANES, cpad.shape[1]), F32)
        h_scr[...] = jnp.zeros_like(h_scr)

    def conv_silu(pad, src_ref, w_ref, bias_ref):
        pad[SUBLANES:SUBLANES + L, :] = src_ref[...]
        acc = bias_ref[...] + w_ref[0:1, :] * pad[pl.ds(SUBLANES - 3, L), :]
        for jj in range(1, SSM_CONV):
            acc = acc + w_ref[jj:jj + 1, :] * pad[pl.ds(SUBLANES - 3 + jj, L), :]
        pad[0:SUBLANES, :] = pad[L:L + SUBLANES, :]
        return _silu(acc)

    xs = conv_silu(xpad, x_ref, cwx_ref, cbx_ref)
    bm = conv_silu(bpad, b_ref, cwb_ref, cbb_ref).astype(BF16)
    cm = conv_silu(cpad, c_ref, cwc_ref, cbc_ref).astype(BF16)

    lane = lax.broadcasted_iota(jnp.int32, (1, LANES), 1)
    head0 = lane < HEAD_DIM
    sub_head0 = lax.broadcasted_iota(jnp.int32, (LANES, 1), 0) < HEAD_DIM
    dt_lanes = (lane >= DT_LANE0) & (lane < DT_LANE0 + SSM_HEADS)
    dtv = jnp.where(dt_lanes, _softplus(l_ref[:, 0:LANES] + dtb_ref[...]), 0.0)
    da = dtv * aneg_ref[...]
    ri = lax.broadcasted_iota(jnp.int32, (L, L), 0)
    ci = lax.broadcasted_iota(jnp.int32, (L, L), 1)
    causal = ri >= ci
    a_cum = _dot_exact_lhs(causal.astype(BF16), da)
    a_cum_t = a_cum.T
    dt_t = dtv.T

    for g in range(SSM_GROUPS):
        bg = bm[:, g * SSM_STATE:(g + 1) * SSM_STATE]
        cg = cm[:, g * SSM_STATE:(g + 1) * SSM_STATE]
        cb = _dot_nt(cg, bg)
        for pp in range(3):
            pair = g * 3 + pp
            xp = xs[:, pair * LANES:(pair + 1) * LANES]
            xpb = xp.astype(BF16)
            yd, sc, ea, cd = [], [], [], []
            for e in range(2):
                hl = DT_LANE0 + 2 * pair + e
                ac_col = a_cum[:, hl:hl + 1]
                ac_row = a_cum_t[hl:hl + 1, :]
                a_last = ac_col[L - 1:L, :]
                dec = jnp.exp(jnp.where(causal, ac_col - ac_row, -jnp.inf))
                wm = (cb * dec * dt_t[hl:hl + 1, :]).astype(BF16)
                yd.append(jnp.dot(wm, xpb, preferred_element_type=F32))
                sc.append(jnp.exp(a_last - ac_col) * dtv[:, hl:hl + 1])
                ea.append(jnp.exp(ac_col))
                cd.append(jnp.exp(a_last))
            hp = h_scr[pair]
            y_off = _dot_nt(cg, hp.astype(BF16)) * jnp.where(head0, ea[0], ea[1])
            xw = (xp * jnp.where(head0, sc[0], sc[1])).astype(BF16)
            h_scr[pair] = jnp.where(sub_head0, cd[0], cd[1]) * hp + _dot_tn(xw, bg)
            y_scr[:, pair * LANES:(pair + 1) * LANES] = jnp.where(head0, yd[0], yd[1]) + y_off

    y = (y_scr[...] + dskip_ref[...] * xs) * _silu(z_ref[...])
    gw = BRANCH_W // SSM_GROUPS
    for g in range(SSM_GROUPS):
        yg = y[:, g * gw:(g + 1) * gw]
        yg = yg * lax.rsqrt(jnp.mean(yg * yg, axis=-1, keepdims=True) + NORM_EPS)
        o_ref[:, g * gw:(g + 1) * gw] = (yg * ng_ref[:, g * gw:(g + 1) * gw]).astype(o_ref.dtype)

    @pl.when(c == nc - 1)
    def _():
        hout_ref[...] = h_scr[...]


def ssd_prompt(proj, tail, lw, bsz, t):
    nc = t // SSM_CHUNK
    L = SSM_CHUNK
    rowblk = lambda width, c0: pl.BlockSpec((L, width), lambda b, c: (b * nc + c, c0 // width))
    par = lambda a: pl.BlockSpec(a.shape, lambda b, c: (0,) * a.ndim)
    params = [lw["cw_x"], lw["cw_b"], lw["cw_c"], lw["cb_x"], lw["cb_b"], lw["cb_c"], lw["dt_bias"], lw["a_neg"],
              lw["d_skip"], lw["ssm_norm_g"]]
    return pl.pallas_call(
        functools.partial(_ssd_prompt_kernel, nc=nc),
        grid=(bsz, nc),
        in_specs=[rowblk(BRANCH_W, C_X), rowblk(512, T_B), rowblk(512, T_C), rowblk(BRANCH_W, C_Z), rowblk(512, T_L)]
        + [par(a) for a in params],
        out_specs=[pl.BlockSpec((L, BRANCH_W), lambda b, c: (b * nc + c, 0)),
                   pl.BlockSpec((None, N_PAIRS, LANES, SSM_STATE), lambda b, c: (b, 0, 0, 0))],
        out_shape=[jax.ShapeDtypeStruct((bsz * t, BRANCH_W), BF16),
                   jax.ShapeDtypeStruct((bsz, N_PAIRS, LANES, SSM_STATE), F32)],
        scratch_shapes=[pltpu.VMEM((L + SUBLANES, BRANCH_W), F32), pltpu.VMEM((L + SUBLANES, 512), F32),
                        pltpu.VMEM((L + SUBLANES, 512), F32), pltpu.VMEM((N_PAIRS, LANES, SSM_STATE), F32),
                        pltpu.VMEM((L, BRANCH_W), F32)],
        compiler_params=_cparams(("parallel", "arbitrary")),
        name="ssd_prompt",
    )(proj, tail, tail, proj, tail, *params)


def _ssd_sample_conv_kernel(new_ref, st_ref, w_ref, b_ref, o_ref):
    acc = b_ref[...] + w_ref[SSM_CONV - 1:SSM_CONV, :] * new_ref[...]
    for jj in range(SSM_CONV - 1):
        acc = acc + w_ref[jj:jj + 1, :] * st_ref[jj]
    o_ref[...] = _silu(acc)


def ssd_sample_conv(xbc_new, conv_st_t, w, b):
    bd, cdim = xbc_new.shape
    return pl.pallas_call(
        _ssd_sample_conv_kernel,
        out_shape=jax.ShapeDtypeStruct((bd, cdim), F32),
        name="ssd_sample_conv",
    )(xbc_new, conv_st_t, w, b.reshape(1, cdim))


def _ssd_sample_step_kernel(h_ref, x_ref, z_ref, b_ref, c_ref, dt_ref, dtb_ref, alog_ref, d_ref, ng_ref, o_ref, hout_ref):
    h = h_ref[...]
    dt = _softplus(dt_ref[...] + dtb_ref[...])
    da = jnp.exp(dt * (-jnp.exp(alog_ref[...])))
    rep = SSM_HEADS // SSM_GROUPS
    bh = jnp.broadcast_to(b_ref[...][:, None], (SSM_GROUPS, rep, 1, SSM_STATE)).reshape(SSM_HEADS, 1, SSM_STATE)
    ch = jnp.broadcast_to(c_ref[...][:, None], (SSM_GROUPS, rep, 1, SSM_STATE)).reshape(SSM_HEADS, 1, SSM_STATE)
    x = x_ref[...]
    hn = da * h + (dt * x) * bh
    hout_ref[...] = hn
    y = jnp.sum(hn * ch, axis=-1, keepdims=True)
    y = (y + d_ref[...] * x) * _silu(z_ref[...])
    y4 = y.reshape(SSM_GROUPS, rep, HEAD_DIM, 1)
    ms = jnp.sum(jnp.sum(y4 * y4, axis=2, keepdims=True), axis=1, keepdims=True) / (rep * HEAD_DIM)
    y4 = y4 * lax.rsqrt(ms + NORM_EPS)
    o_ref[...] = y4.reshape(SSM_HEADS, HEAD_DIM, 1) * ng_ref[...]


def ssd_sample_step(h, x_col, z_col, bmat, cmat, dt_raw, lw, layer):
    bd = x_col.shape[0]
    col = pl.BlockSpec((None, SSM_HEADS, HEAD_DIM, 1), lambda b: (b, 0, 0, 0))
    grp = pl.BlockSpec((None, SSM_GROUPS, 1, SSM_STATE), lambda b: (b, 0, 0, 0))
    hd1 = pl.BlockSpec((SSM_HEADS, 1, 1), lambda b: (0, 0, 0))
    return pl.pallas_call(
        _ssd_sample_step_kernel,
        grid=(bd,),
        in_specs=[pl.BlockSpec((None, None, SSM_HEADS, HEAD_DIM, SSM_STATE), lambda b: (layer, b, 0, 0, 0)),
                  col, col, grp, grp, pl.BlockSpec((None, SSM_HEADS, 1, 1), lambda b: (b, 0, 0, 0)), hd1, hd1, hd1,
                  pl.BlockSpec((SSM_HEADS, HEAD_DIM, 1), lambda b: (0, 0, 0))],
        out_specs=[col, pl.BlockSpec((None, SSM_HEADS, HEAD_DIM, SSM_STATE), lambda b: (b, 0, 0, 0))],
        out_shape=[jax.ShapeDtypeStruct((bd, SSM_HEADS, HEAD_DIM, 1), F32),
                   jax.ShapeDtypeStruct((bd, SSM_HEADS, HEAD_DIM, SSM_STATE), F32)],
        compiler_params=_cparams(("parallel",)),
        name="ssd_sample_step",
    )(h, x_col, z_col, bmat, cmat, dt_raw, lw["dt_bias_h"], lw["a_log_h"], lw["d_h"], lw["ssm_norm_g_col"])


def _rwkv_mix(i, tm, x_refs, h_refs, i_refs, mu_refs, w0_ref, wup_ref, a0_ref, aup_ref, gup_ref):
    def shift(x_ref, h_ref, i_ref, mu_ref):
        x = x_ref[...]
        hrows = h_ref.shape[0]
        prev0 = jnp.where(i == 0, i_ref[...], h_ref[hrows - 1:hrows, :])
        if tm == 1:
            xp = prev0
        else:
            row = lax.broadcasted_iota(jnp.int32, (tm, 1), 0)
            xp = jnp.where(row == 0, prev0, pltpu.roll(x, 1, 0))
        return x + mu_ref[...] * (xp - x)

    ur, uk, uv, ul = (shift(*refs) for refs in zip(x_refs, h_refs, i_refs, mu_refs))
    f = lambda a, w_ref: jnp.dot(a.astype(BF16), w_ref[...], preferred_element_type=F32)
    w_log = -_softplus(-(w0_ref[...] + f(jnp.tanh(ul[:, 0:LANES]), wup_ref))) - 0.5
    a = _sigmoid(a0_ref[...] + f(ul[:, LANES:2 * LANES], aup_ref))
    g = f(_sigmoid(ul[:, 2 * LANES:4 * LANES]), gup_ref)
    return ur, uk, uv, -jnp.exp(w_log), a, g


def _rwkv_prep_kernel(*refs, tm):
    x_refs, h_refs, i_refs, mu_refs = refs[0:4], refs[4:8], refs[8:12], refs[12:16]
    w0_ref, wup_ref, a0_ref, aup_ref, gup_ref, kk_ref, ka_ref = refs[16:23]
    or_ref, old_ref, ok_ref, ov_ref, okk_ref, oa_ref, og_ref = refs[23:30]
    ur, uk, uv, ld, a, g = _rwkv_mix(pl.program_id(1), tm, x_refs, h_refs, i_refs, mu_refs,
                                     w0_ref, wup_ref, a0_ref, aup_ref, gup_ref)
    or_ref[...] = ur
    old_ref[...] = ld
    ok_ref[...] = uk * (1.0 + (a - 1.0) * ka_ref[...])
    ov_ref[...] = uv
    okk_ref[...] = uk * kk_ref[...]
    oa_ref[...] = a
    og_ref[...] = g


def _rwkv_prep_prompt_kernel(*refs, tm):
    x_refs, h_refs, i_refs, mu_refs = refs[0:4], refs[4:8], refs[8:12], refs[12:16]
    w0_ref, wup_ref, a0_ref, aup_ref, gup_ref, kk_ref, ka_ref, rk_ref = refs[16:24]
    opt_ref, ort_ref, oqh_ref, okh_ref, ov_ref, obv_ref, og_ref, oel_ref = refs[24:32]
    L = RWKV_CHUNK
    ur, uk, uv, ld, a, g = _rwkv_mix(pl.program_id(1), tm, x_refs, h_refs, i_refs, mu_refs,
                                     w0_ref, wup_ref, a0_ref, aup_ref, gup_ref)
    kmod = uk * (1.0 + (a - 1.0) * ka_ref[...])
    kkraw = uk * kk_ref[...]
    assert tm == 2 * L
    ri = lax.broadcasted_iota(jnp.int32, (tm, tm), 0)
    ci = lax.broadcasted_iota(jnp.int32, (tm, tm), 1)
    tril = ((ri >= ci) & ((ri < L) == (ci < L))).astype(BF16)
    cum = _dot_exact_lhs(tril, ld, 2)
    e_in = jnp.exp(cum)
    e_inv = jnp.exp(-cum)
    e_ex = jnp.exp(cum - ld)
    ort_ref[...] = (ur * e_in).astype(ort_ref.dtype)
    okh_ref[...] = (kmod * e_inv).astype(okh_ref.dtype)
    ov_ref[...] = uv.astype(ov_ref.dtype)
    og_ref[...] = g.astype(og_ref.dtype)
    for c in range(tm // L):
        oel_ref[c] = e_in[c * L + L - 1:c * L + L, :]
    li = lax.broadcasted_iota(jnp.int32, (LANES, LANES), 0)
    lj = lax.broadcasted_iota(jnp.int32, (LANES, LANES), 1)
    bd_ones = ((li < HEAD_DIM) == (lj < HEAD_DIM)).astype(BF16)
    rkr = ur * kmod * rk_ref[...]
    for p in range(N_PAIRS):
        cs = slice(p * LANES, (p + 1) * LANES)
        kkp = kkraw[:, cs]
        kkn = kkp / jnp.maximum(jnp.sqrt(_dot_exact_rhs(kkp * kkp, bd_ones, 1)), 1e-12)
        opt_ref[:, cs] = (-kkn * e_ex[:, cs]).astype(opt_ref.dtype)
        oqh_ref[:, cs] = (kkn * a[:, cs] * e_inv[:, cs]).astype(oqh_ref.dtype)
        obv_ref[:, cs] = (_dot_exact_rhs(rkr[:, cs], bd_ones, 1) * uv[:, cs]).astype(obv_ref.dtype)


def _rwkv_prep_call(kernel_fn, proj, tail, init, params, bsz, t, tm, out_specs, out_shape, name):
    nt = t // tm
    hrows = SUBLANES if t >= SUBLANES else t
    hper = tm // hrows
    proj = proj.reshape(bsz, t, N_MAIN)
    tail = tail.reshape(bsz, t, N_TAIL)
    blk = lambda width, c0: pl.BlockSpec((None, tm, width), lambda b, i: (b, i, c0 // width))
    halo = lambda width, c0: pl.BlockSpec(
        (None, hrows, width), lambda b, i: (b, jnp.maximum(i * hper - 1, 0), c0 // width))
    ini = lambda width: pl.BlockSpec((None, 1, width), lambda b, i: (b, 0, 0))
    par = lambda a: pl.BlockSpec(a.shape, lambda b, i: (0,) * a.ndim)
    secs = ((BRANCH_W, C_R), (BRANCH_W, C_RK), (BRANCH_W, C_RV), (512, T_L))
    srcs = (proj, proj, proj, tail)
    return pl.pallas_call(
        functools.partial(kernel_fn, tm=tm),
        grid=(bsz, nt),
        in_specs=[blk(*s) for s in secs] + [halo(*s) for s in secs] + [ini(s[0]) for s in secs]
        + [par(a) for a in params],
        out_specs=out_specs,
        out_shape=out_shape,
        compiler_params=_cparams(("parallel", "arbitrary")),
        name=name,
    )(*srcs, *srcs, *init, *params)


def _rwkv_prep_params(lw):
    return [lw["mu_r"], lw["mu_k"], lw["mu_v"], lw["mu_l"], lw["w0"], lw["w_up"], lw["a0"], lw["a_up"], lw["g_up"],
            lw["k_k"], lw["k_a"]]


def rwkv_prep(proj, tail, init, lw, bsz, t):
    tm = min(t, 128)
    out = pl.BlockSpec((None, tm, BRANCH_W), lambda b, i: (b, i, 0))
    shp = jax.ShapeDtypeStruct((bsz, t, BRANCH_W), F32)
    outs = _rwkv_prep_call(_rwkv_prep_kernel, proj, tail, init, _rwkv_prep_params(lw), bsz, t, tm, [out] * 7,
                           [shp] * 7, "rwkv_prep")
    return [o.reshape(bsz * t, BRANCH_W) for o in outs]


def rwkv_prep_prompt(proj, tail, init, lw, bsz, t):
    tm = 128
    nch = tm // RWKV_CHUNK
    out = pl.BlockSpec((None, tm, BRANCH_W), lambda b, i: (b, i, 0))
    shp = jax.ShapeDtypeStruct((bsz, t, BRANCH_W), BF16)
    el_spec = pl.BlockSpec((None, nch, 1, BRANCH_W), lambda b, i: (b, i, 0, 0))
    el_shape = jax.ShapeDtypeStruct((bsz, t // RWKV_CHUNK, 1, BRANCH_W), F32)
    outs = _rwkv_prep_call(_rwkv_prep_prompt_kernel, proj, tail, init, _rwkv_prep_params(lw) + [lw["r_k_row"]], bsz, t,
                           tm, [out] * 7 + [el_spec], [shp] * 7 + [el_shape], "rwkv_prep_prompt")
    return [o.reshape(bsz * t, BRANCH_W) for o in outs[:7]] + [outs[7].reshape(bsz * t // RWKV_CHUNK, 1, BRANCH_W)]


def _rwkv_chain_kernel(pt_ref, rt_ref, qh_ref, kh_ref, v_ref, bv_ref, g_ref, el_ref, lng_ref, lnb_ref, o_ref, sout_ref,
                       s_scr, *, nchunk, nt, npair):
    tstep = pl.program_id(2)
    L = RWKV_CHUNK
    L2 = 2 * L

    @pl.when(tstep == 0)
    def _():
        s_scr[...] = jnp.zeros_like(s_scr)

    lane = lax.broadcasted_iota(jnp.int32, (1, LANES), 1)
    head0 = lane < HEAD_DIM
    ri = lax.broadcasted_iota(jnp.int32, (L2, L2), 0)
    ci = lax.broadcasted_iota(jnp.int32, (L2, L2), 1)
    same = (ri < L) == (ci < L)
    strict = same & (ri > ci)
    incl = same & (ri >= ci)
    eye = (ri == ci).astype(F32)
    bd_ones = same.astype(BF16)
    bf = lambda x: x.astype(BF16)
    dot = lambda a, b: jnp.dot(bf(a), bf(b), preferred_element_type=F32)
    zero = jnp.zeros((), BF16)

    def stack(x):
        return jnp.concatenate([jnp.where(head0, x, zero), jnp.where(head0, zero, x)], axis=0)

    probs = [(c, p) for c in range(nchunk) for p in range(npair)]
    rows = lambda c: pl.ds(c * L, L)
    cols = lambda p: slice(p * LANES, (p + 1) * LANES)
    pm = {k: stack(pt_ref[rows(k[0]), cols(k[1])]) for k in probs}
    rm = {k: stack(rt_ref[rows(k[0]), cols(k[1])]) for k in probs}
    qk = {k: jnp.concatenate([stack(qh_ref[rows(k[0]), cols(k[1])]), stack(kh_ref[rows(k[0]), cols(k[1])])], axis=0)
          for k in probs}
    vm = {k: stack(v_ref[rows(k[0]), cols(k[1])]) for k in probs}
    x = {k: _dot_nt(jnp.concatenate([pm[k], rm[k]], axis=0), qk[k]) for k in probs}
    sums = lambda v: jnp.dot(bf(v), bd_ones, preferred_element_type=F32)
    n_pow = {k: jnp.where(strict, x[k][0:L2, 0:L2], 0.0) for k in probs}
    a_kp = {k: bf(jnp.where(strict, x[k][0:L2, L2:2 * L2], 0.0)) for k in probs}
    r_q = {k: bf(jnp.where(incl, x[k][L2:2 * L2, 0:L2], 0.0)) for k in probs}
    r_k = {k: bf(jnp.where(incl, x[k][L2:2 * L2, L2:2 * L2], 0.0)) for k in probs}
    t_inv = {k: eye + n_pow[k] for k in probs}
    for _ in range(5):
        n_pow = {k: dot(n_pow[k], n_pow[k]) for k in probs}
        t_inv = {k: t_inv[k] + dot(n_pow[k], t_inv[k]) for k in probs}
    t_inv = {k: bf(t_inv[k]) for k in probs}
    pr = {k: jnp.concatenate([pm[k], rm[k]], axis=0) for k in probs}
    akv = {k: dot(jnp.concatenate([a_kp[k], r_k[k]], axis=0), vm[k]) for k in probs}

    state = [s_scr[p] for p in range(npair)]
    ys = {}
    for c in range(nchunk):
        ks = [(c, p) for p in range(npair)]
        ps = {k: dot(pr[k], bf(state[k[1]])) + akv[k] for k in ks}
        rhs = {k: ps[k][0:L2, :] for k in ks}
        rs = {k: ps[k][L2:2 * L2, :] for k in ks}
        u = {k: bf(dot(t_inv[k], rhs[k])) for k in ks}
        upd = {k: _dot_tn(qk[k], jnp.concatenate([u[k], vm[k]], axis=0)) for k in ks}
        for k in ks:
            ys[k] = rs[k] + dot(r_q[k], u[k])
            w_col = jnp.sum(eye * el_ref[c, :, cols(k[1])], axis=1, keepdims=True)
            state[k[1]] = w_col * (state[k[1]] + upd[k])
    for p in range(npair):
        s_scr[p] = state[p]

    y = {k: ys[k][0:L, :] + ys[k][L:L2, :] for k in probs}
    yc = {k: y[k] - sums(y[k]) * (1.0 / HEAD_DIM) for k in probs}
    var = {k: sums(yc[k] * yc[k]) * (1.0 / HEAD_DIM) for k in probs}
    for k in probs:
        cs = cols(k[1])
        yn = yc[k] * lax.rsqrt(var[k] + RWKV_LN_EPS) * lng_ref[:, cs] + lnb_ref[:, cs]
        o_ref[rows(k[0]), cs] = ((yn + bv_ref[rows(k[0]), cs].astype(F32)) * g_ref[rows(k[0]), cs].astype(F32)
                                 ).astype(o_ref.dtype)

    @pl.when(tstep == nt - 1)
    def _():
        sout_ref[...] = s_scr[...]


def rwkv_prompt(parts, lw, bsz, t, npair=6, nchunk=2):
    tb = nchunk * RWKV_CHUNK
    nt = t // tb
    wl = npair * LANES
    blk = pl.BlockSpec((tb, wl), lambda b, p, i: (b * nt + i, p))
    par = pl.BlockSpec((1, wl), lambda b, p, i: (0, p))
    el = pl.BlockSpec((nchunk, 1, wl), lambda b, p, i: (b * nt + i, 0, p))
    return pl.pallas_call(
        functools.partial(_rwkv_chain_kernel, nchunk=nchunk, nt=nt, npair=npair),
        grid=(bsz, N_PAIRS // npair, nt),
        in_specs=[blk] * 7 + [el, par, par],
        out_specs=[blk, pl.BlockSpec((None, npair, LANES, LANES), lambda b, p, i: (b, p, 0, 0))],
        out_shape=[jax.ShapeDtypeStruct((bsz * t, BRANCH_W), BF16),
                   jax.ShapeDtypeStruct((bsz, N_PAIRS, LANES, LANES), F32)],
        scratch_shapes=[pltpu.VMEM((npair, LANES, LANES), F32)],
        compiler_params=_cparams(("parallel", "parallel", "arbitrary")),
        name="rwkv_prompt",
    )(*parts, lw["ln_g_row"], lw["ln_b_row"])


def _rwkv_sample_step_kernel(s_ref, r_ref, ld_ref, k_ref, kk_ref, a_ref, v_ref, g_ref, rk_ref, lng_ref, lnb_ref,
                             o_ref, sout_ref):
    s = s_ref[...]
    r, kmod, kkraw, a = r_ref[...], k_ref[...], kk_ref[...], a_ref[...]
    v = v_ref[...]
    kkn = kkraw / jnp.maximum(jnp.sqrt(jnp.sum(kkraw * kkraw, axis=-1, keepdims=True)), 1e-12)
    sa = jnp.sum(s * (-kkn), axis=-1, keepdims=True)
    sn = s * jnp.exp(ld_ref[...]) + sa * (kkn * a) + v * kmod
    sout_ref[...] = sn
    y = jnp.sum(sn * r, axis=-1, keepdims=True)
    mean = jnp.mean(y, axis=1, keepdims=True)
    yc = y - mean
    var = jnp.mean(yc * yc, axis=1, keepdims=True)
    yn = yc * lax.rsqrt(var + RWKV_LN_EPS) * lng_ref[...] + lnb_ref[...]
    bonus = jnp.sum(r * kmod * rk_ref[...], axis=-1, keepdims=True) * v
    o_ref[...] = (yn + bonus) * g_ref[...]


def rwkv_sample_step(state, rows, cols, lw, layer):
    bd = rows[0].shape[0]
    row = pl.BlockSpec((None, RWKV_HEADS, 1, HEAD_DIM), lambda b: (b, 0, 0, 0))
    col = pl.BlockSpec((None, RWKV_HEADS, HEAD_DIM, 1), lambda b: (b, 0, 0, 0))
    prow = pl.BlockSpec((RWKV_HEADS, 1, HEAD_DIM), lambda b: (0, 0, 0))
    pcol = pl.BlockSpec((RWKV_HEADS, HEAD_DIM, 1), lambda b: (0, 0, 0))
    return pl.pallas_call(
        _rwkv_sample_step_kernel,
        grid=(bd,),
        in_specs=[pl.BlockSpec((None, None, RWKV_HEADS, HEAD_DIM, HEAD_DIM), lambda b: (layer, b, 0, 0, 0))]
        + [row] * 5 + [col] * 2 + [prow, pcol, pcol],
        out_specs=[col, pl.BlockSpec((None, RWKV_HEADS, HEAD_DIM, HEAD_DIM), lambda b: (b, 0, 0, 0))],
        out_shape=[jax.ShapeDtypeStruct((bd, RWKV_HEADS, HEAD_DIM, 1), F32),
                   jax.ShapeDtypeStruct((bd, RWKV_HEADS, HEAD_DIM, HEAD_DIM), F32)],
        compiler_params=_cparams(("parallel",)),
        name="rwkv_sample_step",
    )(state, *rows, *cols, lw["r_k_h"], lw["ln_g_col"], lw["ln_b_col"])


def _cache_update_kernel(k_ref, v_ref, newk_ref, newv_ref, ok_ref, ov_ref):
    win = k_ref.shape[-1]
    is_last = lax.broadcasted_iota(jnp.int32, (1, win), 1) == win - 1
    for c_ref, new_ref, o_ref in ((k_ref, newk_ref, ok_ref), (v_ref, newv_ref, ov_ref)):
        for h in range(ATT_HEADS):
            o_ref[h] = jnp.where(is_last, new_ref[h], pltpu.roll(c_ref[h], win - 1, 1))


def cache_update(cache_k, cache_v, new_k, new_v):
    nl, bd = cache_k.shape[:2]
    win = cache_k.shape[-1]
    blk = pl.BlockSpec((None, None, ATT_HEADS, HEAD_DIM, win), lambda l, b: (l, b, 0, 0, 0))
    new = pl.BlockSpec((None, None, ATT_HEADS, HEAD_DIM, 1), lambda l, b: (l, b, 0, 0, 0))
    shp = jax.ShapeDtypeStruct(cache_k.shape, cache_k.dtype)
    return pl.pallas_call(
        _cache_update_kernel,
        grid=(nl, bd),
        in_specs=[blk, blk, new, new],
        out_specs=[blk, blk],
        out_shape=[shp, shp],
        compiler_params=_cparams(("parallel", "parallel")),
        name="cache_update",
    )(cache_k, cache_v, new_k, new_v)


def _pack_layer(l, p):
    row = lambda a: a.reshape(1, -1)
    pad_rows = lambda a, n: jnp.concatenate([a, jnp.zeros((n - a.shape[0], a.shape[1]), a.dtype)], axis=0).astype(BF16)
    lane_pad = lambda a: jnp.zeros((1, LANES), F32).at[0, DT_LANE0:DT_LANE0 + SSM_HEADS].set(a)
    mu = p["rwkv_mu"][l]
    z1 = lambda n: jnp.zeros((n,), F32)
    cw = p["ssm_conv_w"][l]
    cb = p["ssm_conv_b"][l]
    rep64 = lambda a: jnp.repeat(a, HEAD_DIM)
    return dict(
        ln1_g=p["ln1_g"][l], ln2_g=p["ln2_g"][l],
        cw_x=cw[:, :BRANCH_W], cw_b=cw[:, BRANCH_W:BRANCH_W + 512], cw_c=cw[:, BRANCH_W + 512:],
        cb_x=row(cb[:BRANCH_W]), cb_b=row(cb[BRANCH_W:BRANCH_W + 512]), cb_c=row(cb[BRANCH_W + 512:]),
        conv_w=cw, conv_b=cb,
        dt_bias=lane_pad(p["ssm_dt_bias"][l]), a_neg=lane_pad(-jnp.exp(p["ssm_a_log"][l])),
        d_skip=row(rep64(p["ssm_d"][l])), ssm_norm_g=row(p["ssm_norm_g"][l]),
        dt_bias_h=p["ssm_dt_bias"][l].reshape(SSM_HEADS, 1, 1), a_log_h=p["ssm_a_log"][l].reshape(SSM_HEADS, 1, 1),
        d_h=p["ssm_d"][l].reshape(SSM_HEADS, 1, 1), ssm_norm_g_col=p["ssm_norm_g"][l].reshape(SSM_HEADS, HEAD_DIM, 1),
        mu_r=row(mu[:BRANCH_W]), mu_k=row(mu[BRANCH_W:2 * BRANCH_W]), mu_v=row(mu[2 * BRANCH_W:3 * BRANCH_W]),
        mu_l=row(jnp.concatenate([mu[3 * BRANCH_W:3 * BRANCH_W + LORA_W], z1(32),
                                  mu[3 * BRANCH_W + LORA_W:3 * BRANCH_W + LORA_W + LORA_A], z1(32),
                                  mu[3 * BRANCH_W + LORA_W + LORA_A:]])),
        w0=row(p["rwkv_w0"][l]), w_up=pad_rows(p["rwkv_w_up"][l], LANES), a0=row(p["rwkv_a0"][l]),
        a_up=pad_rows(p["rwkv_a_up"][l], LANES), g_up=p["rwkv_g_up"][l].astype(BF16),
        k_k=row(p["rwkv_k_k"][l]), k_a=row(p["rwkv_k_a"][l]),
        r_k_row=row(p["rwkv_r_k"][l]), ln_g_row=row(p["rwkv_ln_g"][l]), ln_b_row=row(p["rwkv_ln_b"][l]),
        r_k_h=p["rwkv_r_k"][l].reshape(RWKV_HEADS, 1, HEAD_DIM),
        ln_g_col=p["rwkv_ln_g"][l].reshape(RWKV_HEADS, HEAD_DIM, 1),
        ln_b_col=p["rwkv_ln_b"][l].reshape(RWKV_HEADS, HEAD_DIM, 1),
    )


def _pack_big(p):
    w_t = jnp.swapaxes(p["w_in"], 1, 2)
    zrows = lambda n: jnp.zeros((w_t.shape[0], n, D_MODEL), F32)
    w_tail_t = jnp.concatenate([
        w_t[:, _O_XBC + BRANCH_W:_O_DT], w_t[:, _O_LORA:_O_LORA + LORA_W], w_t[:, _O_DT:_O_RW], zrows(8),
        w_t[:, _O_LORA + LORA_W:_O_LORA + LORA_W + LORA_A], zrows(32), w_t[:, _O_LORA + LORA_W + LORA_A:_O_GATE]], axis=1)
    return dict(w_in_t=w_t, w_tail_t=w_tail_t, w_att=p["w_branch_att"].astype(BF16),
                w_ssm=p["w_branch_ssm"].astype(BF16), w_rwkv=p["w_branch_rwkv"].astype(BF16),
                w_out=p["w_out"].astype(BF16), w_up_ff=p["w_up"].astype(BF16), w_down_ff=p["w_down"].astype(BF16))


def _unpack_lora(rows):
    return jnp.concatenate([rows[..., 0:LORA_W], rows[..., LANES:LANES + LORA_A], rows[..., 2 * LANES:]], axis=-1)


def _pack_lora(rows):
    z = lambda n: jnp.zeros(rows.shape[:-1] + (n,), rows.dtype)
    return jnp.concatenate([rows[..., 0:LORA_W], z(32), rows[..., LORA_W:LORA_W + LORA_A], z(32),
                            rows[..., LORA_W + LORA_A:]], axis=-1)


def _alibi_slopes():
    n = 3 * ATT_HEADS
    idx = jnp.arange(1, n + 1, dtype=F32)
    return jnp.exp2(-8.0 * idx / n).reshape(3, ATT_HEADS)


def _finish_layer(x, proj, o_att, o_ssm, o_rwkv, lw, big, layer):
    merged = merge_branches(o_att, o_ssm, o_rwkv, big["w_att"], big["w_ssm"], big["w_rwkv"], proj, layer)
    x = matmul(merged, big["w_out"], layer, epi="res", res=x)
    h2 = rmsnorm(x, lw["ln2_g"], BF16)
    up = matmul(h2, big["w_up_ff"], layer, epi="relu2", out_dtype=BF16)
    return matmul(up, big["w_down_ff"], layer, epi="res", res=x)


def _prompt_layer(x, lw, big, layer, slopes, bsz, t):
    h = rmsnorm(x, lw["ln1_g"], BF16)
    proj, tail = project_main_and_tail(h, big, layer)
    parts = []
    for g in range(3):
        parts += att_prompt_group(proj, slopes, g, bsz, t)
    o_att = att_combine(parts)
    o_ssm, h_fin = ssd_prompt(proj, tail, lw, bsz, t)
    zero_init = [jnp.zeros((bsz, 1, w), F32) for w in (BRANCH_W, BRANCH_W, BRANCH_W, 512)]
    rw = rwkv_prep_prompt(proj, tail, zero_init, lw, bsz, t)
    o_rwkv, s_fin = rwkv_prompt(rw, lw, bsz, t)
    x = _finish_layer(x, proj, o_att, o_ssm, o_rwkv, lw, big, layer)

    p3 = proj.reshape(bsz, t, N_MAIN)
    t3 = tail.reshape(bsz, t, N_TAIL)
    kv = []
    for g in range(3):
        win = ATT_WINDOWS[g]
        for c0 in (C_K, C_V):
            kv.append(p3[:, t - win:, c0 + g * ATT_OUT:c0 + (g + 1) * ATT_OUT].reshape(bsz, win, ATT_HEADS, HEAD_DIM))
    nc = SSM_CONV - 1
    conv_new = jnp.concatenate([p3[:, t - nc:, C_X:C_X + BRANCH_W], t3[:, t - nc:, T_B:T_B + 1024]], -1)
    shift_new = jnp.concatenate([p3[:, t - 1, C_R:C_R + 3 * BRANCH_W], _unpack_lora(t3[:, t - 1, T_L:T_L + 512])], axis=-1)
    ssm_new = h_fin.reshape(bsz, SSM_HEADS, HEAD_DIM, SSM_STATE)
    s6 = s_fin.reshape(bsz, N_PAIRS, 2, HEAD_DIM, 2, HEAD_DIM)
    s_heads = jnp.stack([s6[:, :, 0, :, 0, :], s6[:, :, 1, :, 1, :]], axis=2)
    rwkv_new = jnp.swapaxes(s_heads, -1, -2).reshape(bsz, RWKV_HEADS, HEAD_DIM, HEAD_DIM)
    return x, tuple(kv) + (conv_new, ssm_new, shift_new, rwkv_new)


def _sample_layer(x, lw, big, slopes, st, layer, bd):
    (caches_t, conv_st, ssm_st, shift_st, rwkv_st) = st
    h = rmsnorm(x, lw["ln1_g"], BF16)
    proj, tail = project_main_and_tail(h, big, layer)
    qkv = proj[:, :3 * BRANCH_W].reshape(bd, 3, 3, ATT_HEADS, HEAD_DIM, 1)
    o_att = att_sample(qkv, caches_t, slopes, layer).reshape(bd, ATT_OUT).astype(BF16)
    new_rows = tuple(qkv[:, part, g] for g in range(3) for part in (1, 2))

    xbc_new = jnp.concatenate([proj[:, C_X:C_X + BRANCH_W], tail[:, T_B:T_B + 1024]], axis=-1)
    cst = conv_st[layer]
    xc = ssd_sample_conv(xbc_new, jnp.swapaxes(cst, 0, 1), lw["conv_w"], lw["conv_b"])
    conv_new = jnp.concatenate([cst[:, 1:], xbc_new[:, None]], axis=1)
    x_col = xc[:, :BRANCH_W].reshape(bd, SSM_HEADS, HEAD_DIM, 1)
    z_col = proj[:, C_Z:C_Z + BRANCH_W].reshape(bd, SSM_HEADS, HEAD_DIM, 1)
    bmat = xc[:, BRANCH_W:BRANCH_W + 512].reshape(bd, SSM_GROUPS, 1, SSM_STATE)
    cmat = xc[:, BRANCH_W + 512:].reshape(bd, SSM_GROUPS, 1, SSM_STATE)
    dt_raw = tail[:, T_L + DT_LANE0:T_L + DT_LANE0 + SSM_HEADS].reshape(bd, SSM_HEADS, 1, 1)
    y_col, ssm_new = ssd_sample_step(ssm_st, x_col, z_col, bmat, cmat, dt_raw, lw, layer)
    o_ssm = y_col.reshape(bd, BRANCH_W).astype(BF16)

    sh = shift_st[layer]
    init = [sh[:, None, i * BRANCH_W:(i + 1) * BRANCH_W] for i in range(3)] + [_pack_lora(sh[:, None, 3 * BRANCH_W:])]
    r, ld, kmod, v, kkraw, a, gg = rwkv_prep(proj, tail, init, lw, bd, 1)
    rowf = lambda z: z.reshape(bd, RWKV_HEADS, 1, HEAD_DIM)
    colf = lambda z: z.reshape(bd, RWKV_HEADS, HEAD_DIM, 1)
    o_col, rwkv_new = rwkv_sample_step(rwkv_st, [rowf(r), rowf(ld), rowf(kmod), rowf(kkraw), rowf(a)],
                                       [colf(v), colf(gg)], lw, layer)
    o_rwkv = o_col.reshape(bd, BRANCH_W).astype(BF16)
    shift_new = jnp.concatenate([proj[:, C_R:C_R + 3 * BRANCH_W], _unpack_lora(tail[:, T_L:T_L + 512])], axis=-1)

    x = _finish_layer(x, proj, o_att, o_ssm, o_rwkv, lw, big, layer)
    return x, new_rows, (conv_new, ssm_new, shift_new, rwkv_new)


def kernel(x_prompt, x_sample, cache_att_k0, cache_att_v0, cache_att_k1, cache_att_v1, cache_att_k2, cache_att_v2, state_ssm_conv, state_ssm, state_rwkv_shift, state_rwkv, ln1_g, w_in, ssm_conv_w, ssm_conv_b, ssm_dt_bias, ssm_a_log, ssm_d, ssm_norm_g, rwkv_mu, rwkv_w0, rwkv_w_up, rwkv_a0, rwkv_a_up, rwkv_g_up, rwkv_k_k, rwkv_k_a, rwkv_r_k, rwkv_ln_g, rwkv_ln_b, w_branch_att, w_branch_ssm, w_branch_rwkv, w_out, ln2_g, w_up, w_down, final_g):
    p = dict(ln1_g=ln1_g, w_in=w_in, ssm_conv_w=ssm_conv_w, ssm_conv_b=ssm_conv_b, ssm_dt_bias=ssm_dt_bias,
             ssm_a_log=ssm_a_log, ssm_d=ssm_d, ssm_norm_g=ssm_norm_g, rwkv_mu=rwkv_mu, rwkv_w0=rwkv_w0,
             rwkv_w_up=rwkv_w_up, rwkv_a0=rwkv_a0, rwkv_a_up=rwkv_a_up, rwkv_g_up=rwkv_g_up, rwkv_k_k=rwkv_k_k,
             rwkv_k_a=rwkv_k_a, rwkv_r_k=rwkv_r_k, rwkv_ln_g=rwkv_ln_g, rwkv_ln_b=rwkv_ln_b,
             w_branch_att=w_branch_att, w_branch_ssm=w_branch_ssm, w_branch_rwkv=w_branch_rwkv, w_out=w_out,
             ln2_g=ln2_g, w_up=w_up, w_down=w_down)
    depth = w_in.shape[0]
    bsz, t, _ = x_prompt.shape
    bd = x_sample.shape[0]
    slopes = _alibi_slopes()
    caches = (cache_att_k0, cache_att_v0, cache_att_k1, cache_att_v1, cache_att_k2, cache_att_v2)
    caches_t = tuple(jnp.transpose(c, (0, 1, 3, 4, 2)) for c in caches)
    st = (caches_t, state_ssm_conv, state_ssm, state_rwkv_shift, state_rwkv)
    xp = x_prompt.reshape(bsz * t, D_MODEL)
    xs = x_sample.reshape(bd, D_MODEL)
    big = _pack_big(p)
    p_new, s_new, s_rows = [], [], []
    for l in range(depth):
        lw = _pack_layer(l, p)
        xp, sp = _prompt_layer(xp, lw, big, l, slopes, bsz, t)
        xs, rows, ss = _sample_layer(xs, lw, big, slopes, st, l, bd)
        p_new.append(sp)
        s_new.append(ss)
        s_rows.append(rows)
    y_prompt = rmsnorm(xp, final_g, F32).reshape(bsz, t, D_MODEL)
    y_sample = rmsnorm(xs, final_g, F32).reshape(bd, 1, D_MODEL)
    stack = lambda per_layer, n: tuple(jnp.stack([s[i] for s in per_layer]) for i in range(n))
    new_rows = stack(s_rows, 6)
    s_caches = ()
    for g in range(3):
        upd = cache_update(caches_t[2 * g], caches_t[2 * g + 1], new_rows[2 * g], new_rows[2 * g + 1])
        s_caches += tuple(jnp.transpose(c, (0, 1, 4, 2, 3)) for c in upd)
    return (y_prompt, y_sample) + stack(p_new, 10) + s_caches + stack(s_new, 4)
```

```python
import functools

import jax
import jax.numpy as jnp
from jax import lax
from jax.experimental import pallas as pl
from jax.experimental.pallas import tpu as pltpu

F32 = jnp.float32
BF16 = jnp.bfloat16

D_MODEL = 2048
HEAD_DIM = 64
BRANCH_W = 1536
NORM_EPS = 1e-5
ATT_WINDOWS = (128, 512, 2048)
ATT_DILS = (1, 4, 16)
ATT_SPAN = 128
ATT_HEADS = 8
ATT_OUT = ATT_HEADS * HEAD_DIM
ATT_SCALE = HEAD_DIM ** -0.5
SSM_HEADS = 24
SSM_GROUPS = 4
SSM_STATE = 128
SSM_CONV = 4
SSM_CHUNK = 128
RWKV_HEADS = 24
LORA_W = 96
LORA_A = 96
LORA_G = 256
RWKV_LN_EPS = 64e-5
D_FF = 4 * D_MODEL
RWKV_CHUNK = 64
N_PAIRS = BRANCH_W // 128

LANES = 128
SUBLANES = 8
VMEM_LIMIT_BYTES = 56 * 1024 * 1024

C_Q, C_K, C_V, C_Z, C_X, C_R, C_RK, C_RV = (i * BRANCH_W for i in range(8))
C_G = 8 * BRANCH_W
N_MAIN = C_G + 3 * D_MODEL
T_B = 0
T_C = 512
T_L = 1024
N_TAIL = T_L + 512
DT_LANE0 = LORA_W

_O_QKV, _O_Z, _O_XBC, _O_DT, _O_RW, _O_GATE = 0, 4608, 6144, 8704, 8728, 13784
_O_LORA = _O_RW + 3 * BRANCH_W
_W_IN_COLS = _O_GATE + 3 * D_MODEL
PROJ_TN = 768


def _cparams(sem):
    return pltpu.CompilerParams(dimension_semantics=sem, vmem_limit_bytes=VMEM_LIMIT_BYTES)


def _softplus(x):
    return jnp.maximum(x, 0.0) + jnp.log(1.0 + jnp.exp(-jnp.abs(x)))


def _sigmoid(x):
    return 1.0 / (1.0 + jnp.exp(-x))


def _silu(x):
    return x * _sigmoid(x)


def _split(x, terms):
    out = []
    for _ in range(terms - 1):
        hi = x.astype(BF16)
        out.append(hi)
        x = x - hi.astype(F32)
    out.append(x.astype(BF16))
    return out


def _dot_exact_rhs(x, mat, terms=3):
    parts = [jnp.dot(t, mat, preferred_element_type=F32) for t in _split(x, terms)]
    return functools.reduce(lambda a, b: a + b, parts)


def _dot_exact_lhs(mat, x, terms=3):
    parts = [jnp.dot(mat, t, preferred_element_type=F32) for t in _split(x, terms)]
    return functools.reduce(lambda a, b: a + b, parts)


def _dot_nt(a, b):
    return lax.dot_general(a, b, (((1,), (1,)), ((), ())), preferred_element_type=F32)


def _dot_tn(a, b):
    return lax.dot_general(a, b, (((0,), (0,)), ((), ())), preferred_element_type=F32)


def _rmsnorm_kernel(x_ref, g_ref, o_ref):
    x = x_ref[...]
    y = x * lax.rsqrt(jnp.mean(x * x, axis=-1, keepdims=True) + NORM_EPS)
    o_ref[...] = (y * g_ref[...]).astype(o_ref.dtype)


def rmsnorm(x, g, out_dtype):
    m, d = x.shape
    tm = min(m, 512)
    return pl.pallas_call(
        _rmsnorm_kernel,
        grid=(m // tm,),
        in_specs=[pl.BlockSpec((tm, d), lambda i: (i, 0)), pl.BlockSpec((1, d), lambda i: (0, 0))],
        out_specs=pl.BlockSpec((tm, d), lambda i: (i, 0)),
        out_shape=jax.ShapeDtypeStruct((m, d), out_dtype),
        compiler_params=_cparams(("parallel",)),
        name="rmsnorm",
    )(x, g.reshape(1, d))


def _mm_kernel(*refs, nk, epi):
    refs = list(refs)
    x_ref, w_ref = refs[0:2]
    r_ref = refs.pop(2) if epi == "res" else None
    xs_ref = refs[2]
    rs_ref = refs.pop(3) if epi == "res" else None
    o_ref, os_ref = refs[3:5]
    acc_ref, accs_ref = (refs[5], refs[6]) if nk > 1 else (None, None)
    first_rows = pl.program_id(0) == 0
    k = pl.program_id(2)

    def finish(a, r, o):
        if epi == "res":
            a = r[...] + a
        elif epi == "relu2":
            a = jnp.square(jnp.maximum(a, 0.0))
        o[...] = a.astype(o.dtype)

    def accumulate(x, acc, r, o):
        part = jnp.dot(x[...], w_ref[...], preferred_element_type=F32)
        if nk == 1:
            finish(part, r, o)
            return

        @pl.when(k == 0)
        def _():
            acc[...] = part

        @pl.when(k > 0)
        def _():
            acc[...] += part

        @pl.when(k == nk - 1)
        def _():
            finish(acc[...], r, o)

    accumulate(x_ref, acc_ref, r_ref, o_ref)

    @pl.when(first_rows)
    def _():
        accumulate(xs_ref, accs_ref, rs_ref, os_ref)


def _pick(n, cands):
    for c in cands:
        if n % c == 0:
            return c
    return n


def matmul(x, xs, w, layer, *, epi="none", res=None, res_s=None, out_dtype=F32):
    m, kdim = x.shape
    ms = xs.shape[0]
    n = w.shape[2]
    tm = min(m, 1024 if epi == "res" else 2048)
    tn = _pick(n, (1536, 1024, 512, 256, 128))
    tk = min(kdim, 2048)
    nk = kdim // tk
    nj = n // tn
    side_col = lambda i, j: jnp.where(i == 0, j, nj - 1)
    in_specs = [pl.BlockSpec((tm, tk), lambda i, j, k: (i, k)),
                pl.BlockSpec((None, tk, tn), lambda i, j, k: (layer, k, j))]
    args = [x, w]
    if epi == "res":
        in_specs.append(pl.BlockSpec((tm, tn), lambda i, j, k: (i, j)))
        args.append(res)
    in_specs.append(pl.BlockSpec((ms, tk), lambda i, j, k: (0, k)))
    args.append(xs)
    if epi == "res":
        in_specs.append(pl.BlockSpec((ms, tn), lambda i, j, k: (0, side_col(i, j))))
        args.append(res_s)
    scratch = [pltpu.VMEM((tm, tn), F32), pltpu.VMEM((ms, tn), F32)] if nk > 1 else []
    return pl.pallas_call(
        functools.partial(_mm_kernel, nk=nk, epi=epi),
        grid=(m // tm, nj, nk),
        in_specs=in_specs,
        out_specs=[pl.BlockSpec((tm, tn), lambda i, j, k: (i, j)),
                   pl.BlockSpec((ms, tn), lambda i, j, k: (0, side_col(i, j)))],
        out_shape=[jax.ShapeDtypeStruct((m, n), out_dtype), jax.ShapeDtypeStruct((ms, n), out_dtype)],
        scratch_shapes=scratch,
        compiler_params=_cparams(("arbitrary", "arbitrary", "arbitrary")),
        name="matmul_" + epi,
    )(*args)


def _proj_kernel(x_ref, xs_ref, w_ref, o_ref, os_ref, wb_scr):
    @pl.when(pl.program_id(1) == 0)
    def _():
        wb_scr[...] = w_ref[0].astype(BF16)
        os_ref[...] = _dot_nt(xs_ref[...], wb_scr[...])

    o_ref[...] = _dot_nt(x_ref[...], wb_scr[...])


def _main_row_start(j):
    t_rw, t_gate = C_R // PROJ_TN, C_G // PROJ_TN
    start = j * PROJ_TN + jnp.where(j >= t_rw, _O_RW - C_R, 0) + jnp.where(j >= t_gate, (_O_GATE - C_G) - (_O_RW - C_R), 0)
    return pl.multiple_of(start, SUBLANES)


def project(h, hs, w_t, layer, n_out, row_start):
    m, kdim = h.shape
    ms = hs.shape[0]
    tm = min(m, 2048)
    ell = pl.Element
    return pl.pallas_call(
        _proj_kernel,
        grid=(n_out // PROJ_TN, m // tm),
        in_specs=[pl.BlockSpec((tm, kdim), lambda j, i: (i, 0)),
                  pl.BlockSpec((ms, kdim), lambda j, i: (0, 0)),
                  pl.BlockSpec((ell(1), ell(PROJ_TN), ell(kdim)), lambda j, i: (layer, row_start(j), 0))],
        out_specs=[pl.BlockSpec((tm, PROJ_TN), lambda j, i: (i, j)), pl.BlockSpec((ms, PROJ_TN), lambda j, i: (0, j))],
        out_shape=[jax.ShapeDtypeStruct((m, n_out), F32), jax.ShapeDtypeStruct((ms, n_out), F32)],
        scratch_shapes=[pltpu.VMEM((PROJ_TN, kdim), BF16)],
        compiler_params=_cparams(("parallel", "arbitrary")),
        name="project",
    )(h, hs, w_t)


def project_main_and_tail(h, hs, big, layer):
    main, main_s = project(h, hs, big["w_in_t"], layer, N_MAIN, _main_row_start)
    tail, tail_s = project(h, hs, big["w_tail_t"], layer, N_TAIL, lambda j: pl.multiple_of(j * PROJ_TN, SUBLANES))
    return main, tail, main_s, tail_s


def _merge_kernel(oa_ref, os_ref, or_ref, ga_ref, gs_ref, gr_ref, oa2_ref, os2_ref, or2_ref, ga2_ref, gs2_ref,
                  gr2_ref, wa_ref, ws_ref, wr_ref, o_ref, o2_ref):
    f = lambda a, b: jnp.dot(a[...], b[...], preferred_element_type=F32)

    def merged(a_ref, s_ref, r_ref, g_a, g_s, g_r, out):
        acc = _sigmoid(g_a[...]) * f(a_ref, wa_ref)
        acc = acc + _sigmoid(g_s[...]) * f(s_ref, ws_ref)
        acc = acc + _sigmoid(g_r[...]) * f(r_ref, wr_ref)
        out[...] = acc.astype(out.dtype)

    merged(oa_ref, os_ref, or_ref, ga_ref, gs_ref, gr_ref, o_ref)

    @pl.when(pl.program_id(1) == 0)
    def _():
        merged(oa2_ref, os2_ref, or2_ref, ga2_ref, gs2_ref, gr2_ref, o2_ref)


def merge_branches(outs, proj, outs_s, proj_s, wa, ws, wr, layer):
    m = outs[0].shape[0]
    ms = outs_s[0].shape[0]
    tm = min(m, 512)
    tn = 1024
    gb = C_G // tn
    nj = D_MODEL // tn
    widths = (ATT_OUT, BRANCH_W, BRANCH_W)
    row = lambda width: pl.BlockSpec((tm, width), lambda j, i: (i, 0))
    gate = lambda g: pl.BlockSpec((tm, tn), lambda j, i, g=g: (i, gb + g * nj + j))
    row_s = lambda width: pl.BlockSpec((ms, width), lambda j, i: (0, 0))
    gate_s = lambda g: pl.BlockSpec((ms, tn), lambda j, i, g=g: (0, gb + g * nj + j))
    col = lambda k: pl.BlockSpec((None, k, tn), lambda j, i: (layer, 0, j))
    return pl.pallas_call(
        _merge_kernel,
        grid=(nj, m // tm),
        in_specs=[row(w) for w in widths] + [gate(g) for g in range(3)] + [row_s(w) for w in widths]
        + [gate_s(g) for g in range(3)] + [col(w) for w in widths],
        out_specs=[pl.BlockSpec((tm, tn), lambda j, i: (i, j)), pl.BlockSpec((ms, tn), lambda j, i: (0, j))],
        out_shape=[jax.ShapeDtypeStruct((m, D_MODEL), BF16), jax.ShapeDtypeStruct((ms, D_MODEL), BF16)],
        compiler_params=_cparams(("parallel", "arbitrary")),
        name="merge_branches",
    )(*outs, proj, proj, proj, *outs_s, proj_s, proj_s, proj_s, wa, ws, wr)


def _att_prompt_kernel(sl_ref, *refs, nblks):
    o_ref = refs[15]
    scr = refs[16:]
    for g in range(3):
        _att_group(sl_ref, *refs[5 * g:5 * g + 5], scr[2 * g], scr[2 * g + 1], d=ATT_DILS[g], nblk=nblks[g], g=g)
    a, b, c = scr[1][...], scr[3][...], scr[5][...]
    mx = jnp.maximum(jnp.maximum(a, b), c)
    ea, eb, ec = jnp.exp(a - mx), jnp.exp(b - mx), jnp.exp(c - mx)
    num = ea * scr[0][...] + eb * scr[2][...] + ec * scr[4][...]
    o_ref[...] = (num / (ea + eb + ec)).astype(o_ref.dtype)


def _att_group(sl_ref, q_ref, kc_ref, kp_ref, vc_ref, vp_ref, o_ref, lse_ref, *, d, nblk, g):
    j = pl.program_id(1)
    hp = pl.program_id(2)
    lane = lax.broadcasted_iota(jnp.int32, (1, LANES), 1)
    head0 = lane < HEAD_DIM
    qi = lax.broadcasted_iota(jnp.int32, (ATT_SPAN, 2 * ATT_SPAN), 0)
    kj = lax.broadcasted_iota(jnp.int32, (ATT_SPAN, 2 * ATT_SPAN), 1)
    delta = qi - kj + ATT_SPAN
    band = (delta >= 0) & (delta <= ATT_SPAN)
    first_ok = band & ((kj >= ATT_SPAN) | (j > 0))
    dist = (delta * d).astype(F32)
    slopes = (sl_ref[g, 2 * hp], sl_ref[g, 2 * hp + 1])

    def rows(base):
        return pl.ds(base, ATT_SPAN) if d == 1 else pl.ds(base, ATT_SPAN, stride=d)

    for ub in range(nblk):
        valid = first_ok if ub == 0 else band
        for r in range(d):
            base = ub * ATT_SPAN * d + r
            q = q_ref[rows(base), :]
            if ub == 0:
                kp, vp = kp_ref[rows(r), :], vp_ref[rows(r), :]
            else:
                kp, vp = kc_ref[rows(base - ATT_SPAN * d), :], vc_ref[rows(base - ATT_SPAN * d), :]
            k = jnp.concatenate([kp, kc_ref[rows(base), :]], axis=0).astype(BF16)
            v = jnp.concatenate([vp, vc_ref[rows(base), :]], axis=0).astype(BF16)
            outs, lses = [], []
            for e in range(2):
                sel = head0 if e == 0 else jnp.logical_not(head0)
                qe = jnp.where(sel, q, 0.0).astype(BF16)
                s = _dot_nt(qe, k) * ATT_SCALE
                s = jnp.where(valid, s - slopes[e] * dist, -jnp.inf)
                mx = jnp.max(s, axis=-1, keepdims=True)
                p = jnp.exp(s - mx)
                den = jnp.sum(p, axis=-1, keepdims=True)
                outs.append(jnp.dot(p.astype(BF16), v, preferred_element_type=F32) / den)
                lses.append(mx + jnp.log(den))
            o_ref[rows(base), :] = jnp.where(head0, outs[0], outs[1])
            lse_ref[rows(base), :] = jnp.where(head0, lses[0], lses[1])


def att_prompt(proj, slopes, bsz, t):
    tb = ATT_SPAN * max(ATT_DILS)
    nt = t // tb
    in_specs = [pl.BlockSpec(memory_space=pltpu.SMEM)]
    nblks = []
    for g in range(3):
        sb = ATT_SPAN * ATT_DILS[g]
        nblk = tb // sb
        nblks.append(nblk)
        cq, ck, cv = (C_Q + g * ATT_OUT) // LANES, (C_K + g * ATT_OUT) // LANES, (C_V + g * ATT_OUT) // LANES
        cur = lambda c0: pl.BlockSpec((tb, LANES), lambda b, j, h, c0=c0: (b * nt + j, c0 + h))
        prev = lambda c0, sb=sb, nblk=nblk: pl.BlockSpec(
            (sb, LANES), lambda b, j, h, c0=c0: (jnp.maximum((b * nt + j) * nblk - 1, 0), c0 + h))
        in_specs += [cur(cq), cur(ck), prev(ck), cur(cv), prev(cv)]
    return pl.pallas_call(
        functools.partial(_att_prompt_kernel, nblks=tuple(nblks)),
        grid=(bsz, nt, ATT_HEADS // 2),
        in_specs=in_specs,
        out_specs=pl.BlockSpec((tb, LANES), lambda b, j, h: (b * nt + j, h)),
        out_shape=jax.ShapeDtypeStruct((bsz * t, ATT_OUT), BF16),
        scratch_shapes=[pltpu.VMEM((tb, LANES), F32)] * 6,
        compiler_params=_cparams(("parallel", "parallel", "parallel")),
        name="att_prompt",
    )(slopes, *([proj] * 15))


def _att_sample_kernel(sl_ref, qkv_ref, k0, v0, k1, v1, k2, v2, o_ref):
    caches = ((k0, v0), (k1, v1), (k2, v2))
    outs, lses = [], []
    for g in range(3):
        d = ATT_DILS[g]
        assert d & (d - 1) == 0
        win = caches[g][0].shape[-1]
        q, kn, vn = qkv_ref[0, g], qkv_ref[1, g], qkv_ref[2, g]
        kc = caches[g][0][...]
        vc = caches[g][1][...]
        pos = lax.broadcasted_iota(jnp.int32, (1, 1, win), 2)
        on_grid = (pos & (d - 1)) == 0
        dist = (win - pos).astype(F32)
        s = jnp.sum(kc * q, axis=1, keepdims=True) * ATT_SCALE - sl_ref[g] * dist
        s = jnp.where(on_grid, s, -jnp.inf)
        s_new = jnp.sum(kn * q, axis=1, keepdims=True) * ATT_SCALE
        mx = jnp.maximum(jnp.max(s, axis=2, keepdims=True), s_new)
        p = jnp.exp(s - mx)
        p_new = jnp.exp(s_new - mx)
        den = jnp.sum(p, axis=2, keepdims=True) + p_new
        outs.append((jnp.sum(vc * p, axis=2, keepdims=True) + p_new * vn) / den)
        lses.append(mx + jnp.log(den))
    mx = jnp.maximum(jnp.maximum(lses[0], lses[1]), lses[2])
    es = [jnp.exp(l - mx) for l in lses]
    o_ref[...] = (es[0] * outs[0] + es[1] * outs[1] + es[2] * outs[2]) / (es[0] + es[1] + es[2])


def att_sample(qkv, caches_t, slopes, layer):
    bd = qkv.shape[0]
    in_specs = [pl.BlockSpec((3, ATT_HEADS, 1, 1), lambda b: (0, 0, 0, 0)),
                pl.BlockSpec((None, 3, 3, ATT_HEADS, HEAD_DIM, 1), lambda b: (b, 0, 0, 0, 0, 0))]
    for c in caches_t:
        in_specs.append(pl.BlockSpec((None, None, ATT_HEADS, HEAD_DIM, c.shape[-1]), lambda b: (layer, b, 0, 0, 0)))
    return pl.pallas_call(
        _att_sample_kernel,
        grid=(bd,),
        in_specs=in_specs,
        out_specs=pl.BlockSpec((None, ATT_HEADS, HEAD_DIM, 1), lambda b: (b, 0, 0, 0)),
        out_shape=jax.ShapeDtypeStruct((bd, ATT_HEADS, HEAD_DIM, 1), F32),
        compiler_params=_cparams(("parallel",)),
        name="att_sample",
    )(slopes.reshape(3, ATT_HEADS, 1, 1), qkv, *caches_t)


def _ssd_prompt_kernel(x_ref, b_ref, c_ref, z_ref, l_ref, cwx_ref, cwb_ref, cwc_ref, cbx_ref, cbb_ref, cbc_ref,
                       dtb_ref, aneg_ref, dskip_ref, ng_ref, o_ref, hout_ref, xpad, bpad, cpad, h_scr, y_scr, *, nc):
    c = pl.program_id(1)
    L = SSM_CHUNK

    @pl.when(c == 0)
    def _():
        xpad[0:SUBLANES, :] = jnp.zeros((SUBLANES, xpad.shape[1]), F32)
        bpad[0:SUBLANES, :] = jnp.zeros((SUBLANES, bpad.shape[1]), F32)
        cpad[0:SUBLANES, :] = jnp.zeros((SUBLANES, cpad.shape[1]), F32)
        h_scr[...] = jnp.zeros_like(h_scr)

    def conv_silu(pad, src_ref, w_ref, bias_ref):
        pad[SUBLANES:SUBLANES + L, :] = src_ref[...]
        acc = bias_ref[...] + w_ref[0:1, :] * pad[pl.ds(SUBLANES - 3, L), :]
        for jj in range(1, SSM_CONV):
            acc = acc + w_ref[jj:jj + 1, :] * pad[pl.ds(SUBLANES - 3 + jj, L), :]
        pad[0:SUBLANES, :] = pad[L:L + SUBLANES, :]
        return _silu(acc)

    xs = conv_silu(xpad, x_ref, cwx_ref, cbx_ref)
    bm = conv_silu(bpad, b_ref, cwb_ref, cbb_ref).astype(BF16)
    cm = conv_silu(cpad, c_ref, cwc_ref, cbc_ref).astype(BF16)

    lane = lax.broadcasted_iota(jnp.int32, (1, LANES), 1)
    head0 = lane < HEAD_DIM
    sub_head0 = lax.broadcasted_iota(jnp.int32, (LANES, 1), 0) < HEAD_DIM
    dt_lanes = (lane >= DT_LANE0) & (lane < DT_LANE0 + SSM_HEADS)
    dtv = jnp.where(dt_lanes, _softplus(l_ref[:, 0:LANES] + dtb_ref[...]), 0.0)
    da = dtv * aneg_ref[...]
    ri = lax.broadcasted_iota(jnp.int32, (L, L), 0)
    ci = lax.broadcasted_iota(jnp.int32, (L, L), 1)
    causal = ri >= ci
    a_cum = _dot_exact_lhs(causal.astype(BF16), da)
    a_cum_t = a_cum.T
    dt_t = dtv.T

    for g in range(SSM_GROUPS):
        bg = bm[:, g * SSM_STATE:(g + 1) * SSM_STATE]
        cg = cm[:, g * SSM_STATE:(g + 1) * SSM_STATE]
        cb = _dot_nt(cg, bg)
        for pp in range(3):
            pair = g * 3 + pp
            xp = xs[:, pair * LANES:(pair + 1) * LANES]
            xpb = xp.astype(BF16)
            yd, sc, ea, cd = [], [], [], []
            for e in range(2):
                hl = DT_LANE0 + 2 * pair + e
                ac_col = a_cum[:, hl:hl + 1]
                ac_row = a_cum_t[hl:hl + 1, :]
                a_last = ac_col[L - 1:L, :]
                dec = jnp.exp(jnp.where(causal, ac_col - ac_row, -jnp.inf))
                wm = (cb * dec * dt_t[hl:hl + 1, :]).astype(BF16)
                yd.append(jnp.dot(wm, xpb, preferred_element_type=F32))
                sc.append(jnp.exp(a_last - ac_col) * dtv[:, hl:hl + 1])
                ea.append(jnp.exp(ac_col))
                cd.append(jnp.exp(a_last))
            hp = h_scr[pair]
            y_off = _dot_nt(cg, hp.astype(BF16)) * jnp.where(head0, ea[0], ea[1])
            xw = (xp * jnp.where(head0, sc[0], sc[1])).astype(BF16)
            h_scr[pair] = jnp.where(sub_head0, cd[0], cd[1]) * hp + _dot_tn(xw, bg)
            y_scr[:, pair * LANES:(pair + 1) * LANES] = jnp.where(head0, yd[0], yd[1]) + y_off

    y = (y_scr[...] + dskip_ref[...] * xs) * _silu(z_ref[...])
    gw = BRANCH_W // SSM_GROUPS
    for g in range(SSM_GROUPS):
        yg = y[:, g * gw:(g + 1) * gw]
        yg = yg * lax.rsqrt(jnp.mean(yg * yg, axis=-1, keepdims=True) + NORM_EPS)
        o_ref[:, g * gw:(g + 1) * gw] = (yg * ng_ref[:, g * gw:(g + 1) * gw]).astype(o_ref.dtype)

    @pl.when(c == nc - 1)
    def _():
        hout_ref[...] = h_scr[...]


def ssd_prompt(proj, tail, lw, bsz, t):
    nc = t // SSM_CHUNK
    L = SSM_CHUNK
    rowblk = lambda width, c0: pl.BlockSpec((L, width), lambda b, c: (b * nc + c, c0 // width))
    par = lambda a: pl.BlockSpec(a.shape, lambda b, c: (0,) * a.ndim)
    params = [lw["cw_x"], lw["cw_b"], lw["cw_c"], lw["cb_x"], lw["cb_b"], lw["cb_c"], lw["dt_bias"], lw["a_neg"],
              lw["d_skip"], lw["ssm_norm_g"]]
    return pl.pallas_call(
        functools.partial(_ssd_prompt_kernel, nc=nc),
        grid=(bsz, nc),
        in_specs=[rowblk(BRANCH_W, C_X), rowblk(512, T_B), rowblk(512, T_C), rowblk(BRANCH_W, C_Z), rowblk(512, T_L)]
        + [par(a) for a in params],
        out_specs=[pl.BlockSpec((L, BRANCH_W), lambda b, c: (b * nc + c, 0)),
                   pl.BlockSpec((None, N_PAIRS, LANES, SSM_STATE), lambda b, c: (b, 0, 0, 0))],
        out_shape=[jax.ShapeDtypeStruct((bsz * t, BRANCH_W), BF16),
                   jax.ShapeDtypeStruct((bsz, N_PAIRS, LANES, SSM_STATE), F32)],
        scratch_shapes=[pltpu.VMEM((L + SUBLANES, BRANCH_W), F32), pltpu.VMEM((L + SUBLANES, 512), F32),
                        pltpu.VMEM((L + SUBLANES, 512), F32), pltpu.VMEM((N_PAIRS, LANES, SSM_STATE), F32),
                        pltpu.VMEM((L, BRANCH_W), F32)],
        compiler_params=_cparams(("parallel", "arbitrary")),
        name="ssd_prompt",
    )(proj, tail, tail, proj, tail, *params)


def _ssd_sample_conv_kernel(new_ref, st_ref, w_ref, b_ref, o_ref):
    acc = b_ref[...] + w_ref[SSM_CONV - 1:SSM_CONV, :] * new_ref[...]
    for jj in range(SSM_CONV - 1):
        acc = acc + w_ref[jj:jj + 1, :] * st_ref[jj]
    o_ref[...] = _silu(acc)


def ssd_sample_conv(xbc_new, conv_st_t, w, b):
    bd, cdim = xbc_new.shape
    return pl.pallas_call(
        _ssd_sample_conv_kernel,
        out_shape=jax.ShapeDtypeStruct((bd, cdim), F32),
        name="ssd_sample_conv",
    )(xbc_new, conv_st_t, w, b.reshape(1, cdim))


def _ssd_sample_step_kernel(h_ref, x_ref, z_ref, b_ref, c_ref, dt_ref, dtb_ref, alog_ref, d_ref, ng_ref, o_ref, hout_ref):
    h = h_ref[...]
    dt = _softplus(dt_ref[...] + dtb_ref[...])
    da = jnp.exp(dt * (-jnp.exp(alog_ref[...])))
    rep = SSM_HEADS // SSM_GROUPS
    bh = jnp.broadcast_to(b_ref[...][:, None], (SSM_GROUPS, rep, 1, SSM_STATE)).reshape(SSM_HEADS, 1, SSM_STATE)
    ch = jnp.broadcast_to(c_ref[...][:, None], (SSM_GROUPS, rep, 1, SSM_STATE)).reshape(SSM_HEADS, 1, SSM_STATE)
    x = x_ref[...]
    hn = da * h + (dt * x) * bh
    hout_ref[...] = hn
    y = jnp.sum(hn * ch, axis=-1, keepdims=True)
    y = (y + d_ref[...] * x) * _silu(z_ref[...])
    y4 = y.reshape(SSM_GROUPS, rep, HEAD_DIM, 1)
    ms = jnp.sum(jnp.sum(y4 * y4, axis=2, keepdims=True), axis=1, keepdims=True) / (rep * HEAD_DIM)
    y4 = y4 * lax.rsqrt(ms + NORM_EPS)
    o_ref[...] = y4.reshape(SSM_HEADS, HEAD_DIM, 1) * ng_ref[...]


def ssd_sample_step(h, x_col, z_col, bmat, cmat, dt_raw, lw, layer):
    bd = x_col.shape[0]
    col = pl.BlockSpec((None, SSM_HEADS, HEAD_DIM, 1), lambda b: (b, 0, 0, 0))
    grp = pl.BlockSpec((None, SSM_GROUPS, 1, SSM_STATE), lambda b: (b, 0, 0, 0))
    hd1 = pl.BlockSpec((SSM_HEADS, 1, 1), lambda b: (0, 0, 0))
    return pl.pallas_call(
        _ssd_sample_step_kernel,
        grid=(bd,),
        in_specs=[pl.BlockSpec((None, None, SSM_HEADS, HEAD_DIM, SSM_STATE), lambda b: (layer, b, 0, 0, 0)),
                  col, col, grp, grp, pl.BlockSpec((None, SSM_HEADS, 1, 1), lambda b: (b, 0, 0, 0)), hd1, hd1, hd1,
                  pl.BlockSpec((SSM_HEADS, HEAD_DIM, 1), lambda b: (0, 0, 0))],
        out_specs=[col, pl.BlockSpec((None, SSM_HEADS, HEAD_DIM, SSM_STATE), lambda b: (b, 0, 0, 0))],
        out_shape=[jax.ShapeDtypeStruct((bd, SSM_HEADS, HEAD_DIM, 1), F32),
                   jax.ShapeDtypeStruct((bd, SSM_HEADS, HEAD_DIM, SSM_STATE), F32)],
        compiler_params=_cparams(("parallel",)),
        name="ssd_sample_step",
    )(h, x_col, z_col, bmat, cmat, dt_raw, lw["dt_bias_h"], lw["a_log_h"], lw["d_h"], lw["ssm_norm_g_col"])


def _rwkv_mix(i, tm, x_refs, h_refs, i_refs, mu_refs, w0_ref, wup_ref, a0_ref, aup_ref, gup_ref):
    def shift(x_ref, h_ref, i_ref, mu_ref):
        x = x_ref[...]
        hrows = h_ref.shape[0]
        prev0 = jnp.where(i == 0, i_ref[...], h_ref[hrows - 1:hrows, :])
        if tm == 1:
            xp = prev0
        else:
            row = lax.broadcasted_iota(jnp.int32, (tm, 1), 0)
            xp = jnp.where(row == 0, prev0, pltpu.roll(x, 1, 0))
        return x + mu_ref[...] * (xp - x)

    ur, uk, uv, ul = (shift(*refs) for refs in zip(x_refs, h_refs, i_refs, mu_refs))
    f = lambda a, w_ref: jnp.dot(a.astype(BF16), w_ref[...], preferred_element_type=F32)
    w_log = -_softplus(-(w0_ref[...] + f(jnp.tanh(ul[:, 0:LANES]), wup_ref))) - 0.5
    a = _sigmoid(a0_ref[...] + f(ul[:, LANES:2 * LANES], aup_ref))
    g = f(_sigmoid(ul[:, 2 * LANES:4 * LANES]), gup_ref)
    return ur, uk, uv, -jnp.exp(w_log), a, g


def _rwkv_prep_kernel(*refs, tm):
    x_refs, h_refs, i_refs, mu_refs = refs[0:4], refs[4:8], refs[8:12], refs[12:16]
    w0_ref, wup_ref, a0_ref, aup_ref, gup_ref, kk_ref, ka_ref = refs[16:23]
    or_ref, old_ref, ok_ref, ov_ref, okk_ref, oa_ref, og_ref = refs[23:30]
    ur, uk, uv, ld, a, g = _rwkv_mix(pl.program_id(1), tm, x_refs, h_refs, i_refs, mu_refs,
                                     w0_ref, wup_ref, a0_ref, aup_ref, gup_ref)
    or_ref[...] = ur
    old_ref[...] = ld
    ok_ref[...] = uk * (1.0 + (a - 1.0) * ka_ref[...])
    ov_ref[...] = uv
    okk_ref[...] = uk * kk_ref[...]
    oa_ref[...] = a
    og_ref[...] = g


def _rwkv_prep_prompt_kernel(*refs, tm):
    x_refs, h_refs, i_refs, mu_refs = refs[0:4], refs[4:8], refs[8:12], refs[12:16]
    w0_ref, wup_ref, a0_ref, aup_ref, gup_ref, kk_ref, ka_ref, rk_ref = refs[16:24]
    opt_ref, ort_ref, oqh_ref, okh_ref, ov_ref, obv_ref, og_ref, oel_ref = refs[24:32]
    L = RWKV_CHUNK
    ur, uk, uv, ld, a, g = _rwkv_mix(pl.program_id(1), tm, x_refs, h_refs, i_refs, mu_refs,
                                     w0_ref, wup_ref, a0_ref, aup_ref, gup_ref)
    kmod = uk * (1.0 + (a - 1.0) * ka_ref[...])
    kkraw = uk * kk_ref[...]
    assert tm == 2 * L
    ri = lax.broadcasted_iota(jnp.int32, (tm, tm), 0)
    ci = lax.broadcasted_iota(jnp.int32, (tm, tm), 1)
    tril = ((ri >= ci) & ((ri < L) == (ci < L))).astype(BF16)
    cum = _dot_exact_lhs(tril, ld, 2)
    e_in = jnp.exp(cum)
    e_inv = jnp.exp(-cum)
    e_ex = jnp.exp(cum - ld)
    ort_ref[...] = (ur * e_in).astype(ort_ref.dtype)
    okh_ref[...] = (kmod * e_inv).astype(okh_ref.dtype)
    ov_ref[...] = uv.astype(ov_ref.dtype)
    og_ref[...] = g.astype(og_ref.dtype)
    for c in range(tm // L):
        oel_ref[c] = e_in[c * L + L - 1:c * L + L, :]
    li = lax.broadcasted_iota(jnp.int32, (LANES, LANES), 0)
    lj = lax.broadcasted_iota(jnp.int32, (LANES, LANES), 1)
    bd_ones = ((li < HEAD_DIM) == (lj < HEAD_DIM)).astype(BF16)
    rkr = ur * kmod * rk_ref[...]
    for p in range(N_PAIRS):
        cs = slice(p * LANES, (p + 1) * LANES)
        kkp = kkraw[:, cs]
        kkn = kkp / jnp.maximum(jnp.sqrt(_dot_exact_rhs(kkp * kkp, bd_ones, 1)), 1e-12)
        opt_ref[:, cs] = (-kkn * e_ex[:, cs]).astype(opt_ref.dtype)
        oqh_ref[:, cs] = (kkn * a[:, cs] * e_inv[:, cs]).astype(oqh_ref.dtype)
        obv_ref[:, cs] = (_dot_exact_rhs(rkr[:, cs], bd_ones, 1) * uv[:, cs]).astype(obv_ref.dtype)


def _rwkv_prep_call(kernel_fn, proj, tail, init, params, bsz, t, tm, out_specs, out_shape, name):
    nt = t // tm
    hrows = SUBLANES if t >= SUBLANES else t
    hper = tm // hrows
    proj = proj.reshape(bsz, t, N_MAIN)
    tail = tail.reshape(bsz, t, N_TAIL)
    blk = lambda width, c0: pl.BlockSpec((None, tm, width), lambda b, i: (b, i, c0 // width))
    halo = lambda width, c0: pl.BlockSpec(
        (None, hrows, width), lambda b, i: (b, jnp.maximum(i * hper - 1, 0), c0 // width))
    ini = lambda width: pl.BlockSpec((None, 1, width), lambda b, i: (b, 0, 0))
    par = lambda a: pl.BlockSpec(a.shape, lambda b, i: (0,) * a.ndim)
    secs = ((BRANCH_W, C_R), (BRANCH_W, C_RK), (BRANCH_W, C_RV), (512, T_L))
    srcs = (proj, proj, proj, tail)
    return pl.pallas_call(
        functools.partial(kernel_fn, tm=tm),
        grid=(bsz, nt),
        in_specs=[blk(*s) for s in secs] + [halo(*s) for s in secs] + [ini(s[0]) for s in secs]
        + [par(a) for a in params],
        out_specs=out_specs,
        out_shape=out_shape,
        compiler_params=_cparams(("parallel", "arbitrary")),
        name=name,
    )(*srcs, *srcs, *init, *params)


def _rwkv_prep_params(lw):
    return [lw["mu_r"], lw["mu_k"], lw["mu_v"], lw["mu_l"], lw["w0"], lw["w_up"], lw["a0"], lw["a_up"], lw["g_up"],
            lw["k_k"], lw["k_a"]]


def rwkv_prep(proj, tail, init, lw, bsz, t):
    tm = min(t, 128)
    out = pl.BlockSpec((None, tm, BRANCH_W), lambda b, i: (b, i, 0))
    shp = jax.ShapeDtypeStruct((bsz, t, BRANCH_W), F32)
    outs = _rwkv_prep_call(_rwkv_prep_kernel, proj, tail, init, _rwkv_prep_params(lw), bsz, t, tm, [out] * 7,
                           [shp] * 7, "rwkv_prep")
    return [o.reshape(bsz * t, BRANCH_W) for o in outs]


def rwkv_prep_prompt(proj, tail, init, lw, bsz, t):
    tm = 128
    nch = tm // RWKV_CHUNK
    out = pl.BlockSpec((None, tm, BRANCH_W), lambda b, i: (b, i, 0))
    shp = jax.ShapeDtypeStruct((bsz, t, BRANCH_W), BF16)
    el_spec = pl.BlockSpec((None, nch, 1, BRANCH_W), lambda b, i: (b, i, 0, 0))
    el_shape = jax.ShapeDtypeStruct((bsz, t // RWKV_CHUNK, 1, BRANCH_W), F32)
    outs = _rwkv_prep_call(_rwkv_prep_prompt_kernel, proj, tail, init, _rwkv_prep_params(lw) + [lw["r_k_row"]], bsz, t,
                           tm, [out] * 7 + [el_spec], [shp] * 7 + [el_shape], "rwkv_prep_prompt")
    return [o.reshape(bsz * t, BRANCH_W) for o in outs[:7]] + [outs[7].reshape(bsz * t // RWKV_CHUNK, 1, BRANCH_W)]


def _rwkv_chain_kernel(pt_ref, rt_ref, qh_ref, kh_ref, v_ref, bv_ref, g_ref, el_ref, lng_ref, lnb_ref, o_ref, sout_ref,
                       s_scr, *, nchunk, nt, npair):
    tstep = pl.program_id(2)
    L = RWKV_CHUNK
    L2 = 2 * L

    @pl.when(tstep == 0)
    def _():
        s_scr[...] = jnp.zeros_like(s_scr)

    lane = lax.broadcasted_iota(jnp.int32, (1, LANES), 1)
    head0 = lane < HEAD_DIM
    ri = lax.broadcasted_iota(jnp.int32, (L2, L2), 0)
    ci = lax.broadcasted_iota(jnp.int32, (L2, L2), 1)
    same = (ri < L) == (ci < L)
    strict = same & (ri > ci)
    incl = same & (ri >= ci)
    eye = (ri == ci).astype(F32)
    bd_ones = same.astype(BF16)
    bf = lambda x: x.astype(BF16)
    dot = lambda a, b: jnp.dot(bf(a), bf(b), preferred_element_type=F32)
    zero = jnp.zeros((), BF16)

    def stack(x):
        return jnp.concatenate([jnp.where(head0, x, zero), jnp.where(head0, zero, x)], axis=0)

    probs = [(c, p) for c in range(nchunk) for p in range(npair)]
    rows = lambda c: pl.ds(c * L, L)
    cols = lambda p: slice(p * LANES, (p + 1) * LANES)
    pm = {k: stack(pt_ref[rows(k[0]), cols(k[1])]) for k in probs}
    rm = {k: stack(rt_ref[rows(k[0]), cols(k[1])]) for k in probs}
    qk = {k: jnp.concatenate([stack(qh_ref[rows(k[0]), cols(k[1])]), stack(kh_ref[rows(k[0]), cols(k[1])])], axis=0)
          for k in probs}
    vm = {k: stack(v_ref[rows(k[0]), cols(k[1])]) for k in probs}
    x = {k: _dot_nt(jnp.concatenate([pm[k], rm[k]], axis=0), qk[k]) for k in probs}
    sums = lambda v: jnp.dot(bf(v), bd_ones, preferred_element_type=F32)
    n_pow = {k: jnp.where(strict, x[k][0:L2, 0:L2], 0.0) for k in probs}
    a_kp = {k: bf(jnp.where(strict, x[k][0:L2, L2:2 * L2], 0.0)) for k in probs}
    r_q = {k: bf(jnp.where(incl, x[k][L2:2 * L2, 0:L2], 0.0)) for k in probs}
    r_k = {k: bf(jnp.where(incl, x[k][L2:2 * L2, L2:2 * L2], 0.0)) for k in probs}
    t_inv = {k: eye + n_pow[k] for k in probs}
    for _ in range(5):
        n_pow = {k: dot(n_pow[k], n_pow[k]) for k in probs}
        t_inv = {k: t_inv[k] + dot(n_pow[k], t_inv[k]) for k in probs}
    t_inv = {k: bf(t_inv[k]) for k in probs}
    pr = {k: jnp.concatenate([pm[k], rm[k]], axis=0) for k in probs}
    akv = {k: dot(jnp.concatenate([a_kp[k], r_k[k]], axis=0), vm[k]) for k in probs}

    state = [s_scr[p] for p in range(npair)]
    ys = {}
    for c in range(nchunk):
        ks = [(c, p) for p in range(npair)]
        ps = {k: dot(pr[k], bf(state[k[1]])) + akv[k] for k in ks}
        rhs = {k: ps[k][0:L2, :] for k in ks}
        rs = {k: ps[k][L2:2 * L2, :] for k in ks}
        u = {k: bf(dot(t_inv[k], rhs[k])) for k in ks}
        upd = {k: _dot_tn(qk[k], jnp.concatenate([u[k], vm[k]], axis=0)) for k in ks}
        for k in ks:
            ys[k] = rs[k] + dot(r_q[k], u[k])
            w_col = jnp.sum(eye * el_ref[c, :, cols(k[1])], axis=1, keepdims=True)
            state[k[1]] = w_col * (state[k[1]] + upd[k])
    for p in range(npair):
        s_scr[p] = state[p]

    y = {k: ys[k][0:L, :] + ys[k][L:L2, :] for k in probs}
    yc = {k: y[k] - sums(y[k]) * (1.0 / HEAD_DIM) for k in probs}
    var = {k: sums(yc[k] * yc[k]) * (1.0 / HEAD_DIM) for k in probs}
    for k in probs:
        cs = cols(k[1])
        yn = yc[k] * lax.rsqrt(var[k] + RWKV_LN_EPS) * lng_ref[:, cs] + lnb_ref[:, cs]
        o_ref[rows(k[0]), cs] = ((yn + bv_ref[rows(k[0]), cs].astype(F32)) * g_ref[rows(k[0]), cs].astype(F32)
                                 ).astype(o_ref.dtype)

    @pl.when(tstep == nt - 1)
    def _():
        sout_ref[...] = s_scr[...]


def rwkv_prompt(parts, lw, bsz, t, npair=6, nchunk=2):
    tb = nchunk * RWKV_CHUNK
    nt = t // tb
    wl = npair * LANES
    blk = pl.BlockSpec((tb, wl), lambda b, p, i: (b * nt + i, p))
    par = pl.BlockSpec((1, wl), lambda b, p, i: (0, p))
    el = pl.BlockSpec((nchunk, 1, wl), lambda b, p, i: (b * nt + i, 0, p))
    return pl.pallas_call(
        functools.partial(_rwkv_chain_kernel, nchunk=nchunk, nt=nt, npair=npair),
        grid=(bsz, N_PAIRS // npair, nt),
        in_specs=[blk] * 7 + [el, par, par],
        out_specs=[blk, pl.BlockSpec((None, npair, LANES, LANES), lambda b, p, i: (b, p, 0, 0))],
        out_shape=[jax.ShapeDtypeStruct((bsz * t, BRANCH_W), BF16),
                   jax.ShapeDtypeStruct((bsz, N_PAIRS, LANES, LANES), F32)],
        scratch_shapes=[pltpu.VMEM((npair, LANES, LANES), F32)],
        compiler_params=_cparams(("parallel", "parallel", "arbitrary")),
        name="rwkv_prompt",
    )(*parts, lw["ln_g_row"], lw["ln_b_row"])


def _rwkv_sample_step_kernel(s_ref, r_ref, ld_ref, k_ref, kk_ref, a_ref, v_ref, g_ref, rk_ref, lng_ref, lnb_ref,
                             o_ref, sout_ref):
    s = s_ref[...]
    r, kmod, kkraw, a = r_ref[...], k_ref[...], kk_ref[...], a_ref[...]
    v = v_ref[...]
    kkn = kkraw / jnp.maximum(jnp.sqrt(jnp.sum(kkraw * kkraw, axis=-1, keepdims=True)), 1e-12)
    sa = jnp.sum(s * (-kkn), axis=-1, keepdims=True)
    sn = s * jnp.exp(ld_ref[...]) + sa * (kkn * a) + v * kmod
    sout_ref[...] = sn
    y = jnp.sum(sn * r, axis=-1, keepdims=True)
    mean = jnp.mean(y, axis=1, keepdims=True)
    yc = y - mean
    var = jnp.mean(yc * yc, axis=1, keepdims=True)
    yn = yc * lax.rsqrt(var + RWKV_LN_EPS) * lng_ref[...] + lnb_ref[...]
    bonus = jnp.sum(r * kmod * rk_ref[...], axis=-1, keepdims=True) * v
    o_ref[...] = (yn + bonus) * g_ref[...]


def rwkv_sample_step(state, rows, cols, lw, layer):
    bd = rows[0].shape[0]
    row = pl.BlockSpec((None, RWKV_HEADS, 1, HEAD_DIM), lambda b: (b, 0, 0, 0))
    col = pl.BlockSpec((None, RWKV_HEADS, HEAD_DIM, 1), lambda b: (b, 0, 0, 0))
    prow = pl.BlockSpec((RWKV_HEADS, 1, HEAD_DIM), lambda b: (0, 0, 0))
    pcol = pl.BlockSpec((RWKV_HEADS, HEAD_DIM, 1), lambda b: (0, 0, 0))
    return pl.pallas_call(
        _rwkv_sample_step_kernel,
        grid=(bd,),
        in_specs=[pl.BlockSpec((None, None, RWKV_HEADS, HEAD_DIM, HEAD_DIM), lambda b: (layer, b, 0, 0, 0))]
        + [row] * 5 + [col] * 2 + [prow, pcol, pcol],
        out_specs=[col, pl.BlockSpec((None, RWKV_HEADS, HEAD_DIM, HEAD_DIM), lambda b: (b, 0, 0, 0))],
        out_shape=[jax.ShapeDtypeStruct((bd, RWKV_HEADS, HEAD_DIM, 1), F32),
                   jax.ShapeDtypeStruct((bd, RWKV_HEADS, HEAD_DIM, HEAD_DIM), F32)],
        compiler_params=_cparams(("parallel",)),
        name="rwkv_sample_step",
    )(state, *rows, *cols, lw["r_k_h"], lw["ln_g_col"], lw["ln_b_col"])


def _cache_update_kernel(k_ref, v_ref, newk_ref, newv_ref, ok_ref, ov_ref):
    win = k_ref.shape[-1]
    is_last = lax.broadcasted_iota(jnp.int32, (1, win), 1) == win - 1
    for c_ref, new_ref, o_ref in ((k_ref, newk_ref, ok_ref), (v_ref, newv_ref, ov_ref)):
        for h in range(ATT_HEADS):
            o_ref[h] = jnp.where(is_last, new_ref[h], pltpu.roll(c_ref[h], win - 1, 1))


def cache_update(cache_k, cache_v, new_k, new_v):
    nl, bd = cache_k.shape[:2]
    win = cache_k.shape[-1]
    blk = pl.BlockSpec((None, None, ATT_HEADS, HEAD_DIM, win), lambda l, b: (l, b, 0, 0, 0))
    new = pl.BlockSpec((None, None, ATT_HEADS, HEAD_DIM, 1), lambda l, b: (l, b, 0, 0, 0))
    shp = jax.ShapeDtypeStruct(cache_k.shape, cache_k.dtype)
    return pl.pallas_call(
        _cache_update_kernel,
        grid=(nl, bd),
        in_specs=[blk, blk, new, new],
        out_specs=[blk, blk],
        out_shape=[shp, shp],
        compiler_params=_cparams(("parallel", "parallel")),
        name="cache_update",
    )(cache_k, cache_v, new_k, new_v)


def _pack_layer(l, p):
    row = lambda a: a.reshape(1, -1)
    pad_rows = lambda a, n: jnp.concatenate([a, jnp.zeros((n - a.shape[0], a.shape[1]), a.dtype)], axis=0).astype(BF16)
    lane_pad = lambda a: jnp.zeros((1, LANES), F32).at[0, DT_LANE0:DT_LANE0 + SSM_HEADS].set(a)
    mu = p["rwkv_mu"][l]
    z1 = lambda n: jnp.zeros((n,), F32)
    cw = p["ssm_conv_w"][l]
    cb = p["ssm_conv_b"][l]
    rep64 = lambda a: jnp.repeat(a, HEAD_DIM)
    return dict(
        ln1_g=p["ln1_g"][l], ln2_g=p["ln2_g"][l],
        cw_x=cw[:, :BRANCH_W], cw_b=cw[:, BRANCH_W:BRANCH_W + 512], cw_c=cw[:, BRANCH_W + 512:],
        cb_x=row(cb[:BRANCH_W]), cb_b=row(cb[BRANCH_W:BRANCH_W + 512]), cb_c=row(cb[BRANCH_W + 512:]),
        conv_w=cw, conv_b=cb,
        dt_bias=lane_pad(p["ssm_dt_bias"][l]), a_neg=lane_pad(-jnp.exp(p["ssm_a_log"][l])),
        d_skip=row(rep64(p["ssm_d"][l])), ssm_norm_g=row(p["ssm_norm_g"][l]),
        dt_bias_h=p["ssm_dt_bias"][l].reshape(SSM_HEADS, 1, 1), a_log_h=p["ssm_a_log"][l].reshape(SSM_HEADS, 1, 1),
        d_h=p["ssm_d"][l].reshape(SSM_HEADS, 1, 1), ssm_norm_g_col=p["ssm_norm_g"][l].reshape(SSM_HEADS, HEAD_DIM, 1),
        mu_r=row(mu[:BRANCH_W]), mu_k=row(mu[BRANCH_W:2 * BRANCH_W]), mu_v=row(mu[2 * BRANCH_W:3 * BRANCH_W]),
        mu_l=row(jnp.concatenate([mu[3 * BRANCH_W:3 * BRANCH_W + LORA_W], z1(32),
                                  mu[3 * BRANCH_W + LORA_W:3 * BRANCH_W + LORA_W + LORA_A], z1(32),
                                  mu[3 * BRANCH_W + LORA_W + LORA_A:]])),
        w0=row(p["rwkv_w0"][l]), w_up=pad_rows(p["rwkv_w_up"][l], LANES), a0=row(p["rwkv_a0"][l]),
        a_up=pad_rows(p["rwkv_a_up"][l], LANES), g_up=p["rwkv_g_up"][l].astype(BF16),
        k_k=row(p["rwkv_k_k"][l]), k_a=row(p["rwkv_k_a"][l]),
        r_k_row=row(p["rwkv_r_k"][l]), ln_g_row=row(p["rwkv_ln_g"][l]), ln_b_row=row(p["rwkv_ln_b"][l]),
        r_k_h=p["rwkv_r_k"][l].reshape(RWKV_HEADS, 1, HEAD_DIM),
        ln_g_col=p["rwkv_ln_g"][l].reshape(RWKV_HEADS, HEAD_DIM, 1),
        ln_b_col=p["rwkv_ln_b"][l].reshape(RWKV_HEADS, HEAD_DIM, 1),
    )


def _pack_big(p):
    w_t = jnp.swapaxes(p["w_in"], 1, 2)
    zrows = lambda n: jnp.zeros((w_t.shape[0], n, D_MODEL), F32)
    w_tail_t = jnp.concatenate([
        w_t[:, _O_XBC + BRANCH_W:_O_DT], w_t[:, _O_LORA:_O_LORA + LORA_W], w_t[:, _O_DT:_O_RW], zrows(8),
        w_t[:, _O_LORA + LORA_W:_O_LORA + LORA_W + LORA_A], zrows(32), w_t[:, _O_LORA + LORA_W + LORA_A:_O_GATE]], axis=1)
    return dict(w_in_t=w_t, w_tail_t=w_tail_t, w_att=p["w_branch_att"].astype(BF16),
                w_ssm=p["w_branch_ssm"].astype(BF16), w_rwkv=p["w_branch_rwkv"].astype(BF16),
                w_out=p["w_out"].astype(BF16), w_up_ff=p["w_up"].astype(BF16), w_down_ff=p["w_down"].astype(BF16))


def _unpack_lora(rows):
    return jnp.concatenate([rows[..., 0:LORA_W], rows[..., LANES:LANES + LORA_A], rows[..., 2 * LANES:]], axis=-1)


def _pack_lora(rows):
    z = lambda n: jnp.zeros(rows.shape[:-1] + (n,), rows.dtype)
    return jnp.concatenate([rows[..., 0:LORA_W], z(32), rows[..., LORA_W:LORA_W + LORA_A], z(32),
                            rows[..., LORA_W + LORA_A:]], axis=-1)


def _alibi_slopes():
    n = 3 * ATT_HEADS
    idx = jnp.arange(1, n + 1, dtype=F32)
    return jnp.exp2(-8.0 * idx / n).reshape(3, ATT_HEADS)


def _layer(xp, xs, lw, big, layer, slopes, st, bsz, t, bd):
    hp = rmsnorm(xp, lw["ln1_g"], BF16)
    hs = rmsnorm(xs, lw["ln1_g"], BF16)
    proj, tail, proj_s, tail_s = project_main_and_tail(hp, hs, big, layer)
    outs, state_p = _prompt_mixers(proj, tail, lw, slopes, bsz, t)
    outs_s, new_rows, state_s = _sample_mixers(proj_s, tail_s, lw, slopes, st, layer, bd)
    merged, merged_s = merge_branches(outs, proj, outs_s, proj_s, big["w_att"], big["w_ssm"], big["w_rwkv"], layer)
    xp, xs = matmul(merged, merged_s, big["w_out"], layer, epi="res", res=xp, res_s=xs)
    h2, h2_s = rmsnorm(xp, lw["ln2_g"], BF16), rmsnorm(xs, lw["ln2_g"], BF16)
    up, up_s = matmul(h2, h2_s, big["w_up_ff"], layer, epi="relu2", out_dtype=BF16)
    xp, xs = matmul(up, up_s, big["w_down_ff"], layer, epi="res", res=xp, res_s=xs)
    return xp, xs, state_p, new_rows, state_s


def _prompt_mixers(proj, tail, lw, slopes, bsz, t):
    o_att = att_prompt(proj, slopes, bsz, t)
    o_ssm, h_fin = ssd_prompt(proj, tail, lw, bsz, t)
    zero_init = [jnp.zeros((bsz, 1, w), F32) for w in (BRANCH_W, BRANCH_W, BRANCH_W, 512)]
    rw = rwkv_prep_prompt(proj, tail, zero_init, lw, bsz, t)
    o_rwkv, s_fin = rwkv_prompt(rw, lw, bsz, t)

    p3 = proj.reshape(bsz, t, N_MAIN)
    t3 = tail.reshape(bsz, t, N_TAIL)
    kv = []
    for g in range(3):
        win = ATT_WINDOWS[g]
        for c0 in (C_K, C_V):
            kv.append(p3[:, t - win:, c0 + g * ATT_OUT:c0 + (g + 1) * ATT_OUT].reshape(bsz, win, ATT_HEADS, HEAD_DIM))
    nc = SSM_CONV - 1
    conv_new = jnp.concatenate([p3[:, t - nc:, C_X:C_X + BRANCH_W], t3[:, t - nc:, T_B:T_B + 1024]], -1)
    shift_new = jnp.concatenate([p3[:, t - 1, C_R:C_R + 3 * BRANCH_W], _unpack_lora(t3[:, t - 1, T_L:T_L + 512])], axis=-1)
    ssm_new = h_fin.reshape(bsz, SSM_HEADS, HEAD_DIM, SSM_STATE)
    s6 = s_fin.reshape(bsz, N_PAIRS, 2, HEAD_DIM, 2, HEAD_DIM)
    s_heads = jnp.stack([s6[:, :, 0, :, 0, :], s6[:, :, 1, :, 1, :]], axis=2)
    rwkv_new = jnp.swapaxes(s_heads, -1, -2).reshape(bsz, RWKV_HEADS, HEAD_DIM, HEAD_DIM)
    return (o_att, o_ssm, o_rwkv), tuple(kv) + (conv_new, ssm_new, shift_new, rwkv_new)


def _sample_mixers(proj, tail, lw, slopes, st, layer, bd):
    (caches_t, conv_st, ssm_st, shift_st, rwkv_st) = st
    qkv = proj[:, :3 * BRANCH_W].reshape(bd, 3, 3, ATT_HEADS, HEAD_DIM, 1)
    o_att = att_sample(qkv, caches_t, slopes, layer).reshape(bd, ATT_OUT).astype(BF16)
    new_rows = tuple(qkv[:, part, g] for g in range(3) for part in (1, 2))

    xbc_new = jnp.concatenate([proj[:, C_X:C_X + BRANCH_W], tail[:, T_B:T_B + 1024]], axis=-1)
    cst = conv_st[layer]
    xc = ssd_sample_conv(xbc_new, jnp.swapaxes(cst, 0, 1), lw["conv_w"], lw["conv_b"])
    conv_new = jnp.concatenate([cst[:, 1:], xbc_new[:, None]], axis=1)
    x_col = xc[:, :BRANCH_W].reshape(bd, SSM_HEADS, HEAD_DIM, 1)
    z_col = proj[:, C_Z:C_Z + BRANCH_W].reshape(bd, SSM_HEADS, HEAD_DIM, 1)
    bmat = xc[:, BRANCH_W:BRANCH_W + 512].reshape(bd, SSM_GROUPS, 1, SSM_STATE)
    cmat = xc[:, BRANCH_W + 512:].reshape(bd, SSM_GROUPS, 1, SSM_STATE)
    dt_raw = tail[:, T_L + DT_LANE0:T_L + DT_LANE0 + SSM_HEADS].reshape(bd, SSM_HEADS, 1, 1)
    y_col, ssm_new = ssd_sample_step(ssm_st, x_col, z_col, bmat, cmat, dt_raw, lw, layer)
    o_ssm = y_col.reshape(bd, BRANCH_W).astype(BF16)

    sh = shift_st[layer]
    init = [sh[:, None, i * BRANCH_W:(i + 1) * BRANCH_W] for i in range(3)] + [_pack_lora(sh[:, None, 3 * BRANCH_W:])]
    r, ld, kmod, v, kkraw, a, gg = rwkv_prep(proj, tail, init, lw, bd, 1)
    rowf = lambda z: z.reshape(bd, RWKV_HEADS, 1, HEAD_DIM)
    colf = lambda z: z.reshape(bd, RWKV_HEADS, HEAD_DIM, 1)
    o_col, rwkv_new = rwkv_sample_step(rwkv_st, [rowf(r), rowf(ld), rowf(kmod), rowf(kkraw), rowf(a)],
                                       [colf(v), colf(gg)], lw, layer)
    o_rwkv = o_col.reshape(bd, BRANCH_W).astype(BF16)
    shift_new = jnp.concatenate([proj[:, C_R:C_R + 3 * BRANCH_W], _unpack_lora(tail[:, T_L:T_L + 512])], axis=-1)
    return (o_att, o_ssm, o_rwkv), new_rows, (conv_new, ssm_new, shift_new, rwkv_new)


def kernel(x_prompt, x_sample, cache_att_k0, cache_att_v0, cache_att_k1, cache_att_v1, cache_att_k2, cache_att_v2, state_ssm_conv, state_ssm, state_rwkv_shift, state_rwkv, ln1_g, w_in, ssm_conv_w, ssm_conv_b, ssm_dt_bias, ssm_a_log, ssm_d, ssm_norm_g, rwkv_mu, rwkv_w0, rwkv_w_up, rwkv_a0, rwkv_a_up, rwkv_g_up, rwkv_k_k, rwkv_k_a, rwkv_r_k, rwkv_ln_g, rwkv_ln_b, w_branch_att, w_branch_ssm, w_branch_rwkv, w_out, ln2_g, w_up, w_down, final_g):
    p = dict(ln1_g=ln1_g, w_in=w_in, ssm_conv_w=ssm_conv_w, ssm_conv_b=ssm_conv_b, ssm_dt_bias=ssm_dt_bias,
             ssm_a_log=ssm_a_log, ssm_d=ssm_d, ssm_norm_g=ssm_norm_g, rwkv_mu=rwkv_mu, rwkv_w0=rwkv_w0,
             rwkv_w_up=rwkv_w_up, rwkv_a0=rwkv_a0, rwkv_a_up=rwkv_a_up, rwkv_g_up=rwkv_g_up, rwkv_k_k=rwkv_k_k,
             rwkv_k_a=rwkv_k_a, rwkv_r_k=rwkv_r_k, rwkv_ln_g=rwkv_ln_g, rwkv_ln_b=rwkv_ln_b,
             w_branch_att=w_branch_att, w_branch_ssm=w_branch_ssm, w_branch_rwkv=w_branch_rwkv, w_out=w_out,
             ln2_g=ln2_g, w_up=w_up, w_down=w_down)
    depth = w_in.shape[0]
    bsz, t, _ = x_prompt.shape
    bd = x_sample.shape[0]
    slopes = _alibi_slopes()
    caches = (cache_att_k0, cache_att_v0, cache_att_k1, cache_att_v1, cache_att_k2, cache_att_v2)
    caches_t = tuple(jnp.transpose(c, (0, 1, 3, 4, 2)) for c in caches)
    st = (caches_t, state_ssm_conv, state_ssm, state_rwkv_shift, state_rwkv)
    xp = x_prompt.reshape(bsz * t, D_MODEL)
    xs = x_sample.reshape(bd, D_MODEL)
    big = _pack_big(p)
    p_new, s_new, s_rows = [], [], []
    for l in range(depth):
        lw = _pack_layer(l, p)
        xp, xs, sp, rows, ss = _layer(xp, xs, lw, big, l, slopes, st, bsz, t, bd)
        p_new.append(sp)
        s_new.append(ss)
        s_rows.append(rows)
    y_prompt = rmsnorm(xp, final_g, F32).reshape(bsz, t, D_MODEL)
    y_sample = rmsnorm(xs, final_g, F32).reshape(bd, 1, D_MODEL)
    stack = lambda per_layer, n: tuple(jnp.stack([s[i] for s in per_layer]) for i in range(n))
    new_rows = stack(s_rows, 6)
    s_caches = ()
    for g in range(3):
        upd = cache_update(caches_t[2 * g], caches_t[2 * g + 1], new_rows[2 * g], new_rows[2 * g + 1])
        s_caches += tuple(jnp.transpose(c, (0, 1, 4, 2, 3)) for c in upd)
    return (y_prompt, y_sample) + stack(p_new, 10) + s_caches + stack(s_new, 4)
```

```python
import functools

import jax
import jax.numpy as jnp
from jax import lax
from jax.experimental import pallas as pl
from jax.experimental.pallas import tpu as pltpu

F32 = jnp.float32
BF16 = jnp.bfloat16

D_MODEL = 2048
HEAD_DIM = 64
BRANCH_W = 1536
NORM_EPS = 1e-5
ATT_WINDOWS = (128, 512, 2048)
ATT_DILS = (1, 4, 16)
ATT_SPAN = 128
ATT_HEADS = 8
ATT_OUT = ATT_HEADS * HEAD_DIM
ATT_SCALE = HEAD_DIM ** -0.5
SSM_HEADS = 24
SSM_GROUPS = 4
SSM_STATE = 128
SSM_CONV = 4
SSM_CHUNK = 128
RWKV_HEADS = 24
LORA_W = 96
LORA_A = 96
LORA_G = 256
RWKV_LN_EPS = 64e-5
D_FF = 4 * D_MODEL
RWKV_CHUNK = 64
N_PAIRS = BRANCH_W // 128

LANES = 128
SUBLANES = 8
VMEM_LIMIT_BYTES = 56 * 1024 * 1024

C_Q, C_K, C_V, C_Z, C_X, C_R, C_RK, C_RV = (i * BRANCH_W for i in range(8))
C_G = 8 * BRANCH_W
N_MAIN = C_G + 3 * D_MODEL
T_B = 0
T_C = 512
T_L = 1024
N_TAIL = T_L + 512
DT_LANE0 = LORA_W

_O_QKV, _O_Z, _O_XBC, _O_DT, _O_RW, _O_GATE = 0, 4608, 6144, 8704, 8728, 13784
_O_LORA = _O_RW + 3 * BRANCH_W
_W_IN_COLS = _O_GATE + 3 * D_MODEL
PROJ_TN = 768


def _cparams(sem):
    return pltpu.CompilerParams(dimension_semantics=sem, vmem_limit_bytes=VMEM_LIMIT_BYTES)


def _softplus(x):
    return jnp.maximum(x, 0.0) + jnp.log(1.0 + jnp.exp(-jnp.abs(x)))


def _sigmoid(x):
    return 1.0 / (1.0 + jnp.exp(-x))


def _silu(x):
    return x * _sigmoid(x)


def _split(x, terms):
    out = []
    for _ in range(terms - 1):
        hi = x.astype(BF16)
        out.append(hi)
        x = x - hi.astype(F32)
    out.append(x.astype(BF16))
    return out


def _dot_exact_rhs(x, mat, terms=3):
    parts = [jnp.dot(t, mat, preferred_element_type=F32) for t in _split(x, terms)]
    return functools.reduce(lambda a, b: a + b, parts)


def _dot_exact_lhs(mat, x, terms=3):
    parts = [jnp.dot(mat, t, preferred_element_type=F32) for t in _split(x, terms)]
    return functools.reduce(lambda a, b: a + b, parts)


def _dot_nt(a, b):
    return lax.dot_general(a, b, (((1,), (1,)), ((), ())), preferred_element_type=F32)


def _dot_tn(a, b):
    return lax.dot_general(a, b, (((0,), (0,)), ((), ())), preferred_element_type=F32)


def _rmsnorm_kernel(x_ref, g_ref, o_ref):
    x = x_ref[...]
    y = x * lax.rsqrt(jnp.mean(x * x, axis=-1, keepdims=True) + NORM_EPS)
    o_ref[...] = (y * g_ref[...]).astype(o_ref.dtype)


def rmsnorm(x, g, out_dtype):
    m, d = x.shape
    tm = min(m, 512)
    return pl.pallas_call(
        _rmsnorm_kernel,
        grid=(m // tm,),
        in_specs=[pl.BlockSpec((tm, d), lambda i: (i, 0)), pl.BlockSpec((1, d), lambda i: (0, 0))],
        out_specs=pl.BlockSpec((tm, d), lambda i: (i, 0)),
        out_shape=jax.ShapeDtypeStruct((m, d), out_dtype),
        compiler_params=_cparams(("parallel",)),
        name="rmsnorm",
    )(x, g.reshape(1, d))


def _mm_kernel(*refs, nk, epi):
    refs = list(refs)
    x_ref, w_ref = refs[0:2]
    r_ref = refs.pop(2) if epi == "res" else None
    xs_ref = refs[2]
    rs_ref = refs.pop(3) if epi == "res" else None
    o_ref, os_ref = refs[3:5]
    acc_ref, accs_ref = (refs[5], refs[6]) if nk > 1 else (None, None)
    first_rows = pl.program_id(0) == 0
    k = pl.program_id(2)

    def finish(a, r, o):
        if epi == "res":
            a = r[...] + a
        elif epi == "relu2":
            a = jnp.square(jnp.maximum(a, 0.0))
        o[...] = a.astype(o.dtype)

    def accumulate(x, acc, r, o):
        part = jnp.dot(x[...], w_ref[...], preferred_element_type=F32)
        if nk == 1:
            finish(part, r, o)
            return

        @pl.when(k == 0)
        def _():
            acc[...] = part

        @pl.when(k > 0)
        def _():
            acc[...] += part

        @pl.when(k == nk - 1)
        def _():
            finish(acc[...], r, o)

    accumulate(x_ref, acc_ref, r_ref, o_ref)

    @pl.when(first_rows)
    def _():
        accumulate(xs_ref, accs_ref, rs_ref, os_ref)


def _pick(n, cands):
    for c in cands:
        if n % c == 0:
            return c
    return n


def matmul(x, xs, w, layer, *, epi="none", res=None, res_s=None, out_dtype=F32):
    m, kdim = x.shape
    ms = xs.shape[0]
    n = w.shape[2]
    tm = min(m, 1024 if epi == "res" else 2048)
    tn = _pick(n, (1536, 1024, 512, 256, 128))
    tk = min(kdim, 2048)
    nk = kdim // tk
    nj = n // tn
    side_col = lambda i, j: jnp.where(i == 0, j, nj - 1)
    in_specs = [pl.BlockSpec((tm, tk), lambda i, j, k: (i, k)),
                pl.BlockSpec((None, tk, tn), lambda i, j, k: (layer, k, j))]
    args = [x, w]
    if epi == "res":
        in_specs.append(pl.BlockSpec((tm, tn), lambda i, j, k: (i, j)))
        args.append(res)
    in_specs.append(pl.BlockSpec((ms, tk), lambda i, j, k: (0, k)))
    args.append(xs)
    if epi == "res":
        in_specs.append(pl.BlockSpec((ms, tn), lambda i, j, k: (0, side_col(i, j))))
        args.append(res_s)
    scratch = [pltpu.VMEM((tm, tn), F32), pltpu.VMEM((ms, tn), F32)] if nk > 1 else []
    return pl.pallas_call(
        functools.partial(_mm_kernel, nk=nk, epi=epi),
        grid=(m // tm, nj, nk),
        in_specs=in_specs,
        out_specs=[pl.BlockSpec((tm, tn), lambda i, j, k: (i, j)),
                   pl.BlockSpec((ms, tn), lambda i, j, k: (0, side_col(i, j)))],
        out_shape=[jax.ShapeDtypeStruct((m, n), out_dtype), jax.ShapeDtypeStruct((ms, n), out_dtype)],
        scratch_shapes=scratch,
        compiler_params=_cparams(("arbitrary", "arbitrary", "arbitrary")),
        name="matmul_" + epi,
    )(*args)


def _mm_res_norm_kernel(x_ref, w_ref, r_ref, xs_ref, rs_ref, g_ref, o_ref, on_ref, os_ref, osn_ref, *scr, nk):
    first_rows = pl.program_id(0) == 0
    k = pl.program_id(1)

    def finish(a, r, o, on):
        a = r[...] + a
        o[...] = a
        y = a * lax.rsqrt(jnp.mean(a * a, axis=-1, keepdims=True) + NORM_EPS)
        on[...] = (y * g_ref[...]).astype(on.dtype)

    def accumulate(x, acc, r, o, on):
        part = jnp.dot(x[...], w_ref[...], preferred_element_type=F32)
        if nk == 1:
            finish(part, r, o, on)
            return

        @pl.when(k == 0)
        def _():
            acc[...] = part

        @pl.when(k > 0)
        def _():
            acc[...] += part

        @pl.when(k == nk - 1)
        def _():
            finish(acc[...], r, o, on)

    acc_ref, accs_ref = scr if nk > 1 else (None, None)
    accumulate(x_ref, acc_ref, r_ref, o_ref, on_ref)

    @pl.when(first_rows)
    def _():
        accumulate(xs_ref, accs_ref, rs_ref, os_ref, osn_ref)


def matmul_res_norm(x, xs, w, layer, res, res_s, g, norm_dtype):
    m, kdim = x.shape
    ms = xs.shape[0]
    n = w.shape[2]
    tm = min(m, 512)
    tk = kdim if kdim <= 2048 else 1024
    nk = kdim // tk
    row = lambda width: pl.BlockSpec((tm, width), lambda i, k: (i, 0))
    side = lambda width: pl.BlockSpec((ms, width), lambda i, k: (0, 0))
    scratch = [pltpu.VMEM((tm, n), F32), pltpu.VMEM((ms, n), F32)] if nk > 1 else []
    return pl.pallas_call(
        functools.partial(_mm_res_norm_kernel, nk=nk),
        grid=(m // tm, nk),
        in_specs=[pl.BlockSpec((tm, tk), lambda i, k: (i, k)), pl.BlockSpec((None, tk, n), lambda i, k: (layer, k, 0)),
                  row(n), pl.BlockSpec((ms, tk), lambda i, k: (0, k)), side(n), pl.BlockSpec((1, n), lambda i, k: (0, 0))],
        out_specs=[row(n), row(n), side(n), side(n)],
        out_shape=[jax.ShapeDtypeStruct((m, n), F32), jax.ShapeDtypeStruct((m, n), norm_dtype),
                   jax.ShapeDtypeStruct((ms, n), F32), jax.ShapeDtypeStruct((ms, n), norm_dtype)],
        scratch_shapes=scratch,
        compiler_params=_cparams(("arbitrary", "arbitrary")),
        name="matmul_res_norm",
    )(x, w, res, xs, res_s, g.reshape(1, n))


def _proj_kernel(x_ref, xs_ref, w_ref, o_ref, os_ref, wb_scr):
    @pl.when(pl.program_id(1) == 0)
    def _():
        wb_scr[...] = w_ref[0].astype(BF16)
        os_ref[...] = _dot_nt(xs_ref[...], wb_scr[...])

    o_ref[...] = _dot_nt(x_ref[...], wb_scr[...])


def _main_row_start(j):
    t_rw, t_gate = C_R // PROJ_TN, C_G // PROJ_TN
    start = j * PROJ_TN + jnp.where(j >= t_rw, _O_RW - C_R, 0) + jnp.where(j >= t_gate, (_O_GATE - C_G) - (_O_RW - C_R), 0)
    return pl.multiple_of(start, SUBLANES)


def project(h, hs, w_t, layer, n_out, row_start):
    m, kdim = h.shape
    ms = hs.shape[0]
    tm = min(m, 2048)
    ell = pl.Element
    return pl.pallas_call(
        _proj_kernel,
        grid=(n_out // PROJ_TN, m // tm),
        in_specs=[pl.BlockSpec((tm, kdim), lambda j, i: (i, 0)),
                  pl.BlockSpec((ms, kdim), lambda j, i: (0, 0)),
                  pl.BlockSpec((ell(1), ell(PROJ_TN), ell(kdim)), lambda j, i: (layer, row_start(j), 0))],
        out_specs=[pl.BlockSpec((tm, PROJ_TN), lambda j, i: (i, j)), pl.BlockSpec((ms, PROJ_TN), lambda j, i: (0, j))],
        out_shape=[jax.ShapeDtypeStruct((m, n_out), F32), jax.ShapeDtypeStruct((ms, n_out), F32)],
        scratch_shapes=[pltpu.VMEM((PROJ_TN, kdim), BF16)],
        compiler_params=_cparams(("parallel", "arbitrary")),
        name="project",
    )(h, hs, w_t)


def project_main_and_tail(h, hs, big, layer):
    main, main_s = project(h, hs, big["w_in_t"], layer, N_MAIN, _main_row_start)
    tail, tail_s = project(h, hs, big["w_tail_t"], layer, N_TAIL, lambda j: pl.multiple_of(j * PROJ_TN, SUBLANES))
    return main, tail, main_s, tail_s


def _merge_kernel(oa_ref, os_ref, or_ref, ga_ref, gs_ref, gr_ref, oa2_ref, os2_ref, or2_ref, ga2_ref, gs2_ref,
                  gr2_ref, wa_ref, ws_ref, wr_ref, o_ref, o2_ref):
    f = lambda a, b: jnp.dot(a[...], b[...], preferred_element_type=F32)

    def merged(a_ref, s_ref, r_ref, g_a, g_s, g_r, out):
        acc = _sigmoid(g_a[...]) * f(a_ref, wa_ref)
        acc = acc + _sigmoid(g_s[...]) * f(s_ref, ws_ref)
        acc = acc + _sigmoid(g_r[...]) * f(r_ref, wr_ref)
        out[...] = acc.astype(out.dtype)

    merged(oa_ref, os_ref, or_ref, ga_ref, gs_ref, gr_ref, o_ref)

    @pl.when(pl.program_id(1) == 0)
    def _():
        merged(oa2_ref, os2_ref, or2_ref, ga2_ref, gs2_ref, gr2_ref, o2_ref)


def merge_branches(outs, proj, outs_s, proj_s, wa, ws, wr, layer):
    m = outs[0].shape[0]
    ms = outs_s[0].shape[0]
    tm = min(m, 512)
    tn = 1024
    gb = C_G // tn
    nj = D_MODEL // tn
    widths = (ATT_OUT, BRANCH_W, BRANCH_W)
    row = lambda width: pl.BlockSpec((tm, width), lambda j, i: (i, 0))
    gate = lambda g: pl.BlockSpec((tm, tn), lambda j, i, g=g: (i, gb + g * nj + j))
    row_s = lambda width: pl.BlockSpec((ms, width), lambda j, i: (0, 0))
    gate_s = lambda g: pl.BlockSpec((ms, tn), lambda j, i, g=g: (0, gb + g * nj + j))
    col = lambda k: pl.BlockSpec((None, k, tn), lambda j, i: (layer, 0, j))
    return pl.pallas_call(
        _merge_kernel,
        grid=(nj, m // tm),
        in_specs=[row(w) for w in widths] + [gate(g) for g in range(3)] + [row_s(w) for w in widths]
        + [gate_s(g) for g in range(3)] + [col(w) for w in widths],
        out_specs=[pl.BlockSpec((tm, tn), lambda j, i: (i, j)), pl.BlockSpec((ms, tn), lambda j, i: (0, j))],
        out_shape=[jax.ShapeDtypeStruct((m, D_MODEL), BF16), jax.ShapeDtypeStruct((ms, D_MODEL), BF16)],
        compiler_params=_cparams(("parallel", "arbitrary")),
        name="merge_branches",
    )(*outs, proj, proj, proj, *outs_s, proj_s, proj_s, proj_s, wa, ws, wr)


def _att_prompt_kernel(sl_ref, *refs, nblks):
    o_ref = refs[15]
    scr = refs[16:]
    for g in range(3):
        _att_group(sl_ref, *refs[5 * g:5 * g + 5], scr[2 * g], scr[2 * g + 1], d=ATT_DILS[g], nblk=nblks[g], g=g)
    a, b, c = scr[1][...], scr[3][...], scr[5][...]
    mx = jnp.maximum(jnp.maximum(a, b), c)
    ea, eb, ec = jnp.exp(a - mx), jnp.exp(b - mx), jnp.exp(c - mx)
    num = ea * scr[0][...] + eb * scr[2][...] + ec * scr[4][...]
    o_ref[...] = (num / (ea + eb + ec)).astype(o_ref.dtype)


def _att_group(sl_ref, q_ref, kc_ref, kp_ref, vc_ref, vp_ref, o_ref, lse_ref, *, d, nblk, g):
    j = pl.program_id(1)
    hp = pl.program_id(2)
    lane = lax.broadcasted_iota(jnp.int32, (1, LANES), 1)
    head0 = lane < HEAD_DIM
    qi = lax.broadcasted_iota(jnp.int32, (ATT_SPAN, 2 * ATT_SPAN), 0)
    kj = lax.broadcasted_iota(jnp.int32, (ATT_SPAN, 2 * ATT_SPAN), 1)
    delta = qi - kj + ATT_SPAN
    band = (delta >= 0) & (delta <= ATT_SPAN)
    first_ok = band & ((kj >= ATT_SPAN) | (j > 0))
    dist = (delta * d).astype(F32)
    slopes = (sl_ref[g, 2 * hp], sl_ref[g, 2 * hp + 1])

    def rows(base):
        return pl.ds(base, ATT_SPAN) if d == 1 else pl.ds(base, ATT_SPAN, stride=d)

    for ub in range(nblk):
        valid = first_ok if ub == 0 else band
        for r in range(d):
            base = ub * ATT_SPAN * d + r
            q = q_ref[rows(base), :]
            if ub == 0:
                kp, vp = kp_ref[rows(r), :], vp_ref[rows(r), :]
            else:
                kp, vp = kc_ref[rows(base - ATT_SPAN * d), :], vc_ref[rows(base - ATT_SPAN * d), :]
            k = jnp.concatenate([kp, kc_ref[rows(base), :]], axis=0).astype(BF16)
            v = jnp.concatenate([vp, vc_ref[rows(base), :]], axis=0).astype(BF16)
            outs, lses = [], []
            for e in range(2):
                sel = head0 if e == 0 else jnp.logical_not(head0)
                qe = jnp.where(sel, q, 0.0).astype(BF16)
                s = _dot_nt(qe, k) * ATT_SCALE
                s = jnp.where(valid, s - slopes[e] * dist, -jnp.inf)
                mx = jnp.max(s, axis=-1, keepdims=True)
                p = jnp.exp(s - mx)
                den = jnp.sum(p, axis=-1, keepdims=True)
                outs.append(jnp.dot(p.astype(BF16), v, preferred_element_type=F32) / den)
                lses.append(mx + jnp.log(den))
            o_ref[rows(base), :] = jnp.where(head0, outs[0], outs[1])
            lse_ref[rows(base), :] = jnp.where(head0, lses[0], lses[1])


def att_prompt(proj, slopes, bsz, t):
    tb = ATT_SPAN * max(ATT_DILS)
    nt = t // tb
    in_specs = [pl.BlockSpec(memory_space=pltpu.SMEM)]
    nblks = []
    for g in range(3):
        sb = ATT_SPAN * ATT_DILS[g]
        nblk = tb // sb
        nblks.append(nblk)
        cq, ck, cv = (C_Q + g * ATT_OUT) // LANES, (C_K + g * ATT_OUT) // LANES, (C_V + g * ATT_OUT) // LANES
        cur = lambda c0: pl.BlockSpec((tb, LANES), lambda b, j, h, c0=c0: (b * nt + j, c0 + h))
        prev = lambda c0, sb=sb, nblk=nblk: pl.BlockSpec(
            (sb, LANES), lambda b, j, h, c0=c0: (jnp.maximum((b * nt + j) * nblk - 1, 0), c0 + h))
        in_specs += [cur(cq), cur(ck), prev(ck), cur(cv), prev(cv)]
    return pl.pallas_call(
        functools.partial(_att_prompt_kernel, nblks=tuple(nblks)),
        grid=(bsz, nt, ATT_HEADS // 2),
        in_specs=in_specs,
        out_specs=pl.BlockSpec((tb, LANES), lambda b, j, h: (b * nt + j, h)),
        out_shape=jax.ShapeDtypeStruct((bsz * t, ATT_OUT), BF16),
        scratch_shapes=[pltpu.VMEM((tb, LANES), F32)] * 6,
        compiler_params=_cparams(("parallel", "parallel", "parallel")),
        name="att_prompt",
    )(slopes, *([proj] * 15))


def _att_sample_kernel(sl_ref, qkv_ref, k0, v0, k1, v1, k2, v2, o_ref):
    caches = ((k0, v0), (k1, v1), (k2, v2))
    outs, lses = [], []
    for g in range(3):
        d = ATT_DILS[g]
        assert d & (d - 1) == 0
        win = caches[g][0].shape[-1]
        q, kn, vn = qkv_ref[0, g], qkv_ref[1, g], qkv_ref[2, g]
        kc = caches[g][0][...]
        vc = caches[g][1][...]
        pos = lax.broadcasted_iota(jnp.int32, (1, 1, win), 2)
        on_grid = (pos & (d - 1)) == 0
        dist = (win - pos).astype(F32)
        s = jnp.sum(kc * q, axis=1, keepdims=True) * ATT_SCALE - sl_ref[g] * dist
        s = jnp.where(on_grid, s, -jnp.inf)
        s_new = jnp.sum(kn * q, axis=1, keepdims=True) * ATT_SCALE
        mx = jnp.maximum(jnp.max(s, axis=2, keepdims=True), s_new)
        p = jnp.exp(s - mx)
        p_new = jnp.exp(s_new - mx)
        den = jnp.sum(p, axis=2, keepdims=True) + p_new
        outs.append((jnp.sum(vc * p, axis=2, keepdims=True) + p_new * vn) / den)
        lses.append(mx + jnp.log(den))
    mx = jnp.maximum(jnp.maximum(lses[0], lses[1]), lses[2])
    es = [jnp.exp(l - mx) for l in lses]
    o_ref[...] = (es[0] * outs[0] + es[1] * outs[1] + es[2] * outs[2]) / (es[0] + es[1] + es[2])


def att_sample(qkv, caches_t, slopes, layer):
    bd = qkv.shape[0]
    in_specs = [pl.BlockSpec((3, ATT_HEADS, 1, 1), lambda b: (0, 0, 0, 0)),
                pl.BlockSpec((None, 3, 3, ATT_HEADS, HEAD_DIM, 1), lambda b: (b, 0, 0, 0, 0, 0))]
    for c in caches_t:
        in_specs.append(pl.BlockSpec((None, None, ATT_HEADS, HEAD_DIM, c.shape[-1]), lambda b: (layer, b, 0, 0, 0)))
    return pl.pallas_call(
        _att_sample_kernel,
        grid=(bd,),
        in_specs=in_specs,
        out_specs=pl.BlockSpec((None, ATT_HEADS, HEAD_DIM, 1), lambda b: (b, 0, 0, 0)),
        out_shape=jax.ShapeDtypeStruct((bd, ATT_HEADS, HEAD_DIM, 1), F32),
        compiler_params=_cparams(("parallel",)),
        name="att_sample",
    )(slopes.reshape(3, ATT_HEADS, 1, 1), qkv, *caches_t)


def _ssd_prompt_kernel(x_ref, b_ref, c_ref, z_ref, l_ref, cwx_ref, cwb_ref, cwc_ref, cbx_ref, cbb_ref, cbc_ref,
                       dtb_ref, aneg_ref, dskip_ref, ng_ref, o_ref, hout_ref, xpad, bpad, cpad, h_scr, y_scr, *, nc):
    c = pl.program_id(1)
    L = SSM_CHUNK

    @pl.when(c == 0)
    def _():
        xpad[0:SUBLANES, :] = jnp.zeros((SUBLANES, xpad.shape[1]), F32)
        bpad[0:SUBLANES, :] = jnp.zeros((SUBLANES, bpad.shape[1]), F32)
        cpad[0:SUBLANES, :] = jnp.zeros((SUBLANES, cpad.shape[1]), F32)
        h_scr[...] = jnp.zeros_like(h_scr)

    def conv_silu(pad, src_ref, w_ref, bias_ref):
        pad[SUBLANES:SUBLANES + L, :] = src_ref[...]
        acc = bias_ref[...] + w_ref[0:1, :] * pad[pl.ds(SUBLANES - 3, L), :]
        for jj in range(1, SSM_CONV):
            acc = acc + w_ref[jj:jj + 1, :] * pad[pl.ds(SUBLANES - 3 + jj, L), :]
        pad[0:SUBLANES, :] = pad[L:L + SUBLANES, :]
        return _silu(acc)

    xs = conv_silu(xpad, x_ref, cwx_ref, cbx_ref)
    bm = conv_silu(bpad, b_ref, cwb_ref, cbb_ref).astype(BF16)
    cm = conv_silu(cpad, c_ref, cwc_ref, cbc_ref).astype(BF16)

    lane = lax.broadcasted_iota(jnp.int32, (1, LANES), 1)
    head0 = lane < HEAD_DIM
    sub_head0 = lax.broadcasted_iota(jnp.int32, (LANES, 1), 0) < HEAD_DIM
    dt_lanes = (lane >= DT_LANE0) & (lane < DT_LANE0 + SSM_HEADS)
    dtv = jnp.where(dt_lanes, _softplus(l_ref[:, 0:LANES] + dtb_ref[...]), 0.0)
    da = dtv * aneg_ref[...]
    ri = lax.broadcasted_iota(jnp.int32, (L, L), 0)
    ci = lax.broadcasted_iota(jnp.int32, (L, L), 1)
    causal = ri >= ci
    a_cum = _dot_exact_lhs(causal.astype(BF16), da)
    a_cum_t = a_cum.T
    dt_t = dtv.T

    for g in range(SSM_GROUPS):
        bg = bm[:, g * SSM_STATE:(g + 1) * SSM_STATE]
        cg = cm[:, g * SSM_STATE:(g + 1) * SSM_STATE]
        cb = _dot_nt(cg, bg)
        for pp in range(3):
            pair = g * 3 + pp
            xp = xs[:, pair * LANES:(pair + 1) * LANES]
            xpb = xp.astype(BF16)
            yd, sc, ea, cd = [], [], [], []
            for e in range(2):
                hl = DT_LANE0 + 2 * pair + e
                ac_col = a_cum[:, hl:hl + 1]
                ac_row = a_cum_t[hl:hl + 1, :]
                a_last = ac_col[L - 1:L, :]
                dec = jnp.exp(jnp.where(causal, ac_col - ac_row, -jnp.inf))
                wm = (cb * dec * dt_t[hl:hl + 1, :]).astype(BF16)
                yd.append(jnp.dot(wm, xpb, preferred_element_type=F32))
                sc.append(jnp.exp(a_last - ac_col) * dtv[:, hl:hl + 1])
                ea.append(jnp.exp(ac_col))
                cd.append(jnp.exp(a_last))
            hp = h_scr[pair]
            y_off = _dot_nt(cg, hp.astype(BF16)) * jnp.where(head0, ea[0], ea[1])
            xw = (xp * jnp.where(head0, sc[0], sc[1])).astype(BF16)
            h_scr[pair] = jnp.where(sub_head0, cd[0], cd[1]) * hp + _dot_tn(xw, bg)
            y_scr[:, pair * LANES:(pair + 1) * LANES] = jnp.where(head0, yd[0], yd[1]) + y_off

    y = (y_scr[...] + dskip_ref[...] * xs) * _silu(z_ref[...])
    gw = BRANCH_W // SSM_GROUPS
    for g in range(SSM_GROUPS):
        yg = y[:, g * gw:(g + 1) * gw]
        yg = yg * lax.rsqrt(jnp.mean(yg * yg, axis=-1, keepdims=True) + NORM_EPS)
        o_ref[:, g * gw:(g + 1) * gw] = (yg * ng_ref[:, g * gw:(g + 1) * gw]).astype(o_ref.dtype)

    @pl.when(c == nc - 1)
    def _():
        hout_ref[...] = h_scr[...]


def ssd_prompt(proj, tail, lw, bsz, t):
    nc = t // SSM_CHUNK
    L = SSM_CHUNK
    rowblk = lambda width, c0: pl.BlockSpec((L, width), lambda b, c: (b * nc + c, c0 // width))
    par = lambda a: pl.BlockSpec(a.shape, lambda b, c: (0,) * a.ndim)
    params = [lw["cw_x"], lw["cw_b"], lw["cw_c"], lw["cb_x"], lw["cb_b"], lw["cb_c"], lw["dt_bias"], lw["a_neg"],
              lw["d_skip"], lw["ssm_norm_g"]]
    return pl.pallas_call(
        functools.partial(_ssd_prompt_kernel, nc=nc),
        grid=(bsz, nc),
        in_specs=[rowblk(BRANCH_W, C_X), rowblk(512, T_B), rowblk(512, T_C), rowblk(BRANCH_W, C_Z), rowblk(512, T_L)]
        + [par(a) for a in params],
        out_specs=[pl.BlockSpec((L, BRANCH_W), lambda b, c: (b * nc + c, 0)),
                   pl.BlockSpec((None, N_PAIRS, LANES, SSM_STATE), lambda b, c: (b, 0, 0, 0))],
        out_shape=[jax.ShapeDtypeStruct((bsz * t, BRANCH_W), BF16),
                   jax.ShapeDtypeStruct((bsz, N_PAIRS, LANES, SSM_STATE), F32)],
        scratch_shapes=[pltpu.VMEM((L + SUBLANES, BRANCH_W), F32), pltpu.VMEM((L + SUBLANES, 512), F32),
                        pltpu.VMEM((L + SUBLANES, 512), F32), pltpu.VMEM((N_PAIRS, LANES, SSM_STATE), F32),
                        pltpu.VMEM((L, BRANCH_W), F32)],
        compiler_params=_cparams(("parallel", "arbitrary")),
        name="ssd_prompt",
    )(proj, tail, tail, proj, tail, *params)


def _ssd_sample_conv_kernel(new_ref, st_ref, w_ref, b_ref, o_ref):
    acc = b_ref[...] + w_ref[SSM_CONV - 1:SSM_CONV, :] * new_ref[...]
    for jj in range(SSM_CONV - 1):
        acc = acc + w_ref[jj:jj + 1, :] * st_ref[jj]
    o_ref[...] = _silu(acc)


def ssd_sample_conv(xbc_new, conv_st_t, w, b):
    bd, cdim = xbc_new.shape
    return pl.pallas_call(
        _ssd_sample_conv_kernel,
        out_shape=jax.ShapeDtypeStruct((bd, cdim), F32),
        name="ssd_sample_conv",
    )(xbc_new, conv_st_t, w, b.reshape(1, cdim))


def _ssd_sample_step_kernel(h_ref, x_ref, z_ref, b_ref, c_ref, dt_ref, dtb_ref, alog_ref, d_ref, ng_ref, o_ref, hout_ref):
    h = h_ref[...]
    dt = _softplus(dt_ref[...] + dtb_ref[...])
    da = jnp.exp(dt * (-jnp.exp(alog_ref[...])))
    rep = SSM_HEADS // SSM_GROUPS
    bh = jnp.broadcast_to(b_ref[...][:, None], (SSM_GROUPS, rep, 1, SSM_STATE)).reshape(SSM_HEADS, 1, SSM_STATE)
    ch = jnp.broadcast_to(c_ref[...][:, None], (SSM_GROUPS, rep, 1, SSM_STATE)).reshape(SSM_HEADS, 1, SSM_STATE)
    x = x_ref[...]
    hn = da * h + (dt * x) * bh
    hout_ref[...] = hn
    y = jnp.sum(hn * ch, axis=-1, keepdims=True)
    y = (y + d_ref[...] * x) * _silu(z_ref[...])
    y4 = y.reshape(SSM_GROUPS, rep, HEAD_DIM, 1)
    ms = jnp.sum(jnp.sum(y4 * y4, axis=2, keepdims=True), axis=1, keepdims=True) / (rep * HEAD_DIM)
    y4 = y4 * lax.rsqrt(ms + NORM_EPS)
    o_ref[...] = y4.reshape(SSM_HEADS, HEAD_DIM, 1) * ng_ref[...]


def ssd_sample_step(h, x_col, z_col, bmat, cmat, dt_raw, lw, layer):
    bd = x_col.shape[0]
    col = pl.BlockSpec((None, SSM_HEADS, HEAD_DIM, 1), lambda b: (b, 0, 0, 0))
    grp = pl.BlockSpec((None, SSM_GROUPS, 1, SSM_STATE), lambda b: (b, 0, 0, 0))
    hd1 = pl.BlockSpec((SSM_HEADS, 1, 1), lambda b: (0, 0, 0))
    return pl.pallas_call(
        _ssd_sample_step_kernel,
        grid=(bd,),
        in_specs=[pl.BlockSpec((None, None, SSM_HEADS, HEAD_DIM, SSM_STATE), lambda b: (layer, b, 0, 0, 0)),
                  col, col, grp, grp, pl.BlockSpec((None, SSM_HEADS, 1, 1), lambda b: (b, 0, 0, 0)), hd1, hd1, hd1,
                  pl.BlockSpec((SSM_HEADS, HEAD_DIM, 1), lambda b: (0, 0, 0))],
        out_specs=[col, pl.BlockSpec((None, SSM_HEADS, HEAD_DIM, SSM_STATE), lambda b: (b, 0, 0, 0))],
        out_shape=[jax.ShapeDtypeStruct((bd, SSM_HEADS, HEAD_DIM, 1), F32),
                   jax.ShapeDtypeStruct((bd, SSM_HEADS, HEAD_DIM, SSM_STATE), F32)],
        compiler_params=_cparams(("parallel",)),
        name="ssd_sample_step",
    )(h, x_col, z_col, bmat, cmat, dt_raw, lw["dt_bias_h"], lw["a_log_h"], lw["d_h"], lw["ssm_norm_g_col"])


def _rwkv_mix(i, tm, x_refs, h_refs, i_refs, mu_refs, w0_ref, wup_ref, a0_ref, aup_ref, gup_ref):
    def shift(x_ref, h_ref, i_ref, mu_ref):
        x = x_ref[...]
        hrows = h_ref.shape[0]
        prev0 = jnp.where(i == 0, i_ref[...], h_ref[hrows - 1:hrows, :])
        if tm == 1:
            xp = prev0
        else:
            row = lax.broadcasted_iota(jnp.int32, (tm, 1), 0)
            xp = jnp.where(row == 0, prev0, pltpu.roll(x, 1, 0))
        return x + mu_ref[...] * (xp - x)

    ur, uk, uv, ul = (shift(*refs) for refs in zip(x_refs, h_refs, i_refs, mu_refs))
    f = lambda a, w_ref: jnp.dot(a.astype(BF16), w_ref[...], preferred_element_type=F32)
    w_log = -_softplus(-(w0_ref[...] + f(jnp.tanh(ul[:, 0:LANES]), wup_ref))) - 0.5
    a = _sigmoid(a0_ref[...] + f(ul[:, LANES:2 * LANES], aup_ref))
    g = f(_sigmoid(ul[:, 2 * LANES:4 * LANES]), gup_ref)
    return ur, uk, uv, -jnp.exp(w_log), a, g


def _rwkv_prep_kernel(*refs, tm):
    x_refs, h_refs, i_refs, mu_refs = refs[0:4], refs[4:8], refs[8:12], refs[12:16]
    w0_ref, wup_ref, a0_ref, aup_ref, gup_ref, kk_ref, ka_ref = refs[16:23]
    or_ref, old_ref, ok_ref, ov_ref, okk_ref, oa_ref, og_ref = refs[23:30]
    ur, uk, uv, ld, a, g = _rwkv_mix(pl.program_id(1), tm, x_refs, h_refs, i_refs, mu_refs,
                                     w0_ref, wup_ref, a0_ref, aup_ref, gup_ref)
    or_ref[...] = ur
    old_ref[...] = ld
    ok_ref[...] = uk * (1.0 + (a - 1.0) * ka_ref[...])
    ov_ref[...] = uv
    okk_ref[...] = uk * kk_ref[...]
    oa_ref[...] = a
    og_ref[...] = g


def _rwkv_prep_prompt_kernel(*refs, tm):
    x_refs, h_refs, i_refs, mu_refs = refs[0:4], refs[4:8], refs[8:12], refs[12:16]
    w0_ref, wup_ref, a0_ref, aup_ref, gup_ref, kk_ref, ka_ref, rk_ref = refs[16:24]
    opt_ref, ort_ref, oqh_ref, okh_ref, ov_ref, obv_ref, og_ref, oel_ref = refs[24:32]
    L = RWKV_CHUNK
    ur, uk, uv, ld, a, g = _rwkv_mix(pl.program_id(1), tm, x_refs, h_refs, i_refs, mu_refs,
                                     w0_ref, wup_ref, a0_ref, aup_ref, gup_ref)
    kmod = uk * (1.0 + (a - 1.0) * ka_ref[...])
    kkraw = uk * kk_ref[...]
    assert tm == 2 * L
    ri = lax.broadcasted_iota(jnp.int32, (tm, tm), 0)
    ci = lax.broadcasted_iota(jnp.int32, (tm, tm), 1)
    tril = ((ri >= ci) & ((ri < L) == (ci < L))).astype(BF16)
    cum = _dot_exact_lhs(tril, ld, 2)
    e_in = jnp.exp(cum)
    e_inv = jnp.exp(-cum)
    e_ex = jnp.exp(cum - ld)
    ort_ref[...] = (ur * e_in).astype(ort_ref.dtype)
    okh_ref[...] = (kmod * e_inv).astype(okh_ref.dtype)
    ov_ref[...] = uv.astype(ov_ref.dtype)
    og_ref[...] = g.astype(og_ref.dtype)
    for c in range(tm // L):
        oel_ref[c] = e_in[c * L + L - 1:c * L + L, :]
    li = lax.broadcasted_iota(jnp.int32, (LANES, LANES), 0)
    lj = lax.broadcasted_iota(jnp.int32, (LANES, LANES), 1)
    bd_ones = ((li < HEAD_DIM) == (lj < HEAD_DIM)).astype(BF16)
    rkr = ur * kmod * rk_ref[...]
    for p in range(N_PAIRS):
        cs = slice(p * LANES, (p + 1) * LANES)
        kkp = kkraw[:, cs]
        kkn = kkp / jnp.maximum(jnp.sqrt(_dot_exact_rhs(kkp * kkp, bd_ones, 1)), 1e-12)
        opt_ref[:, cs] = (-kkn * e_ex[:, cs]).astype(opt_ref.dtype)
        oqh_ref[:, cs] = (kkn * a[:, cs] * e_inv[:, cs]).astype(oqh_ref.dtype)
        obv_ref[:, cs] = (_dot_exact_rhs(rkr[:, cs], bd_ones, 1) * uv[:, cs]).astype(obv_ref.dtype)


def _rwkv_prep_call(kernel_fn, proj, tail, init, params, bsz, t, tm, out_specs, out_shape, name):
    nt = t // tm
    hrows = SUBLANES if t >= SUBLANES else t
    hper = tm // hrows
    proj = proj.reshape(bsz, t, N_MAIN)
    tail = tail.reshape(bsz, t, N_TAIL)
    blk = lambda width, c0: pl.BlockSpec((None, tm, width), lambda b, i: (b, i, c0 // width))
    halo = lambda width, c0: pl.BlockSpec(
        (None, hrows, width), lambda b, i: (b, jnp.maximum(i * hper - 1, 0), c0 // width))
    ini = lambda width: pl.BlockSpec((None, 1, width), lambda b, i: (b, 0, 0))
    par = lambda a: pl.BlockSpec(a.shape, lambda b, i: (0,) * a.ndim)
    secs = ((BRANCH_W, C_R), (BRANCH_W, C_RK), (BRANCH_W, C_RV), (512, T_L))
    srcs = (proj, proj, proj, tail)
    return pl.pallas_call(
        functools.partial(kernel_fn, tm=tm),
        grid=(bsz, nt),
        in_specs=[blk(*s) for s in secs] + [halo(*s) for s in secs] + [ini(s[0]) for s in secs]
        + [par(a) for a in params],
        out_specs=out_specs,
        out_shape=out_shape,
        compiler_params=_cparams(("parallel", "arbitrary")),
        name=name,
    )(*srcs, *srcs, *init, *params)


def _rwkv_prep_params(lw):
    return [lw["mu_r"], lw["mu_k"], lw["mu_v"], lw["mu_l"], lw["w0"], lw["w_up"], lw["a0"], lw["a_up"], lw["g_up"],
            lw["k_k"], lw["k_a"]]


def rwkv_prep(proj, tail, init, lw, bsz, t):
    tm = min(t, 128)
    out = pl.BlockSpec((None, tm, BRANCH_W), lambda b, i: (b, i, 0))
    shp = jax.ShapeDtypeStruct((bsz, t, BRANCH_W), F32)
    outs = _rwkv_prep_call(_rwkv_prep_kernel, proj, tail, init, _rwkv_prep_params(lw), bsz, t, tm, [out] * 7,
                           [shp] * 7, "rwkv_prep")
    return [o.reshape(bsz * t, BRANCH_W) for o in outs]


def rwkv_prep_prompt(proj, tail, init, lw, bsz, t):
    tm = 128
    nch = tm // RWKV_CHUNK
    out = pl.BlockSpec((None, tm, BRANCH_W), lambda b, i: (b, i, 0))
    shp = jax.ShapeDtypeStruct((bsz, t, BRANCH_W), BF16)
    el_spec = pl.BlockSpec((None, nch, 1, BRANCH_W), lambda b, i: (b, i, 0, 0))
    el_shape = jax.ShapeDtypeStruct((bsz, t // RWKV_CHUNK, 1, BRANCH_W), F32)
    outs = _rwkv_prep_call(_rwkv_prep_prompt_kernel, proj, tail, init, _rwkv_prep_params(lw) + [lw["r_k_row"]], bsz, t,
                           tm, [out] * 7 + [el_spec], [shp] * 7 + [el_shape], "rwkv_prep_prompt")
    return [o.reshape(bsz * t, BRANCH_W) for o in outs[:7]] + [outs[7].reshape(bsz * t // RWKV_CHUNK, 1, BRANCH_W)]


def _rwkv_chain_kernel(pt_ref, rt_ref, qh_ref, kh_ref, v_ref, bv_ref, g_ref, el_ref, lng_ref, lnb_ref, o_ref, sout_ref,
                       s_scr, *, nchunk, nt, npair):
    tstep = pl.program_id(2)
    L = RWKV_CHUNK
    L2 = 2 * L

    @pl.when(tstep == 0)
    def _():
        s_scr[...] = jnp.zeros_like(s_scr)

    lane = lax.broadcasted_iota(jnp.int32, (1, LANES), 1)
    head0 = lane < HEAD_DIM
    ri = lax.broadcasted_iota(jnp.int32, (L2, L2), 0)
    ci = lax.broadcasted_iota(jnp.int32, (L2, L2), 1)
    same = (ri < L) == (ci < L)
    strict = same & (ri > ci)
    incl = same & (ri >= ci)
    eye = (ri == ci).astype(F32)
    bd_ones = same.astype(BF16)
    bf = lambda x: x.astype(BF16)
    dot = lambda a, b: jnp.dot(bf(a), bf(b), preferred_element_type=F32)
    zero = jnp.zeros((), BF16)

    def stack(x):
        return jnp.concatenate([jnp.where(head0, x, zero), jnp.where(head0, zero, x)], axis=0)

    probs = [(c, p) for c in range(nchunk) for p in range(npair)]
    rows = lambda c: pl.ds(c * L, L)
    cols = lambda p: slice(p * LANES, (p + 1) * LANES)
    pm = {k: stack(pt_ref[rows(k[0]), cols(k[1])]) for k in probs}
    rm = {k: stack(rt_ref[rows(k[0]), cols(k[1])]) for k in probs}
    qk = {k: jnp.concatenate([stack(qh_ref[rows(k[0]), cols(k[1])]), stack(kh_ref[rows(k[0]), cols(k[1])])], axis=0)
          for k in probs}
    vm = {k: stack(v_ref[rows(k[0]), cols(k[1])]) for k in probs}
    x = {k: _dot_nt(jnp.concatenate([pm[k], rm[k]], axis=0), qk[k]) for k in probs}
    sums = lambda v: jnp.dot(bf(v), bd_ones, preferred_element_type=F32)
    n_pow = {k: jnp.where(strict, x[k][0:L2, 0:L2], 0.0) for k in probs}
    a_kp = {k: bf(jnp.where(strict, x[k][0:L2, L2:2 * L2], 0.0)) for k in probs}
    r_q = {k: bf(jnp.where(incl, x[k][L2:2 * L2, 0:L2], 0.0)) for k in probs}
    r_k = {k: bf(jnp.where(incl, x[k][L2:2 * L2, L2:2 * L2], 0.0)) for k in probs}
    t_inv = {k: eye + n_pow[k] for k in probs}
    for _ in range(5):
        n_pow = {k: dot(n_pow[k], n_pow[k]) for k in probs}
        t_inv = {k: t_inv[k] + dot(n_pow[k], t_inv[k]) for k in probs}
    t_inv = {k: bf(t_inv[k]) for k in probs}
    pr = {k: jnp.concatenate([pm[k], rm[k]], axis=0) for k in probs}
    akv = {k: dot(jnp.concatenate([a_kp[k], r_k[k]], axis=0), vm[k]) for k in probs}

    state = [s_scr[p] for p in range(npair)]
    ys = {}
    for c in range(nchunk):
        ks = [(c, p) for p in range(npair)]
        ps = {k: dot(pr[k], bf(state[k[1]])) + akv[k] for k in ks}
        rhs = {k: ps[k][0:L2, :] for k in ks}
        rs = {k: ps[k][L2:2 * L2, :] for k in ks}
        u = {k: bf(dot(t_inv[k], rhs[k])) for k in ks}
        upd = {k: _dot_tn(qk[k], jnp.concatenate([u[k], vm[k]], axis=0)) for k in ks}
        for k in ks:
            ys[k] = rs[k] + dot(r_q[k], u[k])
            w_col = jnp.sum(eye * el_ref[c, :, cols(k[1])], axis=1, keepdims=True)
            state[k[1]] = w_col * (state[k[1]] + upd[k])
    for p in range(npair):
        s_scr[p] = state[p]

    y = {k: ys[k][0:L, :] + ys[k][L:L2, :] for k in probs}
    yc = {k: y[k] - sums(y[k]) * (1.0 / HEAD_DIM) for k in probs}
    var = {k: sums(yc[k] * yc[k]) * (1.0 / HEAD_DIM) for k in probs}
    for k in probs:
        cs = cols(k[1])
        yn = yc[k] * lax.rsqrt(var[k] + RWKV_LN_EPS) * lng_ref[:, cs] + lnb_ref[:, cs]
        o_ref[rows(k[0]), cs] = ((yn + bv_ref[rows(k[0]), cs].astype(F32)) * g_ref[rows(k[0]), cs].astype(F32)
                                 ).astype(o_ref.dtype)

    @pl.when(tstep == nt - 1)
    def _():
        sout_ref[...] = s_scr[...]


def rwkv_prompt(parts, lw, bsz, t, npair=6, nchunk=4):
    tb = nchunk * RWKV_CHUNK
    nt = t // tb
    wl = npair * LANES
    blk = pl.BlockSpec((tb, wl), lambda b, p, i: (b * nt + i, p))
    par = pl.BlockSpec((1, wl), lambda b, p, i: (0, p))
    el = pl.BlockSpec((nchunk, 1, wl), lambda b, p, i: (b * nt + i, 0, p))
    return pl.pallas_call(
        functools.partial(_rwkv_chain_kernel, nchunk=nchunk, nt=nt, npair=npair),
        grid=(bsz, N_PAIRS // npair, nt),
        in_specs=[blk] * 7 + [el, par, par],
        out_specs=[blk, pl.BlockSpec((None, npair, LANES, LANES), lambda b, p, i: (b, p, 0, 0))],
        out_shape=[jax.ShapeDtypeStruct((bsz * t, BRANCH_W), BF16),
                   jax.ShapeDtypeStruct((bsz, N_PAIRS, LANES, LANES), F32)],
        scratch_shapes=[pltpu.VMEM((npair, LANES, LANES), F32)],
        compiler_params=_cparams(("parallel", "parallel", "arbitrary")),
        name="rwkv_prompt",
    )(*parts, lw["ln_g_row"], lw["ln_b_row"])


def _rwkv_sample_step_kernel(s_ref, r_ref, ld_ref, k_ref, kk_ref, a_ref, v_ref, g_ref, rk_ref, lng_ref, lnb_ref,
                             o_ref, sout_ref):
    s = s_ref[...]
    r, kmod, kkraw, a = r_ref[...], k_ref[...], kk_ref[...], a_ref[...]
    v = v_ref[...]
    kkn = kkraw / jnp.maximum(jnp.sqrt(jnp.sum(kkraw * kkraw, axis=-1, keepdims=True)), 1e-12)
    sa = jnp.sum(s * (-kkn), axis=-1, keepdims=True)
    sn = s * jnp.exp(ld_ref[...]) + sa * (kkn * a) + v * kmod
    sout_ref[...] = sn
    y = jnp.sum(sn * r, axis=-1, keepdims=True)
    mean = jnp.mean(y, axis=1, keepdims=True)
    yc = y - mean
    var = jnp.mean(yc * yc, axis=1, keepdims=True)
    yn = yc * lax.rsqrt(var + RWKV_LN_EPS) * lng_ref[...] + lnb_ref[...]
    bonus = jnp.sum(r * kmod * rk_ref[...], axis=-1, keepdims=True) * v
    o_ref[...] = (yn + bonus) * g_ref[...]


def rwkv_sample_step(state, rows, cols, lw, layer):
    bd = rows[0].shape[0]
    row = pl.BlockSpec((None, RWKV_HEADS, 1, HEAD_DIM), lambda b: (b, 0, 0, 0))
    col = pl.BlockSpec((None, RWKV_HEADS, HEAD_DIM, 1), lambda b: (b, 0, 0, 0))
    prow = pl.BlockSpec((RWKV_HEADS, 1, HEAD_DIM), lambda b: (0, 0, 0))
    pcol = pl.BlockSpec((RWKV_HEADS, HEAD_DIM, 1), lambda b: (0, 0, 0))
    return pl.pallas_call(
        _rwkv_sample_step_kernel,
        grid=(bd,),
        in_specs=[pl.BlockSpec((None, None, RWKV_HEADS, HEAD_DIM, HEAD_DIM), lambda b: (layer, b, 0, 0, 0))]
        + [row] * 5 + [col] * 2 + [prow, pcol, pcol],
        out_specs=[col, pl.BlockSpec((None, RWKV_HEADS, HEAD_DIM, HEAD_DIM), lambda b: (b, 0, 0, 0))],
        out_shape=[jax.ShapeDtypeStruct((bd, RWKV_HEADS, HEAD_DIM, 1), F32),
                   jax.ShapeDtypeStruct((bd, RWKV_HEADS, HEAD_DIM, HEAD_DIM), F32)],
        compiler_params=_cparams(("parallel",)),
        name="rwkv_sample_step",
    )(state, *rows, *cols, lw["r_k_h"], lw["ln_g_col"], lw["ln_b_col"])


def _cache_update_kernel(k_ref, v_ref, newk_ref, newv_ref, ok_ref, ov_ref):
    win = k_ref.shape[-1]
    is_last = lax.broadcasted_iota(jnp.int32, (1, win), 1) == win - 1
    for c_ref, new_ref, o_ref in ((k_ref, newk_ref, ok_ref), (v_ref, newv_ref, ov_ref)):
        for h in range(ATT_HEADS):
            o_ref[h] = jnp.where(is_last, new_ref[h], pltpu.roll(c_ref[h], win - 1, 1))


def cache_update(cache_k, cache_v, new_k, new_v):
    nl, bd = cache_k.shape[:2]
    win = cache_k.shape[-1]
    blk = pl.BlockSpec((None, None, ATT_HEADS, HEAD_DIM, win), lambda l, b: (l, b, 0, 0, 0))
    new = pl.BlockSpec((None, None, ATT_HEADS, HEAD_DIM, 1), lambda l, b: (l, b, 0, 0, 0))
    shp = jax.ShapeDtypeStruct(cache_k.shape, cache_k.dtype)
    return pl.pallas_call(
        _cache_update_kernel,
        grid=(nl, bd),
        in_specs=[blk, blk, new, new],
        out_specs=[blk, blk],
        out_shape=[shp, shp],
        compiler_params=_cparams(("parallel", "parallel")),
        name="cache_update",
    )(cache_k, cache_v, new_k, new_v)


def _pack_layer(l, p):
    row = lambda a: a.reshape(1, -1)
    pad_rows = lambda a, n: jnp.concatenate([a, jnp.zeros((n - a.shape[0], a.shape[1]), a.dtype)], axis=0).astype(BF16)
    lane_pad = lambda a: jnp.zeros((1, LANES), F32).at[0, DT_LANE0:DT_LANE0 + SSM_HEADS].set(a)
    mu = p["rwkv_mu"][l]
    z1 = lambda n: jnp.zeros((n,), F32)
    cw = p["ssm_conv_w"][l]
    cb = p["ssm_conv_b"][l]
    rep64 = lambda a: jnp.repeat(a, HEAD_DIM)
    return dict(
        ln1_g=p["ln1_g"][l], ln2_g=p["ln2_g"][l],
        cw_x=cw[:, :BRANCH_W], cw_b=cw[:, BRANCH_W:BRANCH_W + 512], cw_c=cw[:, BRANCH_W + 512:],
        cb_x=row(cb[:BRANCH_W]), cb_b=row(cb[BRANCH_W:BRANCH_W + 512]), cb_c=row(cb[BRANCH_W + 512:]),
        conv_w=cw, conv_b=cb,
        dt_bias=lane_pad(p["ssm_dt_bias"][l]), a_neg=lane_pad(-jnp.exp(p["ssm_a_log"][l])),
        d_skip=row(rep64(p["ssm_d"][l])), ssm_norm_g=row(p["ssm_norm_g"][l]),
        dt_bias_h=p["ssm_dt_bias"][l].reshape(SSM_HEADS, 1, 1), a_log_h=p["ssm_a_log"][l].reshape(SSM_HEADS, 1, 1),
        d_h=p["ssm_d"][l].reshape(SSM_HEADS, 1, 1), ssm_norm_g_col=p["ssm_norm_g"][l].reshape(SSM_HEADS, HEAD_DIM, 1),
        mu_r=row(mu[:BRANCH_W]), mu_k=row(mu[BRANCH_W:2 * BRANCH_W]), mu_v=row(mu[2 * BRANCH_W:3 * BRANCH_W]),
        mu_l=row(jnp.concatenate([mu[3 * BRANCH_W:3 * BRANCH_W + LORA_W], z1(32),
                                  mu[3 * BRANCH_W + LORA_W:3 * BRANCH_W + LORA_W + LORA_A], z1(32),
                                  mu[3 * BRANCH_W + LORA_W + LORA_A:]])),
        w0=row(p["rwkv_w0"][l]), w_up=pad_rows(p["rwkv_w_up"][l], LANES), a0=row(p["rwkv_a0"][l]),
        a_up=pad_rows(p["rwkv_a_up"][l], LANES), g_up=p["rwkv_g_up"][l].astype(BF16),
        k_k=row(p["rwkv_k_k"][l]), k_a=row(p["rwkv_k_a"][l]),
        r_k_row=row(p["rwkv_r_k"][l]), ln_g_row=row(p["rwkv_ln_g"][l]), ln_b_row=row(p["rwkv_ln_b"][l]),
        r_k_h=p["rwkv_r_k"][l].reshape(RWKV_HEADS, 1, HEAD_DIM),
        ln_g_col=p["rwkv_ln_g"][l].reshape(RWKV_HEADS, HEAD_DIM, 1),
        ln_b_col=p["rwkv_ln_b"][l].reshape(RWKV_HEADS, HEAD_DIM, 1),
    )


def _pack_big(p):
    w_t = jnp.swapaxes(p["w_in"], 1, 2)
    zrows = lambda n: jnp.zeros((w_t.shape[0], n, D_MODEL), F32)
    w_tail_t = jnp.concatenate([
        w_t[:, _O_XBC + BRANCH_W:_O_DT], w_t[:, _O_LORA:_O_LORA + LORA_W], w_t[:, _O_DT:_O_RW], zrows(8),
        w_t[:, _O_LORA + LORA_W:_O_LORA + LORA_W + LORA_A], zrows(32), w_t[:, _O_LORA + LORA_W + LORA_A:_O_GATE]], axis=1)
    return dict(w_in_t=w_t, w_tail_t=w_tail_t, w_att=p["w_branch_att"].astype(BF16),
                w_ssm=p["w_branch_ssm"].astype(BF16), w_rwkv=p["w_branch_rwkv"].astype(BF16),
                w_out=p["w_out"].astype(BF16), w_up_ff=p["w_up"].astype(BF16), w_down_ff=p["w_down"].astype(BF16))


def _unpack_lora(rows):
    return jnp.concatenate([rows[..., 0:LORA_W], rows[..., LANES:LANES + LORA_A], rows[..., 2 * LANES:]], axis=-1)


def _pack_lora(rows):
    z = lambda n: jnp.zeros(rows.shape[:-1] + (n,), rows.dtype)
    return jnp.concatenate([rows[..., 0:LORA_W], z(32), rows[..., LORA_W:LORA_W + LORA_A], z(32),
                            rows[..., LORA_W + LORA_A:]], axis=-1)


def _alibi_slopes():
    n = 3 * ATT_HEADS
    idx = jnp.arange(1, n + 1, dtype=F32)
    return jnp.exp2(-8.0 * idx / n).reshape(3, ATT_HEADS)


def _layer(xp, xs, hp, hs, lw, big, layer, slopes, st, bsz, t, bd, next_g, next_dtype):
    proj, tail, proj_s, tail_s = project_main_and_tail(hp, hs, big, layer)
    outs, state_p = _prompt_mixers(proj, tail, lw, slopes, bsz, t)
    outs_s, new_rows, state_s = _sample_mixers(proj_s, tail_s, lw, slopes, st, layer, bd)
    merged, merged_s = merge_branches(outs, proj, outs_s, proj_s, big["w_att"], big["w_ssm"], big["w_rwkv"], layer)
    xp, h2, xs, h2_s = matmul_res_norm(merged, merged_s, big["w_out"], layer, xp, xs, lw["ln2_g"], BF16)
    up, up_s = matmul(h2, h2_s, big["w_up_ff"], layer, epi="relu2", out_dtype=BF16)
    xp, hp, xs, hs = matmul_res_norm(up, up_s, big["w_down_ff"], layer, xp, xs, next_g, next_dtype)
    return xp, xs, hp, hs, state_p, new_rows, state_s


def _prompt_mixers(proj, tail, lw, slopes, bsz, t):
    o_att = att_prompt(proj, slopes, bsz, t)
    o_ssm, h_fin = ssd_prompt(proj, tail, lw, bsz, t)
    zero_init = [jnp.zeros((bsz, 1, w), F32) for w in (BRANCH_W, BRANCH_W, BRANCH_W, 512)]
    rw = rwkv_prep_prompt(proj, tail, zero_init, lw, bsz, t)
    o_rwkv, s_fin = rwkv_prompt(rw, lw, bsz, t)

    p3 = proj.reshape(bsz, t, N_MAIN)
    t3 = tail.reshape(bsz, t, N_TAIL)
    kv = []
    for g in range(3):
        win = ATT_WINDOWS[g]
        for c0 in (C_K, C_V):
            kv.append(p3[:, t - win:, c0 + g * ATT_OUT:c0 + (g + 1) * ATT_OUT].reshape(bsz, win, ATT_HEADS, HEAD_DIM))
    nc = SSM_CONV - 1
    conv_new = jnp.concatenate([p3[:, t - nc:, C_X:C_X + BRANCH_W], t3[:, t - nc:, T_B:T_B + 1024]], -1)
    shift_new = jnp.concatenate([p3[:, t - 1, C_R:C_R + 3 * BRANCH_W], _unpack_lora(t3[:, t - 1, T_L:T_L + 512])], axis=-1)
    ssm_new = h_fin.reshape(bsz, SSM_HEADS, HEAD_DIM, SSM_STATE)
    s6 = s_fin.reshape(bsz, N_PAIRS, 2, HEAD_DIM, 2, HEAD_DIM)
    s_heads = jnp.stack([s6[:, :, 0, :, 0, :], s6[:, :, 1, :, 1, :]], axis=2)
    rwkv_new = jnp.swapaxes(s_heads, -1, -2).reshape(bsz, RWKV_HEADS, HEAD_DIM, HEAD_DIM)
    return (o_att, o_ssm, o_rwkv), tuple(kv) + (conv_new, ssm_new, shift_new, rwkv_new)


def _sample_mixers(proj, tail, lw, slopes, st, layer, bd):
    (caches_t, conv_st, ssm_st, shift_st, rwkv_st) = st
    qkv = proj[:, :3 * BRANCH_W].reshape(bd, 3, 3, ATT_HEADS, HEAD_DIM, 1)
    o_att = att_sample(qkv, caches_t, slopes, layer).reshape(bd, ATT_OUT).astype(BF16)
    new_rows = tuple(qkv[:, part, g] for g in range(3) for part in (1, 2))

    xbc_new = jnp.concatenate([proj[:, C_X:C_X + BRANCH_W], tail[:, T_B:T_B + 1024]], axis=-1)
    cst = conv_st[layer]
    xc = ssd_sample_conv(xbc_new, jnp.swapaxes(cst, 0, 1), lw["conv_w"], lw["conv_b"])
    conv_new = jnp.concatenate([cst[:, 1:], xbc_new[:, None]], axis=1)
    x_col = xc[:, :BRANCH_W].reshape(bd, SSM_HEADS, HEAD_DIM, 1)
    z_col = proj[:, C_Z:C_Z + BRANCH_W].reshape(bd, SSM_HEADS, HEAD_DIM, 1)
    bmat = xc[:, BRANCH_W:BRANCH_W + 512].reshape(bd, SSM_GROUPS, 1, SSM_STATE)
    cmat = xc[:, BRANCH_W + 512:].reshape(bd, SSM_GROUPS, 1, SSM_STATE)
    dt_raw = tail[:, T_L + DT_LANE0:T_L + DT_LANE0 + SSM_HEADS].reshape(bd, SSM_HEADS, 1, 1)
    y_col, ssm_new = ssd_sample_step(ssm_st, x_col, z_col, bmat, cmat, dt_raw, lw, layer)
    o_ssm = y_col.reshape(bd, BRANCH_W).astype(BF16)

    sh = shift_st[layer]
    init = [sh[:, None, i * BRANCH_W:(i + 1) * BRANCH_W] for i in range(3)] + [_pack_lora(sh[:, None, 3 * BRANCH_W:])]
    r, ld, kmod, v, kkraw, a, gg = rwkv_prep(proj, tail, init, lw, bd, 1)
    rowf = lambda z: z.reshape(bd, RWKV_HEADS, 1, HEAD_DIM)
    colf = lambda z: z.reshape(bd, RWKV_HEADS, HEAD_DIM, 1)
    o_col, rwkv_new = rwkv_sample_step(rwkv_st, [rowf(r), rowf(ld), rowf(kmod), rowf(kkraw), rowf(a)],
                                       [colf(v), colf(gg)], lw, layer)
    o_rwkv = o_col.reshape(bd, BRANCH_W).astype(BF16)
    shift_new = jnp.concatenate([proj[:, C_R:C_R + 3 * BRANCH_W], _unpack_lora(tail[:, T_L:T_L + 512])], axis=-1)
    return (o_att, o_ssm, o_rwkv), new_rows, (conv_new, ssm_new, shift_new, rwkv_new)


def kernel(x_prompt, x_sample, cache_att_k0, cache_att_v0, cache_att_k1, cache_att_v1, cache_att_k2, cache_att_v2, state_ssm_conv, state_ssm, state_rwkv_shift, state_rwkv, ln1_g, w_in, ssm_conv_w, ssm_conv_b, ssm_dt_bias, ssm_a_log, ssm_d, ssm_norm_g, rwkv_mu, rwkv_w0, rwkv_w_up, rwkv_a0, rwkv_a_up, rwkv_g_up, rwkv_k_k, rwkv_k_a, rwkv_r_k, rwkv_ln_g, rwkv_ln_b, w_branch_att, w_branch_ssm, w_branch_rwkv, w_out, ln2_g, w_up, w_down, final_g):
    p = dict(ln1_g=ln1_g, w_in=w_in, ssm_conv_w=ssm_conv_w, ssm_conv_b=ssm_conv_b, ssm_dt_bias=ssm_dt_bias,
             ssm_a_log=ssm_a_log, ssm_d=ssm_d, ssm_norm_g=ssm_norm_g, rwkv_mu=rwkv_mu, rwkv_w0=rwkv_w0,
             rwkv_w_up=rwkv_w_up, rwkv_a0=rwkv_a0, rwkv_a_up=rwkv_a_up, rwkv_g_up=rwkv_g_up, rwkv_k_k=rwkv_k_k,
             rwkv_k_a=rwkv_k_a, rwkv_r_k=rwkv_r_k, rwkv_ln_g=rwkv_ln_g, rwkv_ln_b=rwkv_ln_b,
             w_branch_att=w_branch_att, w_branch_ssm=w_branch_ssm, w_branch_rwkv=w_branch_rwkv, w_out=w_out,
             ln2_g=ln2_g, w_up=w_up, w_down=w_down)
    depth = w_in.shape[0]
    bsz, t, _ = x_prompt.shape
    bd = x_sample.shape[0]
    slopes = _alibi_slopes()
    caches = (cache_att_k0, cache_att_v0, cache_att_k1, cache_att_v1, cache_att_k2, cache_att_v2)
    caches_t = tuple(jnp.transpose(c, (0, 1, 3, 4, 2)) for c in caches)
    st = (caches_t, state_ssm_conv, state_ssm, state_rwkv_shift, state_rwkv)
    xp = x_prompt.reshape(bsz * t, D_MODEL)
    xs = x_sample.reshape(bd, D_MODEL)
    big = _pack_big(p)
    p_new, s_new, s_rows = [], [], []
    hp = rmsnorm(xp, ln1_g[0], BF16)
    hs = rmsnorm(xs, ln1_g[0], BF16)
    for l in range(depth):
        lw = _pack_layer(l, p)
        last = l == depth - 1
        next_g, next_dtype = (final_g, F32) if last else (ln1_g[l + 1], BF16)
        xp, xs, hp, hs, sp, rows, ss = _layer(xp, xs, hp, hs, lw, big, l, slopes, st, bsz, t, bd, next_g, next_dtype)
        p_new.append(sp)
        s_new.append(ss)
        s_rows.append(rows)
    y_prompt = hp.reshape(bsz, t, D_MODEL)
    y_sample = hs.reshape(bd, 1, D_MODEL)
    stack = lambda per_layer, n: tuple(jnp.stack([s[i] for s in per_layer]) for i in range(n))
    new_rows = stack(s_rows, 6)
    s_caches = ()
    for g in range(3):
        upd = cache_update(caches_t[2 * g], caches_t[2 * g + 1], new_rows[2 * g], new_rows[2 * g + 1])
        s_caches += tuple(jnp.transpose(c, (0, 1, 4, 2, 3)) for c in upd)
    return (y_prompt, y_sample) + stack(p_new, 10) + s_caches + stack(s_new, 4)
```

```python
import functools

import jax
import jax.numpy as jnp
from jax import lax
from jax.experimental import pallas as pl
from jax.experimental.pallas import tpu as pltpu

F32 = jnp.float32
BF16 = jnp.bfloat16

D_MODEL = 2048
HEAD_DIM = 64
BRANCH_W = 1536
NORM_EPS = 1e-5
ATT_WINDOWS = (128, 512, 2048)
ATT_DILS = (1, 4, 16)
ATT_SPAN = 128
ATT_HEADS = 8
ATT_OUT = ATT_HEADS * HEAD_DIM
ATT_SCALE = HEAD_DIM ** -0.5
SSM_HEADS = 24
SSM_GROUPS = 4
SSM_STATE = 128
SSM_CONV = 4
SSM_CHUNK = 128
RWKV_HEADS = 24
LORA_W = 96
LORA_A = 96
LORA_G = 256
RWKV_LN_EPS = 64e-5
D_FF = 4 * D_MODEL
RWKV_CHUNK = 64
N_PAIRS = BRANCH_W // 128

LANES = 128
SUBLANES = 8
VMEM_LIMIT_BYTES = 56 * 1024 * 1024

C_Q, C_K, C_V, C_Z, C_X, C_R, C_RK, C_RV = (i * BRANCH_W for i in range(8))
C_G = 8 * BRANCH_W
N_MAIN = C_G + 3 * D_MODEL
T_B = 0
T_C = 512
T_L = 1024
N_TAIL = T_L + 512
DT_LANE0 = LORA_W

_O_QKV, _O_Z, _O_XBC, _O_DT, _O_RW, _O_GATE = 0, 4608, 6144, 8704, 8728, 13784
_O_LORA = _O_RW + 3 * BRANCH_W
_W_IN_COLS = _O_GATE + 3 * D_MODEL
PROJ_TN = 768


def _cparams(sem):
    return pltpu.CompilerParams(dimension_semantics=sem, vmem_limit_bytes=VMEM_LIMIT_BYTES)


def _softplus(x):
    return jnp.maximum(x, 0.0) + jnp.log(1.0 + jnp.exp(-jnp.abs(x)))


def _sigmoid(x):
    return 1.0 / (1.0 + jnp.exp(-x))


def _silu(x):
    return x * _sigmoid(x)


def _split(x, terms):
    out = []
    for _ in range(terms - 1):
        hi = x.astype(BF16)
        out.append(hi)
        x = x - hi.astype(F32)
    out.append(x.astype(BF16))
    return out


def _dot_exact_rhs(x, mat, terms=3):
    parts = [jnp.dot(t, mat, preferred_element_type=F32) for t in _split(x, terms)]
    return functools.reduce(lambda a, b: a + b, parts)


def _dot_exact_lhs(mat, x, terms=3):
    parts = [jnp.dot(mat, t, preferred_element_type=F32) for t in _split(x, terms)]
    return functools.reduce(lambda a, b: a + b, parts)


def _dot_nt(a, b):
    return lax.dot_general(a, b, (((1,), (1,)), ((), ())), preferred_element_type=F32)


def _dot_tn(a, b):
    return lax.dot_general(a, b, (((0,), (0,)), ((), ())), preferred_element_type=F32)


def _rmsnorm_kernel(x_ref, g_ref, o_ref):
    x = x_ref[...]
    y = x * lax.rsqrt(jnp.mean(x * x, axis=-1, keepdims=True) + NORM_EPS)
    o_ref[...] = (y * g_ref[...]).astype(o_ref.dtype)


def rmsnorm(x, g, out_dtype):
    m, d = x.shape
    tm = min(m, 512)
    return pl.pallas_call(
        _rmsnorm_kernel,
        grid=(m // tm,),
        in_specs=[pl.BlockSpec((tm, d), lambda i: (i, 0)), pl.BlockSpec((1, d), lambda i: (0, 0))],
        out_specs=pl.BlockSpec((tm, d), lambda i: (i, 0)),
        out_shape=jax.ShapeDtypeStruct((m, d), out_dtype),
        compiler_params=_cparams(("parallel",)),
        name="rmsnorm",
    )(x, g.reshape(1, d))


def _mm_kernel(*refs, nk, epi):
    refs = list(refs)
    x_ref, w_ref = refs[0:2]
    r_ref = refs.pop(2) if epi == "res" else None
    xs_ref = refs[2]
    rs_ref = refs.pop(3) if epi == "res" else None
    o_ref, os_ref = refs[3:5]
    acc_ref, accs_ref = (refs[5], refs[6]) if nk > 1 else (None, None)
    first_rows = pl.program_id(0) == 0
    k = pl.program_id(2)

    def finish(a, r, o):
        if epi == "res":
            a = r[...] + a
        elif epi == "relu2":
            a = jnp.square(jnp.maximum(a, 0.0))
        o[...] = a.astype(o.dtype)

    def accumulate(x, acc, r, o):
        part = jnp.dot(x[...], w_ref[...].astype(x.dtype), preferred_element_type=F32)
        if nk == 1:
            finish(part, r, o)
            return

        @pl.when(k == 0)
        def _():
            acc[...] = part

        @pl.when(k > 0)
        def _():
            acc[...] += part

        @pl.when(k == nk - 1)
        def _():
            finish(acc[...], r, o)

    accumulate(x_ref, acc_ref, r_ref, o_ref)

    @pl.when(first_rows)
    def _():
        accumulate(xs_ref, accs_ref, rs_ref, os_ref)


def _pick(n, cands):
    for c in cands:
        if n % c == 0:
            return c
    return n


def matmul(x, xs, w, layer, *, epi="none", res=None, res_s=None, out_dtype=F32):
    m, kdim = x.shape
    ms = xs.shape[0]
    n = w.shape[2]
    tm = min(m, 1024 if (epi == "res" or w.dtype == F32) else 2048)
    tn = _pick(n, (1536, 1024, 512, 256, 128))
    tk = min(kdim, 2048)
    nk = kdim // tk
    nj = n // tn
    side_col = lambda i, j: jnp.where(i == 0, j, nj - 1)
    in_specs = [pl.BlockSpec((tm, tk), lambda i, j, k: (i, k)),
                pl.BlockSpec((None, tk, tn), lambda i, j, k: (layer, k, j))]
    args = [x, w]
    if epi == "res":
        in_specs.append(pl.BlockSpec((tm, tn), lambda i, j, k: (i, j)))
        args.append(res)
    in_specs.append(pl.BlockSpec((ms, tk), lambda i, j, k: (0, k)))
    args.append(xs)
    if epi == "res":
        in_specs.append(pl.BlockSpec((ms, tn), lambda i, j, k: (0, side_col(i, j))))
        args.append(res_s)
    scratch = [pltpu.VMEM((tm, tn), F32), pltpu.VMEM((ms, tn), F32)] if nk > 1 else []
    return pl.pallas_call(
        functools.partial(_mm_kernel, nk=nk, epi=epi),
        grid=(m // tm, nj, nk),
        in_specs=in_specs,
        out_specs=[pl.BlockSpec((tm, tn), lambda i, j, k: (i, j)),
                   pl.BlockSpec((ms, tn), lambda i, j, k: (0, side_col(i, j)))],
        out_shape=[jax.ShapeDtypeStruct((m, n), out_dtype), jax.ShapeDtypeStruct((ms, n), out_dtype)],
        scratch_shapes=scratch,
        compiler_params=_cparams(("arbitrary", "arbitrary", "arbitrary")),
        name="matmul_" + epi,
    )(*args)


def _mm_res_norm_kernel(x_ref, w_ref, r_ref, xs_ref, rs_ref, g_ref, o_ref, on_ref, os_ref, osn_ref, *scr, nk):
    first_rows = pl.program_id(0) == 0
    k = pl.program_id(1)

    def finish(a, r, o, on):
        a = r[...] + a
        o[...] = a
        y = a * lax.rsqrt(jnp.mean(a * a, axis=-1, keepdims=True) + NORM_EPS)
        on[...] = (y * g_ref[...]).astype(on.dtype)

    def accumulate(x, acc, r, o, on):
        part = jnp.dot(x[...], w_ref[...], preferred_element_type=F32)
        if nk == 1:
            finish(part, r, o, on)
            return

        @pl.when(k == 0)
        def _():
            acc[...] = part

        @pl.when(k > 0)
        def _():
            acc[...] += part

        @pl.when(k == nk - 1)
        def _():
            finish(acc[...], r, o, on)

    acc_ref, accs_ref = scr if nk > 1 else (None, None)
    accumulate(x_ref, acc_ref, r_ref, o_ref, on_ref)

    @pl.when(first_rows)
    def _():
        accumulate(xs_ref, accs_ref, rs_ref, os_ref, osn_ref)


def matmul_res_norm(x, xs, w, layer, res, res_s, g, norm_dtype):
    m, kdim = x.shape
    ms = xs.shape[0]
    n = w.shape[2]
    tm = min(m, 512)
    tk = kdim if kdim <= 2048 else 1024
    nk = kdim // tk
    row = lambda width: pl.BlockSpec((tm, width), lambda i, k: (i, 0))
    side = lambda width: pl.BlockSpec((ms, width), lambda i, k: (0, 0))
    scratch = [pltpu.VMEM((tm, n), F32), pltpu.VMEM((ms, n), F32)] if nk > 1 else []
    return pl.pallas_call(
        functools.partial(_mm_res_norm_kernel, nk=nk),
        grid=(m // tm, nk),
        in_specs=[pl.BlockSpec((tm, tk), lambda i, k: (i, k)), pl.BlockSpec((None, tk, n), lambda i, k: (layer, k, 0)),
                  row(n), pl.BlockSpec((ms, tk), lambda i, k: (0, k)), side(n), pl.BlockSpec((1, n), lambda i, k: (0, 0))],
        out_specs=[row(n), row(n), side(n), side(n)],
        out_shape=[jax.ShapeDtypeStruct((m, n), F32), jax.ShapeDtypeStruct((m, n), norm_dtype),
                   jax.ShapeDtypeStruct((ms, n), F32), jax.ShapeDtypeStruct((ms, n), norm_dtype)],
        scratch_shapes=scratch,
        compiler_params=_cparams(("arbitrary", "arbitrary")),
        name="matmul_res_norm",
    )(x, w, res, xs, res_s, g.reshape(1, n))


def _proj_kernel(x_ref, xs_ref, w_ref, o_ref, os_ref, wb_scr):
    @pl.when(pl.program_id(1) == 0)
    def _():
        wb_scr[...] = w_ref[0].astype(BF16)
        os_ref[...] = _dot_nt(xs_ref[...], wb_scr[...])

    o_ref[...] = _dot_nt(x_ref[...], wb_scr[...])


def _main_row_start(j):
    t_rw, t_gate = C_R // PROJ_TN, C_G // PROJ_TN
    start = j * PROJ_TN + jnp.where(j >= t_rw, _O_RW - C_R, 0) + jnp.where(j >= t_gate, (_O_GATE - C_G) - (_O_RW - C_R), 0)
    return pl.multiple_of(start, SUBLANES)


def project(h, hs, w_t, layer, n_out, row_start):
    m, kdim = h.shape
    ms = hs.shape[0]
    tm = min(m, 2048)
    ell = pl.Element
    return pl.pallas_call(
        _proj_kernel,
        grid=(n_out // PROJ_TN, m // tm),
        in_specs=[pl.BlockSpec((tm, kdim), lambda j, i: (i, 0)),
                  pl.BlockSpec((ms, kdim), lambda j, i: (0, 0)),
                  pl.BlockSpec((ell(1), ell(PROJ_TN), ell(kdim)), lambda j, i: (layer, row_start(j), 0))],
        out_specs=[pl.BlockSpec((tm, PROJ_TN), lambda j, i: (i, j)), pl.BlockSpec((ms, PROJ_TN), lambda j, i: (0, j))],
        out_shape=[jax.ShapeDtypeStruct((m, n_out), F32), jax.ShapeDtypeStruct((ms, n_out), F32)],
        scratch_shapes=[pltpu.VMEM((PROJ_TN, kdim), BF16)],
        compiler_params=_cparams(("parallel", "arbitrary")),
        name="project",
    )(h, hs, w_t)


def project_main_and_tail(h, hs, big, layer):
    main, main_s = project(h, hs, big["w_in_t"], layer, N_MAIN, _main_row_start)
    tail, tail_s = project(h, hs, big["w_tail_t"], layer, N_TAIL, lambda j: pl.multiple_of(j * PROJ_TN, SUBLANES))
    return main, tail, main_s, tail_s


def _merge_kernel(oa_ref, os_ref, or_ref, ga_ref, gs_ref, gr_ref, oa2_ref, os2_ref, or2_ref, ga2_ref, gs2_ref,
                  gr2_ref, wa_ref, ws_ref, wr_ref, o_ref, o2_ref):
    f = lambda a, b: jnp.dot(a[...], b[...], preferred_element_type=F32)

    def merged(a_ref, s_ref, r_ref, g_a, g_s, g_r, out):
        acc = _sigmoid(g_a[...]) * f(a_ref, wa_ref)
        acc = acc + _sigmoid(g_s[...]) * f(s_ref, ws_ref)
        acc = acc + _sigmoid(g_r[...]) * f(r_ref, wr_ref)
        out[...] = acc.astype(out.dtype)

    merged(oa_ref, os_ref, or_ref, ga_ref, gs_ref, gr_ref, o_ref)

    @pl.when(pl.program_id(1) == 0)
    def _():
        merged(oa2_ref, os2_ref, or2_ref, ga2_ref, gs2_ref, gr2_ref, o2_ref)


def merge_branches(outs, proj, outs_s, proj_s, wa, ws, wr, layer):
    m = outs[0].shape[0]
    ms = outs_s[0].shape[0]
    tm = min(m, 512)
    tn = 1024
    gb = C_G // tn
    nj = D_MODEL // tn
    widths = (ATT_OUT, BRANCH_W, BRANCH_W)
    row = lambda width: pl.BlockSpec((tm, width), lambda j, i: (i, 0))
    gate = lambda g: pl.BlockSpec((tm, tn), lambda j, i, g=g: (i, gb + g * nj + j))
    row_s = lambda width: pl.BlockSpec((ms, width), lambda j, i: (0, 0))
    gate_s = lambda g: pl.BlockSpec((ms, tn), lambda j, i, g=g: (0, gb + g * nj + j))
    col = lambda k: pl.BlockSpec((None, k, tn), lambda j, i: (layer, 0, j))
    return pl.pallas_call(
        _merge_kernel,
        grid=(nj, m // tm),
        in_specs=[row(w) for w in widths] + [gate(g) for g in range(3)] + [row_s(w) for w in widths]
        + [gate_s(g) for g in range(3)] + [col(w) for w in widths],
        out_specs=[pl.BlockSpec((tm, tn), lambda j, i: (i, j)), pl.BlockSpec((ms, tn), lambda j, i: (0, j))],
        out_shape=[jax.ShapeDtypeStruct((m, D_MODEL), BF16), jax.ShapeDtypeStruct((ms, D_MODEL), BF16)],
        compiler_params=_cparams(("parallel", "arbitrary")),
        name="merge_branches",
    )(*outs, proj, proj, proj, *outs_s, proj_s, proj_s, proj_s, wa, ws, wr)


def _att_prompt_kernel(sl_ref, *refs, nblks):
    o_ref = refs[15]
    scr = refs[16:]
    for g in range(3):
        _att_group(sl_ref, *refs[5 * g:5 * g + 5], scr[2 * g], scr[2 * g + 1], d=ATT_DILS[g], nblk=nblks[g], g=g)
    a, b, c = scr[1][...], scr[3][...], scr[5][...]
    mx = jnp.maximum(jnp.maximum(a, b), c)
    ea, eb, ec = jnp.exp(a - mx), jnp.exp(b - mx), jnp.exp(c - mx)
    num = ea * scr[0][...] + eb * scr[2][...] + ec * scr[4][...]
    o_ref[...] = (num / (ea + eb + ec)).astype(o_ref.dtype)


def _att_group(sl_ref, q_ref, kc_ref, kp_ref, vc_ref, vp_ref, o_ref, lse_ref, *, d, nblk, g):
    j = pl.program_id(1)
    hp = pl.program_id(2)
    lane = lax.broadcasted_iota(jnp.int32, (1, LANES), 1)
    head0 = lane < HEAD_DIM
    qi = lax.broadcasted_iota(jnp.int32, (ATT_SPAN, 2 * ATT_SPAN), 0)
    kj = lax.broadcasted_iota(jnp.int32, (ATT_SPAN, 2 * ATT_SPAN), 1)
    delta = qi - kj + ATT_SPAN
    band = (delta >= 0) & (delta <= ATT_SPAN)
    first_ok = band & ((kj >= ATT_SPAN) | (j > 0))
    dist = (delta * d).astype(F32)
    slopes = (sl_ref[g, 2 * hp], sl_ref[g, 2 * hp + 1])

    def rows(base):
        return pl.ds(base, ATT_SPAN) if d == 1 else pl.ds(base, ATT_SPAN, stride=d)

    for ub in range(nblk):
        valid = first_ok if ub == 0 else band
        for r in range(d):
            base = ub * ATT_SPAN * d + r
            q = q_ref[rows(base), :]
            if ub == 0:
                kp, vp = kp_ref[rows(r), :], vp_ref[rows(r), :]
            else:
                kp, vp = kc_ref[rows(base - ATT_SPAN * d), :], vc_ref[rows(base - ATT_SPAN * d), :]
            k = jnp.concatenate([kp, kc_ref[rows(base), :]], axis=0).astype(BF16)
            v = jnp.concatenate([vp, vc_ref[rows(base), :]], axis=0).astype(BF16)
            outs, lses = [], []
            for e in range(2):
                sel = head0 if e == 0 else jnp.logical_not(head0)
                qe = jnp.where(sel, q, 0.0).astype(BF16)
                s = _dot_nt(qe, k) * ATT_SCALE
                s = jnp.where(valid, s - slopes[e] * dist, -jnp.inf)
                mx = jnp.max(s, axis=-1, keepdims=True)
                p = jnp.exp(s - mx)
                den = jnp.sum(p, axis=-1, keepdims=True)
                outs.append(jnp.dot(p.astype(BF16), v, preferred_element_type=F32) / den)
                lses.append(mx + jnp.log(den))
            o_ref[rows(base), :] = jnp.where(head0, outs[0], outs[1])
            lse_ref[rows(base), :] = jnp.where(head0, lses[0], lses[1])


def att_prompt(proj, slopes, bsz, t):
    tb = ATT_SPAN * max(ATT_DILS)
    nt = t // tb
    in_specs = [pl.BlockSpec(memory_space=pltpu.SMEM)]
    nblks = []
    for g in range(3):
        sb = ATT_SPAN * ATT_DILS[g]
        nblk = tb // sb
        nblks.append(nblk)
        cq, ck, cv = (C_Q + g * ATT_OUT) // LANES, (C_K + g * ATT_OUT) // LANES, (C_V + g * ATT_OUT) // LANES
        cur = lambda c0: pl.BlockSpec((tb, LANES), lambda b, j, h, c0=c0: (b * nt + j, c0 + h))
        prev = lambda c0, sb=sb, nblk=nblk: pl.BlockSpec(
            (sb, LANES), lambda b, j, h, c0=c0: (jnp.maximum((b * nt + j) * nblk - 1, 0), c0 + h))
        in_specs += [cur(cq), cur(ck), prev(ck), cur(cv), prev(cv)]
    return pl.pallas_call(
        functools.partial(_att_prompt_kernel, nblks=tuple(nblks)),
        grid=(bsz, nt, ATT_HEADS // 2),
        in_specs=in_specs,
        out_specs=pl.BlockSpec((tb, LANES), lambda b, j, h: (b * nt + j, h)),
        out_shape=jax.ShapeDtypeStruct((bsz * t, ATT_OUT), BF16),
        scratch_shapes=[pltpu.VMEM((tb, LANES), F32)] * 6,
        compiler_params=_cparams(("parallel", "parallel", "parallel")),
        name="att_prompt",
    )(slopes, *([proj] * 15))


def _att_sample_kernel(sl_ref, qkv_ref, k0, v0, k1, v1, k2, v2, o_ref):
    caches = ((k0, v0), (k1, v1), (k2, v2))
    outs, lses = [], []
    for g in range(3):
        d = ATT_DILS[g]
        assert d & (d - 1) == 0
        win = caches[g][0].shape[-1]
        q, kn, vn = qkv_ref[0, g], qkv_ref[1, g], qkv_ref[2, g]
        kc = caches[g][0][...]
        vc = caches[g][1][...]
        pos = lax.broadcasted_iota(jnp.int32, (1, 1, win), 2)
        on_grid = (pos & (d - 1)) == 0
        dist = (win - pos).astype(F32)
        s = jnp.sum(kc * q, axis=1, keepdims=True) * ATT_SCALE - sl_ref[g] * dist
        s = jnp.where(on_grid, s, -jnp.inf)
        s_new = jnp.sum(kn * q, axis=1, keepdims=True) * ATT_SCALE
        mx = jnp.maximum(jnp.max(s, axis=2, keepdims=True), s_new)
        p = jnp.exp(s - mx)
        p_new = jnp.exp(s_new - mx)
        den = jnp.sum(p, axis=2, keepdims=True) + p_new
        outs.append((jnp.sum(vc * p, axis=2, keepdims=True) + p_new * vn) / den)
        lses.append(mx + jnp.log(den))
    mx = jnp.maximum(jnp.maximum(lses[0], lses[1]), lses[2])
    es = [jnp.exp(l - mx) for l in lses]
    o_ref[...] = (es[0] * outs[0] + es[1] * outs[1] + es[2] * outs[2]) / (es[0] + es[1] + es[2])


def att_sample(qkv, caches_t, slopes, layer):
    bd = qkv.shape[0]
    in_specs = [pl.BlockSpec((3, ATT_HEADS, 1, 1), lambda b: (0, 0, 0, 0)),
                pl.BlockSpec((None, 3, 3, ATT_HEADS, HEAD_DIM, 1), lambda b: (b, 0, 0, 0, 0, 0))]
    for c in caches_t:
        in_specs.append(pl.BlockSpec((None, None, ATT_HEADS, HEAD_DIM, c.shape[-1]), lambda b: (layer, b, 0, 0, 0)))
    return pl.pallas_call(
        _att_sample_kernel,
        grid=(bd,),
        in_specs=in_specs,
        out_specs=pl.BlockSpec((None, ATT_HEADS, HEAD_DIM, 1), lambda b: (b, 0, 0, 0)),
        out_shape=jax.ShapeDtypeStruct((bd, ATT_HEADS, HEAD_DIM, 1), F32),
        compiler_params=_cparams(("parallel",)),
        name="att_sample",
    )(slopes.reshape(3, ATT_HEADS, 1, 1), qkv, *caches_t)


def _ssd_prompt_kernel(x_ref, b_ref, c_ref, z_ref, l_ref, cwx_ref, cwb_ref, cwc_ref, cbx_ref, cbb_ref, cbc_ref,
                       dtb_ref, aneg_ref, dskip_ref, ng_ref, o_ref, hout_ref, xpad, bpad, cpad, h_scr, y_scr, *, nc):
    c = pl.program_id(1)
    L = SSM_CHUNK

    @pl.when(c == 0)
    def _():
        xpad[0:SUBLANES, :] = jnp.zeros((SUBLANES, xpad.shape[1]), F32)
        bpad[0:SUBLANES, :] = jnp.zeros((SUBLANES, bpad.shape[1]), F32)
        cpad[0:SUBLANES, :] = jnp.zeros((SUBLANES, cpad.shape[1]), F32)
        h_scr[...] = jnp.zeros_like(h_scr)

    def conv_silu(pad, src_ref, w_ref, bias_ref):
        pad[SUBLANES:SUBLANES + L, :] = src_ref[...]
        acc = bias_ref[...] + w_ref[0:1, :] * pad[pl.ds(SUBLANES - 3, L), :]
        for jj in range(1, SSM_CONV):
            acc = acc + w_ref[jj:jj + 1, :] * pad[pl.ds(SUBLANES - 3 + jj, L), :]
        pad[0:SUBLANES, :] = pad[L:L + SUBLANES, :]
        return _silu(acc)

    xs = conv_silu(xpad, x_ref, cwx_ref, cbx_ref)
    bm = conv_silu(bpad, b_ref, cwb_ref, cbb_ref).astype(BF16)
    cm = conv_silu(cpad, c_ref, cwc_ref, cbc_ref).astype(BF16)

    lane = lax.broadcasted_iota(jnp.int32, (1, LANES), 1)
    head0 = lane < HEAD_DIM
    sub_head0 = lax.broadcasted_iota(jnp.int32, (LANES, 1), 0) < HEAD_DIM
    dt_lanes = (lane >= DT_LANE0) & (lane < DT_LANE0 + SSM_HEADS)
    dtv = jnp.where(dt_lanes, _softplus(l_ref[:, 0:LANES] + dtb_ref[...]), 0.0)
    da = dtv * aneg_ref[...]
    ri = lax.broadcasted_iota(jnp.int32, (L, L), 0)
    ci = lax.broadcasted_iota(jnp.int32, (L, L), 1)
    causal = ri >= ci
    a_cum = _dot_exact_lhs(causal.astype(BF16), da)
    a_cum_t = a_cum.T
    dt_t = dtv.T

    for g in range(SSM_GROUPS):
        bg = bm[:, g * SSM_STATE:(g + 1) * SSM_STATE]
        cg = cm[:, g * SSM_STATE:(g + 1) * SSM_STATE]
        cb = _dot_nt(cg, bg)
        for pp in range(3):
            pair = g * 3 + pp
            xp = xs[:, pair * LANES:(pair + 1) * LANES]
            xpb = xp.astype(BF16)
            yd, sc, ea, cd = [], [], [], []
            for e in range(2):
                hl = DT_LANE0 + 2 * pair + e
                ac_col = a_cum[:, hl:hl + 1]
                ac_row = a_cum_t[hl:hl + 1, :]
                a_last = ac_col[L - 1:L, :]
                dec = jnp.exp(jnp.where(causal, ac_col - ac_row, -jnp.inf))
                wm = (cb * dec * dt_t[hl:hl + 1, :]).astype(BF16)
                yd.append(jnp.dot(wm, xpb, preferred_element_type=F32))
                sc.append(jnp.exp(a_last - ac_col) * dtv[:, hl:hl + 1])
                ea.append(jnp.exp(ac_col))
                cd.append(jnp.exp(a_last))
            hp = h_scr[pair]
            y_off = _dot_nt(cg, hp.astype(BF16)) * jnp.where(head0, ea[0], ea[1])
            xw = (xp * jnp.where(head0, sc[0], sc[1])).astype(BF16)
            h_scr[pair] = jnp.where(sub_head0, cd[0], cd[1]) * hp + _dot_tn(xw, bg)
            y_scr[:, pair * LANES:(pair + 1) * LANES] = jnp.where(head0, yd[0], yd[1]) + y_off

    y = (y_scr[...] + dskip_ref[...] * xs) * _silu(z_ref[...])
    gw = BRANCH_W // SSM_GROUPS
    for g in range(SSM_GROUPS):
        yg = y[:, g * gw:(g + 1) * gw]
        yg = yg * lax.rsqrt(jnp.mean(yg * yg, axis=-1, keepdims=True) + NORM_EPS)
        o_ref[:, g * gw:(g + 1) * gw] = (yg * ng_ref[:, g * gw:(g + 1) * gw]).astype(o_ref.dtype)

    @pl.when(c == nc - 1)
    def _():
        hout_ref[...] = h_scr[...]


def ssd_prompt(proj, tail, lw, bsz, t):
    nc = t // SSM_CHUNK
    L = SSM_CHUNK
    rowblk = lambda width, c0: pl.BlockSpec((L, width), lambda b, c: (b * nc + c, c0 // width))
    par = lambda a: pl.BlockSpec(a.shape, lambda b, c: (0,) * a.ndim)
    params = [lw["cw_x"], lw["cw_b"], lw["cw_c"], lw["cb_x"], lw["cb_b"], lw["cb_c"], lw["dt_bias"], lw["a_neg"],
              lw["d_skip"], lw["ssm_norm_g"]]
    return pl.pallas_call(
        functools.partial(_ssd_prompt_kernel, nc=nc),
        grid=(bsz, nc),
        in_specs=[rowblk(BRANCH_W, C_X), rowblk(512, T_B), rowblk(512, T_C), rowblk(BRANCH_W, C_Z), rowblk(512, T_L)]
        + [par(a) for a in params],
        out_specs=[pl.BlockSpec((L, BRANCH_W), lambda b, c: (b * nc + c, 0)),
                   pl.BlockSpec((None, N_PAIRS, LANES, SSM_STATE), lambda b, c: (b, 0, 0, 0))],
        out_shape=[jax.ShapeDtypeStruct((bsz * t, BRANCH_W), BF16),
                   jax.ShapeDtypeStruct((bsz, N_PAIRS, LANES, SSM_STATE), F32)],
        scratch_shapes=[pltpu.VMEM((L + SUBLANES, BRANCH_W), F32), pltpu.VMEM((L + SUBLANES, 512), F32),
                        pltpu.VMEM((L + SUBLANES, 512), F32), pltpu.VMEM((N_PAIRS, LANES, SSM_STATE), F32),
                        pltpu.VMEM((L, BRANCH_W), F32)],
        compiler_params=_cparams(("parallel", "arbitrary")),
        name="ssd_prompt",
    )(proj, tail, tail, proj, tail, *params)


def _ssd_sample_conv_kernel(new_ref, st_ref, w_ref, b_ref, o_ref):
    acc = b_ref[...] + w_ref[SSM_CONV - 1:SSM_CONV, :] * new_ref[...]
    for jj in range(SSM_CONV - 1):
        acc = acc + w_ref[jj:jj + 1, :] * st_ref[jj]
    o_ref[...] = _silu(acc)


def ssd_sample_conv(xbc_new, conv_st_t, w, b):
    bd, cdim = xbc_new.shape
    return pl.pallas_call(
        _ssd_sample_conv_kernel,
        out_shape=jax.ShapeDtypeStruct((bd, cdim), F32),
        name="ssd_sample_conv",
    )(xbc_new, conv_st_t, w, b.reshape(1, cdim))


def _ssd_sample_step_kernel(h_ref, x_ref, z_ref, b_ref, c_ref, dt_ref, dtb_ref, alog_ref, d_ref, ng_ref, o_ref, hout_ref):
    h = h_ref[...]
    dt = _softplus(dt_ref[...] + dtb_ref[...])
    da = jnp.exp(dt * (-jnp.exp(alog_ref[...])))
    rep = SSM_HEADS // SSM_GROUPS
    bh = jnp.broadcast_to(b_ref[...][:, None], (SSM_GROUPS, rep, 1, SSM_STATE)).reshape(SSM_HEADS, 1, SSM_STATE)
    ch = jnp.broadcast_to(c_ref[...][:, None], (SSM_GROUPS, rep, 1, SSM_STATE)).reshape(SSM_HEADS, 1, SSM_STATE)
    x = x_ref[...]
    hn = da * h + (dt * x) * bh
    hout_ref[...] = hn
    y = jnp.sum(hn * ch, axis=-1, keepdims=True)
    y = (y + d_ref[...] * x) * _silu(z_ref[...])
    y4 = y.reshape(SSM_GROUPS, rep, HEAD_DIM, 1)
    ms = jnp.sum(jnp.sum(y4 * y4, axis=2, keepdims=True), axis=1, keepdims=True) / (rep * HEAD_DIM)
    y4 = y4 * lax.rsqrt(ms + NORM_EPS)
    o_ref[...] = y4.reshape(SSM_HEADS, HEAD_DIM, 1) * ng_ref[...]


def ssd_sample_step(h, x_col, z_col, bmat, cmat, dt_raw, lw, layer):
    bd = x_col.shape[0]
    col = pl.BlockSpec((None, SSM_HEADS, HEAD_DIM, 1), lambda b: (b, 0, 0, 0))
    grp = pl.BlockSpec((None, SSM_GROUPS, 1, SSM_STATE), lambda b: (b, 0, 0, 0))
    hd1 = pl.BlockSpec((SSM_HEADS, 1, 1), lambda b: (0, 0, 0))
    return pl.pallas_call(
        _ssd_sample_step_kernel,
        grid=(bd,),
        in_specs=[pl.BlockSpec((None, None, SSM_HEADS, HEAD_DIM, SSM_STATE), lambda b: (layer, b, 0, 0, 0)),
                  col, col, grp, grp, pl.BlockSpec((None, SSM_HEADS, 1, 1), lambda b: (b, 0, 0, 0)), hd1, hd1, hd1,
                  pl.BlockSpec((SSM_HEADS, HEAD_DIM, 1), lambda b: (0, 0, 0))],
        out_specs=[col, pl.BlockSpec((None, SSM_HEADS, HEAD_DIM, SSM_STATE), lambda b: (b, 0, 0, 0))],
        out_shape=[jax.ShapeDtypeStruct((bd, SSM_HEADS, HEAD_DIM, 1), F32),
                   jax.ShapeDtypeStruct((bd, SSM_HEADS, HEAD_DIM, SSM_STATE), F32)],
        compiler_params=_cparams(("parallel",)),
        name="ssd_sample_step",
    )(h, x_col, z_col, bmat, cmat, dt_raw, lw["dt_bias_h"], lw["a_log_h"], lw["d_h"], lw["ssm_norm_g_col"])


def _rwkv_mix(i, tm, x_refs, h_refs, i_refs, mu_refs, w0_ref, wup_ref, a0_ref, aup_ref, gup_ref):
    def shift(x_ref, h_ref, i_ref, mu_ref):
        x = x_ref[...]
        hrows = h_ref.shape[0]
        prev0 = jnp.where(i == 0, i_ref[...], h_ref[hrows - 1:hrows, :])
        if tm == 1:
            xp = prev0
        else:
            row = lax.broadcasted_iota(jnp.int32, (tm, 1), 0)
            xp = jnp.where(row == 0, prev0, pltpu.roll(x, 1, 0))
        return x + mu_ref[...] * (xp - x)

    ur, uk, uv, ul = (shift(*refs) for refs in zip(x_refs, h_refs, i_refs, mu_refs))
    f = lambda a, w_ref: jnp.dot(a.astype(BF16), w_ref[...], preferred_element_type=F32)
    w_log = -_softplus(-(w0_ref[...] + f(jnp.tanh(ul[:, 0:LANES]), wup_ref))) - 0.5
    a = _sigmoid(a0_ref[...] + f(ul[:, LANES:2 * LANES], aup_ref))
    g = f(_sigmoid(ul[:, 2 * LANES:4 * LANES]), gup_ref)
    return ur, uk, uv, -jnp.exp(w_log), a, g


def _rwkv_prep_kernel(*refs, tm):
    x_refs, h_refs, i_refs, mu_refs = refs[0:4], refs[4:8], refs[8:12], refs[12:16]
    w0_ref, wup_ref, a0_ref, aup_ref, gup_ref, kk_ref, ka_ref = refs[16:23]
    or_ref, old_ref, ok_ref, ov_ref, okk_ref, oa_ref, og_ref = refs[23:30]
    ur, uk, uv, ld, a, g = _rwkv_mix(pl.program_id(1), tm, x_refs, h_refs, i_refs, mu_refs,
                                     w0_ref, wup_ref, a0_ref, aup_ref, gup_ref)
    or_ref[...] = ur
    old_ref[...] = ld
    ok_ref[...] = uk * (1.0 + (a - 1.0) * ka_ref[...])
    ov_ref[...] = uv
    okk_ref[...] = uk * kk_ref[...]
    oa_ref[...] = a
    og_ref[...] = g


def _rwkv_prep_prompt_kernel(*refs, tm):
    x_refs, h_refs, i_refs, mu_refs = refs[0:4], refs[4:8], refs[8:12], refs[12:16]
    w0_ref, wup_ref, a0_ref, aup_ref, gup_ref, kk_ref, ka_ref, rk_ref = refs[16:24]
    opt_ref, ort_ref, oqh_ref, okh_ref, ov_ref, obv_ref, og_ref, oel_ref = refs[24:32]
    L = RWKV_CHUNK
    ur, uk, uv, ld, a, g = _rwkv_mix(pl.program_id(1), tm, x_refs, h_refs, i_refs, mu_refs,
                                     w0_ref, wup_ref, a0_ref, aup_ref, gup_ref)
    kmod = uk * (1.0 + (a - 1.0) * ka_ref[...])
    kkraw = uk * kk_ref[...]
    assert tm == 2 * L
    ri = lax.broadcasted_iota(jnp.int32, (tm, tm), 0)
    ci = lax.broadcasted_iota(jnp.int32, (tm, tm), 1)
    tril = ((ri >= ci) & ((ri < L) == (ci < L))).astype(BF16)
    cum = _dot_exact_lhs(tril, ld, 2)
    e_in = jnp.exp(cum)
    e_inv = jnp.exp(-cum)
    e_ex = jnp.exp(cum - ld)
    ort_ref[...] = (ur * e_in).astype(ort_ref.dtype)
    okh_ref[...] = (kmod * e_inv).astype(okh_ref.dtype)
    ov_ref[...] = uv.astype(ov_ref.dtype)
    og_ref[...] = g.astype(og_ref.dtype)
    for c in range(tm // L):
        oel_ref[c] = e_in[c * L + L - 1:c * L + L, :]
    li = lax.broadcasted_iota(jnp.int32, (LANES, LANES), 0)
    lj = lax.broadcasted_iota(jnp.int32, (LANES, LANES), 1)
    bd_ones = ((li < HEAD_DIM) == (lj < HEAD_DIM)).astype(BF16)
    rkr = ur * kmod * rk_ref[...]
    for p in range(N_PAIRS):
        cs = slice(p * LANES, (p + 1) * LANES)
        kkp = kkraw[:, cs]
        kkn = kkp / jnp.maximum(jnp.sqrt(_dot_exact_rhs(kkp * kkp, bd_ones, 1)), 1e-12)
        opt_ref[:, cs] = (-kkn * e_ex[:, cs]).astype(opt_ref.dtype)
        oqh_ref[:, cs] = (kkn * a[:, cs] * e_inv[:, cs]).astype(oqh_ref.dtype)
        obv_ref[:, cs] = (_dot_exact_rhs(rkr[:, cs], bd_ones, 1) * uv[:, cs]).astype(obv_ref.dtype)


def _rwkv_prep_call(kernel_fn, proj, tail, init, params, bsz, t, tm, out_specs, out_shape, name):
    nt = t // tm
    hrows = SUBLANES if t >= SUBLANES else t
    hper = tm // hrows
    proj = proj.reshape(bsz, t, N_MAIN)
    tail = tail.reshape(bsz, t, N_TAIL)
    blk = lambda width, c0: pl.BlockSpec((None, tm, width), lambda b, i: (b, i, c0 // width))
    halo = lambda width, c0: pl.BlockSpec(
        (None, hrows, width), lambda b, i: (b, jnp.maximum(i * hper - 1, 0), c0 // width))
    ini = lambda width: pl.BlockSpec((None, 1, width), lambda b, i: (b, 0, 0))
    par = lambda a: pl.BlockSpec(a.shape, lambda b, i: (0,) * a.ndim)
    secs = ((BRANCH_W, C_R), (BRANCH_W, C_RK), (BRANCH_W, C_RV), (512, T_L))
    srcs = (proj, proj, proj, tail)
    return pl.pallas_call(
        functools.partial(kernel_fn, tm=tm),
        grid=(bsz, nt),
        in_specs=[blk(*s) for s in secs] + [halo(*s) for s in secs] + [ini(s[0]) for s in secs]
        + [par(a) for a in params],
        out_specs=out_specs,
        out_shape=out_shape,
        compiler_params=_cparams(("parallel", "arbitrary")),
        name=name,
    )(*srcs, *srcs, *init, *params)


def _rwkv_prep_params(lw):
    return [lw["mu_r"], lw["mu_k"], lw["mu_v"], lw["mu_l"], lw["w0"], lw["w_up"], lw["a0"], lw["a_up"], lw["g_up"],
            lw["k_k"], lw["k_a"]]


def rwkv_prep(proj, tail, init, lw, bsz, t):
    tm = min(t, 128)
    out = pl.BlockSpec((None, tm, BRANCH_W), lambda b, i: (b, i, 0))
    shp = jax.ShapeDtypeStruct((bsz, t, BRANCH_W), F32)
    outs = _rwkv_prep_call(_rwkv_prep_kernel, proj, tail, init, _rwkv_prep_params(lw), bsz, t, tm, [out] * 7,
                           [shp] * 7, "rwkv_prep")
    return [o.reshape(bsz * t, BRANCH_W) for o in outs]


def rwkv_prep_prompt(proj, tail, init, lw, bsz, t):
    tm = 128
    nch = tm // RWKV_CHUNK
    out = pl.BlockSpec((None, tm, BRANCH_W), lambda b, i: (b, i, 0))
    shp = jax.ShapeDtypeStruct((bsz, t, BRANCH_W), BF16)
    el_spec = pl.BlockSpec((None, nch, 1, BRANCH_W), lambda b, i: (b, i, 0, 0))
    el_shape = jax.ShapeDtypeStruct((bsz, t // RWKV_CHUNK, 1, BRANCH_W), F32)
    outs = _rwkv_prep_call(_rwkv_prep_prompt_kernel, proj, tail, init, _rwkv_prep_params(lw) + [lw["r_k_row"]], bsz, t,
                           tm, [out] * 7 + [el_spec], [shp] * 7 + [el_shape], "rwkv_prep_prompt")
    return [o.reshape(bsz * t, BRANCH_W) for o in outs[:7]] + [outs[7].reshape(bsz * t // RWKV_CHUNK, 1, BRANCH_W)]


def _rwkv_chain_kernel(pt_ref, rt_ref, qh_ref, kh_ref, v_ref, bv_ref, g_ref, el_ref, lng_ref, lnb_ref, o_ref, sout_ref,
                       s_scr, *, nchunk, nt, npair):
    tstep = pl.program_id(2)
    L = RWKV_CHUNK
    L2 = 2 * L

    @pl.when(tstep == 0)
    def _():
        s_scr[...] = jnp.zeros_like(s_scr)

    lane = lax.broadcasted_iota(jnp.int32, (1, LANES), 1)
    head0 = lane < HEAD_DIM
    ri = lax.broadcasted_iota(jnp.int32, (L2, L2), 0)
    ci = lax.broadcasted_iota(jnp.int32, (L2, L2), 1)
    same = (ri < L) == (ci < L)
    strict = same & (ri > ci)
    incl = same & (ri >= ci)
    eye = (ri == ci).astype(F32)
    bd_ones = same.astype(BF16)
    bf = lambda x: x.astype(BF16)
    dot = lambda a, b: jnp.dot(bf(a), bf(b), preferred_element_type=F32)
    zero = jnp.zeros((), BF16)

    def stack(x):
        return jnp.concatenate([jnp.where(head0, x, zero), jnp.where(head0, zero, x)], axis=0)

    probs = [(c, p) for c in range(nchunk) for p in range(npair)]
    rows = lambda c: pl.ds(c * L, L)
    cols = lambda p: slice(p * LANES, (p + 1) * LANES)
    pm = {k: stack(pt_ref[rows(k[0]), cols(k[1])]) for k in probs}
    rm = {k: stack(rt_ref[rows(k[0]), cols(k[1])]) for k in probs}
    qk = {k: jnp.concatenate([stack(qh_ref[rows(k[0]), cols(k[1])]), stack(kh_ref[rows(k[0]), cols(k[1])])], axis=0)
          for k in probs}
    vm = {k: stack(v_ref[rows(k[0]), cols(k[1])]) for k in probs}
    x = {k: _dot_nt(jnp.concatenate([pm[k], rm[k]], axis=0), qk[k]) for k in probs}
    sums = lambda v: jnp.dot(bf(v), bd_ones, preferred_element_type=F32)
    n_pow = {k: jnp.where(strict, x[k][0:L2, 0:L2], 0.0) for k in probs}
    a_kp = {k: bf(jnp.where(strict, x[k][0:L2, L2:2 * L2], 0.0)) for k in probs}
    r_q = {k: bf(jnp.where(incl, x[k][L2:2 * L2, 0:L2], 0.0)) for k in probs}
    r_k = {k: bf(jnp.where(incl, x[k][L2:2 * L2, L2:2 * L2], 0.0)) for k in probs}
    t_inv = {k: eye + n_pow[k] for k in probs}
    for _ in range(5):
        n_pow = {k: dot(n_pow[k], n_pow[k]) for k in probs}
        t_inv = {k: t_inv[k] + dot(n_pow[k], t_inv[k]) for k in probs}
    t_inv = {k: bf(t_inv[k]) for k in probs}
    pr = {k: jnp.concatenate([pm[k], rm[k]], axis=0) for k in probs}
    akv = {k: dot(jnp.concatenate([a_kp[k], r_k[k]], axis=0), vm[k]) for k in probs}

    state = [s_scr[p] for p in range(npair)]
    ys = {}
    for c in range(nchunk):
        ks = [(c, p) for p in range(npair)]
        ps = {k: dot(pr[k], bf(state[k[1]])) + akv[k] for k in ks}
        rhs = {k: ps[k][0:L2, :] for k in ks}
        rs = {k: ps[k][L2:2 * L2, :] for k in ks}
        u = {k: bf(dot(t_inv[k], rhs[k])) for k in ks}
        upd = {k: _dot_tn(qk[k], jnp.concatenate([u[k], vm[k]], axis=0)) for k in ks}
        for k in ks:
            ys[k] = rs[k] + dot(r_q[k], u[k])
            w_col = jnp.sum(eye * el_ref[c, :, cols(k[1])], axis=1, keepdims=True)
            state[k[1]] = w_col * (state[k[1]] + upd[k])
    for p in range(npair):
        s_scr[p] = state[p]

    y = {k: ys[k][0:L, :] + ys[k][L:L2, :] for k in probs}
    yc = {k: y[k] - sums(y[k]) * (1.0 / HEAD_DIM) for k in probs}
    var = {k: sums(yc[k] * yc[k]) * (1.0 / HEAD_DIM) for k in probs}
    for k in probs:
        cs = cols(k[1])
        yn = yc[k] * lax.rsqrt(var[k] + RWKV_LN_EPS) * lng_ref[:, cs] + lnb_ref[:, cs]
        o_ref[rows(k[0]), cs] = ((yn + bv_ref[rows(k[0]), cs].astype(F32)) * g_ref[rows(k[0]), cs].astype(F32)
                                 ).astype(o_ref.dtype)

    @pl.when(tstep == nt - 1)
    def _():
        sout_ref[...] = s_scr[...]


def rwkv_prompt(parts, lw, bsz, t, npair=6, nchunk=4):
    tb = nchunk * RWKV_CHUNK
    nt = t // tb
    wl = npair * LANES
    blk = pl.BlockSpec((tb, wl), lambda b, p, i: (b * nt + i, p))
    par = pl.BlockSpec((1, wl), lambda b, p, i: (0, p))
    el = pl.BlockSpec((nchunk, 1, wl), lambda b, p, i: (b * nt + i, 0, p))
    return pl.pallas_call(
        functools.partial(_rwkv_chain_kernel, nchunk=nchunk, nt=nt, npair=npair),
        grid=(bsz, N_PAIRS // npair, nt),
        in_specs=[blk] * 7 + [el, par, par],
        out_specs=[blk, pl.BlockSpec((None, npair, LANES, LANES), lambda b, p, i: (b, p, 0, 0))],
        out_shape=[jax.ShapeDtypeStruct((bsz * t, BRANCH_W), BF16),
                   jax.ShapeDtypeStruct((bsz, N_PAIRS, LANES, LANES), F32)],
        scratch_shapes=[pltpu.VMEM((npair, LANES, LANES), F32)],
        compiler_params=_cparams(("parallel", "parallel", "arbitrary")),
        name="rwkv_prompt",
    )(*parts, lw["ln_g_row"], lw["ln_b_row"])


def _rwkv_sample_step_kernel(s_ref, r_ref, ld_ref, k_ref, kk_ref, a_ref, v_ref, g_ref, rk_ref, lng_ref, lnb_ref,
                             o_ref, sout_ref):
    s = s_ref[...]
    r, kmod, kkraw, a = r_ref[...], k_ref[...], kk_ref[...], a_ref[...]
    v = v_ref[...]
    kkn = kkraw / jnp.maximum(jnp.sqrt(jnp.sum(kkraw * kkraw, axis=-1, keepdims=True)), 1e-12)
    sa = jnp.sum(s * (-kkn), axis=-1, keepdims=True)
    sn = s * jnp.exp(ld_ref[...]) + sa * (kkn * a) + v * kmod
    sout_ref[...] = sn
    y = jnp.sum(sn * r, axis=-1, keepdims=True)
    mean = jnp.mean(y, axis=1, keepdims=True)
    yc = y - mean
    var = jnp.mean(yc * yc, axis=1, keepdims=True)
    yn = yc * lax.rsqrt(var + RWKV_LN_EPS) * lng_ref[...] + lnb_ref[...]
    bonus = jnp.sum(r * kmod * rk_ref[...], axis=-1, keepdims=True) * v
    o_ref[...] = (yn + bonus) * g_ref[...]


def rwkv_sample_step(state, rows, cols, lw, layer):
    bd = rows[0].shape[0]
    row = pl.BlockSpec((None, RWKV_HEADS, 1, HEAD_DIM), lambda b: (b, 0, 0, 0))
    col = pl.BlockSpec((None, RWKV_HEADS, HEAD_DIM, 1), lambda b: (b, 0, 0, 0))
    prow = pl.BlockSpec((RWKV_HEADS, 1, HEAD_DIM), lambda b: (0, 0, 0))
    pcol = pl.BlockSpec((RWKV_HEADS, HEAD_DIM, 1), lambda b: (0, 0, 0))
    return pl.pallas_call(
        _rwkv_sample_step_kernel,
        grid=(bd,),
        in_specs=[pl.BlockSpec((None, None, RWKV_HEADS, HEAD_DIM, HEAD_DIM), lambda b: (layer, b, 0, 0, 0))]
        + [row] * 5 + [col] * 2 + [prow, pcol, pcol],
        out_specs=[col, pl.BlockSpec((None, RWKV_HEADS, HEAD_DIM, HEAD_DIM), lambda b: (b, 0, 0, 0))],
        out_shape=[jax.ShapeDtypeStruct((bd, RWKV_HEADS, HEAD_DIM, 1), F32),
                   jax.ShapeDtypeStruct((bd, RWKV_HEADS, HEAD_DIM, HEAD_DIM), F32)],
        compiler_params=_cparams(("parallel",)),
        name="rwkv_sample_step",
    )(state, *rows, *cols, lw["r_k_h"], lw["ln_g_col"], lw["ln_b_col"])


def _cache_update_kernel(k_ref, v_ref, newk_ref, newv_ref, ok_ref, ov_ref):
    win = k_ref.shape[-1]
    is_last = lax.broadcasted_iota(jnp.int32, (1, win), 1) == win - 1
    for c_ref, new_ref, o_ref in ((k_ref, newk_ref, ok_ref), (v_ref, newv_ref, ov_ref)):
        for h in range(ATT_HEADS):
            o_ref[h] = jnp.where(is_last, new_ref[h], pltpu.roll(c_ref[h], win - 1, 1))


def cache_update(cache_k, cache_v, new_k, new_v):
    nl, bd = cache_k.shape[:2]
    win = cache_k.shape[-1]
    blk = pl.BlockSpec((None, None, ATT_HEADS, HEAD_DIM, win), lambda l, b: (l, b, 0, 0, 0))
    new = pl.BlockSpec((None, None, ATT_HEADS, HEAD_DIM, 1), lambda l, b: (l, b, 0, 0, 0))
    shp = jax.ShapeDtypeStruct(cache_k.shape, cache_k.dtype)
    return pl.pallas_call(
        _cache_update_kernel,
        grid=(nl, bd),
        in_specs=[blk, blk, new, new],
        out_specs=[blk, blk],
        out_shape=[shp, shp],
        compiler_params=_cparams(("parallel", "parallel")),
        name="cache_update",
    )(cache_k, cache_v, new_k, new_v)


def _pack_layer(l, p):
    row = lambda a: a.reshape(1, -1)
    pad_rows = lambda a, n: jnp.concatenate([a, jnp.zeros((n - a.shape[0], a.shape[1]), a.dtype)], axis=0).astype(BF16)
    lane_pad = lambda a: jnp.zeros((1, LANES), F32).at[0, DT_LANE0:DT_LANE0 + SSM_HEADS].set(a)
    mu = p["rwkv_mu"][l]
    z1 = lambda n: jnp.zeros((n,), F32)
    cw = p["ssm_conv_w"][l]
    cb = p["ssm_conv_b"][l]
    rep64 = lambda a: jnp.repeat(a, HEAD_DIM)
    return dict(
        ln1_g=p["ln1_g"][l], ln2_g=p["ln2_g"][l],
        cw_x=cw[:, :BRANCH_W], cw_b=cw[:, BRANCH_W:BRANCH_W + 512], cw_c=cw[:, BRANCH_W + 512:],
        cb_x=row(cb[:BRANCH_W]), cb_b=row(cb[BRANCH_W:BRANCH_W + 512]), cb_c=row(cb[BRANCH_W + 512:]),
        conv_w=cw, conv_b=cb,
        dt_bias=lane_pad(p["ssm_dt_bias"][l]), a_neg=lane_pad(-jnp.exp(p["ssm_a_log"][l])),
        d_skip=row(rep64(p["ssm_d"][l])), ssm_norm_g=row(p["ssm_norm_g"][l]),
        dt_bias_h=p["ssm_dt_bias"][l].reshape(SSM_HEADS, 1, 1), a_log_h=p["ssm_a_log"][l].reshape(SSM_HEADS, 1, 1),
        d_h=p["ssm_d"][l].reshape(SSM_HEADS, 1, 1), ssm_norm_g_col=p["ssm_norm_g"][l].reshape(SSM_HEADS, HEAD_DIM, 1),
        mu_r=row(mu[:BRANCH_W]), mu_k=row(mu[BRANCH_W:2 * BRANCH_W]), mu_v=row(mu[2 * BRANCH_W:3 * BRANCH_W]),
        mu_l=row(jnp.concatenate([mu[3 * BRANCH_W:3 * BRANCH_W + LORA_W], z1(32),
                                  mu[3 * BRANCH_W + LORA_W:3 * BRANCH_W + LORA_W + LORA_A], z1(32),
                                  mu[3 * BRANCH_W + LORA_W + LORA_A:]])),
        w0=row(p["rwkv_w0"][l]), w_up=pad_rows(p["rwkv_w_up"][l], LANES), a0=row(p["rwkv_a0"][l]),
        a_up=pad_rows(p["rwkv_a_up"][l], LANES), g_up=p["rwkv_g_up"][l].astype(BF16),
        k_k=row(p["rwkv_k_k"][l]), k_a=row(p["rwkv_k_a"][l]),
        r_k_row=row(p["rwkv_r_k"][l]), ln_g_row=row(p["rwkv_ln_g"][l]), ln_b_row=row(p["rwkv_ln_b"][l]),
        r_k_h=p["rwkv_r_k"][l].reshape(RWKV_HEADS, 1, HEAD_DIM),
        ln_g_col=p["rwkv_ln_g"][l].reshape(RWKV_HEADS, HEAD_DIM, 1),
        ln_b_col=p["rwkv_ln_b"][l].reshape(RWKV_HEADS, HEAD_DIM, 1),
    )


def _pack_big(p):
    w_t = jnp.swapaxes(p["w_in"], 1, 2)
    zrows = lambda n: jnp.zeros((w_t.shape[0], n, D_MODEL), F32)
    w_tail_t = jnp.concatenate([
        w_t[:, _O_XBC + BRANCH_W:_O_DT], w_t[:, _O_LORA:_O_LORA + LORA_W], w_t[:, _O_DT:_O_RW], zrows(8),
        w_t[:, _O_LORA + LORA_W:_O_LORA + LORA_W + LORA_A], zrows(32), w_t[:, _O_LORA + LORA_W + LORA_A:_O_GATE]], axis=1)
    return dict(w_in_t=w_t, w_tail_t=w_tail_t, w_att=p["w_branch_att"].astype(BF16),
                w_ssm=p["w_branch_ssm"].astype(BF16), w_rwkv=p["w_branch_rwkv"].astype(BF16),
                w_out=p["w_out"].astype(BF16), w_up_ff=p["w_up"], w_down_ff=p["w_down"].astype(BF16))


def _unpack_lora(rows):
    return jnp.concatenate([rows[..., 0:LORA_W], rows[..., LANES:LANES + LORA_A], rows[..., 2 * LANES:]], axis=-1)


def _pack_lora(rows):
    z = lambda n: jnp.zeros(rows.shape[:-1] + (n,), rows.dtype)
    return jnp.concatenate([rows[..., 0:LORA_W], z(32), rows[..., LORA_W:LORA_W + LORA_A], z(32),
                            rows[..., LORA_W + LORA_A:]], axis=-1)


def _alibi_slopes():
    n = 3 * ATT_HEADS
    idx = jnp.arange(1, n + 1, dtype=F32)
    return jnp.exp2(-8.0 * idx / n).reshape(3, ATT_HEADS)


def _layer(xp, xs, hp, hs, lw, big, layer, slopes, st, bsz, t, bd, next_g, next_dtype):
    proj, tail, proj_s, tail_s = project_main_and_tail(hp, hs, big, layer)
    outs, state_p = _prompt_mixers(proj, tail, lw, slopes, bsz, t)
    outs_s, new_rows, state_s = _sample_mixers(proj_s, tail_s, lw, slopes, st, layer, bd)
    merged, merged_s = merge_branches(outs, proj, outs_s, proj_s, big["w_att"], big["w_ssm"], big["w_rwkv"], layer)
    xp, h2, xs, h2_s = matmul_res_norm(merged, merged_s, big["w_out"], layer, xp, xs, lw["ln2_g"], BF16)
    up, up_s = matmul(h2, h2_s, big["w_up_ff"], layer, epi="relu2", out_dtype=BF16)
    xp, xs = matmul(up, up_s, big["w_down_ff"], layer, epi="res", res=xp, res_s=xs)
    hp, hs = rmsnorm(xp, next_g, next_dtype), rmsnorm(xs, next_g, next_dtype)
    return xp, xs, hp, hs, state_p, new_rows, state_s


def _prompt_mixers(proj, tail, lw, slopes, bsz, t):
    o_att = att_prompt(proj, slopes, bsz, t)
    o_ssm, h_fin = ssd_prompt(proj, tail, lw, bsz, t)
    zero_init = [jnp.zeros((bsz, 1, w), F32) for w in (BRANCH_W, BRANCH_W, BRANCH_W, 512)]
    rw = rwkv_prep_prompt(proj, tail, zero_init, lw, bsz, t)
    o_rwkv, s_fin = rwkv_prompt(rw, lw, bsz, t)

    p3 = proj.reshape(bsz, t, N_MAIN)
    t3 = tail.reshape(bsz, t, N_TAIL)
    kv = []
    for g in range(3):
        win = ATT_WINDOWS[g]
        for c0 in (C_K, C_V):
            kv.append(p3[:, t - win:, c0 + g * ATT_OUT:c0 + (g + 1) * ATT_OUT].reshape(bsz, win, ATT_HEADS, HEAD_DIM))
    nc = SSM_CONV - 1
    conv_new = jnp.concatenate([p3[:, t - nc:, C_X:C_X + BRANCH_W], t3[:, t - nc:, T_B:T_B + 1024]], -1)
    shift_new = jnp.concatenate([p3[:, t - 1, C_R:C_R + 3 * BRANCH_W], _unpack_lora(t3[:, t - 1, T_L:T_L + 512])], axis=-1)
    ssm_new = h_fin.reshape(bsz, SSM_HEADS, HEAD_DIM, SSM_STATE)
    s6 = s_fin.reshape(bsz, N_PAIRS, 2, HEAD_DIM, 2, HEAD_DIM)
    s_heads = jnp.stack([s6[:, :, 0, :, 0, :], s6[:, :, 1, :, 1, :]], axis=2)
    rwkv_new = jnp.swapaxes(s_heads, -1, -2).reshape(bsz, RWKV_HEADS, HEAD_DIM, HEAD_DIM)
    return (o_att, o_ssm, o_rwkv), tuple(kv) + (conv_new, ssm_new, shift_new, rwkv_new)


def _sample_mixers(proj, tail, lw, slopes, st, layer, bd):
    (caches_t, conv_st, ssm_st, shift_st, rwkv_st) = st
    qkv = proj[:, :3 * BRANCH_W].reshape(bd, 3, 3, ATT_HEADS, HEAD_DIM, 1)
    o_att = att_sample(qkv, caches_t, slopes, layer).reshape(bd, ATT_OUT).astype(BF16)
    new_rows = tuple(qkv[:, part, g] for g in range(3) for part in (1, 2))

    xbc_new = jnp.concatenate([proj[:, C_X:C_X + BRANCH_W], tail[:, T_B:T_B + 1024]], axis=-1)
    cst = conv_st[layer]
    xc = ssd_sample_conv(xbc_new, jnp.swapaxes(cst, 0, 1), lw["conv_w"], lw["conv_b"])
    conv_new = jnp.concatenate([cst[:, 1:], xbc_new[:, None]], axis=1)
    x_col = xc[:, :BRANCH_W].reshape(bd, SSM_HEADS, HEAD_DIM, 1)
    z_col = proj[:, C_Z:C_Z + BRANCH_W].reshape(bd, SSM_HEADS, HEAD_DIM, 1)
    bmat = xc[:, BRANCH_W:BRANCH_W + 512].reshape(bd, SSM_GROUPS, 1, SSM_STATE)
    cmat = xc[:, BRANCH_W + 512:].reshape(bd, SSM_GROUPS, 1, SSM_STATE)
    dt_raw = tail[:, T_L + DT_LANE0:T_L + DT_LANE0 + SSM_HEADS].reshape(bd, SSM_HEADS, 1, 1)
    y_col, ssm_new = ssd_sample_step(ssm_st, x_col, z_col, bmat, cmat, dt_raw, lw, layer)
    o_ssm = y_col.reshape(bd, BRANCH_W).astype(BF16)

    sh = shift_st[layer]
    init = [sh[:, None, i * BRANCH_W:(i + 1) * BRANCH_W] for i in range(3)] + [_pack_lora(sh[:, None, 3 * BRANCH_W:])]
    r, ld, kmod, v, kkraw, a, gg = rwkv_prep(proj, tail, init, lw, bd, 1)
    rowf = lambda z: z.reshape(bd, RWKV_HEADS, 1, HEAD_DIM)
    colf = lambda z: z.reshape(bd, RWKV_HEADS, HEAD_DIM, 1)
    o_col, rwkv_new = rwkv_sample_step(rwkv_st, [rowf(r), rowf(ld), rowf(kmod), rowf(kkraw), rowf(a)],
                                       [colf(v), colf(gg)], lw, layer)
    o_rwkv = o_col.reshape(bd, BRANCH_W).astype(BF16)
    shift_new = jnp.concatenate([proj[:, C_R:C_R + 3 * BRANCH_W], _unpack_lora(tail[:, T_L:T_L + 512])], axis=-1)
    return (o_att, o_ssm, o_rwkv), new_rows, (conv_new, ssm_new, shift_new, rwkv_new)


def kernel(x_prompt, x_sample, cache_att_k0, cache_att_v0, cache_att_k1, cache_att_v1, cache_att_k2, cache_att_v2, state_ssm_conv, state_ssm, state_rwkv_shift, state_rwkv, ln1_g, w_in, ssm_conv_w, ssm_conv_b, ssm_dt_bias, ssm_a_log, ssm_d, ssm_norm_g, rwkv_mu, rwkv_w0, rwkv_w_up, rwkv_a0, rwkv_a_up, rwkv_g_up, rwkv_k_k, rwkv_k_a, rwkv_r_k, rwkv_ln_g, rwkv_ln_b, w_branch_att, w_branch_ssm, w_branch_rwkv, w_out, ln2_g, w_up, w_down, final_g):
    p = dict(ln1_g=ln1_g, w_in=w_in, ssm_conv_w=ssm_conv_w, ssm_conv_b=ssm_conv_b, ssm_dt_bias=ssm_dt_bias,
             ssm_a_log=ssm_a_log, ssm_d=ssm_d, ssm_norm_g=ssm_norm_g, rwkv_mu=rwkv_mu, rwkv_w0=rwkv_w0,
             rwkv_w_up=rwkv_w_up, rwkv_a0=rwkv_a0, rwkv_a_up=rwkv_a_up, rwkv_g_up=rwkv_g_up, rwkv_k_k=rwkv_k_k,
             rwkv_k_a=rwkv_k_a, rwkv_r_k=rwkv_r_k, rwkv_ln_g=rwkv_ln_g, rwkv_ln_b=rwkv_ln_b,
             w_branch_att=w_branch_att, w_branch_ssm=w_branch_ssm, w_branch_rwkv=w_branch_rwkv, w_out=w_out,
             ln2_g=ln2_g, w_up=w_up, w_down=w_down)
    depth = w_in.shape[0]
    bsz, t, _ = x_prompt.shape
    bd = x_sample.shape[0]
    slopes = _alibi_slopes()
    caches = (cache_att_k0, cache_att_v0, cache_att_k1, cache_att_v1, cache_att_k2, cache_att_v2)
    caches_t = tuple(jnp.transpose(c, (0, 1, 3, 4, 2)) for c in caches)
    st = (caches_t, state_ssm_conv, state_ssm, state_rwkv_shift, state_rwkv)
    xp = x_prompt.reshape(bsz * t, D_MODEL)
    xs = x_sample.reshape(bd, D_MODEL)
    big = _pack_big(p)
    p_new, s_new, s_rows = [], [], []
    hp = rmsnorm(xp, ln1_g[0], BF16)
    hs = rmsnorm(xs, ln1_g[0], BF16)
    for l in range(depth):
        lw = _pack_layer(l, p)
        last = l == depth - 1
        next_g, next_dtype = (final_g, F32) if last else (ln1_g[l + 1], BF16)
        xp, xs, hp, hs, sp, rows, ss = _layer(xp, xs, hp, hs, lw, big, l, slopes, st, bsz, t, bd, next_g, next_dtype)
        p_new.append(sp)
        s_new.append(ss)
        s_rows.append(rows)
    y_prompt = hp.reshape(bsz, t, D_MODEL)
    y_sample = hs.reshape(bd, 1, D_MODEL)
    stack = lambda per_layer, n: tuple(jnp.stack([s[i] for s in per_layer]) for i in range(n))
    new_rows = stack(s_rows, 6)
    s_caches = ()
    for g in range(3):
        upd = cache_update(caches_t[2 * g], caches_t[2 * g + 1], new_rows[2 * g], new_rows[2 * g + 1])
        s_caches += tuple(jnp.transpose(c, (0, 1, 4, 2, 3)) for c in upd)
    return (y_prompt, y_sample) + stack(p_new, 10) + s_caches + stack(s_new, 4)
```

```python
import functools

import jax
import jax.numpy as jnp
from jax import lax
from jax.experimental import pallas as pl
from jax.experimental.pallas import tpu as pltpu

F32 = jnp.float32
BF16 = jnp.bfloat16

D_MODEL = 2048
HEAD_DIM = 64
BRANCH_W = 1536
NORM_EPS = 1e-5
ATT_WINDOWS = (128, 512, 2048)
ATT_DILS = (1, 4, 16)
ATT_SPAN = 128
ATT_HEADS = 8
ATT_OUT = ATT_HEADS * HEAD_DIM
ATT_SCALE = HEAD_DIM ** -0.5
SSM_HEADS = 24
SSM_GROUPS = 4
SSM_STATE = 128
SSM_CONV = 4
SSM_CHUNK = 128
RWKV_HEADS = 24
LORA_W = 96
LORA_A = 96
LORA_G = 256
RWKV_LN_EPS = 64e-5
D_FF = 4 * D_MODEL
RWKV_CHUNK = 64
N_PAIRS = BRANCH_W // 128

LANES = 128
SUBLANES = 8
VMEM_LIMIT_BYTES = 56 * 1024 * 1024

C_Q, C_K, C_V, C_Z, C_X, C_R, C_RK, C_RV = (i * BRANCH_W for i in range(8))
C_G = 8 * BRANCH_W
N_MAIN = C_G + 3 * D_MODEL
T_B = 0
T_C = 512
T_L = 1024
N_TAIL = T_L + 512
DT_LANE0 = LORA_W

_O_QKV, _O_Z, _O_XBC, _O_DT, _O_RW, _O_GATE = 0, 4608, 6144, 8704, 8728, 13784
_O_LORA = _O_RW + 3 * BRANCH_W
_W_IN_COLS = _O_GATE + 3 * D_MODEL
PROJ_TN = 768


def _cparams(sem):
    return pltpu.CompilerParams(dimension_semantics=sem, vmem_limit_bytes=VMEM_LIMIT_BYTES)


def _softplus(x):
    return jnp.maximum(x, 0.0) + jnp.log(1.0 + jnp.exp(-jnp.abs(x)))


def _sigmoid(x):
    return 1.0 / (1.0 + jnp.exp(-x))


def _silu(x):
    return x * _sigmoid(x)


def _split(x, terms):
    out = []
    for _ in range(terms - 1):
        hi = x.astype(BF16)
        out.append(hi)
        x = x - hi.astype(F32)
    out.append(x.astype(BF16))
    return out


def _dot_exact_rhs(x, mat, terms=3):
    parts = [jnp.dot(t, mat, preferred_element_type=F32) for t in _split(x, terms)]
    return functools.reduce(lambda a, b: a + b, parts)


def _dot_exact_lhs(mat, x, terms=3):
    parts = [jnp.dot(mat, t, preferred_element_type=F32) for t in _split(x, terms)]
    return functools.reduce(lambda a, b: a + b, parts)


def _dot_nt(a, b):
    return lax.dot_general(a, b, (((1,), (1,)), ((), ())), preferred_element_type=F32)


def _dot_tn(a, b):
    return lax.dot_general(a, b, (((0,), (0,)), ((), ())), preferred_element_type=F32)


def _rmsnorm_kernel(x_ref, g_ref, o_ref):
    x = x_ref[...]
    y = x * lax.rsqrt(jnp.mean(x * x, axis=-1, keepdims=True) + NORM_EPS)
    o_ref[...] = (y * g_ref[...]).astype(o_ref.dtype)


def rmsnorm(x, g, out_dtype):
    m, d = x.shape
    tm = min(m, 512)
    return pl.pallas_call(
        _rmsnorm_kernel,
        grid=(m // tm,),
        in_specs=[pl.BlockSpec((tm, d), lambda i: (i, 0)), pl.BlockSpec((1, d), lambda i: (0, 0))],
        out_specs=pl.BlockSpec((tm, d), lambda i: (i, 0)),
        out_shape=jax.ShapeDtypeStruct((m, d), out_dtype),
        compiler_params=_cparams(("parallel",)),
        name="rmsnorm",
    )(x, g.reshape(1, d))


def _mm_kernel(*refs, nk, epi):
    refs = list(refs)
    x_ref, w_ref = refs[0:2]
    r_ref = refs.pop(2) if epi == "res" else None
    xs_ref = refs[2]
    rs_ref = refs.pop(3) if epi == "res" else None
    o_ref, os_ref = refs[3:5]
    acc_ref, accs_ref = (refs[5], refs[6]) if nk > 1 else (None, None)
    first_rows = pl.program_id(0) == 0
    k = pl.program_id(2)

    def finish(a, r, o):
        if epi == "res":
            a = r[...] + a
        elif epi == "relu2":
            a = jnp.square(jnp.maximum(a, 0.0))
        o[...] = a.astype(o.dtype)

    def accumulate(x, acc, r, o):
        part = jnp.dot(x[...], w_ref[...].astype(x.dtype), preferred_element_type=F32)
        if nk == 1:
            finish(part, r, o)
            return

        @pl.when(k == 0)
        def _():
            acc[...] = part

        @pl.when(k > 0)
        def _():
            acc[...] += part

        @pl.when(k == nk - 1)
        def _():
            finish(acc[...], r, o)

    accumulate(x_ref, acc_ref, r_ref, o_ref)

    @pl.when(first_rows)
    def _():
        accumulate(xs_ref, accs_ref, rs_ref, os_ref)


def _pick(n, cands):
    for c in cands:
        if n % c == 0:
            return c
    return n


def matmul(x, xs, w, layer, *, epi="none", res=None, res_s=None, out_dtype=F32):
    m, kdim = x.shape
    ms = xs.shape[0]
    n = w.shape[2]
    tm = min(m, 1024 if (epi == "res" or w.dtype == F32) else 2048)
    tn = _pick(n, (1536, 1024, 512, 256, 128))
    tk = min(kdim, 2048)
    nk = kdim // tk
    nj = n // tn
    side_col = lambda i, j: jnp.where(i == 0, j, nj - 1)
    in_specs = [pl.BlockSpec((tm, tk), lambda i, j, k: (i, k)),
                pl.BlockSpec((None, tk, tn), lambda i, j, k: (layer, k, j))]
    args = [x, w]
    if epi == "res":
        in_specs.append(pl.BlockSpec((tm, tn), lambda i, j, k: (i, j)))
        args.append(res)
    in_specs.append(pl.BlockSpec((ms, tk), lambda i, j, k: (0, k)))
    args.append(xs)
    if epi == "res":
        in_specs.append(pl.BlockSpec((ms, tn), lambda i, j, k: (0, side_col(i, j))))
        args.append(res_s)
    scratch = [pltpu.VMEM((tm, tn), F32), pltpu.VMEM((ms, tn), F32)] if nk > 1 else []
    return pl.pallas_call(
        functools.partial(_mm_kernel, nk=nk, epi=epi),
        grid=(m // tm, nj, nk),
        in_specs=in_specs,
        out_specs=[pl.BlockSpec((tm, tn), lambda i, j, k: (i, j)),
                   pl.BlockSpec((ms, tn), lambda i, j, k: (0, side_col(i, j)))],
        out_shape=[jax.ShapeDtypeStruct((m, n), out_dtype), jax.ShapeDtypeStruct((ms, n), out_dtype)],
        scratch_shapes=scratch,
        compiler_params=_cparams(("arbitrary", "arbitrary", "arbitrary")),
        name="matmul_" + epi,
    )(*args)


def _mm_res_norm_kernel(x_ref, w_ref, r_ref, xs_ref, rs_ref, g_ref, o_ref, on_ref, os_ref, osn_ref, *scr, nk):
    first_rows = pl.program_id(0) == 0
    k = pl.program_id(1)

    def finish(a, r, o, on):
        a = r[...] + a
        o[...] = a
        y = a * lax.rsqrt(jnp.mean(a * a, axis=-1, keepdims=True) + NORM_EPS)
        on[...] = (y * g_ref[...]).astype(on.dtype)

    def accumulate(x, acc, r, o, on):
        part = jnp.dot(x[...], w_ref[...], preferred_element_type=F32)
        if nk == 1:
            finish(part, r, o, on)
            return

        @pl.when(k == 0)
        def _():
            acc[...] = part

        @pl.when(k > 0)
        def _():
            acc[...] += part

        @pl.when(k == nk - 1)
        def _():
            finish(acc[...], r, o, on)

    acc_ref, accs_ref = scr if nk > 1 else (None, None)
    accumulate(x_ref, acc_ref, r_ref, o_ref, on_ref)

    @pl.when(first_rows)
    def _():
        accumulate(xs_ref, accs_ref, rs_ref, os_ref, osn_ref)


def matmul_res_norm(x, xs, w, layer, res, res_s, g, norm_dtype):
    m, kdim = x.shape
    ms = xs.shape[0]
    n = w.shape[2]
    tm = min(m, 512)
    tk = kdim if kdim <= 2048 else 1024
    nk = kdim // tk
    row = lambda width: pl.BlockSpec((tm, width), lambda i, k: (i, 0))
    side = lambda width: pl.BlockSpec((ms, width), lambda i, k: (0, 0))
    scratch = [pltpu.VMEM((tm, n), F32), pltpu.VMEM((ms, n), F32)] if nk > 1 else []
    return pl.pallas_call(
        functools.partial(_mm_res_norm_kernel, nk=nk),
        grid=(m // tm, nk),
        in_specs=[pl.BlockSpec((tm, tk), lambda i, k: (i, k)), pl.BlockSpec((None, tk, n), lambda i, k: (layer, k, 0)),
                  row(n), pl.BlockSpec((ms, tk), lambda i, k: (0, k)), side(n), pl.BlockSpec((1, n), lambda i, k: (0, 0))],
        out_specs=[row(n), row(n), side(n), side(n)],
        out_shape=[jax.ShapeDtypeStruct((m, n), F32), jax.ShapeDtypeStruct((m, n), norm_dtype),
                   jax.ShapeDtypeStruct((ms, n), F32), jax.ShapeDtypeStruct((ms, n), norm_dtype)],
        scratch_shapes=scratch,
        compiler_params=_cparams(("arbitrary", "arbitrary")),
        name="matmul_res_norm",
    )(x, w, res, xs, res_s, g.reshape(1, n))


def _proj_kernel(x_ref, xs_ref, w_ref, o_ref, os_ref, wb_scr):
    @pl.when(pl.program_id(1) == 0)
    def _():
        wb_scr[...] = w_ref[0].astype(BF16)
        os_ref[...] = _dot_nt(xs_ref[...], wb_scr[...])

    o_ref[...] = _dot_nt(x_ref[...], wb_scr[...])


def _main_row_start(j):
    t_rw, t_gate = C_R // PROJ_TN, C_G // PROJ_TN
    start = j * PROJ_TN + jnp.where(j >= t_rw, _O_RW - C_R, 0) + jnp.where(j >= t_gate, (_O_GATE - C_G) - (_O_RW - C_R), 0)
    return pl.multiple_of(start, SUBLANES)


def project(h, hs, w_t, layer, n_out, row_start):
    m, kdim = h.shape
    ms = hs.shape[0]
    tm = min(m, 2048)
    ell = pl.Element
    return pl.pallas_call(
        _proj_kernel,
        grid=(n_out // PROJ_TN, m // tm),
        in_specs=[pl.BlockSpec((tm, kdim), lambda j, i: (i, 0)),
                  pl.BlockSpec((ms, kdim), lambda j, i: (0, 0)),
                  pl.BlockSpec((ell(1), ell(PROJ_TN), ell(kdim)), lambda j, i: (layer, row_start(j), 0))],
        out_specs=[pl.BlockSpec((tm, PROJ_TN), lambda j, i: (i, j)), pl.BlockSpec((ms, PROJ_TN), lambda j, i: (0, j))],
        out_shape=[jax.ShapeDtypeStruct((m, n_out), F32), jax.ShapeDtypeStruct((ms, n_out), F32)],
        scratch_shapes=[pltpu.VMEM((PROJ_TN, kdim), BF16)],
        compiler_params=_cparams(("parallel", "arbitrary")),
        name="project",
    )(h, hs, w_t)


def project_main_and_tail(h, hs, big, layer):
    main, main_s = project(h, hs, big["w_in_t"], layer, N_MAIN, _main_row_start)
    tail, tail_s = project(h, hs, big["w_tail_t"], layer, N_TAIL, lambda j: pl.multiple_of(j * PROJ_TN, SUBLANES))
    return main, tail, main_s, tail_s


def _merge_kernel(oa_ref, os_ref, or_ref, ga_ref, gs_ref, gr_ref, oa2_ref, os2_ref, or2_ref, ga2_ref, gs2_ref,
                  gr2_ref, wa_ref, ws_ref, wr_ref, o_ref, o2_ref):
    f = lambda a, b: jnp.dot(a[...], b[...], preferred_element_type=F32)

    def merged(a_ref, s_ref, r_ref, g_a, g_s, g_r, out):
        acc = _sigmoid(g_a[...]) * f(a_ref, wa_ref)
        acc = acc + _sigmoid(g_s[...]) * f(s_ref, ws_ref)
        acc = acc + _sigmoid(g_r[...]) * f(r_ref, wr_ref)
        out[...] = acc.astype(out.dtype)

    merged(oa_ref, os_ref, or_ref, ga_ref, gs_ref, gr_ref, o_ref)

    @pl.when(pl.program_id(1) == 0)
    def _():
        merged(oa2_ref, os2_ref, or2_ref, ga2_ref, gs2_ref, gr2_ref, o2_ref)


def merge_branches(outs, proj, outs_s, proj_s, wa, ws, wr, layer):
    m = outs[0].shape[0]
    ms = outs_s[0].shape[0]
    tm = min(m, 512)
    tn = 1024
    gb = C_G // tn
    nj = D_MODEL // tn
    widths = (ATT_OUT, BRANCH_W, BRANCH_W)
    row = lambda width: pl.BlockSpec((tm, width), lambda j, i: (i, 0))
    gate = lambda g: pl.BlockSpec((tm, tn), lambda j, i, g=g: (i, gb + g * nj + j))
    row_s = lambda width: pl.BlockSpec((ms, width), lambda j, i: (0, 0))
    gate_s = lambda g: pl.BlockSpec((ms, tn), lambda j, i, g=g: (0, gb + g * nj + j))
    col = lambda k: pl.BlockSpec((None, k, tn), lambda j, i: (layer, 0, j))
    return pl.pallas_call(
        _merge_kernel,
        grid=(nj, m // tm),
        in_specs=[row(w) for w in widths] + [gate(g) for g in range(3)] + [row_s(w) for w in widths]
        + [gate_s(g) for g in range(3)] + [col(w) for w in widths],
        out_specs=[pl.BlockSpec((tm, tn), lambda j, i: (i, j)), pl.BlockSpec((ms, tn), lambda j, i: (0, j))],
        out_shape=[jax.ShapeDtypeStruct((m, D_MODEL), BF16), jax.ShapeDtypeStruct((ms, D_MODEL), BF16)],
        compiler_params=_cparams(("parallel", "arbitrary")),
        name="merge_branches",
    )(*outs, proj, proj, proj, *outs_s, proj_s, proj_s, proj_s, wa, ws, wr)


def _att_prompt_kernel(sl_ref, *refs, nblks):
    o_ref = refs[15]
    scr = refs[16:]
    for g in range(3):
        _att_group(sl_ref, *refs[5 * g:5 * g + 5], scr[2 * g], scr[2 * g + 1], d=ATT_DILS[g], nblk=nblks[g], g=g)
    a, b, c = scr[1][...], scr[3][...], scr[5][...]
    mx = jnp.maximum(jnp.maximum(a, b), c)
    ea, eb, ec = jnp.exp(a - mx), jnp.exp(b - mx), jnp.exp(c - mx)
    num = ea * scr[0][...] + eb * scr[2][...] + ec * scr[4][...]
    o_ref[...] = (num / (ea + eb + ec)).astype(o_ref.dtype)


def _att_group(sl_ref, q_ref, kc_ref, kp_ref, vc_ref, vp_ref, o_ref, lse_ref, *, d, nblk, g):
    j = pl.program_id(1)
    hp = pl.program_id(2)
    lane = lax.broadcasted_iota(jnp.int32, (1, LANES), 1)
    head0 = lane < HEAD_DIM
    qi = lax.broadcasted_iota(jnp.int32, (ATT_SPAN, 2 * ATT_SPAN), 0)
    kj = lax.broadcasted_iota(jnp.int32, (ATT_SPAN, 2 * ATT_SPAN), 1)
    delta = qi - kj + ATT_SPAN
    band = (delta >= 0) & (delta <= ATT_SPAN)
    first_ok = band & ((kj >= ATT_SPAN) | (j > 0))
    dist = (delta * d).astype(F32)
    slopes = (sl_ref[g, 2 * hp], sl_ref[g, 2 * hp + 1])

    def rows(base):
        return pl.ds(base, ATT_SPAN) if d == 1 else pl.ds(base, ATT_SPAN, stride=d)

    for ub in range(nblk):
        valid = first_ok if ub == 0 else band
        for r in range(d):
            base = ub * ATT_SPAN * d + r
            q = q_ref[rows(base), :]
            if ub == 0:
                kp, vp = kp_ref[rows(r), :], vp_ref[rows(r), :]
            else:
                kp, vp = kc_ref[rows(base - ATT_SPAN * d), :], vc_ref[rows(base - ATT_SPAN * d), :]
            k = jnp.concatenate([kp, kc_ref[rows(base), :]], axis=0).astype(BF16)
            v = jnp.concatenate([vp, vc_ref[rows(base), :]], axis=0).astype(BF16)
            outs, lses = [], []
            for e in range(2):
                sel = head0 if e == 0 else jnp.logical_not(head0)
                qe = jnp.where(sel, q, 0.0).astype(BF16)
                s = _dot_nt(qe, k) * ATT_SCALE
                s = jnp.where(valid, s - slopes[e] * dist, -jnp.inf)
                mx = jnp.max(s, axis=-1, keepdims=True)
                p = jnp.exp(s - mx)
                den = jnp.sum(p, axis=-1, keepdims=True)
                outs.append(jnp.dot(p.astype(BF16), v, preferred_element_type=F32) / den)
                lses.append(mx + jnp.log(den))
            o_ref[rows(base), :] = jnp.where(head0, outs[0], outs[1])
            lse_ref[rows(base), :] = jnp.where(head0, lses[0], lses[1])


def att_prompt(proj, slopes, bsz, t):
    tb = ATT_SPAN * max(ATT_DILS)
    nt = t // tb
    in_specs = [pl.BlockSpec(memory_space=pltpu.SMEM)]
    nblks = []
    for g in range(3):
        sb = ATT_SPAN * ATT_DILS[g]
        nblk = tb // sb
        nblks.append(nblk)
        cq, ck, cv = (C_Q + g * ATT_OUT) // LANES, (C_K + g * ATT_OUT) // LANES, (C_V + g * ATT_OUT) // LANES
        cur = lambda c0: pl.BlockSpec((tb, LANES), lambda b, j, h, c0=c0: (b * nt + j, c0 + h))
        prev = lambda c0, sb=sb, nblk=nblk: pl.BlockSpec(
            (sb, LANES), lambda b, j, h, c0=c0: (jnp.maximum((b * nt + j) * nblk - 1, 0), c0 + h))
        in_specs += [cur(cq), cur(ck), prev(ck), cur(cv), prev(cv)]
    return pl.pallas_call(
        functools.partial(_att_prompt_kernel, nblks=tuple(nblks)),
        grid=(bsz, nt, ATT_HEADS // 2),
        in_specs=in_specs,
        out_specs=pl.BlockSpec((tb, LANES), lambda b, j, h: (b * nt + j, h)),
        out_shape=jax.ShapeDtypeStruct((bsz * t, ATT_OUT), BF16),
        scratch_shapes=[pltpu.VMEM((tb, LANES), F32)] * 6,
        compiler_params=_cparams(("parallel", "parallel", "parallel")),
        name="att_prompt",
    )(slopes, *([proj] * 15))


def _att_sample_kernel(sl_ref, qkv_ref, k0, v0, k1, v1, k2, v2, o_ref):
    caches = ((k0, v0), (k1, v1), (k2, v2))
    outs, lses = [], []
    for g in range(3):
        d = ATT_DILS[g]
        assert d & (d - 1) == 0
        win = caches[g][0].shape[-1]
        q, kn, vn = qkv_ref[0, g], qkv_ref[1, g], qkv_ref[2, g]
        kc = caches[g][0][...]
        vc = caches[g][1][...]
        pos = lax.broadcasted_iota(jnp.int32, (1, 1, win), 2)
        on_grid = (pos & (d - 1)) == 0
        dist = (win - pos).astype(F32)
        s = jnp.sum(kc * q, axis=1, keepdims=True) * ATT_SCALE - sl_ref[g] * dist
        s = jnp.where(on_grid, s, -jnp.inf)
        s_new = jnp.sum(kn * q, axis=1, keepdims=True) * ATT_SCALE
        mx = jnp.maximum(jnp.max(s, axis=2, keepdims=True), s_new)
        p = jnp.exp(s - mx)
        p_new = jnp.exp(s_new - mx)
        den = jnp.sum(p, axis=2, keepdims=True) + p_new
        outs.append((jnp.sum(vc * p, axis=2, keepdims=True) + p_new * vn) / den)
        lses.append(mx + jnp.log(den))
    mx = jnp.maximum(jnp.maximum(lses[0], lses[1]), lses[2])
    es = [jnp.exp(l - mx) for l in lses]
    o_ref[...] = (es[0] * outs[0] + es[1] * outs[1] + es[2] * outs[2]) / (es[0] + es[1] + es[2])


def att_sample(qkv, caches_t, slopes, layer):
    bd = qkv.shape[0]
    in_specs = [pl.BlockSpec((3, ATT_HEADS, 1, 1), lambda b: (0, 0, 0, 0)),
                pl.BlockSpec((None, 3, 3, ATT_HEADS, HEAD_DIM, 1), lambda b: (b, 0, 0, 0, 0, 0))]
    for c in caches_t:
        in_specs.append(pl.BlockSpec((None, None, ATT_HEADS, HEAD_DIM, c.shape[-1]), lambda b: (layer, b, 0, 0, 0)))
    return pl.pallas_call(
        _att_sample_kernel,
        grid=(bd,),
        in_specs=in_specs,
        out_specs=pl.BlockSpec((None, ATT_HEADS, HEAD_DIM, 1), lambda b: (b, 0, 0, 0)),
        out_shape=jax.ShapeDtypeStruct((bd, ATT_HEADS, HEAD_DIM, 1), F32),
        compiler_params=_cparams(("parallel",)),
        name="att_sample",
    )(slopes.reshape(3, ATT_HEADS, 1, 1), qkv, *caches_t)


def _ssd_prompt_kernel(x_ref, b_ref, c_ref, z_ref, l_ref, cwx_ref, cwb_ref, cwc_ref, cbx_ref, cbb_ref, cbc_ref,
                       dtb_ref, aneg_ref, dskip_ref, ng_ref, o_ref, hout_ref, xpad, bpad, cpad, h_scr, y_scr, *, nc):
    c = pl.program_id(1)
    L = SSM_CHUNK

    @pl.when(c == 0)
    def _():
        xpad[0:SUBLANES, :] = jnp.zeros((SUBLANES, xpad.shape[1]), F32)
        bpad[0:SUBLANES, :] = jnp.zeros((SUBLANES, bpad.shape[1]), F32)
        cpad[0:SUBLANES, :] = jnp.zeros((SUBLANES, cpad.shape[1]), F32)
        h_scr[...] = jnp.zeros_like(h_scr)

    def conv_silu(pad, src_ref, w_ref, bias_ref):
        pad[SUBLANES:SUBLANES + L, :] = src_ref[...]
        acc = bias_ref[...] + w_ref[0:1, :] * pad[pl.ds(SUBLANES - 3, L), :]
        for jj in range(1, SSM_CONV):
            acc = acc + w_ref[jj:jj + 1, :] * pad[pl.ds(SUBLANES - 3 + jj, L), :]
        pad[0:SUBLANES, :] = pad[L:L + SUBLANES, :]
        return _silu(acc)

    xs = conv_silu(xpad, x_ref, cwx_ref, cbx_ref)
    bm = conv_silu(bpad, b_ref, cwb_ref, cbb_ref).astype(BF16)
    cm = conv_silu(cpad, c_ref, cwc_ref, cbc_ref).astype(BF16)

    lane = lax.broadcasted_iota(jnp.int32, (1, LANES), 1)
    head0 = lane < HEAD_DIM
    sub_head0 = lax.broadcasted_iota(jnp.int32, (LANES, 1), 0) < HEAD_DIM
    dt_lanes = (lane >= DT_LANE0) & (lane < DT_LANE0 + SSM_HEADS)
    dtv = jnp.where(dt_lanes, _softplus(l_ref[:, 0:LANES] + dtb_ref[...]), 0.0)
    da = dtv * aneg_ref[...]
    ri = lax.broadcasted_iota(jnp.int32, (L, L), 0)
    ci = lax.broadcasted_iota(jnp.int32, (L, L), 1)
    causal = ri >= ci
    a_cum = _dot_exact_lhs(causal.astype(BF16), da)
    a_cum_t = a_cum.T
    dt_t = dtv.T

    for g in range(SSM_GROUPS):
        bg = bm[:, g * SSM_STATE:(g + 1) * SSM_STATE]
        cg = cm[:, g * SSM_STATE:(g + 1) * SSM_STATE]
        cb = _dot_nt(cg, bg)
        for pp in range(3):
            pair = g * 3 + pp
            xp = xs[:, pair * LANES:(pair + 1) * LANES]
            xpb = xp.astype(BF16)
            yd, sc, ea, cd = [], [], [], []
            for e in range(2):
                hl = DT_LANE0 + 2 * pair + e
                ac_col = a_cum[:, hl:hl + 1]
                ac_row = a_cum_t[hl:hl + 1, :]
                a_last = ac_col[L - 1:L, :]
                dec = jnp.exp(jnp.where(causal, ac_col - ac_row, -jnp.inf))
                wm = (cb * dec * dt_t[hl:hl + 1, :]).astype(BF16)
                yd.append(jnp.dot(wm, xpb, preferred_element_type=F32))
                sc.append(jnp.exp(a_last - ac_col) * dtv[:, hl:hl + 1])
                ea.append(jnp.exp(ac_col))
                cd.append(jnp.exp(a_last))
            hp = h_scr[pair]
            y_off = _dot_nt(cg, hp.astype(BF16)) * jnp.where(head0, ea[0], ea[1])
            xw = (xp * jnp.where(head0, sc[0], sc[1])).astype(BF16)
            h_scr[pair] = jnp.where(sub_head0, cd[0], cd[1]) * hp + _dot_tn(xw, bg)
            y_scr[:, pair * LANES:(pair + 1) * LANES] = jnp.where(head0, yd[0], yd[1]) + y_off

    y = (y_scr[...] + dskip_ref[...] * xs) * _silu(z_ref[...])
    gw = BRANCH_W // SSM_GROUPS
    for g in range(SSM_GROUPS):
        yg = y[:, g * gw:(g + 1) * gw]
        yg = yg * lax.rsqrt(jnp.mean(yg * yg, axis=-1, keepdims=True) + NORM_EPS)
        o_ref[:, g * gw:(g + 1) * gw] = (yg * ng_ref[:, g * gw:(g + 1) * gw]).astype(o_ref.dtype)

    @pl.when(c == nc - 1)
    def _():
        hout_ref[...] = h_scr[...]


def ssd_prompt(proj, tail, lw, bsz, t):
    nc = t // SSM_CHUNK
    L = SSM_CHUNK
    rowblk = lambda width, c0: pl.BlockSpec((L, width), lambda b, c: (b * nc + c, c0 // width))
    par = lambda a: pl.BlockSpec(a.shape, lambda b, c: (0,) * a.ndim)
    params = [lw["cw_x"], lw["cw_b"], lw["cw_c"], lw["cb_x"], lw["cb_b"], lw["cb_c"], lw["dt_bias"], lw["a_neg"],
              lw["d_skip"], lw["ssm_norm_g"]]
    return pl.pallas_call(
        functools.partial(_ssd_prompt_kernel, nc=nc),
        grid=(bsz, nc),
        in_specs=[rowblk(BRANCH_W, C_X), rowblk(512, T_B), rowblk(512, T_C), rowblk(BRANCH_W, C_Z), rowblk(512, T_L)]
        + [par(a) for a in params],
        out_specs=[pl.BlockSpec((L, BRANCH_W), lambda b, c: (b * nc + c, 0)),
                   pl.BlockSpec((None, N_PAIRS, LANES, SSM_STATE), lambda b, c: (b, 0, 0, 0))],
        out_shape=[jax.ShapeDtypeStruct((bsz * t, BRANCH_W), BF16),
                   jax.ShapeDtypeStruct((bsz, N_PAIRS, LANES, SSM_STATE), F32)],
        scratch_shapes=[pltpu.VMEM((L + SUBLANES, BRANCH_W), F32), pltpu.VMEM((L + SUBLANES, 512), F32),
                        pltpu.VMEM((L + SUBLANES, 512), F32), pltpu.VMEM((N_PAIRS, LANES, SSM_STATE), F32),
                        pltpu.VMEM((L, BRANCH_W), F32)],
        compiler_params=_cparams(("parallel", "arbitrary")),
        name="ssd_prompt",
    )(proj, tail, tail, proj, tail, *params)


def _ssd_sample_conv_kernel(new_ref, st_ref, w_ref, b_ref, o_ref):
    acc = b_ref[...] + w_ref[SSM_CONV - 1:SSM_CONV, :] * new_ref[...]
    for jj in range(SSM_CONV - 1):
        acc = acc + w_ref[jj:jj + 1, :] * st_ref[jj]
    o_ref[...] = _silu(acc)


def ssd_sample_conv(xbc_new, conv_st_t, w, b):
    bd, cdim = xbc_new.shape
    return pl.pallas_call(
        _ssd_sample_conv_kernel,
        out_shape=jax.ShapeDtypeStruct((bd, cdim), F32),
        name="ssd_sample_conv",
    )(xbc_new, conv_st_t, w, b.reshape(1, cdim))


def _ssd_sample_step_kernel(h_ref, x_ref, z_ref, b_ref, c_ref, dt_ref, dtb_ref, alog_ref, d_ref, ng_ref, o_ref, hout_ref):
    h = h_ref[...]
    dt = _softplus(dt_ref[...] + dtb_ref[...])
    da = jnp.exp(dt * (-jnp.exp(alog_ref[...])))
    rep = SSM_HEADS // SSM_GROUPS
    bh = jnp.broadcast_to(b_ref[...][:, None], (SSM_GROUPS, rep, 1, SSM_STATE)).reshape(SSM_HEADS, 1, SSM_STATE)
    ch = jnp.broadcast_to(c_ref[...][:, None], (SSM_GROUPS, rep, 1, SSM_STATE)).reshape(SSM_HEADS, 1, SSM_STATE)
    x = x_ref[...]
    hn = da * h + (dt * x) * bh
    hout_ref[...] = hn
    y = jnp.sum(hn * ch, axis=-1, keepdims=True)
    y = (y + d_ref[...] * x) * _silu(z_ref[...])
    y4 = y.reshape(SSM_GROUPS, rep, HEAD_DIM, 1)
    ms = jnp.sum(jnp.sum(y4 * y4, axis=2, keepdims=True), axis=1, keepdims=True) / (rep * HEAD_DIM)
    y4 = y4 * lax.rsqrt(ms + NORM_EPS)
    o_ref[...] = y4.reshape(SSM_HEADS, HEAD_DIM, 1) * ng_ref[...]


def ssd_sample_step(h, x_col, z_col, bmat, cmat, dt_raw, lw, layer):
    bd = x_col.shape[0]
    col = pl.BlockSpec((None, SSM_HEADS, HEAD_DIM, 1), lambda b: (b, 0, 0, 0))
    grp = pl.BlockSpec((None, SSM_GROUPS, 1, SSM_STATE), lambda b: (b, 0, 0, 0))
    hd1 = pl.BlockSpec((SSM_HEADS, 1, 1), lambda b: (0, 0, 0))
    return pl.pallas_call(
        _ssd_sample_step_kernel,
        grid=(bd,),
        in_specs=[pl.BlockSpec((None, None, SSM_HEADS, HEAD_DIM, SSM_STATE), lambda b: (layer, b, 0, 0, 0)),
                  col, col, grp, grp, pl.BlockSpec((None, SSM_HEADS, 1, 1), lambda b: (b, 0, 0, 0)), hd1, hd1, hd1,
                  pl.BlockSpec((SSM_HEADS, HEAD_DIM, 1), lambda b: (0, 0, 0))],
        out_specs=[col, pl.BlockSpec((None, SSM_HEADS, HEAD_DIM, SSM_STATE), lambda b: (b, 0, 0, 0))],
        out_shape=[jax.ShapeDtypeStruct((bd, SSM_HEADS, HEAD_DIM, 1), F32),
                   jax.ShapeDtypeStruct((bd, SSM_HEADS, HEAD_DIM, SSM_STATE), F32)],
        compiler_params=_cparams(("parallel",)),
        name="ssd_sample_step",
    )(h, x_col, z_col, bmat, cmat, dt_raw, lw["dt_bias_h"], lw["a_log_h"], lw["d_h"], lw["ssm_norm_g_col"])


def _rwkv_mix(i, tm, x_refs, h_refs, i_refs, mu_refs, w0_ref, wup_ref, a0_ref, aup_ref, gup_ref):
    def shift(x_ref, h_ref, i_ref, mu_ref):
        x = x_ref[...]
        hrows = h_ref.shape[0]
        prev0 = jnp.where(i == 0, i_ref[...], h_ref[hrows - 1:hrows, :])
        if tm == 1:
            xp = prev0
        else:
            row = lax.broadcasted_iota(jnp.int32, (tm, 1), 0)
            xp = jnp.where(row == 0, prev0, pltpu.roll(x, 1, 0))
        return x + mu_ref[...] * (xp - x)

    ur, uk, uv, ul = (shift(*refs) for refs in zip(x_refs, h_refs, i_refs, mu_refs))
    f = lambda a, w_ref: jnp.dot(a.astype(BF16), w_ref[...], preferred_element_type=F32)
    w_log = -_softplus(-(w0_ref[...] + f(jnp.tanh(ul[:, 0:LANES]), wup_ref))) - 0.5
    a = _sigmoid(a0_ref[...] + f(ul[:, LANES:2 * LANES], aup_ref))
    g = f(_sigmoid(ul[:, 2 * LANES:4 * LANES]), gup_ref)
    return ur, uk, uv, -jnp.exp(w_log), a, g


def _rwkv_prep_kernel(*refs, tm):
    x_refs, h_refs, i_refs, mu_refs = refs[0:4], refs[4:8], refs[8:12], refs[12:16]
    w0_ref, wup_ref, a0_ref, aup_ref, gup_ref, kk_ref, ka_ref = refs[16:23]
    or_ref, old_ref, ok_ref, ov_ref, okk_ref, oa_ref, og_ref = refs[23:30]
    ur, uk, uv, ld, a, g = _rwkv_mix(pl.program_id(1), tm, x_refs, h_refs, i_refs, mu_refs,
                                     w0_ref, wup_ref, a0_ref, aup_ref, gup_ref)
    or_ref[...] = ur
    old_ref[...] = ld
    ok_ref[...] = uk * (1.0 + (a - 1.0) * ka_ref[...])
    ov_ref[...] = uv
    okk_ref[...] = uk * kk_ref[...]
    oa_ref[...] = a
    og_ref[...] = g


def _rwkv_prep_prompt_kernel(*refs, tm):
    x_refs, h_refs, i_refs, mu_refs = refs[0:4], refs[4:8], refs[8:12], refs[12:16]
    w0_ref, wup_ref, a0_ref, aup_ref, gup_ref, kk_ref, ka_ref, rk_ref = refs[16:24]
    opt_ref, ort_ref, oqh_ref, okh_ref, ov_ref, obv_ref, og_ref, oel_ref = refs[24:32]
    L = RWKV_CHUNK
    ur, uk, uv, ld, a, g = _rwkv_mix(pl.program_id(1), tm, x_refs, h_refs, i_refs, mu_refs,
                                     w0_ref, wup_ref, a0_ref, aup_ref, gup_ref)
    kmod = uk * (1.0 + (a - 1.0) * ka_ref[...])
    kkraw = uk * kk_ref[...]
    assert tm == 2 * L
    ri = lax.broadcasted_iota(jnp.int32, (tm, tm), 0)
    ci = lax.broadcasted_iota(jnp.int32, (tm, tm), 1)
    tril = ((ri >= ci) & ((ri < L) == (ci < L))).astype(BF16)
    cum = _dot_exact_lhs(tril, ld, 2)
    e_in = jnp.exp(cum)
    e_inv = jnp.exp(-cum)
    e_ex = jnp.exp(cum - ld)
    ort_ref[...] = (ur * e_in).astype(ort_ref.dtype)
    okh_ref[...] = (kmod * e_inv).astype(okh_ref.dtype)
    ov_ref[...] = uv.astype(ov_ref.dtype)
    og_ref[...] = g.astype(og_ref.dtype)
    for c in range(tm // L):
        oel_ref[c] = e_in[c * L + L - 1:c * L + L, :]
    li = lax.broadcasted_iota(jnp.int32, (LANES, LANES), 0)
    lj = lax.broadcasted_iota(jnp.int32, (LANES, LANES), 1)
    bd_ones = ((li < HEAD_DIM) == (lj < HEAD_DIM)).astype(BF16)
    rkr = ur * kmod * rk_ref[...]
    for p in range(N_PAIRS):
        cs = slice(p * LANES, (p + 1) * LANES)
        kkp = kkraw[:, cs]
        kkn = kkp / jnp.maximum(jnp.sqrt(_dot_exact_rhs(kkp * kkp, bd_ones, 1)), 1e-12)
        opt_ref[:, cs] = (-kkn * e_ex[:, cs]).astype(opt_ref.dtype)
        oqh_ref[:, cs] = (kkn * a[:, cs] * e_inv[:, cs]).astype(oqh_ref.dtype)
        obv_ref[:, cs] = (_dot_exact_rhs(rkr[:, cs], bd_ones, 1) * uv[:, cs]).astype(obv_ref.dtype)


def _rwkv_prep_call(kernel_fn, proj, tail, init, params, bsz, t, tm, out_specs, out_shape, name):
    nt = t // tm
    hrows = SUBLANES if t >= SUBLANES else t
    hper = tm // hrows
    proj = proj.reshape(bsz, t, N_MAIN)
    tail = tail.reshape(bsz, t, N_TAIL)
    blk = lambda width, c0: pl.BlockSpec((None, tm, width), lambda b, i: (b, i, c0 // width))
    halo = lambda width, c0: pl.BlockSpec(
        (None, hrows, width), lambda b, i: (b, jnp.maximum(i * hper - 1, 0), c0 // width))
    ini = lambda width: pl.BlockSpec((None, 1, width), lambda b, i: (b, 0, 0))
    par = lambda a: pl.BlockSpec(a.shape, lambda b, i: (0,) * a.ndim)
    secs = ((BRANCH_W, C_R), (BRANCH_W, C_RK), (BRANCH_W, C_RV), (512, T_L))
    srcs = (proj, proj, proj, tail)
    return pl.pallas_call(
        functools.partial(kernel_fn, tm=tm),
        grid=(bsz, nt),
        in_specs=[blk(*s) for s in secs] + [halo(*s) for s in secs] + [ini(s[0]) for s in secs]
        + [par(a) for a in params],
        out_specs=out_specs,
        out_shape=out_shape,
        compiler_params=_cparams(("parallel", "arbitrary")),
        name=name,
    )(*srcs, *srcs, *init, *params)


def _rwkv_prep_params(lw):
    return [lw["mu_r"], lw["mu_k"], lw["mu_v"], lw["mu_l"], lw["w0"], lw["w_up"], lw["a0"], lw["a_up"], lw["g_up"],
            lw["k_k"], lw["k_a"]]


def rwkv_prep(proj, tail, init, lw, bsz, t):
    tm = min(t, 128)
    out = pl.BlockSpec((None, tm, BRANCH_W), lambda b, i: (b, i, 0))
    shp = jax.ShapeDtypeStruct((bsz, t, BRANCH_W), F32)
    outs = _rwkv_prep_call(_rwkv_prep_kernel, proj, tail, init, _rwkv_prep_params(lw), bsz, t, tm, [out] * 7,
                           [shp] * 7, "rwkv_prep")
    return [o.reshape(bsz * t, BRANCH_W) for o in outs]


def rwkv_prep_prompt(proj, tail, init, lw, bsz, t):
    tm = 128
    nch = tm // RWKV_CHUNK
    out = pl.BlockSpec((None, tm, BRANCH_W), lambda b, i: (b, i, 0))
    shp = jax.ShapeDtypeStruct((bsz, t, BRANCH_W), BF16)
    el_spec = pl.BlockSpec((None, nch, 1, BRANCH_W), lambda b, i: (b, i, 0, 0))
    el_shape = jax.ShapeDtypeStruct((bsz, t // RWKV_CHUNK, 1, BRANCH_W), F32)
    outs = _rwkv_prep_call(_rwkv_prep_prompt_kernel, proj, tail, init, _rwkv_prep_params(lw) + [lw["r_k_row"]], bsz, t,
                           tm, [out] * 7 + [el_spec], [shp] * 7 + [el_shape], "rwkv_prep_prompt")
    return [o.reshape(bsz * t, BRANCH_W) for o in outs[:7]] + [outs[7].reshape(bsz * t // RWKV_CHUNK, 1, BRANCH_W)]


def _rwkv_chain_kernel(pt_ref, rt_ref, qh_ref, kh_ref, v_ref, bv_ref, g_ref, el_ref, lng_ref, lnb_ref, o_ref, sout_ref,
                       s_scr, *, nchunk, nt, npair):
    tstep = pl.program_id(2)
    L = RWKV_CHUNK
    L2 = 2 * L

    @pl.when(tstep == 0)
    def _():
        s_scr[...] = jnp.zeros_like(s_scr)

    lane = lax.broadcasted_iota(jnp.int32, (1, LANES), 1)
    head0 = lane < HEAD_DIM
    ri = lax.broadcasted_iota(jnp.int32, (L2, L2), 0)
    ci = lax.broadcasted_iota(jnp.int32, (L2, L2), 1)
    same = (ri < L) == (ci < L)
    strict = same & (ri > ci)
    incl = same & (ri >= ci)
    eye = (ri == ci).astype(F32)
    bd_ones = same.astype(BF16)
    bf = lambda x: x.astype(BF16)
    dot = lambda a, b: jnp.dot(bf(a), bf(b), preferred_element_type=F32)
    zero = jnp.zeros((), BF16)

    def stack(x):
        return jnp.concatenate([jnp.where(head0, x, zero), jnp.where(head0, zero, x)], axis=0)

    probs = [(c, p) for c in range(nchunk) for p in range(npair)]
    rows = lambda c: pl.ds(c * L, L)
    cols = lambda p: slice(p * LANES, (p + 1) * LANES)
    pm = {k: stack(pt_ref[rows(k[0]), cols(k[1])]) for k in probs}
    rm = {k: stack(rt_ref[rows(k[0]), cols(k[1])]) for k in probs}
    qk = {k: jnp.concatenate([stack(qh_ref[rows(k[0]), cols(k[1])]), stack(kh_ref[rows(k[0]), cols(k[1])])], axis=0)
          for k in probs}
    vm = {k: stack(v_ref[rows(k[0]), cols(k[1])]) for k in probs}
    x = {k: _dot_nt(jnp.concatenate([pm[k], rm[k]], axis=0), qk[k]) for k in probs}
    sums = lambda v: jnp.dot(bf(v), bd_ones, preferred_element_type=F32)
    n_pow = {k: jnp.where(strict, x[k][0:L2, 0:L2], 0.0) for k in probs}
    a_kp = {k: bf(jnp.where(strict, x[k][0:L2, L2:2 * L2], 0.0)) for k in probs}
    r_q = {k: bf(jnp.where(incl, x[k][L2:2 * L2, 0:L2], 0.0)) for k in probs}
    r_k = {k: bf(jnp.where(incl, x[k][L2:2 * L2, L2:2 * L2], 0.0)) for k in probs}
    t_inv = {k: eye + n_pow[k] for k in probs}
    for _ in range(5):
        n_pow = {k: dot(n_pow[k], n_pow[k]) for k in probs}
        t_inv = {k: t_inv[k] + dot(n_pow[k], t_inv[k]) for k in probs}
    t_inv = {k: bf(t_inv[k]) for k in probs}
    pr = {k: jnp.concatenate([pm[k], rm[k]], axis=0) for k in probs}
    akv = {k: dot(jnp.concatenate([a_kp[k], r_k[k]], axis=0), vm[k]) for k in probs}

    state = [s_scr[p] for p in range(npair)]
    ys = {}
    for c in range(nchunk):
        ks = [(c, p) for p in range(npair)]
        ps = {k: dot(pr[k], bf(state[k[1]])) + akv[k] for k in ks}
        rhs = {k: ps[k][0:L2, :] for k in ks}
        rs = {k: ps[k][L2:2 * L2, :] for k in ks}
        u = {k: bf(dot(t_inv[k], rhs[k])) for k in ks}
        upd = {k: _dot_tn(qk[k], jnp.concatenate([u[k], vm[k]], axis=0)) for k in ks}
        for k in ks:
            ys[k] = rs[k] + dot(r_q[k], u[k])
            w_col = jnp.sum(eye * el_ref[c, :, cols(k[1])], axis=1, keepdims=True)
            state[k[1]] = w_col * (state[k[1]] + upd[k])
    for p in range(npair):
        s_scr[p] = state[p]

    y = {k: ys[k][0:L, :] + ys[k][L:L2, :] for k in probs}
    yc = {k: y[k] - sums(y[k]) * (1.0 / HEAD_DIM) for k in probs}
    var = {k: sums(yc[k] * yc[k]) * (1.0 / HEAD_DIM) for k in probs}
    for k in probs:
        cs = cols(k[1])
        yn = yc[k] * lax.rsqrt(var[k] + RWKV_LN_EPS) * lng_ref[:, cs] + lnb_ref[:, cs]
        o_ref[rows(k[0]), cs] = ((yn + bv_ref[rows(k[0]), cs].astype(F32)) * g_ref[rows(k[0]), cs].astype(F32)
                                 ).astype(o_ref.dtype)

    @pl.when(tstep == nt - 1)
    def _():
        sout_ref[...] = s_scr[...]


def rwkv_prompt(parts, lw, bsz, t, npair=12, nchunk=2):
    tb = nchunk * RWKV_CHUNK
    nt = t // tb
    wl = npair * LANES
    blk = pl.BlockSpec((tb, wl), lambda b, p, i: (b * nt + i, p))
    par = pl.BlockSpec((1, wl), lambda b, p, i: (0, p))
    el = pl.BlockSpec((nchunk, 1, wl), lambda b, p, i: (b * nt + i, 0, p))
    return pl.pallas_call(
        functools.partial(_rwkv_chain_kernel, nchunk=nchunk, nt=nt, npair=npair),
        grid=(bsz, N_PAIRS // npair, nt),
        in_specs=[blk] * 7 + [el, par, par],
        out_specs=[blk, pl.BlockSpec((None, npair, LANES, LANES), lambda b, p, i: (b, p, 0, 0))],
        out_shape=[jax.ShapeDtypeStruct((bsz * t, BRANCH_W), BF16),
                   jax.ShapeDtypeStruct((bsz, N_PAIRS, LANES, LANES), F32)],
        scratch_shapes=[pltpu.VMEM((npair, LANES, LANES), F32)],
        compiler_params=_cparams(("parallel", "parallel", "arbitrary")),
        name="rwkv_prompt",
    )(*parts, lw["ln_g_row"], lw["ln_b_row"])


def _rwkv_sample_step_kernel(s_ref, r_ref, ld_ref, k_ref, kk_ref, a_ref, v_ref, g_ref, rk_ref, lng_ref, lnb_ref,
                             o_ref, sout_ref):
    s = s_ref[...]
    r, kmod, kkraw, a = r_ref[...], k_ref[...], kk_ref[...], a_ref[...]
    v = v_ref[...]
    kkn = kkraw / jnp.maximum(jnp.sqrt(jnp.sum(kkraw * kkraw, axis=-1, keepdims=True)), 1e-12)
    sa = jnp.sum(s * (-kkn), axis=-1, keepdims=True)
    sn = s * jnp.exp(ld_ref[...]) + sa * (kkn * a) + v * kmod
    sout_ref[...] = sn
    y = jnp.sum(sn * r, axis=-1, keepdims=True)
    mean = jnp.mean(y, axis=1, keepdims=True)
    yc = y - mean
    var = jnp.mean(yc * yc, axis=1, keepdims=True)
    yn = yc * lax.rsqrt(var + RWKV_LN_EPS) * lng_ref[...] + lnb_ref[...]
    bonus = jnp.sum(r * kmod * rk_ref[...], axis=-1, keepdims=True) * v
    o_ref[...] = (yn + bonus) * g_ref[...]


def rwkv_sample_step(state, rows, cols, lw, layer):
    bd = rows[0].shape[0]
    row = pl.BlockSpec((None, RWKV_HEADS, 1, HEAD_DIM), lambda b: (b, 0, 0, 0))
    col = pl.BlockSpec((None, RWKV_HEADS, HEAD_DIM, 1), lambda b: (b, 0, 0, 0))
    prow = pl.BlockSpec((RWKV_HEADS, 1, HEAD_DIM), lambda b: (0, 0, 0))
    pcol = pl.BlockSpec((RWKV_HEADS, HEAD_DIM, 1), lambda b: (0, 0, 0))
    return pl.pallas_call(
        _rwkv_sample_step_kernel,
        grid=(bd,),
        in_specs=[pl.BlockSpec((None, None, RWKV_HEADS, HEAD_DIM, HEAD_DIM), lambda b: (layer, b, 0, 0, 0))]
        + [row] * 5 + [col] * 2 + [prow, pcol, pcol],
        out_specs=[col, pl.BlockSpec((None, RWKV_HEADS, HEAD_DIM, HEAD_DIM), lambda b: (b, 0, 0, 0))],
        out_shape=[jax.ShapeDtypeStruct((bd, RWKV_HEADS, HEAD_DIM, 1), F32),
                   jax.ShapeDtypeStruct((bd, RWKV_HEADS, HEAD_DIM, HEAD_DIM), F32)],
        compiler_params=_cparams(("parallel",)),
        name="rwkv_sample_step",
    )(state, *rows, *cols, lw["r_k_h"], lw["ln_g_col"], lw["ln_b_col"])


def _cache_update_kernel(k_ref, v_ref, newk_ref, newv_ref, ok_ref, ov_ref):
    win = k_ref.shape[-1]
    is_last = lax.broadcasted_iota(jnp.int32, (1, win), 1) == win - 1
    for c_ref, new_ref, o_ref in ((k_ref, newk_ref, ok_ref), (v_ref, newv_ref, ov_ref)):
        for h in range(ATT_HEADS):
            o_ref[h] = jnp.where(is_last, new_ref[h], pltpu.roll(c_ref[h], win - 1, 1))


def cache_update(cache_k, cache_v, new_k, new_v):
    nl, bd = cache_k.shape[:2]
    win = cache_k.shape[-1]
    blk = pl.BlockSpec((None, None, ATT_HEADS, HEAD_DIM, win), lambda l, b: (l, b, 0, 0, 0))
    new = pl.BlockSpec((None, None, ATT_HEADS, HEAD_DIM, 1), lambda l, b: (l, b, 0, 0, 0))
    shp = jax.ShapeDtypeStruct(cache_k.shape, cache_k.dtype)
    return pl.pallas_call(
        _cache_update_kernel,
        grid=(nl, bd),
        in_specs=[blk, blk, new, new],
        out_specs=[blk, blk],
        out_shape=[shp, shp],
        compiler_params=_cparams(("parallel", "parallel")),
        name="cache_update",
    )(cache_k, cache_v, new_k, new_v)


def _pack_layer(l, p):
    row = lambda a: a.reshape(1, -1)
    pad_rows = lambda a, n: jnp.concatenate([a, jnp.zeros((n - a.shape[0], a.shape[1]), a.dtype)], axis=0).astype(BF16)
    lane_pad = lambda a: jnp.zeros((1, LANES), F32).at[0, DT_LANE0:DT_LANE0 + SSM_HEADS].set(a)
    mu = p["rwkv_mu"][l]
    z1 = lambda n: jnp.zeros((n,), F32)
    cw = p["ssm_conv_w"][l]
    cb = p["ssm_conv_b"][l]
    rep64 = lambda a: jnp.repeat(a, HEAD_DIM)
    return dict(
        ln1_g=p["ln1_g"][l], ln2_g=p["ln2_g"][l],
        cw_x=cw[:, :BRANCH_W], cw_b=cw[:, BRANCH_W:BRANCH_W + 512], cw_c=cw[:, BRANCH_W + 512:],
        cb_x=row(cb[:BRANCH_W]), cb_b=row(cb[BRANCH_W:BRANCH_W + 512]), cb_c=row(cb[BRANCH_W + 512:]),
        conv_w=cw, conv_b=cb,
        dt_bias=lane_pad(p["ssm_dt_bias"][l]), a_neg=lane_pad(-jnp.exp(p["ssm_a_log"][l])),
        d_skip=row(rep64(p["ssm_d"][l])), ssm_norm_g=row(p["ssm_norm_g"][l]),
        dt_bias_h=p["ssm_dt_bias"][l].reshape(SSM_HEADS, 1, 1), a_log_h=p["ssm_a_log"][l].reshape(SSM_HEADS, 1, 1),
        d_h=p["ssm_d"][l].reshape(SSM_HEADS, 1, 1), ssm_norm_g_col=p["ssm_norm_g"][l].reshape(SSM_HEADS, HEAD_DIM, 1),
        mu_r=row(mu[:BRANCH_W]), mu_k=row(mu[BRANCH_W:2 * BRANCH_W]), mu_v=row(mu[2 * BRANCH_W:3 * BRANCH_W]),
        mu_l=row(jnp.concatenate([mu[3 * BRANCH_W:3 * BRANCH_W + LORA_W], z1(32),
                                  mu[3 * BRANCH_W + LORA_W:3 * BRANCH_W + LORA_W + LORA_A], z1(32),
                                  mu[3 * BRANCH_W + LORA_W + LORA_A:]])),
        w0=row(p["rwkv_w0"][l]), w_up=pad_rows(p["rwkv_w_up"][l], LANES), a0=row(p["rwkv_a0"][l]),
        a_up=pad_rows(p["rwkv_a_up"][l], LANES), g_up=p["rwkv_g_up"][l].astype(BF16),
        k_k=row(p["rwkv_k_k"][l]), k_a=row(p["rwkv_k_a"][l]),
        r_k_row=row(p["rwkv_r_k"][l]), ln_g_row=row(p["rwkv_ln_g"][l]), ln_b_row=row(p["rwkv_ln_b"][l]),
        r_k_h=p["rwkv_r_k"][l].reshape(RWKV_HEADS, 1, HEAD_DIM),
        ln_g_col=p["rwkv_ln_g"][l].reshape(RWKV_HEADS, HEAD_DIM, 1),
        ln_b_col=p["rwkv_ln_b"][l].reshape(RWKV_HEADS, HEAD_DIM, 1),
    )


def _pack_big(p):
    w_t = jnp.swapaxes(p["w_in"], 1, 2)
    zrows = lambda n: jnp.zeros((w_t.shape[0], n, D_MODEL), F32)
    w_tail_t = jnp.concatenate([
        w_t[:, _O_XBC + BRANCH_W:_O_DT], w_t[:, _O_LORA:_O_LORA + LORA_W], w_t[:, _O_DT:_O_RW], zrows(8),
        w_t[:, _O_LORA + LORA_W:_O_LORA + LORA_W + LORA_A], zrows(32), w_t[:, _O_LORA + LORA_W + LORA_A:_O_GATE]], axis=1)
    return dict(w_in_t=w_t, w_tail_t=w_tail_t, w_att=p["w_branch_att"].astype(BF16),
                w_ssm=p["w_branch_ssm"].astype(BF16), w_rwkv=p["w_branch_rwkv"].astype(BF16),
                w_out=p["w_out"].astype(BF16), w_up_ff=p["w_up"], w_down_ff=p["w_down"].astype(BF16))


def _unpack_lora(rows):
    return jnp.concatenate([rows[..., 0:LORA_W], rows[..., LANES:LANES + LORA_A], rows[..., 2 * LANES:]], axis=-1)


def _pack_lora(rows):
    z = lambda n: jnp.zeros(rows.shape[:-1] + (n,), rows.dtype)
    return jnp.concatenate([rows[..., 0:LORA_W], z(32), rows[..., LORA_W:LORA_W + LORA_A], z(32),
                            rows[..., LORA_W + LORA_A:]], axis=-1)


def _alibi_slopes():
    n = 3 * ATT_HEADS
    idx = jnp.arange(1, n + 1, dtype=F32)
    return jnp.exp2(-8.0 * idx / n).reshape(3, ATT_HEADS)


def _layer(xp, xs, hp, hs, lw, big, layer, slopes, st, bsz, t, bd, next_g, next_dtype):
    proj, tail, proj_s, tail_s = project_main_and_tail(hp, hs, big, layer)
    outs, state_p = _prompt_mixers(proj, tail, lw, slopes, bsz, t)
    outs_s, new_rows, state_s = _sample_mixers(proj_s, tail_s, lw, slopes, st, layer, bd)
    merged, merged_s = merge_branches(outs, proj, outs_s, proj_s, big["w_att"], big["w_ssm"], big["w_rwkv"], layer)
    xp, h2, xs, h2_s = matmul_res_norm(merged, merged_s, big["w_out"], layer, xp, xs, lw["ln2_g"], BF16)
    up, up_s = matmul(h2, h2_s, big["w_up_ff"], layer, epi="relu2", out_dtype=BF16)
    xp, xs = matmul(up, up_s, big["w_down_ff"], layer, epi="res", res=xp, res_s=xs)
    hp, hs = rmsnorm(xp, next_g, next_dtype), rmsnorm(xs, next_g, next_dtype)
    return xp, xs, hp, hs, state_p, new_rows, state_s


def _prompt_mixers(proj, tail, lw, slopes, bsz, t):
    o_att = att_prompt(proj, slopes, bsz, t)
    o_ssm, h_fin = ssd_prompt(proj, tail, lw, bsz, t)
    zero_init = [jnp.zeros((bsz, 1, w), F32) for w in (BRANCH_W, BRANCH_W, BRANCH_W, 512)]
    rw = rwkv_prep_prompt(proj, tail, zero_init, lw, bsz, t)
    o_rwkv, s_fin = rwkv_prompt(rw, lw, bsz, t)

    p3 = proj.reshape(bsz, t, N_MAIN)
    t3 = tail.reshape(bsz, t, N_TAIL)
    kv = []
    for g in range(3):
        win = ATT_WINDOWS[g]
        for c0 in (C_K, C_V):
            kv.append(p3[:, t - win:, c0 + g * ATT_OUT:c0 + (g + 1) * ATT_OUT].reshape(bsz, win, ATT_HEADS, HEAD_DIM))
    nc = SSM_CONV - 1
    conv_new = jnp.concatenate([p3[:, t - nc:, C_X:C_X + BRANCH_W], t3[:, t - nc:, T_B:T_B + 1024]], -1)
    shift_new = jnp.concatenate([p3[:, t - 1, C_R:C_R + 3 * BRANCH_W], _unpack_lora(t3[:, t - 1, T_L:T_L + 512])], axis=-1)
    ssm_new = h_fin.reshape(bsz, SSM_HEADS, HEAD_DIM, SSM_STATE)
    s6 = s_fin.reshape(bsz, N_PAIRS, 2, HEAD_DIM, 2, HEAD_DIM)
    s_heads = jnp.stack([s6[:, :, 0, :, 0, :], s6[:, :, 1, :, 1, :]], axis=2)
    rwkv_new = jnp.swapaxes(s_heads, -1, -2).reshape(bsz, RWKV_HEADS, HEAD_DIM, HEAD_DIM)
    return (o_att, o_ssm, o_rwkv), tuple(kv) + (conv_new, ssm_new, shift_new, rwkv_new)


def _sample_mixers(proj, tail, lw, slopes, st, layer, bd):
    (caches_t, conv_st, ssm_st, shift_st, rwkv_st) = st
    qkv = proj[:, :3 * BRANCH_W].reshape(bd, 3, 3, ATT_HEADS, HEAD_DIM, 1)
    o_att = att_sample(qkv, caches_t, slopes, layer).reshape(bd, ATT_OUT).astype(BF16)
    new_rows = tuple(qkv[:, part, g] for g in range(3) for part in (1, 2))

    xbc_new = jnp.concatenate([proj[:, C_X:C_X + BRANCH_W], tail[:, T_B:T_B + 1024]], axis=-1)
    cst = conv_st[layer]
    xc = ssd_sample_conv(xbc_new, jnp.swapaxes(cst, 0, 1), lw["conv_w"], lw["conv_b"])
    conv_new = jnp.concatenate([cst[:, 1:], xbc_new[:, None]], axis=1)
    x_col = xc[:, :BRANCH_W].reshape(bd, SSM_HEADS, HEAD_DIM, 1)
    z_col = proj[:, C_Z:C_Z + BRANCH_W].reshape(bd, SSM_HEADS, HEAD_DIM, 1)
    bmat = xc[:, BRANCH_W:BRANCH_W + 512].reshape(bd, SSM_GROUPS, 1, SSM_STATE)
    cmat = xc[:, BRANCH_W + 512:].reshape(bd, SSM_GROUPS, 1, SSM_STATE)
    dt_raw = tail[:, T_L + DT_LANE0:T_L + DT_LANE0 + SSM_HEADS].reshape(bd, SSM_HEADS, 1, 1)
    y_col, ssm_new = ssd_sample_step(ssm_st, x_col, z_col, bmat, cmat, dt_raw, lw, layer)
    o_ssm = y_col.reshape(bd, BRANCH_W).astype(BF16)

    sh = shift_st[layer]
    init = [sh[:, None, i * BRANCH_W:(i + 1) * BRANCH_W] for i in range(3)] + [_pack_lora(sh[:, None, 3 * BRANCH_W:])]
    r, ld, kmod, v, kkraw, a, gg = rwkv_prep(proj, tail, init, lw, bd, 1)
    rowf = lambda z: z.reshape(bd, RWKV_HEADS, 1, HEAD_DIM)
    colf = lambda z: z.reshape(bd, RWKV_HEADS, HEAD_DIM, 1)
    o_col, rwkv_new = rwkv_sample_step(rwkv_st, [rowf(r), rowf(ld), rowf(kmod), rowf(kkraw), rowf(a)],
                                       [colf(v), colf(gg)], lw, layer)
    o_rwkv = o_col.reshape(bd, BRANCH_W).astype(BF16)
    shift_new = jnp.concatenate([proj[:, C_R:C_R + 3 * BRANCH_W], _unpack_lora(tail[:, T_L:T_L + 512])], axis=-1)
    return (o_att, o_ssm, o_rwkv), new_rows, (conv_new, ssm_new, shift_new, rwkv_new)


def kernel(x_prompt, x_sample, cache_att_k0, cache_att_v0, cache_att_k1, cache_att_v1, cache_att_k2, cache_att_v2, state_ssm_conv, state_ssm, state_rwkv_shift, state_rwkv, ln1_g, w_in, ssm_conv_w, ssm_conv_b, ssm_dt_bias, ssm_a_log, ssm_d, ssm_norm_g, rwkv_mu, rwkv_w0, rwkv_w_up, rwkv_a0, rwkv_a_up, rwkv_g_up, rwkv_k_k, rwkv_k_a, rwkv_r_k, rwkv_ln_g, rwkv_ln_b, w_branch_att, w_branch_ssm, w_branch_rwkv, w_out, ln2_g, w_up, w_down, final_g):
    p = dict(ln1_g=ln1_g, w_in=w_in, ssm_conv_w=ssm_conv_w, ssm_conv_b=ssm_conv_b, ssm_dt_bias=ssm_dt_bias,
             ssm_a_log=ssm_a_log, ssm_d=ssm_d, ssm_norm_g=ssm_norm_g, rwkv_mu=rwkv_mu, rwkv_w0=rwkv_w0,
             rwkv_w_up=rwkv_w_up, rwkv_a0=rwkv_a0, rwkv_a_up=rwkv_a_up, rwkv_g_up=rwkv_g_up, rwkv_k_k=rwkv_k_k,
             rwkv_k_a=rwkv_k_a, rwkv_r_k=rwkv_r_k, rwkv_ln_g=rwkv_ln_g, rwkv_ln_b=rwkv_ln_b,
             w_branch_att=w_branch_att, w_branch_ssm=w_branch_ssm, w_branch_rwkv=w_branch_rwkv, w_out=w_out,
             ln2_g=ln2_g, w_up=w_up, w_down=w_down)
    depth = w_in.shape[0]
    bsz, t, _ = x_prompt.shape
    bd = x_sample.shape[0]
    slopes = _alibi_slopes()
    caches = (cache_att_k0, cache_att_v0, cache_att_k1, cache_att_v1, cache_att_k2, cache_att_v2)
    caches_t = tuple(jnp.transpose(c, (0, 1, 3, 4, 2)) for c in caches)
    st = (caches_t, state_ssm_conv, state_ssm, state_rwkv_shift, state_rwkv)
    xp = x_prompt.reshape(bsz * t, D_MODEL)
    xs = x_sample.reshape(bd, D_MODEL)
    big = _pack_big(p)
    p_new, s_new, s_rows = [], [], []
    hp = rmsnorm(xp, ln1_g[0], BF16)
    hs = rmsnorm(xs, ln1_g[0], BF16)
    for l in range(depth):
        lw = _pack_layer(l, p)
        last = l == depth - 1
        next_g, next_dtype = (final_g, F32) if last else (ln1_g[l + 1], BF16)
        xp, xs, hp, hs, sp, rows, ss = _layer(xp, xs, hp, hs, lw, big, l, slopes, st, bsz, t, bd, next_g, next_dtype)
        p_new.append(sp)
        s_new.append(ss)
        s_rows.append(rows)
    y_prompt = hp.reshape(bsz, t, D_MODEL)
    y_sample = hs.reshape(bd, 1, D_MODEL)
    stack = lambda per_layer, n: tuple(jnp.stack([s[i] for s in per_layer]) for i in range(n))
    new_rows = stack(s_rows, 6)
    s_caches = ()
    for g in range(3):
        upd = cache_update(caches_t[2 * g], caches_t[2 * g + 1], new_rows[2 * g], new_rows[2 * g + 1])
        s_caches += tuple(jnp.transpose(c, (0, 1, 4, 2, 3)) for c in upd)
    return (y_prompt, y_sample) + stack(p_new, 10) + s_caches + stack(s_new, 4)
```

```python
import functools

import jax
import jax.numpy as jnp
from jax import lax
from jax.experimental import pallas as pl
from jax.experimental.pallas import tpu as pltpu

F32 = jnp.float32
BF16 = jnp.bfloat16

D_MODEL = 2048
HEAD_DIM = 64
BRANCH_W = 1536
NORM_EPS = 1e-5
ATT_WINDOWS = (128, 512, 2048)
ATT_DILS = (1, 4, 16)
ATT_SPAN = 128
ATT_HEADS = 8
ATT_OUT = ATT_HEADS * HEAD_DIM
ATT_SCALE = HEAD_DIM ** -0.5
SSM_HEADS = 24
SSM_GROUPS = 4
SSM_STATE = 128
SSM_CONV = 4
SSM_CHUNK = 128
RWKV_HEADS = 24
LORA_W = 96
LORA_A = 96
LORA_G = 256
RWKV_LN_EPS = 64e-5
D_FF = 4 * D_MODEL
RWKV_CHUNK = 64
N_PAIRS = BRANCH_W // 128

LANES = 128
SUBLANES = 8
VMEM_LIMIT_BYTES = 56 * 1024 * 1024

C_Q, C_K, C_V, C_Z, C_X, C_R, C_RK, C_RV = (i * BRANCH_W for i in range(8))
C_G = 8 * BRANCH_W
N_MAIN = C_G + 3 * D_MODEL
T_B = 0
T_C = 512
T_L = 1024
N_TAIL = T_L + 512
DT_LANE0 = LORA_W

_O_QKV, _O_Z, _O_XBC, _O_DT, _O_RW, _O_GATE = 0, 4608, 6144, 8704, 8728, 13784
_O_LORA = _O_RW + 3 * BRANCH_W
_W_IN_COLS = _O_GATE + 3 * D_MODEL
PROJ_TN = 768


def _cparams(sem):
    return pltpu.CompilerParams(dimension_semantics=sem, vmem_limit_bytes=VMEM_LIMIT_BYTES)


def _softplus(x):
    return jnp.maximum(x, 0.0) + jnp.log(1.0 + jnp.exp(-jnp.abs(x)))


def _sigmoid(x):
    return 1.0 / (1.0 + jnp.exp(-x))


def _silu(x):
    return x * _sigmoid(x)


def _split(x, terms):
    out = []
    for _ in range(terms - 1):
        hi = x.astype(BF16)
        out.append(hi)
        x = x - hi.astype(F32)
    out.append(x.astype(BF16))
    return out


def _dot_exact_rhs(x, mat, terms=3):
    parts = [jnp.dot(t, mat, preferred_element_type=F32) for t in _split(x, terms)]
    return functools.reduce(lambda a, b: a + b, parts)


def _dot_exact_lhs(mat, x, terms=3):
    parts = [jnp.dot(mat, t, preferred_element_type=F32) for t in _split(x, terms)]
    return functools.reduce(lambda a, b: a + b, parts)


def _dot_nt(a, b):
    return lax.dot_general(a, b, (((1,), (1,)), ((), ())), preferred_element_type=F32)


def _dot_tn(a, b):
    return lax.dot_general(a, b, (((0,), (0,)), ((), ())), preferred_element_type=F32)


def _rmsnorm_kernel(x_ref, g_ref, o_ref):
    x = x_ref[...]
    y = x * lax.rsqrt(jnp.mean(x * x, axis=-1, keepdims=True) + NORM_EPS)
    o_ref[...] = (y * g_ref[...]).astype(o_ref.dtype)


def rmsnorm(x, g, out_dtype):
    m, d = x.shape
    tm = min(m, 512)
    return pl.pallas_call(
        _rmsnorm_kernel,
        grid=(m // tm,),
        in_specs=[pl.BlockSpec((tm, d), lambda i: (i, 0)), pl.BlockSpec((1, d), lambda i: (0, 0))],
        out_specs=pl.BlockSpec((tm, d), lambda i: (i, 0)),
        out_shape=jax.ShapeDtypeStruct((m, d), out_dtype),
        compiler_params=_cparams(("parallel",)),
        name="rmsnorm",
    )(x, g.reshape(1, d))


def _mm_kernel(*refs, nk, epi):
    refs = list(refs)
    x_ref, w_ref = refs[0:2]
    r_ref = refs.pop(2) if epi == "res" else None
    xs_ref = refs[2]
    rs_ref = refs.pop(3) if epi == "res" else None
    o_ref, os_ref = refs[3:5]
    acc_ref, accs_ref = (refs[5], refs[6]) if nk > 1 else (None, None)
    first_rows = pl.program_id(0) == 0
    k = pl.program_id(2)

    def finish(a, r, o):
        if epi == "res":
            a = r[...] + a
        elif epi == "relu2":
            a = jnp.square(jnp.maximum(a, 0.0))
        o[...] = a.astype(o.dtype)

    def accumulate(x, acc, r, o):
        part = jnp.dot(x[...], w_ref[...].astype(x.dtype), preferred_element_type=F32)
        if nk == 1:
            finish(part, r, o)
            return

        @pl.when(k == 0)
        def _():
            acc[...] = part

        @pl.when(k > 0)
        def _():
            acc[...] += part

        @pl.when(k == nk - 1)
        def _():
            finish(acc[...], r, o)

    accumulate(x_ref, acc_ref, r_ref, o_ref)

    @pl.when(first_rows)
    def _():
        accumulate(xs_ref, accs_ref, rs_ref, os_ref)


def _pick(n, cands):
    for c in cands:
        if n % c == 0:
            return c
    return n


def matmul(x, xs, w, layer, *, epi="none", res=None, res_s=None, out_dtype=F32):
    m, kdim = x.shape
    ms = xs.shape[0]
    n = w.shape[2]
    tm = min(m, 1024 if (epi == "res" or w.dtype == F32) else 2048)
    tn = _pick(n, (1536, 1024, 512, 256, 128))
    tk = min(kdim, 2048)
    nk = kdim // tk
    nj = n // tn
    side_col = lambda i, j: jnp.where(i == 0, j, nj - 1)
    in_specs = [pl.BlockSpec((tm, tk), lambda i, j, k: (i, k)),
                pl.BlockSpec((None, tk, tn), lambda i, j, k: (layer, k, j))]
    args = [x, w]
    if epi == "res":
        in_specs.append(pl.BlockSpec((tm, tn), lambda i, j, k: (i, j)))
        args.append(res)
    in_specs.append(pl.BlockSpec((ms, tk), lambda i, j, k: (0, k)))
    args.append(xs)
    if epi == "res":
        in_specs.append(pl.BlockSpec((ms, tn), lambda i, j, k: (0, side_col(i, j))))
        args.append(res_s)
    scratch = [pltpu.VMEM((tm, tn), F32), pltpu.VMEM((ms, tn), F32)] if nk > 1 else []
    return pl.pallas_call(
        functools.partial(_mm_kernel, nk=nk, epi=epi),
        grid=(m // tm, nj, nk),
        in_specs=in_specs,
        out_specs=[pl.BlockSpec((tm, tn), lambda i, j, k: (i, j)),
                   pl.BlockSpec((ms, tn), lambda i, j, k: (0, side_col(i, j)))],
        out_shape=[jax.ShapeDtypeStruct((m, n), out_dtype), jax.ShapeDtypeStruct((ms, n), out_dtype)],
        scratch_shapes=scratch,
        compiler_params=_cparams(("arbitrary", "arbitrary", "arbitrary")),
        name="matmul_" + epi,
    )(*args)


def _mm_res_norm_kernel(x_ref, w_ref, r_ref, xs_ref, rs_ref, g_ref, o_ref, on_ref, os_ref, osn_ref, *scr, nk):
    first_rows = pl.program_id(0) == 0
    k = pl.program_id(1)

    def finish(a, r, o, on):
        a = r[...] + a
        o[...] = a
        y = a * lax.rsqrt(jnp.mean(a * a, axis=-1, keepdims=True) + NORM_EPS)
        on[...] = (y * g_ref[...]).astype(on.dtype)

    def accumulate(x, acc, r, o, on):
        part = jnp.dot(x[...], w_ref[...], preferred_element_type=F32)
        if nk == 1:
            finish(part, r, o, on)
            return

        @pl.when(k == 0)
        def _():
            acc[...] = part

        @pl.when(k > 0)
        def _():
            acc[...] += part

        @pl.when(k == nk - 1)
        def _():
            finish(acc[...], r, o, on)

    acc_ref, accs_ref = scr if nk > 1 else (None, None)
    accumulate(x_ref, acc_ref, r_ref, o_ref, on_ref)

    @pl.when(first_rows)
    def _():
        accumulate(xs_ref, accs_ref, rs_ref, os_ref, osn_ref)


def matmul_res_norm(x, xs, w, layer, res, res_s, g, norm_dtype):
    m, kdim = x.shape
    ms = xs.shape[0]
    n = w.shape[2]
    tm = min(m, 512)
    tk = kdim if kdim <= 2048 else 1024
    nk = kdim // tk
    row = lambda width: pl.BlockSpec((tm, width), lambda i, k: (i, 0))
    side = lambda width: pl.BlockSpec((ms, width), lambda i, k: (0, 0))
    scratch = [pltpu.VMEM((tm, n), F32), pltpu.VMEM((ms, n), F32)] if nk > 1 else []
    return pl.pallas_call(
        functools.partial(_mm_res_norm_kernel, nk=nk),
        grid=(m // tm, nk),
        in_specs=[pl.BlockSpec((tm, tk), lambda i, k: (i, k)), pl.BlockSpec((None, tk, n), lambda i, k: (layer, k, 0)),
                  row(n), pl.BlockSpec((ms, tk), lambda i, k: (0, k)), side(n), pl.BlockSpec((1, n), lambda i, k: (0, 0))],
        out_specs=[row(n), row(n), side(n), side(n)],
        out_shape=[jax.ShapeDtypeStruct((m, n), F32), jax.ShapeDtypeStruct((m, n), norm_dtype),
                   jax.ShapeDtypeStruct((ms, n), F32), jax.ShapeDtypeStruct((ms, n), norm_dtype)],
        scratch_shapes=scratch,
        compiler_params=_cparams(("arbitrary", "arbitrary")),
        name="matmul_res_norm",
    )(x, w, res, xs, res_s, g.reshape(1, n))


def _proj_kernel(x_ref, xs_ref, w_ref, o_ref, os_ref, wb_scr):
    @pl.when(pl.program_id(1) == 0)
    def _():
        wb_scr[...] = w_ref[0].astype(BF16)
        os_ref[...] = _dot_nt(xs_ref[...], wb_scr[...])

    o_ref[...] = _dot_nt(x_ref[...], wb_scr[...])


def _main_row_start(j):
    t_rw, t_gate = C_R // PROJ_TN, C_G // PROJ_TN
    start = j * PROJ_TN + jnp.where(j >= t_rw, _O_RW - C_R, 0) + jnp.where(j >= t_gate, (_O_GATE - C_G) - (_O_RW - C_R), 0)
    return pl.multiple_of(start, SUBLANES)


def project(h, hs, w_t, layer, n_out, row_start):
    m, kdim = h.shape
    ms = hs.shape[0]
    tm = min(m, 2048)
    ell = pl.Element
    return pl.pallas_call(
        _proj_kernel,
        grid=(n_out // PROJ_TN, m // tm),
        in_specs=[pl.BlockSpec((tm, kdim), lambda j, i: (i, 0)),
                  pl.BlockSpec((ms, kdim), lambda j, i: (0, 0)),
                  pl.BlockSpec((ell(1), ell(PROJ_TN), ell(kdim)), lambda j, i: (layer, row_start(j), 0))],
        out_specs=[pl.BlockSpec((tm, PROJ_TN), lambda j, i: (i, j)), pl.BlockSpec((ms, PROJ_TN), lambda j, i: (0, j))],
        out_shape=[jax.ShapeDtypeStruct((m, n_out), F32), jax.ShapeDtypeStruct((ms, n_out), F32)],
        scratch_shapes=[pltpu.VMEM((PROJ_TN, kdim), BF16)],
        compiler_params=_cparams(("parallel", "arbitrary")),
        name="project",
    )(h, hs, w_t)


def project_main_and_tail(h, hs, big, layer):
    main, main_s = project(h, hs, big["w_in_t"], layer, N_MAIN, _main_row_start)
    tail, tail_s = project(h, hs, big["w_tail_t"], layer, N_TAIL, lambda j: pl.multiple_of(j * PROJ_TN, SUBLANES))
    return main, tail, main_s, tail_s


def _merge_kernel(oa_ref, os_ref, or_ref, ga_ref, gs_ref, gr_ref, oa2_ref, os2_ref, or2_ref, ga2_ref, gs2_ref,
                  gr2_ref, wa_ref, ws_ref, wr_ref, o_ref, o2_ref):
    f = lambda a, b: jnp.dot(a[...], b[...], preferred_element_type=F32)

    def merged(a_ref, s_ref, r_ref, g_a, g_s, g_r, out):
        acc = _sigmoid(g_a[...]) * f(a_ref, wa_ref)
        acc = acc + _sigmoid(g_s[...]) * f(s_ref, ws_ref)
        acc = acc + _sigmoid(g_r[...]) * f(r_ref, wr_ref)
        out[...] = acc.astype(out.dtype)

    merged(oa_ref, os_ref, or_ref, ga_ref, gs_ref, gr_ref, o_ref)

    @pl.when(pl.program_id(1) == 0)
    def _():
        merged(oa2_ref, os2_ref, or2_ref, ga2_ref, gs2_ref, gr2_ref, o2_ref)


def merge_branches(outs, proj, outs_s, proj_s, wa, ws, wr, layer):
    m = outs[0].shape[0]
    ms = outs_s[0].shape[0]
    tm = min(m, 512)
    tn = 1024
    gb = C_G // tn
    nj = D_MODEL // tn
    widths = (ATT_OUT, BRANCH_W, BRANCH_W)
    row = lambda width: pl.BlockSpec((tm, width), lambda j, i: (i, 0))
    gate = lambda g: pl.BlockSpec((tm, tn), lambda j, i, g=g: (i, gb + g * nj + j))
    row_s = lambda width: pl.BlockSpec((ms, width), lambda j, i: (0, 0))
    gate_s = lambda g: pl.BlockSpec((ms, tn), lambda j, i, g=g: (0, gb + g * nj + j))
    col = lambda k: pl.BlockSpec((None, k, tn), lambda j, i: (layer, 0, j))
    return pl.pallas_call(
        _merge_kernel,
        grid=(nj, m // tm),
        in_specs=[row(w) for w in widths] + [gate(g) for g in range(3)] + [row_s(w) for w in widths]
        + [gate_s(g) for g in range(3)] + [col(w) for w in widths],
        out_specs=[pl.BlockSpec((tm, tn), lambda j, i: (i, j)), pl.BlockSpec((ms, tn), lambda j, i: (0, j))],
        out_shape=[jax.ShapeDtypeStruct((m, D_MODEL), BF16), jax.ShapeDtypeStruct((ms, D_MODEL), BF16)],
        compiler_params=_cparams(("parallel", "arbitrary")),
        name="merge_branches",
    )(*outs, proj, proj, proj, *outs_s, proj_s, proj_s, proj_s, wa, ws, wr)


def _att_prompt_kernel(sl_ref, *refs, nblks):
    o_ref = refs[15]
    scr = refs[16:]
    for g in range(3):
        _att_group(sl_ref, *refs[5 * g:5 * g + 5], scr[2 * g], scr[2 * g + 1], d=ATT_DILS[g], nblk=nblks[g], g=g)
    a, b, c = scr[1][...], scr[3][...], scr[5][...]
    mx = jnp.maximum(jnp.maximum(a, b), c)
    ea, eb, ec = jnp.exp(a - mx), jnp.exp(b - mx), jnp.exp(c - mx)
    num = ea * scr[0][...] + eb * scr[2][...] + ec * scr[4][...]
    o_ref[...] = (num / (ea + eb + ec)).astype(o_ref.dtype)


def _att_group(sl_ref, q_ref, kc_ref, kp_ref, vc_ref, vp_ref, o_ref, lse_ref, *, d, nblk, g):
    j = pl.program_id(1)
    hp = pl.program_id(2)
    lane = lax.broadcasted_iota(jnp.int32, (1, LANES), 1)
    head0 = lane < HEAD_DIM
    qi = lax.broadcasted_iota(jnp.int32, (ATT_SPAN, 2 * ATT_SPAN), 0)
    kj = lax.broadcasted_iota(jnp.int32, (ATT_SPAN, 2 * ATT_SPAN), 1)
    delta = qi - kj + ATT_SPAN
    band = (delta >= 0) & (delta <= ATT_SPAN)
    first_ok = band & ((kj >= ATT_SPAN) | (j > 0))
    dist = (delta * d).astype(F32)
    slopes = (sl_ref[g, 2 * hp], sl_ref[g, 2 * hp + 1])

    def rows(base):
        return pl.ds(base, ATT_SPAN) if d == 1 else pl.ds(base, ATT_SPAN, stride=d)

    for ub in range(nblk):
        valid = first_ok if ub == 0 else band
        for r in range(d):
            base = ub * ATT_SPAN * d + r
            q = q_ref[rows(base), :]
            if ub == 0:
                kp, vp = kp_ref[rows(r), :], vp_ref[rows(r), :]
            else:
                kp, vp = kc_ref[rows(base - ATT_SPAN * d), :], vc_ref[rows(base - ATT_SPAN * d), :]
            k = jnp.concatenate([kp, kc_ref[rows(base), :]], axis=0).astype(BF16)
            v = jnp.concatenate([vp, vc_ref[rows(base), :]], axis=0).astype(BF16)
            outs, lses = [], []
            for e in range(2):
                sel = head0 if e == 0 else jnp.logical_not(head0)
                qe = jnp.where(sel, q, 0.0).astype(BF16)
                s = _dot_nt(qe, k) * ATT_SCALE
                s = jnp.where(valid, s - slopes[e] * dist, -jnp.inf)
                mx = jnp.max(s, axis=-1, keepdims=True)
                p = jnp.exp(s - mx)
                den = jnp.sum(p, axis=-1, keepdims=True)
                outs.append(jnp.dot(p.astype(BF16), v, preferred_element_type=F32) / den)
                lses.append(mx + jnp.log(den))
            o_ref[rows(base), :] = jnp.where(head0, outs[0], outs[1])
            lse_ref[rows(base), :] = jnp.where(head0, lses[0], lses[1])


def att_prompt(proj, slopes, bsz, t):
    tb = ATT_SPAN * max(ATT_DILS)
    nt = t // tb
    in_specs = [pl.BlockSpec(memory_space=pltpu.SMEM)]
    nblks = []
    for g in range(3):
        sb = ATT_SPAN * ATT_DILS[g]
        nblk = tb // sb
        nblks.append(nblk)
        cq, ck, cv = (C_Q + g * ATT_OUT) // LANES, (C_K + g * ATT_OUT) // LANES, (C_V + g * ATT_OUT) // LANES
        cur = lambda c0: pl.BlockSpec((tb, LANES), lambda b, j, h, c0=c0: (b * nt + j, c0 + h))
        prev = lambda c0, sb=sb, nblk=nblk: pl.BlockSpec(
            (sb, LANES), lambda b, j, h, c0=c0: (jnp.maximum((b * nt + j) * nblk - 1, 0), c0 + h))
        in_specs += [cur(cq), cur(ck), prev(ck), cur(cv), prev(cv)]
    return pl.pallas_call(
        functools.partial(_att_prompt_kernel, nblks=tuple(nblks)),
        grid=(bsz, nt, ATT_HEADS // 2),
        in_specs=in_specs,
        out_specs=pl.BlockSpec((tb, LANES), lambda b, j, h: (b * nt + j, h)),
        out_shape=jax.ShapeDtypeStruct((bsz * t, ATT_OUT), BF16),
        scratch_shapes=[pltpu.VMEM((tb, LANES), F32)] * 6,
        compiler_params=_cparams(("parallel", "parallel", "parallel")),
        name="att_prompt",
    )(slopes, *([proj] * 15))


def _att_sample_kernel(sl_ref, qkv_ref, k0, v0, k1, v1, k2, v2, o_ref):
    caches = ((k0, v0), (k1, v1), (k2, v2))
    outs, lses = [], []
    for g in range(3):
        d = ATT_DILS[g]
        assert d & (d - 1) == 0
        win = caches[g][0].shape[-1]
        q, kn, vn = qkv_ref[0, g], qkv_ref[1, g], qkv_ref[2, g]
        kc = caches[g][0][...]
        vc = caches[g][1][...]
        pos = lax.broadcasted_iota(jnp.int32, (1, 1, win), 2)
        on_grid = (pos & (d - 1)) == 0
        dist = (win - pos).astype(F32)
        s = jnp.sum(kc * q, axis=1, keepdims=True) * ATT_SCALE - sl_ref[g] * dist
        s = jnp.where(on_grid, s, -jnp.inf)
        s_new = jnp.sum(kn * q, axis=1, keepdims=True) * ATT_SCALE
        mx = jnp.maximum(jnp.max(s, axis=2, keepdims=True), s_new)
        p = jnp.exp(s - mx)
        p_new = jnp.exp(s_new - mx)
        den = jnp.sum(p, axis=2, keepdims=True) + p_new
        outs.append((jnp.sum(vc * p, axis=2, keepdims=True) + p_new * vn) / den)
        lses.append(mx + jnp.log(den))
    mx = jnp.maximum(jnp.maximum(lses[0], lses[1]), lses[2])
    es = [jnp.exp(l - mx) for l in lses]
    o_ref[...] = (es[0] * outs[0] + es[1] * outs[1] + es[2] * outs[2]) / (es[0] + es[1] + es[2])


def att_sample(qkv, caches_t, slopes, layer):
    bd = qkv.shape[0]
    in_specs = [pl.BlockSpec((3, ATT_HEADS, 1, 1), lambda b: (0, 0, 0, 0)),
                pl.BlockSpec((None, 3, 3, ATT_HEADS, HEAD_DIM, 1), lambda b: (b, 0, 0, 0, 0, 0))]
    for c in caches_t:
        in_specs.append(pl.BlockSpec((None, None, ATT_HEADS, HEAD_DIM, c.shape[-1]), lambda b: (layer, b, 0, 0, 0)))
    return pl.pallas_call(
        _att_sample_kernel,
        grid=(bd,),
        in_specs=in_specs,
        out_specs=pl.BlockSpec((None, ATT_HEADS, HEAD_DIM, 1), lambda b: (b, 0, 0, 0)),
        out_shape=jax.ShapeDtypeStruct((bd, ATT_HEADS, HEAD_DIM, 1), F32),
        compiler_params=_cparams(("parallel",)),
        name="att_sample",
    )(slopes.reshape(3, ATT_HEADS, 1, 1), qkv, *caches_t)


def _ssd_prompt_kernel(x_ref, b_ref, c_ref, z_ref, l_ref, cwx_ref, cwb_ref, cwc_ref, cbx_ref, cbb_ref, cbc_ref,
                       dtb_ref, aneg_ref, dskip_ref, ng_ref, o_ref, hout_ref, xpad, bpad, cpad, h_scr, y_scr, *, nc):
    c = pl.program_id(1)
    L = SSM_CHUNK

    @pl.when(c == 0)
    def _():
        xpad[0:SUBLANES, :] = jnp.zeros((SUBLANES, xpad.shape[1]), F32)
        bpad[0:SUBLANES, :] = jnp.zeros((SUBLANES, bpad.shape[1]), F32)
        cpad[0:SUBLANES, :] = jnp.zeros((SUBLANES, cpad.shape[1]), F32)
        h_scr[...] = jnp.zeros_like(h_scr)

    def conv_silu(pad, src_ref, w_ref, bias_ref):
        pad[SUBLANES:SUBLANES + L, :] = src_ref[...]
        acc = bias_ref[...] + w_ref[0:1, :] * pad[pl.ds(SUBLANES - 3, L), :]
        for jj in range(1, SSM_CONV):
            acc = acc + w_ref[jj:jj + 1, :] * pad[pl.ds(SUBLANES - 3 + jj, L), :]
        pad[0:SUBLANES, :] = pad[L:L + SUBLANES, :]
        return _silu(acc)

    xs = conv_silu(xpad, x_ref, cwx_ref, cbx_ref)
    bm = conv_silu(bpad, b_ref, cwb_ref, cbb_ref).astype(BF16)
    cm = conv_silu(cpad, c_ref, cwc_ref, cbc_ref).astype(BF16)

    lane = lax.broadcasted_iota(jnp.int32, (1, LANES), 1)
    head0 = lane < HEAD_DIM
    sub_head0 = lax.broadcasted_iota(jnp.int32, (LANES, 1), 0) < HEAD_DIM
    dt_lanes = (lane >= DT_LANE0) & (lane < DT_LANE0 + SSM_HEADS)
    dtv = jnp.where(dt_lanes, _softplus(l_ref[:, 0:LANES] + dtb_ref[...]), 0.0)
    da = dtv * aneg_ref[...]
    ri = lax.broadcasted_iota(jnp.int32, (L, L), 0)
    ci = lax.broadcasted_iota(jnp.int32, (L, L), 1)
    causal = ri >= ci
    a_cum = _dot_exact_lhs(causal.astype(BF16), da)
    a_cum_t = a_cum.T
    dt_t = dtv.T

    for g in range(SSM_GROUPS):
        bg = bm[:, g * SSM_STATE:(g + 1) * SSM_STATE]
        cg = cm[:, g * SSM_STATE:(g + 1) * SSM_STATE]
        cb = _dot_nt(cg, bg)
        for pp in range(3):
            pair = g * 3 + pp
            xp = xs[:, pair * LANES:(pair + 1) * LANES]
            xpb = xp.astype(BF16)
            yd, sc, ea, cd = [], [], [], []
            for e in range(2):
                hl = DT_LANE0 + 2 * pair + e
                ac_col = a_cum[:, hl:hl + 1]
                ac_row = a_cum_t[hl:hl + 1, :]
                a_last = ac_col[L - 1:L, :]
                dec = jnp.exp(jnp.where(causal, ac_col - ac_row, -jnp.inf))
                wm = (cb * dec * dt_t[hl:hl + 1, :]).astype(BF16)
                yd.append(jnp.dot(wm, xpb, preferred_element_type=F32))
                sc.append(jnp.exp(a_last - ac_col) * dtv[:, hl:hl + 1])
                ea.append(jnp.exp(ac_col))
                cd.append(jnp.exp(a_last))
            hp = h_scr[pair]
            y_off = _dot_nt(cg, hp.astype(BF16)) * jnp.where(head0, ea[0], ea[1])
            xw = (xp * jnp.where(head0, sc[0], sc[1])).astype(BF16)
            h_scr[pair] = jnp.where(sub_head0, cd[0], cd[1]) * hp + _dot_tn(xw, bg)
            y_scr[:, pair * LANES:(pair + 1) * LANES] = jnp.where(head0, yd[0], yd[1]) + y_off

    y = (y_scr[...] + dskip_ref[...] * xs) * _silu(z_ref[...])
    gw = BRANCH_W // SSM_GROUPS
    for g in range(SSM_GROUPS):
        yg = y[:, g * gw:(g + 1) * gw]
        yg = yg * lax.rsqrt(jnp.mean(yg * yg, axis=-1, keepdims=True) + NORM_EPS)
        o_ref[:, g * gw:(g + 1) * gw] = (yg * ng_ref[:, g * gw:(g + 1) * gw]).astype(o_ref.dtype)

    @pl.when(c == nc - 1)
    def _():
        hout_ref[...] = h_scr[...]


def ssd_prompt(proj, tail, lw, bsz, t):
    nc = t // SSM_CHUNK
    L = SSM_CHUNK
    rowblk = lambda width, c0: pl.BlockSpec((L, width), lambda b, c: (b * nc + c, c0 // width))
    par = lambda a: pl.BlockSpec(a.shape, lambda b, c: (0,) * a.ndim)
    params = [lw["cw_x"], lw["cw_b"], lw["cw_c"], lw["cb_x"], lw["cb_b"], lw["cb_c"], lw["dt_bias"], lw["a_neg"],
              lw["d_skip"], lw["ssm_norm_g"]]
    return pl.pallas_call(
        functools.partial(_ssd_prompt_kernel, nc=nc),
        grid=(bsz, nc),
        in_specs=[rowblk(BRANCH_W, C_X), rowblk(512, T_B), rowblk(512, T_C), rowblk(BRANCH_W, C_Z), rowblk(512, T_L)]
        + [par(a) for a in params],
        out_specs=[pl.BlockSpec((L, BRANCH_W), lambda b, c: (b * nc + c, 0)),
                   pl.BlockSpec((None, N_PAIRS, LANES, SSM_STATE), lambda b, c: (b, 0, 0, 0))],
        out_shape=[jax.ShapeDtypeStruct((bsz * t, BRANCH_W), BF16),
                   jax.ShapeDtypeStruct((bsz, N_PAIRS, LANES, SSM_STATE), F32)],
        scratch_shapes=[pltpu.VMEM((L + SUBLANES, BRANCH_W), F32), pltpu.VMEM((L + SUBLANES, 512), F32),
                        pltpu.VMEM((L + SUBLANES, 512), F32), pltpu.VMEM((N_PAIRS, LANES, SSM_STATE), F32),
                        pltpu.VMEM((L, BRANCH_W), F32)],
        compiler_params=_cparams(("parallel", "arbitrary")),
        name="ssd_prompt",
    )(proj, tail, tail, proj, tail, *params)


def _ssd_sample_conv_kernel(new_ref, st_ref, w_ref, b_ref, o_ref):
    acc = b_ref[...] + w_ref[SSM_CONV - 1:SSM_CONV, :] * new_ref[...]
    for jj in range(SSM_CONV - 1):
        acc = acc + w_ref[jj:jj + 1, :] * st_ref[jj]
    o_ref[...] = _silu(acc)


def ssd_sample_conv(xbc_new, conv_st_t, w, b):
    bd, cdim = xbc_new.shape
    return pl.pallas_call(
        _ssd_sample_conv_kernel,
        out_shape=jax.ShapeDtypeStruct((bd, cdim), F32),
        name="ssd_sample_conv",
    )(xbc_new, conv_st_t, w, b.reshape(1, cdim))


def _ssd_sample_step_kernel(h_ref, x_ref, z_ref, b_ref, c_ref, dt_ref, dtb_ref, alog_ref, d_ref, ng_ref, o_ref, hout_ref):
    h = h_ref[...]
    dt = _softplus(dt_ref[...] + dtb_ref[...])
    da = jnp.exp(dt * (-jnp.exp(alog_ref[...])))
    rep = SSM_HEADS // SSM_GROUPS
    bh = jnp.broadcast_to(b_ref[...][:, None], (SSM_GROUPS, rep, 1, SSM_STATE)).reshape(SSM_HEADS, 1, SSM_STATE)
    ch = jnp.broadcast_to(c_ref[...][:, None], (SSM_GROUPS, rep, 1, SSM_STATE)).reshape(SSM_HEADS, 1, SSM_STATE)
    x = x_ref[...]
    hn = da * h + (dt * x) * bh
    hout_ref[...] = hn
    y = jnp.sum(hn * ch, axis=-1, keepdims=True)
    y = (y + d_ref[...] * x) * _silu(z_ref[...])
    y4 = y.reshape(SSM_GROUPS, rep, HEAD_DIM, 1)
    ms = jnp.sum(jnp.sum(y4 * y4, axis=2, keepdims=True), axis=1, keepdims=True) / (rep * HEAD_DIM)
    y4 = y4 * lax.rsqrt(ms + NORM_EPS)
    o_ref[...] = y4.reshape(SSM_HEADS, HEAD_DIM, 1) * ng_ref[...]


def ssd_sample_step(h, x_col, z_col, bmat, cmat, dt_raw, lw, layer):
    bd = x_col.shape[0]
    col = pl.BlockSpec((None, SSM_HEADS, HEAD_DIM, 1), lambda b: (b, 0, 0, 0))
    grp = pl.BlockSpec((None, SSM_GROUPS, 1, SSM_STATE), lambda b: (b, 0, 0, 0))
    hd1 = pl.BlockSpec((SSM_HEADS, 1, 1), lambda b: (0, 0, 0))
    return pl.pallas_call(
        _ssd_sample_step_kernel,
        grid=(bd,),
        in_specs=[pl.BlockSpec((None, None, SSM_HEADS, HEAD_DIM, SSM_STATE), lambda b: (layer, b, 0, 0, 0)),
                  col, col, grp, grp, pl.BlockSpec((None, SSM_HEADS, 1, 1), lambda b: (b, 0, 0, 0)), hd1, hd1, hd1,
                  pl.BlockSpec((SSM_HEADS, HEAD_DIM, 1), lambda b: (0, 0, 0))],
        out_specs=[col, pl.BlockSpec((None, SSM_HEADS, HEAD_DIM, SSM_STATE), lambda b: (b, 0, 0, 0))],
        out_shape=[jax.ShapeDtypeStruct((bd, SSM_HEADS, HEAD_DIM, 1), F32),
                   jax.ShapeDtypeStruct((bd, SSM_HEADS, HEAD_DIM, SSM_STATE), F32)],
        compiler_params=_cparams(("parallel",)),
        name="ssd_sample_step",
    )(h, x_col, z_col, bmat, cmat, dt_raw, lw["dt_bias_h"], lw["a_log_h"], lw["d_h"], lw["ssm_norm_g_col"])


def _rwkv_mix(i, tm, x_refs, h_refs, i_refs, mu_refs, w0_ref, wup_ref, a0_ref, aup_ref, gup_ref):
    def shift(x_ref, h_ref, i_ref, mu_ref):
        x = x_ref[...]
        hrows = h_ref.shape[0]
        prev0 = jnp.where(i == 0, i_ref[...], h_ref[hrows - 1:hrows, :])
        if tm == 1:
            xp = prev0
        else:
            row = lax.broadcasted_iota(jnp.int32, (tm, 1), 0)
            xp = jnp.where(row == 0, prev0, pltpu.roll(x, 1, 0))
        return x + mu_ref[...] * (xp - x)

    ur, uk, uv, ul = (shift(*refs) for refs in zip(x_refs, h_refs, i_refs, mu_refs))
    f = lambda a, w_ref: jnp.dot(a.astype(BF16), w_ref[...], preferred_element_type=F32)
    w_log = -_softplus(-(w0_ref[...] + f(jnp.tanh(ul[:, 0:LANES]), wup_ref))) - 0.5
    a = _sigmoid(a0_ref[...] + f(ul[:, LANES:2 * LANES], aup_ref))
    g = f(_sigmoid(ul[:, 2 * LANES:4 * LANES]), gup_ref)
    return ur, uk, uv, -jnp.exp(w_log), a, g


def _rwkv_prep_kernel(*refs, tm):
    x_refs, h_refs, i_refs, mu_refs = refs[0:4], refs[4:8], refs[8:12], refs[12:16]
    w0_ref, wup_ref, a0_ref, aup_ref, gup_ref, kk_ref, ka_ref = refs[16:23]
    or_ref, old_ref, ok_ref, ov_ref, okk_ref, oa_ref, og_ref = refs[23:30]
    ur, uk, uv, ld, a, g = _rwkv_mix(pl.program_id(1), tm, x_refs, h_refs, i_refs, mu_refs,
                                     w0_ref, wup_ref, a0_ref, aup_ref, gup_ref)
    or_ref[...] = ur
    old_ref[...] = ld
    ok_ref[...] = uk * (1.0 + (a - 1.0) * ka_ref[...])
    ov_ref[...] = uv
    okk_ref[...] = uk * kk_ref[...]
    oa_ref[...] = a
    og_ref[...] = g


def _rwkv_prep_prompt_kernel(*refs, tm):
    x_refs, h_refs, i_refs, mu_refs = refs[0:4], refs[4:8], refs[8:12], refs[12:16]
    w0_ref, wup_ref, a0_ref, aup_ref, gup_ref, kk_ref, ka_ref, rk_ref = refs[16:24]
    opt_ref, ort_ref, oqh_ref, okh_ref, ov_ref, obv_ref, og_ref, oel_ref = refs[24:32]
    L = RWKV_CHUNK
    ur, uk, uv, ld, a, g = _rwkv_mix(pl.program_id(1), tm, x_refs, h_refs, i_refs, mu_refs,
                                     w0_ref, wup_ref, a0_ref, aup_ref, gup_ref)
    kmod = uk * (1.0 + (a - 1.0) * ka_ref[...])
    kkraw = uk * kk_ref[...]
    assert tm == 2 * L
    ri = lax.broadcasted_iota(jnp.int32, (tm, tm), 0)
    ci = lax.broadcasted_iota(jnp.int32, (tm, tm), 1)
    tril = ((ri >= ci) & ((ri < L) == (ci < L))).astype(BF16)
    cum = _dot_exact_lhs(tril, ld, 2)
    e_in = jnp.exp(cum)
    e_inv = jnp.exp(-cum)
    e_ex = jnp.exp(cum - ld)
    ort_ref[...] = (ur * e_in).astype(ort_ref.dtype)
    okh_ref[...] = (kmod * e_inv).astype(okh_ref.dtype)
    ov_ref[...] = uv.astype(ov_ref.dtype)
    og_ref[...] = g.astype(og_ref.dtype)
    for c in range(tm // L):
        oel_ref[c] = e_in[c * L + L - 1:c * L + L, :]
    li = lax.broadcasted_iota(jnp.int32, (LANES, LANES), 0)
    lj = lax.broadcasted_iota(jnp.int32, (LANES, LANES), 1)
    bd_ones = ((li < HEAD_DIM) == (lj < HEAD_DIM)).astype(BF16)
    rkr = ur * kmod * rk_ref[...]
    for p in range(N_PAIRS):
        cs = slice(p * LANES, (p + 1) * LANES)
        kkp = kkraw[:, cs]
        kkn = kkp / jnp.maximum(jnp.sqrt(_dot_exact_rhs(kkp * kkp, bd_ones, 1)), 1e-12)
        opt_ref[:, cs] = (-kkn * e_ex[:, cs]).astype(opt_ref.dtype)
        oqh_ref[:, cs] = (kkn * a[:, cs] * e_inv[:, cs]).astype(oqh_ref.dtype)
        obv_ref[:, cs] = (_dot_exact_rhs(rkr[:, cs], bd_ones, 1) * uv[:, cs]).astype(obv_ref.dtype)


def _rwkv_prep_call(kernel_fn, proj, tail, init, params, bsz, t, tm, out_specs, out_shape, name):
    nt = t // tm
    hrows = SUBLANES if t >= SUBLANES else t
    hper = tm // hrows
    proj = proj.reshape(bsz, t, N_MAIN)
    tail = tail.reshape(bsz, t, N_TAIL)
    blk = lambda width, c0: pl.BlockSpec((None, tm, width), lambda b, i: (b, i, c0 // width))
    halo = lambda width, c0: pl.BlockSpec(
        (None, hrows, width), lambda b, i: (b, jnp.maximum(i * hper - 1, 0), c0 // width))
    ini = lambda width: pl.BlockSpec((None, 1, width), lambda b, i: (b, 0, 0))
    par = lambda a: pl.BlockSpec(a.shape, lambda b, i: (0,) * a.ndim)
    secs = ((BRANCH_W, C_R), (BRANCH_W, C_RK), (BRANCH_W, C_RV), (512, T_L))
    srcs = (proj, proj, proj, tail)
    return pl.pallas_call(
        functools.partial(kernel_fn, tm=tm),
        grid=(bsz, nt),
        in_specs=[blk(*s) for s in secs] + [halo(*s) for s in secs] + [ini(s[0]) for s in secs]
        + [par(a) for a in params],
        out_specs=out_specs,
        out_shape=out_shape,
        compiler_params=_cparams(("parallel", "arbitrary")),
        name=name,
    )(*srcs, *srcs, *init, *params)


def _rwkv_prep_params(lw):
    return [lw["mu_r"], lw["mu_k"], lw["mu_v"], lw["mu_l"], lw["w0"], lw["w_up"], lw["a0"], lw["a_up"], lw["g_up"],
            lw["k_k"], lw["k_a"]]


def rwkv_prep(proj, tail, init, lw, bsz, t):
    tm = min(t, 128)
    out = pl.BlockSpec((None, tm, BRANCH_W), lambda b, i: (b, i, 0))
    shp = jax.ShapeDtypeStruct((bsz, t, BRANCH_W), F32)
    outs = _rwkv_prep_call(_rwkv_prep_kernel, proj, tail, init, _rwkv_prep_params(lw), bsz, t, tm, [out] * 7,
                           [shp] * 7, "rwkv_prep")
    return [o.reshape(bsz * t, BRANCH_W) for o in outs]


def rwkv_prep_prompt(proj, tail, init, lw, bsz, t):
    tm = 128
    nch = tm // RWKV_CHUNK
    out = pl.BlockSpec((None, tm, BRANCH_W), lambda b, i: (b, i, 0))
    shp = jax.ShapeDtypeStruct((bsz, t, BRANCH_W), BF16)
    el_spec = pl.BlockSpec((None, nch, 1, BRANCH_W), lambda b, i: (b, i, 0, 0))
    el_shape = jax.ShapeDtypeStruct((bsz, t // RWKV_CHUNK, 1, BRANCH_W), F32)
    outs = _rwkv_prep_call(_rwkv_prep_prompt_kernel, proj, tail, init, _rwkv_prep_params(lw) + [lw["r_k_row"]], bsz, t,
                           tm, [out] * 7 + [el_spec], [shp] * 7 + [el_shape], "rwkv_prep_prompt")
    return [o.reshape(bsz * t, BRANCH_W) for o in outs[:7]] + [outs[7].reshape(bsz * t // RWKV_CHUNK, 1, BRANCH_W)]


def _rwkv_chain_kernel(pt_ref, rt_ref, qh_ref, kh_ref, v_ref, bv_ref, g_ref, el_ref, lng_ref, lnb_ref, o_ref, sout_ref,
                       s_scr, *, nchunk, nt, npair):
    tstep = pl.program_id(2)
    L = RWKV_CHUNK
    L2 = 2 * L

    @pl.when(tstep == 0)
    def _():
        s_scr[...] = jnp.zeros_like(s_scr)

    lane = lax.broadcasted_iota(jnp.int32, (1, LANES), 1)
    head0 = lane < HEAD_DIM
    ri = lax.broadcasted_iota(jnp.int32, (L2, L2), 0)
    ci = lax.broadcasted_iota(jnp.int32, (L2, L2), 1)
    same = (ri < L) == (ci < L)
    strict = same & (ri > ci)
    incl = same & (ri >= ci)
    eye = (ri == ci).astype(F32)
    bd_ones = same.astype(BF16)
    bf = lambda x: x.astype(BF16)
    dot = lambda a, b: jnp.dot(bf(a), bf(b), preferred_element_type=F32)
    zero = jnp.zeros((), BF16)

    def stack(x):
        return jnp.concatenate([jnp.where(head0, x, zero), jnp.where(head0, zero, x)], axis=0)

    probs = [(c, p) for c in range(nchunk) for p in range(npair)]
    rows = lambda c: pl.ds(c * L, L)
    cols = lambda p: slice(p * LANES, (p + 1) * LANES)
    pm = {k: stack(pt_ref[rows(k[0]), cols(k[1])]) for k in probs}
    rm = {k: stack(rt_ref[rows(k[0]), cols(k[1])]) for k in probs}
    qk = {k: jnp.concatenate([stack(qh_ref[rows(k[0]), cols(k[1])]), stack(kh_ref[rows(k[0]), cols(k[1])])], axis=0)
          for k in probs}
    vm = {k: stack(v_ref[rows(k[0]), cols(k[1])]) for k in probs}
    x = {k: _dot_nt(jnp.concatenate([pm[k], rm[k]], axis=0), qk[k]) for k in probs}
    sums = lambda v: jnp.dot(bf(v), bd_ones, preferred_element_type=F32)
    n_pow = {k: jnp.where(strict, x[k][0:L2, 0:L2], 0.0) for k in probs}
    a_kp = {k: bf(jnp.where(strict, x[k][0:L2, L2:2 * L2], 0.0)) for k in probs}
    r_q = {k: bf(jnp.where(incl, x[k][L2:2 * L2, 0:L2], 0.0)) for k in probs}
    r_k = {k: bf(jnp.where(incl, x[k][L2:2 * L2, L2:2 * L2], 0.0)) for k in probs}
    t_inv = {k: eye + n_pow[k] for k in probs}
    for _ in range(5):
        n_pow = {k: dot(n_pow[k], n_pow[k]) for k in probs}
        t_inv = {k: t_inv[k] + dot(n_pow[k], t_inv[k]) for k in probs}
    t_inv = {k: bf(t_inv[k]) for k in probs}
    pr = {k: jnp.concatenate([pm[k], rm[k]], axis=0) for k in probs}
    akv = {k: dot(jnp.concatenate([a_kp[k], r_k[k]], axis=0), vm[k]) for k in probs}

    state = [s_scr[p] for p in range(npair)]
    ys = {}
    for c in range(nchunk):
        ks = [(c, p) for p in range(npair)]
        ps = {k: dot(pr[k], bf(state[k[1]])) + akv[k] for k in ks}
        rhs = {k: ps[k][0:L2, :] for k in ks}
        rs = {k: ps[k][L2:2 * L2, :] for k in ks}
        u = {k: bf(dot(t_inv[k], rhs[k])) for k in ks}
        upd = {k: _dot_tn(qk[k], jnp.concatenate([u[k], vm[k]], axis=0)) for k in ks}
        for k in ks:
            ys[k] = rs[k] + dot(r_q[k], u[k])
            w_col = jnp.sum(eye * el_ref[c, :, cols(k[1])], axis=1, keepdims=True)
            state[k[1]] = w_col * (state[k[1]] + upd[k])
    for p in range(npair):
        s_scr[p] = state[p]

    y = {k: ys[k][0:L, :] + ys[k][L:L2, :] for k in probs}
    yc = {k: y[k] - sums(y[k]) * (1.0 / HEAD_DIM) for k in probs}
    var = {k: sums(yc[k] * yc[k]) * (1.0 / HEAD_DIM) for k in probs}
    for k in probs:
        cs = cols(k[1])
        yn = yc[k] * lax.rsqrt(var[k] + RWKV_LN_EPS) * lng_ref[:, cs] + lnb_ref[:, cs]
        o_ref[rows(k[0]), cs] = ((yn + bv_ref[rows(k[0]), cs].astype(F32)) * g_ref[rows(k[0]), cs].astype(F32)
                                 ).astype(o_ref.dtype)

    @pl.when(tstep == nt - 1)
    def _():
        sout_ref[...] = s_scr[...]


def rwkv_prompt(parts, lw, bsz, t, npair=12, nchunk=4):
    tb = nchunk * RWKV_CHUNK
    nt = t // tb
    wl = npair * LANES
    blk = pl.BlockSpec((tb, wl), lambda b, p, i: (b * nt + i, p))
    par = pl.BlockSpec((1, wl), lambda b, p, i: (0, p))
    el = pl.BlockSpec((nchunk, 1, wl), lambda b, p, i: (b * nt + i, 0, p))
    return pl.pallas_call(
        functools.partial(_rwkv_chain_kernel, nchunk=nchunk, nt=nt, npair=npair),
        grid=(bsz, N_PAIRS // npair, nt),
        in_specs=[blk] * 7 + [el, par, par],
        out_specs=[blk, pl.BlockSpec((None, npair, LANES, LANES), lambda b, p, i: (b, p, 0, 0))],
        out_shape=[jax.ShapeDtypeStruct((bsz * t, BRANCH_W), BF16),
                   jax.ShapeDtypeStruct((bsz, N_PAIRS, LANES, LANES), F32)],
        scratch_shapes=[pltpu.VMEM((npair, LANES, LANES), F32)],
        compiler_params=_cparams(("parallel", "parallel", "arbitrary")),
        name="rwkv_prompt",
    )(*parts, lw["ln_g_row"], lw["ln_b_row"])


def _rwkv_sample_step_kernel(s_ref, r_ref, ld_ref, k_ref, kk_ref, a_ref, v_ref, g_ref, rk_ref, lng_ref, lnb_ref,
                             o_ref, sout_ref):
    s = s_ref[...]
    r, kmod, kkraw, a = r_ref[...], k_ref[...], kk_ref[...], a_ref[...]
    v = v_ref[...]
    kkn = kkraw / jnp.maximum(jnp.sqrt(jnp.sum(kkraw * kkraw, axis=-1, keepdims=True)), 1e-12)
    sa = jnp.sum(s * (-kkn), axis=-1, keepdims=True)
    sn = s * jnp.exp(ld_ref[...]) + sa * (kkn * a) + v * kmod
    sout_ref[...] = sn
    y = jnp.sum(sn * r, axis=-1, keepdims=True)
    mean = jnp.mean(y, axis=1, keepdims=True)
    yc = y - mean
    var = jnp.mean(yc * yc, axis=1, keepdims=True)
    yn = yc * lax.rsqrt(var + RWKV_LN_EPS) * lng_ref[...] + lnb_ref[...]
    bonus = jnp.sum(r * kmod * rk_ref[...], axis=-1, keepdims=True) * v
    o_ref[...] = (yn + bonus) * g_ref[...]


def rwkv_sample_step(state, rows, cols, lw, layer):
    bd = rows[0].shape[0]
    row = pl.BlockSpec((None, RWKV_HEADS, 1, HEAD_DIM), lambda b: (b, 0, 0, 0))
    col = pl.BlockSpec((None, RWKV_HEADS, HEAD_DIM, 1), lambda b: (b, 0, 0, 0))
    prow = pl.BlockSpec((RWKV_HEADS, 1, HEAD_DIM), lambda b: (0, 0, 0))
    pcol = pl.BlockSpec((RWKV_HEADS, HEAD_DIM, 1), lambda b: (0, 0, 0))
    return pl.pallas_call(
        _rwkv_sample_step_kernel,
        grid=(bd,),
        in_specs=[pl.BlockSpec((None, None, RWKV_HEADS, HEAD_DIM, HEAD_DIM), lambda b: (layer, b, 0, 0, 0))]
        + [row] * 5 + [col] * 2 + [prow, pcol, pcol],
        out_specs=[col, pl.BlockSpec((None, RWKV_HEADS, HEAD_DIM, HEAD_DIM), lambda b: (b, 0, 0, 0))],
        out_shape=[jax.ShapeDtypeStruct((bd, RWKV_HEADS, HEAD_DIM, 1), F32),
                   jax.ShapeDtypeStruct((bd, RWKV_HEADS, HEAD_DIM, HEAD_DIM), F32)],
        compiler_params=_cparams(("parallel",)),
        name="rwkv_sample_step",
    )(state, *rows, *cols, lw["r_k_h"], lw["ln_g_col"], lw["ln_b_col"])


def _cache_update_kernel(k_ref, v_ref, newk_ref, newv_ref, ok_ref, ov_ref):
    win = k_ref.shape[-1]
    is_last = lax.broadcasted_iota(jnp.int32, (1, win), 1) == win - 1
    for c_ref, new_ref, o_ref in ((k_ref, newk_ref, ok_ref), (v_ref, newv_ref, ov_ref)):
        for h in range(ATT_HEADS):
            o_ref[h] = jnp.where(is_last, new_ref[h], pltpu.roll(c_ref[h], win - 1, 1))


def cache_update(cache_k, cache_v, new_k, new_v):
    nl, bd = cache_k.shape[:2]
    win = cache_k.shape[-1]
    blk = pl.BlockSpec((None, None, ATT_HEADS, HEAD_DIM, win), lambda l, b: (l, b, 0, 0, 0))
    new = pl.BlockSpec((None, None, ATT_HEADS, HEAD_DIM, 1), lambda l, b: (l, b, 0, 0, 0))
    shp = jax.ShapeDtypeStruct(cache_k.shape, cache_k.dtype)
    return pl.pallas_call(
        _cache_update_kernel,
        grid=(nl, bd),
        in_specs=[blk, blk, new, new],
        out_specs=[blk, blk],
        out_shape=[shp, shp],
        compiler_params=_cparams(("parallel", "parallel")),
        name="cache_update",
    )(cache_k, cache_v, new_k, new_v)


def _pack_layer(l, p):
    row = lambda a: a.reshape(1, -1)
    pad_rows = lambda a, n: jnp.concatenate([a, jnp.zeros((n - a.shape[0], a.shape[1]), a.dtype)], axis=0).astype(BF16)
    lane_pad = lambda a: jnp.zeros((1, LANES), F32).at[0, DT_LANE0:DT_LANE0 + SSM_HEADS].set(a)
    mu = p["rwkv_mu"][l]
    z1 = lambda n: jnp.zeros((n,), F32)
    cw = p["ssm_conv_w"][l]
    cb = p["ssm_conv_b"][l]
    rep64 = lambda a: jnp.repeat(a, HEAD_DIM)
    return dict(
        ln1_g=p["ln1_g"][l], ln2_g=p["ln2_g"][l],
        cw_x=cw[:, :BRANCH_W], cw_b=cw[:, BRANCH_W:BRANCH_W + 512], cw_c=cw[:, BRANCH_W + 512:],
        cb_x=row(cb[:BRANCH_W]), cb_b=row(cb[BRANCH_W:BRANCH_W + 512]), cb_c=row(cb[BRANCH_W + 512:]),
        conv_w=cw, conv_b=cb,
        dt_bias=lane_pad(p["ssm_dt_bias"][l]), a_neg=lane_pad(-jnp.exp(p["ssm_a_log"][l])),
        d_skip=row(rep64(p["ssm_d"][l])), ssm_norm_g=row(p["ssm_norm_g"][l]),
        dt_bias_h=p["ssm_dt_bias"][l].reshape(SSM_HEADS, 1, 1), a_log_h=p["ssm_a_log"][l].reshape(SSM_HEADS, 1, 1),
        d_h=p["ssm_d"][l].reshape(SSM_HEADS, 1, 1), ssm_norm_g_col=p["ssm_norm_g"][l].reshape(SSM_HEADS, HEAD_DIM, 1),
        mu_r=row(mu[:BRANCH_W]), mu_k=row(mu[BRANCH_W:2 * BRANCH_W]), mu_v=row(mu[2 * BRANCH_W:3 * BRANCH_W]),
        mu_l=row(jnp.concatenate([mu[3 * BRANCH_W:3 * BRANCH_W + LORA_W], z1(32),
                                  mu[3 * BRANCH_W + LORA_W:3 * BRANCH_W + LORA_W + LORA_A], z1(32),
                                  mu[3 * BRANCH_W + LORA_W + LORA_A:]])),
        w0=row(p["rwkv_w0"][l]), w_up=pad_rows(p["rwkv_w_up"][l], LANES), a0=row(p["rwkv_a0"][l]),
        a_up=pad_rows(p["rwkv_a_up"][l], LANES), g_up=p["rwkv_g_up"][l].astype(BF16),
        k_k=row(p["rwkv_k_k"][l]), k_a=row(p["rwkv_k_a"][l]),
        r_k_row=row(p["rwkv_r_k"][l]), ln_g_row=row(p["rwkv_ln_g"][l]), ln_b_row=row(p["rwkv_ln_b"][l]),
        r_k_h=p["rwkv_r_k"][l].reshape(RWKV_HEADS, 1, HEAD_DIM),
        ln_g_col=p["rwkv_ln_g"][l].reshape(RWKV_HEADS, HEAD_DIM, 1),
        ln_b_col=p["rwkv_ln_b"][l].reshape(RWKV_HEADS, HEAD_DIM, 1),
    )


def _pack_big(p):
    w_t = jnp.swapaxes(p["w_in"], 1, 2)
    zrows = lambda n: jnp.zeros((w_t.shape[0], n, D_MODEL), F32)
    w_tail_t = jnp.concatenate([
        w_t[:, _O_XBC + BRANCH_W:_O_DT], w_t[:, _O_LORA:_O_LORA + LORA_W], w_t[:, _O_DT:_O_RW], zrows(8),
        w_t[:, _O_LORA + LORA_W:_O_LORA + LORA_W + LORA_A], zrows(32), w_t[:, _O_LORA + LORA_W + LORA_A:_O_GATE]], axis=1)
    return dict(w_in_t=w_t, w_tail_t=w_tail_t, w_att=p["w_branch_att"].astype(BF16),
                w_ssm=p["w_branch_ssm"].astype(BF16), w_rwkv=p["w_branch_rwkv"].astype(BF16),
                w_out=p["w_out"].astype(BF16), w_up_ff=p["w_up"], w_down_ff=p["w_down"].astype(BF16))


def _unpack_lora(rows):
    return jnp.concatenate([rows[..., 0:LORA_W], rows[..., LANES:LANES + LORA_A], rows[..., 2 * LANES:]], axis=-1)


def _pack_lora(rows):
    z = lambda n: jnp.zeros(rows.shape[:-1] + (n,), rows.dtype)
    return jnp.concatenate([rows[..., 0:LORA_W], z(32), rows[..., LORA_W:LORA_W + LORA_A], z(32),
                            rows[..., LORA_W + LORA_A:]], axis=-1)


def _alibi_slopes():
    n = 3 * ATT_HEADS
    idx = jnp.arange(1, n + 1, dtype=F32)
    return jnp.exp2(-8.0 * idx / n).reshape(3, ATT_HEADS)


def _layer(xp, xs, hp, hs, lw, big, layer, slopes, st, bsz, t, bd, next_g, next_dtype):
    proj, tail, proj_s, tail_s = project_main_and_tail(hp, hs, big, layer)
    outs, state_p = _prompt_mixers(proj, tail, lw, slopes, bsz, t)
    outs_s, new_rows, state_s = _sample_mixers(proj_s, tail_s, lw, slopes, st, layer, bd)
    merged, merged_s = merge_branches(outs, proj, outs_s, proj_s, big["w_att"], big["w_ssm"], big["w_rwkv"], layer)
    xp, h2, xs, h2_s = matmul_res_norm(merged, merged_s, big["w_out"], layer, xp, xs, lw["ln2_g"], BF16)
    up, up_s = matmul(h2, h2_s, big["w_up_ff"], layer, epi="relu2", out_dtype=BF16)
    xp, xs = matmul(up, up_s, big["w_down_ff"], layer, epi="res", res=xp, res_s=xs)
    hp, hs = rmsnorm(xp, next_g, next_dtype), rmsnorm(xs, next_g, next_dtype)
    return xp, xs, hp, hs, state_p, new_rows, state_s


def _prompt_mixers(proj, tail, lw, slopes, bsz, t):
    o_att = att_prompt(proj, slopes, bsz, t)
    o_ssm, h_fin = ssd_prompt(proj, tail, lw, bsz, t)
    zero_init = [jnp.zeros((bsz, 1, w), F32) for w in (BRANCH_W, BRANCH_W, BRANCH_W, 512)]
    rw = rwkv_prep_prompt(proj, tail, zero_init, lw, bsz, t)
    o_rwkv, s_fin = rwkv_prompt(rw, lw, bsz, t)

    p3 = proj.reshape(bsz, t, N_MAIN)
    t3 = tail.reshape(bsz, t, N_TAIL)
    kv = []
    for g in range(3):
        win = ATT_WINDOWS[g]
        for c0 in (C_K, C_V):
            kv.append(p3[:, t - win:, c0 + g * ATT_OUT:c0 + (g + 1) * ATT_OUT].reshape(bsz, win, ATT_HEADS, HEAD_DIM))
    nc = SSM_CONV - 1
    conv_new = jnp.concatenate([p3[:, t - nc:, C_X:C_X + BRANCH_W], t3[:, t - nc:, T_B:T_B + 1024]], -1)
    shift_new = jnp.concatenate([p3[:, t - 1, C_R:C_R + 3 * BRANCH_W], _unpack_lora(t3[:, t - 1, T_L:T_L + 512])], axis=-1)
    ssm_new = h_fin.reshape(bsz, SSM_HEADS, HEAD_DIM, SSM_STATE)
    s6 = s_fin.reshape(bsz, N_PAIRS, 2, HEAD_DIM, 2, HEAD_DIM)
    s_heads = jnp.stack([s6[:, :, 0, :, 0, :], s6[:, :, 1, :, 1, :]], axis=2)
    rwkv_new = jnp.swapaxes(s_heads, -1, -2).reshape(bsz, RWKV_HEADS, HEAD_DIM, HEAD_DIM)
    return (o_att, o_ssm, o_rwkv), tuple(kv) + (conv_new, ssm_new, shift_new, rwkv_new)


def _sample_mixers(proj, tail, lw, slopes, st, layer, bd):
    (caches_t, conv_st, ssm_st, shift_st, rwkv_st) = st
    qkv = proj[:, :3 * BRANCH_W].reshape(bd, 3, 3, ATT_HEADS, HEAD_DIM, 1)
    o_att = att_sample(qkv, caches_t, slopes, layer).reshape(bd, ATT_OUT).astype(BF16)
    new_rows = tuple(qkv[:, part, g] for g in range(3) for part in (1, 2))

    xbc_new = jnp.concatenate([proj[:, C_X:C_X + BRANCH_W], tail[:, T_B:T_B + 1024]], axis=-1)
    cst = conv_st[layer]
    xc = ssd_sample_conv(xbc_new, jnp.swapaxes(cst, 0, 1), lw["conv_w"], lw["conv_b"])
    conv_new = jnp.concatenate([cst[:, 1:], xbc_new[:, None]], axis=1)
    x_col = xc[:, :BRANCH_W].reshape(bd, SSM_HEADS, HEAD_DIM, 1)
    z_col = proj[:, C_Z:C_Z + BRANCH_W].reshape(bd, SSM_HEADS, HEAD_DIM, 1)
    bmat = xc[:, BRANCH_W:BRANCH_W + 512].reshape(bd, SSM_GROUPS, 1, SSM_STATE)
    cmat = xc[:, BRANCH_W + 512:].reshape(bd, SSM_GROUPS, 1, SSM_STATE)
    dt_raw = tail[:, T_L + DT_LANE0:T_L + DT_LANE0 + SSM_HEADS].reshape(bd, SSM_HEADS, 1, 1)
    y_col, ssm_new = ssd_sample_step(ssm_st, x_col, z_col, bmat, cmat, dt_raw, lw, layer)
    o_ssm = y_col.reshape(bd, BRANCH_W).astype(BF16)

    sh = shift_st[layer]
    init = [sh[:, None, i * BRANCH_W:(i + 1) * BRANCH_W] for i in range(3)] + [_pack_lora(sh[:, None, 3 * BRANCH_W:])]
    r, ld, kmod, v, kkraw, a, gg = rwkv_prep(proj, tail, init, lw, bd, 1)
    rowf = lambda z: z.reshape(bd, RWKV_HEADS, 1, HEAD_DIM)
    colf = lambda z: z.reshape(bd, RWKV_HEADS, HEAD_DIM, 1)
    o_col, rwkv_new = rwkv_sample_step(rwkv_st, [rowf(r), rowf(ld), rowf(kmod), rowf(kkraw), rowf(a)],
                                       [colf(v), colf(gg)], lw, layer)
    o_rwkv = o_col.reshape(bd, BRANCH_W).astype(BF16)
    shift_new = jnp.concatenate([proj[:, C_R:C_R + 3 * BRANCH_W], _unpack_lora(tail[:, T_L:T_L + 512])], axis=-1)
    return (o_att, o_ssm, o_rwkv), new_rows, (conv_new, ssm_new, shift_new, rwkv_new)


def kernel(x_prompt, x_sample, cache_att_k0, cache_att_v0, cache_att_k1, cache_att_v1, cache_att_k2, cache_att_v2, state_ssm_conv, state_ssm, state_rwkv_shift, state_rwkv, ln1_g, w_in, ssm_conv_w, ssm_conv_b, ssm_dt_bias, ssm_a_log, ssm_d, ssm_norm_g, rwkv_mu, rwkv_w0, rwkv_w_up, rwkv_a0, rwkv_a_up, rwkv_g_up, rwkv_k_k, rwkv_k_a, rwkv_r_k, rwkv_ln_g, rwkv_ln_b, w_branch_att, w_branch_ssm, w_branch_rwkv, w_out, ln2_g, w_up, w_down, final_g):
    p = dict(ln1_g=ln1_g, w_in=w_in, ssm_conv_w=ssm_conv_w, ssm_conv_b=ssm_conv_b, ssm_dt_bias=ssm_dt_bias,
             ssm_a_log=ssm_a_log, ssm_d=ssm_d, ssm_norm_g=ssm_norm_g, rwkv_mu=rwkv_mu, rwkv_w0=rwkv_w0,
             rwkv_w_up=rwkv_w_up, rwkv_a0=rwkv_a0, rwkv_a_up=rwkv_a_up, rwkv_g_up=rwkv_g_up, rwkv_k_k=rwkv_k_k,
             rwkv_k_a=rwkv_k_a, rwkv_r_k=rwkv_r_k, rwkv_ln_g=rwkv_ln_g, rwkv_ln_b=rwkv_ln_b,
             w_branch_att=w_branch_att, w_branch_ssm=w_branch_ssm, w_branch_rwkv=w_branch_rwkv, w_out=w_out,
             ln2_g=ln2_g, w_up=w_up, w_down=w_down)
    depth = w_in.shape[0]
    bsz, t, _ = x_prompt.shape
    bd = x_sample.shape[0]
    slopes = _alibi_slopes()
    caches = (cache_att_k0, cache_att_v0, cache_att_k1, cache_att_v1, cache_att_k2, cache_att_v2)
    caches_t = tuple(jnp.transpose(c, (0, 1, 3, 4, 2)) for c in caches)
    st = (caches_t, state_ssm_conv, state_ssm, state_rwkv_shift, state_rwkv)
    xp = x_prompt.reshape(bsz * t, D_MODEL)
    xs = x_sample.reshape(bd, D_MODEL)
    big = _pack_big(p)
    p_new, s_new, s_rows = [], [], []
    hp = rmsnorm(xp, ln1_g[0], BF16)
    hs = rmsnorm(xs, ln1_g[0], BF16)
    for l in range(depth):
        lw = _pack_layer(l, p)
        last = l == depth - 1
        next_g, next_dtype = (final_g, F32) if last else (ln1_g[l + 1], BF16)
        xp, xs, hp, hs, sp, rows, ss = _layer(xp, xs, hp, hs, lw, big, l, slopes, st, bsz, t, bd, next_g, next_dtype)
        p_new.append(sp)
        s_new.append(ss)
        s_rows.append(rows)
    y_prompt = hp.reshape(bsz, t, D_MODEL)
    y_sample = hs.reshape(bd, 1, D_MODEL)
    stack = lambda per_layer, n: tuple(jnp.stack([s[i] for s in per_layer]) for i in range(n))
    new_rows = stack(s_rows, 6)
    s_caches = ()
    for g in range(3):
        upd = cache_update(caches_t[2 * g], caches_t[2 * g + 1], new_rows[2 * g], new_rows[2 * g + 1])
        s_caches += tuple(jnp.transpose(c, (0, 1, 4, 2, 3)) for c in upd)
    return (y_prompt, y_sample) + stack(p_new, 10) + s_caches + stack(s_new, 4)
```
